```python
import math
import jax
import jax.numpy as jnp
from jax import lax
import numpy as np

D_MODEL = 2048
BATCH = 2
SEQ = 8192
DEPTH = 4

GRID_W = 64
CTX_LEN = 256
N_MIXERS = 4
NORM_EPS = 1e-6
NEG_INF = -1e30
HEAD_DIM = 64
ROPE_BASE = 10000.0

HY_ORDER = 2
HY_SHORT = 3
HY_BANDS = 16
HY_EMB = 2 * HY_BANDS + 1
HY_FILTER_WIDTH = 64
HY_DECAY_TARGET = 1e-2
HY_FAST_DECAY = 0.3
HY_SLOW_DECAY = 1.5

CF_CONV_WIDTH = 31

WA_HEADS = D_MODEL // HEAD_DIM
WA_KV_HEADS = WA_HEADS // 4
WA_GROUP = WA_HEADS // WA_KV_HEADS
WA_WINDOW = 128
WA_BLOCK = 128

NA_HEADS = D_MODEL // HEAD_DIM
NA_WIN_ROWS = 8
NA_WIN_COLS = 16

N_EXPERTS = 32
TOP_K = 4
D_EXPERT = 3 * D_MODEL // 8
SWIGLU_LIMIT = 7.0
SWIGLU_ALPHA = 1.702
MOE_BLOCK = 128

kernel_name = 'hybrid_dit_hyena_conformer_swa_natten_moe'

F32 = jnp.float32


def _n_layers_of(kind):
    return len(range(kind, DEPTH, N_MIXERS))


def rms_norm(x, g):
    xf = x.astype(F32)
    y = xf * lax.rsqrt(jnp.mean(xf * xf, axis=-1, keepdims=True) + NORM_EPS)
    return (y * g.astype(F32)).astype(x.dtype)


def layer_norm(x, g, b):
    xf = x.astype(F32)
    xc = xf - jnp.mean(xf, axis=-1, keepdims=True)
    var = jnp.mean(xc * xc, axis=-1, keepdims=True)
    return (xc * lax.rsqrt(var + NORM_EPS) * g.astype(F32) + b.astype(F32)).astype(x.dtype)


def modulate(x, shift, scale):
    return x * (1 + scale) + shift


def depthwise_conv(x, w, b):
    k = w.shape[0]
    y = lax.conv_general_dilated(x, w[:, None, :].astype(x.dtype), window_strides=(1,),
                                 padding=[(k // 2, k // 2)],
                                 dimension_numbers=('NWC', 'WIO', 'NWC'),
                                 feature_group_count=x.shape[-1])
    return y + b


def grid_positions(n_tokens):
    t = jnp.arange(n_tokens, dtype=jnp.int32)
    return (t // GRID_W).astype(F32), (t % GRID_W).astype(F32)


def rope_1d(x, pos):
    half = x.shape[-1] // 2
    inv_freq = ROPE_BASE ** (-jnp.arange(half, dtype=F32) / half)
    ang = pos[:, None] * inv_freq[None, :]
    cos = jnp.cos(ang)[None, :, None, :]
    sin = jnp.sin(ang)[None, :, None, :]
    x1 = x[..., :half].astype(F32)
    x2 = x[..., half:].astype(F32)
    return jnp.concatenate([x1 * cos - x2 * sin, x2 * cos + x1 * sin], axis=-1).astype(x.dtype)


def axial_rope(x, row, col):
    d = x.shape[-1] // 2
    return jnp.concatenate([rope_1d(x[..., :d], row), rope_1d(x[..., d:], col)], axis=-1)


def hyena_filters(n, f_w1, f_b1, f_w2, f_b2, f_w3, f_b3, f_freq, f_w4):
    t = jnp.linspace(0.0, 1.0, n, dtype=F32)[:, None]
    ang = (2.0 * math.pi / n) * jnp.arange(n, dtype=F32)[:, None]
    bands = jnp.linspace(1e-4, HY_BANDS - 1, HY_BANDS, dtype=F32)[None, :]
    emb = jnp.concatenate([t, jnp.cos(bands * ang), -jnp.sin(bands * ang)], axis=-1)
    a = jnp.sin(f_freq * (emb @ f_w1 + f_b1))
    a = jnp.sin(f_freq * (a @ f_w2 + f_b2))
    a = jnp.sin(f_freq * (a @ f_w3 + f_b3))
    h = (a @ f_w4).astype(F32).reshape(n, HY_ORDER, 2, D_MODEL)
    deltas = jnp.abs(jnp.linspace(math.log(HY_DECAY_TARGET) / HY_FAST_DECAY,
                                  math.log(HY_DECAY_TARGET) / HY_SLOW_DECAY, D_MODEL, dtype=F32))
    decay = jnp.exp(-t * deltas[None, :])
    return h * decay[:, None, None, :]


def long_conv_bidir(u, h_fwd, h_bwd, skip):
    n = u.shape[1]
    taps = jnp.concatenate([h_fwd, jnp.zeros((1, h_fwd.shape[1]), F32), h_bwd[:0:-1]], axis=0)
    uf = u.astype(F32)
    spec = jnp.fft.rfft(uf, n=2 * n, axis=1) * jnp.fft.rfft(taps, n=2 * n, axis=0)[None]
    y = jnp.fft.irfft(spec, n=2 * n, axis=1)[:, :n]
    return (y + uf * skip.astype(F32)).astype(u.dtype)


def hyena_sequence(h, w_in, b_in, w_short, b_short, f_w1, f_b1, f_w2, f_b2, f_w3, f_b3,
                   f_freq, f_w4, skip, w_out, b_out):
    n = h.shape[1]
    z = depthwise_conv(h @ w_in + b_in, w_short, b_short)
    x1, x2, v = jnp.split(z, 3, axis=-1)
    filt = hyena_filters(n, f_w1, f_b1, f_w2, f_b2, f_w3, f_b3, f_freq, f_w4)
    y = x1 * long_conv_bidir(v, filt[:, 0, 0], filt[:, 0, 1], skip[0])
    y = x2 * long_conv_bidir(y, filt[:, 1, 0], filt[:, 1, 1], skip[1])
    return y @ w_out + b_out


def hyena_mixer(hl, hc, *params):
    yl = hyena_sequence(hl, *params)
    yc = None if hc is None else hyena_sequence(hc, *params)
    return yl, yc


def conformer_sequence(h, w_pw1, b_pw1, w_dw, b_dw, ln_g, ln_b, w_pw2, b_pw2):
    d = h.shape[-1]
    a = h @ w_pw1 + b_pw1
    a = a[..., :d] * jax.nn.sigmoid(a[..., d:])
    a = depthwise_conv(a, w_dw, b_dw)
    a = layer_norm(a, ln_g, ln_b)
    return jax.nn.silu(a) @ w_pw2 + b_pw2


def conformer_mixer(hl, hc, *params):
    yl = conformer_sequence(hl, *params)
    yc = None if hc is None else conformer_sequence(hc, *params)
    return yl, yc


def softmax_with_sink(s, sink):
    m = jnp.maximum(jnp.max(s, axis=-1, keepdims=True), sink)
    p = jnp.exp(s - m)
    return p / (jnp.sum(p, axis=-1, keepdims=True) + jnp.exp(sink - m))


def proj_heads(h, w_qkv, n_q, n_kv, with_q=True):
    bsz, n, _ = h.shape
    q_width = n_q * HEAD_DIM
    if with_q:
        qkv = h @ w_qkv
        q = qkv[..., :q_width].reshape(bsz, n, n_q, HEAD_DIM)
        kv = qkv[..., q_width:]
    else:
        q = None
        kv = h @ w_qkv[:, q_width:]
    k, v = jnp.split(kv, 2, axis=-1)
    return q, k.reshape(bsz, n, n_kv, HEAD_DIM), v.reshape(bsz, n, n_kv, HEAD_DIM)


def window_gqa(hl, hc, w_qkv, w_o, sinks, ctx_out):
    bsz, n, d = hl.shape
    q, k, v = proj_heads(hl, w_qkv, WA_HEADS, WA_KV_HEADS)
    row, col = grid_positions(n)
    q = axial_rope(q, row, col).reshape(bsz, n, WA_KV_HEADS, WA_GROUP, HEAD_DIM)
    k = axial_rope(k, row, col)
    qc, kc, vc = proj_heads(hc, w_qkv, WA_HEADS, WA_KV_HEADS, with_q=ctx_out)
    scale = HEAD_DIM ** -0.5
    sink = sinks.astype(F32).reshape(1, WA_KV_HEADS, WA_GROUP, 1, 1)
    span = WA_BLOCK + 2 * WA_WINDOW
    pad = ((0, 0), (WA_WINDOW, WA_WINDOW), (0, 0), (0, 0))
    kp = jnp.pad(k, pad)
    vp = jnp.pad(v, pad)

    def query_block(b):
        start = b * WA_BLOCK
        qb = lax.dynamic_slice_in_dim(q, start, WA_BLOCK, axis=1)
        kb = lax.dynamic_slice_in_dim(kp, start, span, axis=1)
        vb = lax.dynamic_slice_in_dim(vp, start, span, axis=1)
        qpos = start + jnp.arange(WA_BLOCK)
        kpos = start - WA_WINDOW + jnp.arange(span)
        valid = ((jnp.abs(kpos[None, :] - qpos[:, None]) <= WA_WINDOW)
                 & (kpos >= 0)[None, :] & (kpos < n)[None, :])
        s_loc = jnp.einsum('bqhgd,bkhd->bhgqk', qb, kb).astype(F32) * scale
        s_loc = jnp.where(valid, s_loc, NEG_INF)
        s_ctx = jnp.einsum('bqhgd,bkhd->bhgqk', qb, kc).astype(F32) * scale
        p = softmax_with_sink(jnp.concatenate([s_loc, s_ctx], axis=-1), sink).astype(vb.dtype)
        o = (jnp.einsum('bhgqk,bkhd->bqhgd', p[..., :span], vb)
             + jnp.einsum('bhgqk,bkhd->bqhgd', p[..., span:], vc))
        return o.reshape(bsz, WA_BLOCK, d)

    o = lax.map(query_block, jnp.arange(n // WA_BLOCK))
    yl = o.transpose(1, 0, 2, 3).reshape(bsz, n, d) @ w_o
    yc = None
    if ctx_out:
        qc = qc.reshape(bsz, -1, WA_KV_HEADS, WA_GROUP, HEAD_DIM)
        s = jnp.einsum('bqhgd,bkhd->bhgqk', qc, kc).astype(F32) * scale
        p = softmax_with_sink(s, sink).astype(vc.dtype)
        yc = jnp.einsum('bhgqk,bkhd->bqhgd', p, vc).reshape(bsz, -1, d) @ w_o
    return yl, yc


def neighbourhood_attn(hl, hc, w_qkv, w_o, rpb, ctx_out):
    bsz, n, d = hl.shape
    rows = n // GRID_W
    wr = min(NA_WIN_ROWS, rows)
    n_nb = wr * NA_WIN_COLS
    q, k, v = proj_heads(hl, w_qkv, NA_HEADS, NA_HEADS)
    qg, kg, vg = (t.reshape(bsz, rows, GRID_W, NA_HEADS, HEAD_DIM) for t in (q, k, v))
    qc, kc, vc = proj_heads(hc, w_qkv, NA_HEADS, NA_HEADS, with_q=ctx_out)
    cols = jnp.arange(GRID_W)
    col_start = jnp.clip(cols - NA_WIN_COLS // 2, 0, GRID_W - NA_WIN_COLS)
    col_idx = col_start[:, None] + jnp.arange(NA_WIN_COLS)[None, :]
    dcol = col_idx - cols[:, None]
    scale = HEAD_DIM ** -0.5

    def row_block(r):
        r0 = jnp.clip(r - wr // 2, 0, rows - wr)
        qr = lax.dynamic_index_in_dim(qg, r, axis=1, keepdims=False)

        def gather_neighbours(t):
            strip = lax.dynamic_slice_in_dim(t, r0, wr, axis=1)
            g = strip[:, :, col_idx]
            return g.transpose(0, 2, 1, 3, 4, 5).reshape(bsz, GRID_W, n_nb, NA_HEADS, HEAD_DIM)

        kn = gather_neighbours(kg)
        vn = gather_neighbours(vg)
        drow = r0 + jnp.arange(wr) - r
        bias = rpb[:, drow[:, None, None] + NA_WIN_ROWS - 1, dcol[None] + NA_WIN_COLS - 1]
        bias = bias.transpose(0, 2, 1, 3).reshape(NA_HEADS, GRID_W, n_nb).astype(F32)
        s_nb = jnp.einsum('bqhd,bqkhd->bhqk', qr, kn).astype(F32) * scale + bias[None]
        s_cx = jnp.einsum('bqhd,bkhd->bhqk', qr, kc).astype(F32) * scale
        p = jax.nn.softmax(jnp.concatenate([s_nb, s_cx], axis=-1), axis=-1).astype(vn.dtype)
        return (jnp.einsum('bhqk,bqkhd->bqhd', p[..., :n_nb], vn)
                + jnp.einsum('bhqk,bkhd->bqhd', p[..., n_nb:], vc))

    o = lax.map(row_block, jnp.arange(rows))
    yl = o.transpose(1, 0, 2, 3, 4).reshape(bsz, n, d) @ w_o
    yc = None
    if ctx_out:
        s = jnp.einsum('bqhd,bkhd->bhqk', qc, kc).astype(F32) * scale
        p = jax.nn.softmax(s, axis=-1).astype(vc.dtype)
        yc = jnp.einsum('bhqk,bkhd->bqhd', p, vc).reshape(bsz, -1, d) @ w_o
    return yl, yc


def moe_ffn(h, w_router, b_router, w_gu, b_gu, w_down, b_down):
    n_tok, d = h.shape
    logits = (h @ w_router + b_router).astype(F32)
    top_val, top_idx = lax.top_k(logits, TOP_K)
    gates = jax.nn.softmax(top_val, axis=-1)
    n_asg = n_tok * TOP_K
    flat_e = top_idx.reshape(n_asg)
    order = jnp.argsort(flat_e)
    sorted_e = flat_e[order]
    tok = (order // TOP_K).astype(jnp.int32)
    w_sorted = gates.reshape(n_asg)[order]
    counts = jnp.bincount(flat_e, length=N_EXPERTS)
    padded = (counts + MOE_BLOCK - 1) // MOE_BLOCK * MOE_BLOCK
    pad_end = jnp.cumsum(padded)
    pad_start = pad_end - padded
    grp_start = jnp.cumsum(counts) - counts
    dest = pad_start[sorted_e] + jnp.arange(n_asg) - grp_start[sorted_e]
    n_blk = -(-(n_asg + N_EXPERTS * (MOE_BLOCK - 1)) // MOE_BLOCK)
    n_slot = n_blk * MOE_BLOCK
    slot_tok = jnp.full((n_slot,), n_tok, jnp.int32).at[dest].set(tok)
    slot_w = jnp.zeros((n_slot,), F32).at[dest].set(w_sorted)
    blk_e = jnp.minimum(jnp.searchsorted(pad_end, jnp.arange(n_blk) * MOE_BLOCK, side='right'),
                        N_EXPERTS - 1)
    h_pad = jnp.concatenate([h, jnp.zeros((1, d), h.dtype)], axis=0)

    def expert_block(args):
        toks, e = args
        xb = h_pad[toks]
        gu = xb @ w_gu[e] + b_gu[e]
        g = jnp.minimum(gu[:, :D_EXPERT], SWIGLU_LIMIT)
        u = jnp.clip(gu[:, D_EXPERT:], -SWIGLU_LIMIT, SWIGLU_LIMIT)
        return ((u + 1) * (g * jax.nn.sigmoid(SWIGLU_ALPHA * g))) @ w_down[e] + b_down[e]

    y_slot = lax.map(expert_block, (slot_tok.reshape(n_blk, MOE_BLOCK), blk_e))
    y_slot = y_slot.reshape(n_slot, d) * slot_w[:, None].astype(y_slot.dtype)
    return jax.ops.segment_sum(y_slot, slot_tok, num_segments=n_tok + 1)[:n_tok]


def setup_inputs(seed: int = 0) -> dict:
    key = jax.random.key(seed)
    keys = iter(jax.random.split(key, 48))
    d, e, f = D_MODEL, N_EXPERTS, D_EXPERT
    n_a, n_b, n_c, n_d = (_n_layers_of(k) for k in range(N_MIXERS))
    wa_width = (WA_HEADS + 2 * WA_KV_HEADS) * HEAD_DIM

    def normal(shape, scale):
        return jax.random.normal(next(keys), shape, F32) * scale

    def gain(shape):
        return 1.0 + normal(shape, 0.02)

    return {
        'x': normal((BATCH, SEQ, d), 1.0),
        'c': normal((BATCH, d), 1.0),
        'ctx': normal((BATCH, CTX_LEN, d), 1.0),
        'c_ctx': normal((d,), 1.0),
        'w_ada': normal((DEPTH, d, 6 * d), 0.5 * d ** -0.5),
        'b_ada': normal((DEPTH, 6 * d), 0.02),
        'g_mix': gain((DEPTH, d)),
        'g_ffn': gain((DEPTH, d)),
        'hy_w_in': normal((n_a, d, 3 * d), d ** -0.5),
        'hy_b_in': normal((n_a, 3 * d), 0.02),
        'hy_w_short': normal((n_a, HY_SHORT, 3 * d), HY_SHORT ** -0.5),
        'hy_b_short': normal((n_a, 3 * d), 0.02),
        'hy_f_w1': normal((n_a, HY_EMB, HY_FILTER_WIDTH), HY_EMB ** -0.5),
        'hy_f_b1': normal((n_a, HY_FILTER_WIDTH), 0.1),
        'hy_f_w2': normal((n_a, HY_FILTER_WIDTH, HY_FILTER_WIDTH), HY_FILTER_WIDTH ** -0.5),
        'hy_f_b2': normal((n_a, HY_FILTER_WIDTH), 0.1),
        'hy_f_w3': normal((n_a, HY_FILTER_WIDTH, HY_FILTER_WIDTH), HY_FILTER_WIDTH ** -0.5),
        'hy_f_b3': normal((n_a, HY_FILTER_WIDTH), 0.1),
        'hy_f_freq': gain((n_a, HY_FILTER_WIDTH)),
        'hy_f_w4': normal((n_a, HY_FILTER_WIDTH, HY_ORDER * 2 * d), 0.03 * HY_FILTER_WIDTH ** -0.5),
        'hy_skip': normal((n_a, HY_ORDER, d), 0.5),
        'hy_w_out': normal((n_a, d, d), d ** -0.5),
        'hy_b_out': normal((n_a, d), 0.02),
        'cf_w_pw1': normal((n_b, d, 2 * d), d ** -0.5),
        'cf_b_pw1': normal((n_b, 2 * d), 0.02),
        'cf_w_dw': normal((n_b, CF_CONV_WIDTH, d), CF_CONV_WIDTH ** -0.5),
        'cf_b_dw': normal((n_b, d), 0.02),
        'cf_ln_g': gain((n_b, d)),
        'cf_ln_b': normal((n_b, d), 0.02),
        'cf_w_pw2': normal((n_b, d, d), d ** -0.5),
        'cf_b_pw2': normal((n_b, d), 0.02),
        'wa_w_qkv': normal((n_c, d, wa_width), d ** -0.5),
        'wa_w_o': normal((n_c, d, d), d ** -0.5),
        'wa_sinks': normal((n_c, WA_HEADS), 0.5),
        'na_w_qkv': normal((n_d, d, 3 * d), d ** -0.5),
        'na_w_o': normal((n_d, d, d), d ** -0.5),
        'na_rpb': normal((n_d, NA_HEADS, 2 * NA_WIN_ROWS - 1, 2 * NA_WIN_COLS - 1), 0.1),
        'moe_w_router': normal((DEPTH, d, e), d ** -0.5),
        'moe_b_router': normal((DEPTH, e), 0.01),
        'moe_w_gu': normal((DEPTH, e, d, 2 * f), d ** -0.5),
        'moe_b_gu': normal((DEPTH, e, 2 * f), 0.02),
        'moe_w_down': normal((DEPTH, e, f, d), f ** -0.5),
        'moe_b_down': normal((DEPTH, e, d), 0.02),
        'g_final': gain((d,)),
    }


def reference(x, c, ctx, c_ctx, w_ada, b_ada, g_mix, g_ffn,
              hy_w_in, hy_b_in, hy_w_short, hy_b_short, hy_f_w1, hy_f_b1, hy_f_w2, hy_f_b2,
              hy_f_w3, hy_f_b3, hy_f_freq, hy_f_w4, hy_skip, hy_w_out, hy_b_out,
              cf_w_pw1, cf_b_pw1, cf_w_dw, cf_b_dw, cf_ln_g, cf_ln_b, cf_w_pw2, cf_b_pw2,
              wa_w_qkv, wa_w_o, wa_sinks,
              na_w_qkv, na_w_o, na_rpb,
              moe_w_router, moe_b_router, moe_w_gu, moe_b_gu, moe_w_down, moe_b_down,
              g_final):
    bsz, n, d = x.shape
    cond_l = jax.nn.silu(c)
    cond_c = jax.nn.silu(c_ctx)
    for i in range(DEPTH):
        kind, j = i % N_MIXERS, i // N_MIXERS
        last = i == DEPTH - 1
        mod_l = jnp.split(cond_l @ w_ada[i] + b_ada[i], 6, axis=-1)
        mod_c = jnp.split(cond_c @ w_ada[i] + b_ada[i], 6, axis=-1)

        hl = modulate(rms_norm(x, g_mix[i]), mod_l[0][:, None], mod_l[1][:, None])
        need_ctx = (not last) or kind >= 2
        hc = modulate(rms_norm(ctx, g_mix[i]), mod_c[0], mod_c[1]) if need_ctx else None
        if kind == 0:
            yl, yc = hyena_mixer(hl, hc, hy_w_in[j], hy_b_in[j], hy_w_short[j], hy_b_short[j],
                                 hy_f_w1[j], hy_f_b1[j], hy_f_w2[j], hy_f_b2[j], hy_f_w3[j],
                                 hy_f_b3[j], hy_f_freq[j], hy_f_w4[j], hy_skip[j],
                                 hy_w_out[j], hy_b_out[j])
        elif kind == 1:
            yl, yc = conformer_mixer(hl, hc, cf_w_pw1[j], cf_b_pw1[j], cf_w_dw[j], cf_b_dw[j],
                                     cf_ln_g[j], cf_ln_b[j], cf_w_pw2[j], cf_b_pw2[j])
        elif kind == 2:
            yl, yc = window_gqa(hl, hc, wa_w_qkv[j], wa_w_o[j], wa_sinks[j], not last)
        else:
            yl, yc = neighbourhood_attn(hl, hc, na_w_qkv[j], na_w_o[j], na_rpb[j], not last)
        x = x + mod_l[2][:, None] * yl
        if not last:
            ctx = ctx + mod_c[2] * yc

        hl = modulate(rms_norm(x, g_ffn[i]), mod_l[3][:, None], mod_l[4][:, None]).reshape(bsz * n, d)
        if last:
            yl = moe_ffn(hl, moe_w_router[i], moe_b_router[i], moe_w_gu[i], moe_b_gu[i],
                         moe_w_down[i], moe_b_down[i])
        else:
            hc = modulate(rms_norm(ctx, g_ffn[i]), mod_c[3], mod_c[4]).reshape(-1, d)
            y = moe_ffn(jnp.concatenate([hl, hc], axis=0), moe_w_router[i], moe_b_router[i],
                        moe_w_gu[i], moe_b_gu[i], moe_w_down[i], moe_b_down[i])
            yl = y[:bsz * n]
            ctx = ctx + mod_c[5] * y[bsz * n:].reshape(ctx.shape)
        x = x + mod_l[5][:, None] * yl.reshape(bsz, n, d)
    return rms_norm(x, g_final)
```

```python
import functools
import math

import jax
import jax.numpy as jnp
from jax import lax
from jax.experimental import pallas as pl
from jax.experimental.pallas import tpu as pltpu

F32 = jnp.float32
BF16 = jnp.bfloat16

GRID_W = 64
N_MIXERS = 4
NORM_EPS = 1e-6
NEG_INF = -1e30
HEAD_DIM = 64
ROPE_BASE = 10000.0

HY_ORDER = 2
HY_BANDS = 16
HY_DECAY_TARGET = 1e-2
HY_FAST_DECAY = 0.3
HY_SLOW_DECAY = 1.5

WA_GROUP = 4
WA_WINDOW = 128
NA_WIN_ROWS = 8
NA_WIN_COLS = 16

TOP_K = 4
SWIGLU_LIMIT = 7.0
SWIGLU_ALPHA = 1.702

LANES = 128
ROW_TILE = 512
MOE_ROWS = 256
VMEM_LIMIT = 56 * 1024 * 1024


def _cparams(n_axes):
    return pltpu.CompilerParams(dimension_semantics=("arbitrary",) * n_axes,
                                vmem_limit_bytes=VMEM_LIMIT)


def _col_tile(n, pref=1024):
    t = min(pref, n)
    while n % t:
        t //= 2
    return t


def _adaln_kernel(c_ref, w_ref, b_ref, o_ref):
    w = w_ref[0]
    w_hi = w.astype(BF16)
    w_lo = (w - w_hi.astype(F32)).astype(BF16)
    c = c_ref[...]
    c_hi = c.astype(BF16)
    c_lo = (c - c_hi.astype(F32)).astype(BF16)
    acc = jnp.dot(c_hi, w_hi, preferred_element_type=F32)
    acc += jnp.dot(c_hi, w_lo, preferred_element_type=F32)
    acc += jnp.dot(c_lo, w_hi, preferred_element_type=F32)
    o_ref[0] = acc + b_ref[0]


def adaln(cond, w_ada, b_ada):
    depth, d, n6 = w_ada.shape
    tn = _col_tile(n6, 1024)
    return pl.pallas_call(
        _adaln_kernel,
        grid=(depth, n6 // tn),
        in_specs=[pl.BlockSpec((8, d), lambda l, j: (0, 0)),
                  pl.BlockSpec((1, d, tn), lambda l, j: (l, 0, j)),
                  pl.BlockSpec((1, 1, tn), lambda l, j: (l, 0, j))],
        out_specs=pl.BlockSpec((1, 8, tn), lambda l, j: (l, 0, j)),
        out_shape=jax.ShapeDtypeStruct((depth, 8, n6), F32),
        compiler_params=_cparams(2),
        name="adaln",
    )(cond, w_ada, b_ada.reshape(depth, 1, n6))


def _norm_mod(x, g, shift, scale):
    y = x * lax.rsqrt(jnp.mean(x * x, axis=-1, keepdims=True) + NORM_EPS)
    return (y * g) * (1 + scale) + shift


def _rope_tile(x, cos, sin):
    lane = lax.broadcasted_iota(jnp.int32, x.shape, 1)
    nxt = pltpu.roll(x, LANES - 16, axis=1)
    prv = pltpu.roll(x, 16, axis=1)
    partner = jnp.where((lane // 16) % 2 == 0, nxt, prv)
    return x * cos + partner * sin


def _nm_mm_kernel(x_ref, g_ref, sh_ref, sc_ref, w_ref, b_ref, *rest, mode, n_rope):
    if mode == "rope":
        cos_ref, sin_ref, o_ref, h_ref = rest
    else:
        o_ref, h_ref = rest
    j = pl.program_id(1)

    @pl.when(j == 0)
    def _():
        h_ref[...] = _norm_mod(x_ref[...], g_ref[...], sh_ref[0], sc_ref[0]).astype(BF16)

    h = h_ref[...]
    if mode == "glu":
        a = jnp.dot(h, w_ref[0], preferred_element_type=F32) + b_ref[0]
        gate = jnp.dot(h, w_ref[1], preferred_element_type=F32) + b_ref[1]
        o_ref[...] = (a * jax.nn.sigmoid(gate)).astype(o_ref.dtype)
        return
    acc = jnp.dot(h, w_ref[...], preferred_element_type=F32) + b_ref[...]
    if mode == "rope":
        @pl.when(j < n_rope)
        def _():
            cos = cos_ref[...]
            sin = sin_ref[...]
            for g in range(acc.shape[1] // LANES):
                sl = slice(g * LANES, (g + 1) * LANES)
                o_ref[:, sl] = _rope_tile(acc[:, sl], cos, sin).astype(o_ref.dtype)

        @pl.when(j >= n_rope)
        def _():
            o_ref[...] = acc.astype(o_ref.dtype)
    else:
        o_ref[...] = acc.astype(o_ref.dtype)


def nm_matmul(x, g, shift, scale, w, b, *, seg_rows, mode="plain", rope=None, n_rope_cols=0,
              tm=ROW_TILE, tn=None):
    m, d = x.shape
    n_seg = shift.shape[0]
    n = w.shape[-1]
    tn = tn or (_col_tile(math.gcd(n, n_rope_cols), 512) if mode == "rope" else _col_tile(n, 1024))
    seg = lambda i, j: (jnp.minimum(i * tm // seg_rows, n_seg - 1), 0, 0)
    in_specs = [pl.BlockSpec((tm, d), lambda i, j: (i, 0)),
                pl.BlockSpec((1, d), lambda i, j: (0, 0)),
                pl.BlockSpec((1, 1, d), seg),
                pl.BlockSpec((1, 1, d), seg)]
    if mode == "glu":
        in_specs += [pl.BlockSpec((2, d, tn), lambda i, j: (0, 0, j)),
                     pl.BlockSpec((2, 1, tn), lambda i, j: (0, 0, j))]
    else:
        in_specs += [pl.BlockSpec((d, tn), lambda i, j: (0, j)),
                     pl.BlockSpec((1, tn), lambda i, j: (0, j))]
    args = [x, g, shift, scale, w, b]
    if mode == "rope":
        in_specs += [pl.BlockSpec((tm, LANES), lambda i, j: (i, 0))] * 2
        args += list(rope)
    return pl.pallas_call(
        functools.partial(_nm_mm_kernel, mode=mode, n_rope=n_rope_cols // tn),
        grid=(m // tm, n // tn),
        in_specs=in_specs,
        out_specs=pl.BlockSpec((tm, tn), lambda i, j: (i, j)),
        out_shape=jax.ShapeDtypeStruct((m, n), BF16),
        scratch_shapes=[pltpu.VMEM((tm, d), BF16)],
        compiler_params=_cparams(2),
        name="nm_matmul_" + mode,
    )(*args)


def _mm_res_kernel(a_ref, w_ref, b_ref, res_ref, gate_ref, o_ref):
    acc = jnp.dot(a_ref[...], w_ref[...], preferred_element_type=F32) + b_ref[...]
    o_ref[...] = res_ref[...] + gate_ref[0] * acc


def mm_res(a, w, b, res, gate, *, seg_rows, tm=ROW_TILE, tn=None):
    m, k = a.shape
    n = w.shape[1]
    n_seg = gate.shape[0]
    tn = tn or _col_tile(n, 1024)
    seg = lambda i, j: (jnp.minimum(i * tm // seg_rows, n_seg - 1), 0, j)
    return pl.pallas_call(
        _mm_res_kernel,
        grid=(m // tm, n // tn),
        in_specs=[pl.BlockSpec((tm, k), lambda i, j: (i, 0)),
                  pl.BlockSpec((k, tn), lambda i, j: (0, j)),
                  pl.BlockSpec((1, tn), lambda i, j: (0, j)),
                  pl.BlockSpec((tm, tn), lambda i, j: (i, j)),
                  pl.BlockSpec((1, 1, tn), seg)],
        out_specs=pl.BlockSpec((tm, tn), lambda i, j: (i, j)),
        out_shape=jax.ShapeDtypeStruct((m, n), F32),
        compiler_params=_cparams(2),
        name="mm_res",
    )(a, w, b, res, gate)


CONV_HALO = 16


def _dwconv_kernel(prev_ref, x_ref, next_ref, w_ref, b_ref, *rest, taps, blocks_per_seq, post, sub):
    if post == "ln_silu":
        g_ref, beta_ref, o_ref, win_ref = rest
    else:
        o_ref, win_ref = rest
    i = pl.program_id(0)
    pos = i % blocks_per_seq
    rows, c = x_ref.shape
    half = taps // 2
    zero = jnp.zeros((CONV_HALO, c), F32)
    win_ref[CONV_HALO:CONV_HALO + rows, :] = x_ref[...].astype(F32)

    @pl.when(pos == 0)
    def _():
        win_ref[0:CONV_HALO, :] = zero

    @pl.when(pos > 0)
    def _():
        win_ref[0:CONV_HALO, :] = prev_ref[...].astype(F32)

    @pl.when(pos == blocks_per_seq - 1)
    def _():
        win_ref[CONV_HALO + rows:, :] = zero

    @pl.when(pos < blocks_per_seq - 1)
    def _():
        win_ref[CONV_HALO + rows:, :] = next_ref[...].astype(F32)

    bias = b_ref[...]
    for s in range(rows // sub):
        base = CONV_HALO + s * sub - half
        acc = jnp.broadcast_to(bias, (sub, c))
        for t in range(taps):
            acc = acc + w_ref[t] * win_ref[base + t:base + t + sub, :]
        if post == "ln_silu":
            mu = jnp.mean(acc, axis=-1, keepdims=True)
            xc = acc - mu
            var = jnp.mean(xc * xc, axis=-1, keepdims=True)
            y = xc * lax.rsqrt(var + NORM_EPS) * g_ref[...] + beta_ref[...]
            acc = y * jax.nn.sigmoid(y)
        o_ref[s * sub:(s + 1) * sub, :] = acc.astype(o_ref.dtype)


def dwconv(x, w, b, *, seq_lens, post=None, ln=None, rows=256, tc=None, out_dtype=None):
    m, c = x.shape
    taps = w.shape[0]
    tc = tc or c
    out_dtype = out_dtype or BF16
    sub = 8
    w = jnp.broadcast_to(w[:, None, :], (taps, sub, c))
    outs = []
    start = 0
    for total, seq_len in seq_lens:
        r = min(rows, seq_len)
        bps = seq_len // r
        nblk = total // r
        hb = r // CONV_HALO
        off = start // r
        offh = start // CONV_HALO
        nh = m // CONV_HALO
        in_specs = [pl.BlockSpec((CONV_HALO, tc), lambda i, j: (jnp.maximum(offh + i * hb - 1, 0), j)),
                    pl.BlockSpec((r, tc), lambda i, j: (off + i, j)),
                    pl.BlockSpec((CONV_HALO, tc), lambda i, j: (jnp.minimum(offh + (i + 1) * hb, nh - 1), j)),
                    pl.BlockSpec((taps, sub, tc), lambda i, j: (0, 0, j)),
                    pl.BlockSpec((1, tc), lambda i, j: (0, j))]
        args = [x, x, x, w, b]
        if post == "ln_silu":
            in_specs += [pl.BlockSpec((1, tc), lambda i, j: (0, j))] * 2
            args += list(ln)
        outs.append(pl.pallas_call(
            functools.partial(_dwconv_kernel, taps=taps, blocks_per_seq=bps, post=post, sub=sub),
            grid=(nblk, c // tc),
            in_specs=in_specs,
            out_specs=pl.BlockSpec((r, tc), lambda i, j: (i, j)),
            out_shape=jax.ShapeDtypeStruct((total, c), out_dtype),
            scratch_shapes=[pltpu.VMEM((r + 2 * CONV_HALO, tc), F32)],
            compiler_params=_cparams(2),
            name="dwconv%d" % taps,
        )(*args))
        start += total
    return jnp.concatenate(outs, axis=0) if len(outs) > 1 else outs[0]


def _masked_halves(q, lane_lo):
    zero = jnp.zeros_like(q)
    return jnp.where(lane_lo, q, zero), jnp.where(lane_lo, zero, q)


def _softmax_pv(s_parts, v_parts, sink):
    m = s_parts[0].max(axis=-1, keepdims=True)
    for s in s_parts[1:]:
        m = jnp.maximum(m, s.max(axis=-1, keepdims=True))
    if sink is not None:
        m = jnp.maximum(m, sink)
    denom = jnp.exp(sink - m) if sink is not None else 0.0
    o = None
    for s, v in zip(s_parts, v_parts):
        p = jnp.exp(s - m)
        denom = denom + p.sum(axis=-1, keepdims=True)
        pv = jnp.dot(p.astype(BF16), v, preferred_element_type=F32)
        o = pv if o is None else o + pv
    return o / denom


def _nt_dot(a, b):
    return lax.dot_general(a, b, (((1,), (1,)), ((), ())), preferred_element_type=F32)


def _wattn_kernel(sink_ref, q_ref, *rest, local, seq_len, blk):
    if local:
        k0, k1, k2, v0, v1, v2, kc_ref, vc_ref, o_ref = rest
    else:
        kc_ref, vc_ref, o_ref = rest
    p = pl.program_id(2)
    i = pl.program_id(1)
    lane_lo = lax.broadcasted_iota(jnp.int32, (blk, LANES), 1) < HEAD_DIM
    kc = kc_ref[...]
    vc = vc_ref[...]
    if local:
        kl = jnp.concatenate([k0[...], k1[...], k2[...]], axis=0)
        vl = jnp.concatenate([v0[...], v1[...], v2[...]], axis=0)
        qpos = i * blk + lax.broadcasted_iota(jnp.int32, (blk, 3 * blk), 0)
        kpos = (i - 1) * blk + lax.broadcasted_iota(jnp.int32, (blk, 3 * blk), 1)
        valid = (jnp.abs(kpos - qpos) <= WA_WINDOW) & (kpos >= 0) & (kpos < seq_len)
        valid = jnp.concatenate([valid] * WA_GROUP, axis=0)
    qa, qb = [], []
    for g in range(WA_GROUP):
        a, b = _masked_halves(q_ref[:, g * LANES:(g + 1) * LANES], lane_lo)
        qa.append(a)
        qb.append(b)
    outs = []
    for half, qs in enumerate((qa, qb)):
        qs = jnp.concatenate(qs, axis=0)
        sink = jnp.concatenate(
            [jnp.full((blk, 1), sink_ref[8 * p + 4 * half + g], F32) for g in range(WA_GROUP)], axis=0)
        s_parts, v_parts = [], []
        if local:
            s_parts.append(jnp.where(valid, _nt_dot(qs, kl), NEG_INF))
            v_parts.append(vl)
        s_parts.append(_nt_dot(qs, kc))
        v_parts.append(vc)
        outs.append(_softmax_pv(s_parts, v_parts, sink))
    for g in range(WA_GROUP):
        rows = slice(g * blk, (g + 1) * blk)
        o_ref[:, g * LANES:(g + 1) * LANES] = jnp.where(lane_lo, outs[0][rows], outs[1][rows]).astype(o_ref.dtype)


def window_attention(qkv, sinks, *, batch, seq_len, ctx_len, d, blk=128):
    n_pairs = d // (2 * WA_GROUP * HEAD_DIM)
    kcol = d // LANES
    vcol = kcol + n_pairs
    nblk = seq_len // blk
    cb0 = batch * seq_len // ctx_len
    qw = WA_GROUP * LANES
    kern = functools.partial(_wattn_kernel, seq_len=seq_len)
    smem = pl.BlockSpec(memory_space=pltpu.SMEM)

    def kspec(col0, shift):
        return pl.BlockSpec((blk, LANES),
                            lambda b, i, p: (b * nblk + jnp.clip(i + shift, 0, nblk - 1), col0 + p))

    ctx_k = pl.BlockSpec((ctx_len, LANES), lambda b, i, p: (cb0 + b, kcol + p))
    ctx_v = pl.BlockSpec((ctx_len, LANES), lambda b, i, p: (cb0 + b, vcol + p))
    lat = pl.pallas_call(
        functools.partial(kern, local=True, blk=blk),
        grid=(batch, nblk, n_pairs),
        in_specs=[smem, pl.BlockSpec((blk, qw), lambda b, i, p: (b * nblk + i, p))]
        + [kspec(kcol, s) for s in (-1, 0, 1)] + [kspec(vcol, s) for s in (-1, 0, 1)] + [ctx_k, ctx_v],
        out_specs=pl.BlockSpec((blk, qw), lambda b, i, p: (b * nblk + i, p)),
        out_shape=jax.ShapeDtypeStruct((batch * seq_len, d), BF16),
        compiler_params=_cparams(3),
        name="window_attn",
    )(sinks, qkv, *([qkv] * 8))
    ctx = pl.pallas_call(
        functools.partial(kern, local=False, blk=ctx_len),
        grid=(batch, 1, n_pairs),
        in_specs=[smem, pl.BlockSpec((ctx_len, qw), lambda b, i, p: (cb0 + b, p)), ctx_k, ctx_v],
        out_specs=pl.BlockSpec((ctx_len, qw), lambda b, i, p: (b, p)),
        out_shape=jax.ShapeDtypeStruct((batch * ctx_len, d), BF16),
        compiler_params=_cparams(3),
        name="ctx_attn",
    )(sinks, qkv, qkv, qkv)
    return jnp.concatenate([lat, ctx], axis=0)


def _nattn_kernel(q_ref, k0, k1, k2, v0, v1, v2, kc_ref, vc_ref, bias_ref, o_ref, ks_ref, vs_ref,
                  *, grid_rows, rows_per_blk):
    j = pl.program_id(2)
    blk = k0.shape[0]
    for t, (kr, vr) in enumerate(((k0, v0), (k1, v1), (k2, v2))):
        ks_ref[t * blk:(t + 1) * blk, :] = kr[...]
        vs_ref[t * blk:(t + 1) * blk, :] = vr[...]
    kc = kc_ref[...]
    vc = vc_ref[...]
    lane_lo = lax.broadcasted_iota(jnp.int32, (GRID_W, LANES), 1) < HEAD_DIM
    strip = NA_WIN_ROWS * GRID_W
    for r in range(rows_per_blk):
        row = j * rows_per_blk + r
        r0 = jnp.clip(row - NA_WIN_ROWS // 2, 0, grid_rows - NA_WIN_ROWS)
        start = pl.multiple_of((r0 - (j - 1) * rows_per_blk) * GRID_W, GRID_W)
        cls = row - r0
        qa, qb = _masked_halves(q_ref[r * GRID_W:(r + 1) * GRID_W, :], lane_lo)
        qs = jnp.concatenate([qa, qb], axis=0)
        kn = ks_ref[pl.ds(start, strip), :]
        vn = vs_ref[pl.ds(start, strip), :]
        s_nb = _nt_dot(qs, kn) + bias_ref[cls]
        s_cx = _nt_dot(qs, kc)
        o = _softmax_pv([s_nb, s_cx], [vn, vc], None)
        o_ref[r * GRID_W:(r + 1) * GRID_W, :] = jnp.where(lane_lo, o[:GRID_W], o[GRID_W:]).astype(o_ref.dtype)


def _na_bias_table(rpb):
    h = rpb.shape[0]
    cls = jnp.arange(NA_WIN_ROWS)
    drow = jnp.arange(NA_WIN_ROWS)[None, :] - cls[:, None] + NA_WIN_ROWS - 1
    cols = jnp.arange(GRID_W)
    col_start = jnp.clip(cols - NA_WIN_COLS // 2, 0, GRID_W - NA_WIN_COLS)
    inwin = (cols[None, :] >= col_start[:, None]) & (cols[None, :] < col_start[:, None] + NA_WIN_COLS)
    dcol = jnp.clip(cols[None, :] - cols[:, None] + NA_WIN_COLS - 1, 0, 2 * NA_WIN_COLS - 2)
    tab = rpb[:, drow[:, :, None, None], dcol[None, None]]
    tab = jnp.where(inwin[None, None, None], tab.astype(F32), NEG_INF)
    tab = tab.transpose(0, 1, 3, 2, 4).reshape(h, NA_WIN_ROWS, GRID_W, NA_WIN_ROWS * GRID_W)
    tab = tab.reshape(h // 2, 2, NA_WIN_ROWS, GRID_W, NA_WIN_ROWS * GRID_W).transpose(0, 2, 1, 3, 4)
    return tab.reshape(h // 2, NA_WIN_ROWS, 2 * GRID_W, NA_WIN_ROWS * GRID_W)


def neighbourhood_attention(qkv, rpb, *, batch, seq_len, ctx_len, d, with_ctx_out):
    n_pairs = d // LANES
    rows_per_blk = NA_WIN_ROWS
    blk = rows_per_blk * GRID_W
    grid_rows = seq_len // GRID_W
    nblk = seq_len // blk
    cb0 = batch * seq_len // ctx_len
    bias = _na_bias_table(rpb)

    def kspec(col0, shift):
        return pl.BlockSpec((blk, LANES),
                            lambda b, p, j: (b * nblk + jnp.clip(j + shift, 0, nblk - 1), col0 + p))

    ctx_k = pl.BlockSpec((ctx_len, LANES), lambda b, p, j: (cb0 + b, n_pairs + p))
    ctx_v = pl.BlockSpec((ctx_len, LANES), lambda b, p, j: (cb0 + b, 2 * n_pairs + p))
    lat = pl.pallas_call(
        functools.partial(_nattn_kernel, grid_rows=grid_rows, rows_per_blk=rows_per_blk),
        grid=(batch, n_pairs, nblk),
        in_specs=[pl.BlockSpec((blk, LANES), lambda b, p, j: (b * nblk + j, p))]
        + [kspec(n_pairs, s) for s in (-1, 0, 1)] + [kspec(2 * n_pairs, s) for s in (-1, 0, 1)]
        + [ctx_k, ctx_v,
           pl.BlockSpec((None, NA_WIN_ROWS, 2 * GRID_W, NA_WIN_ROWS * GRID_W), lambda b, p, j: (p, 0, 0, 0))],
        out_specs=pl.BlockSpec((blk, LANES), lambda b, p, j: (b * nblk + j, p)),
        out_shape=jax.ShapeDtypeStruct((batch * seq_len, d), BF16),
        scratch_shapes=[pltpu.VMEM((3 * blk, LANES), BF16), pltpu.VMEM((3 * blk, LANES), BF16)],
        compiler_params=_cparams(3),
        name="neighbourhood_attn",
    )(qkv, *([qkv] * 8), bias)
    if not with_ctx_out:
        return lat
    ctx = pl.pallas_call(
        _cattn_kernel,
        grid=(batch, n_pairs),
        in_specs=[pl.BlockSpec((ctx_len, LANES), lambda b, p: (cb0 + b, p)),
                  pl.BlockSpec((ctx_len, LANES), lambda b, p: (cb0 + b, n_pairs + p)),
                  pl.BlockSpec((ctx_len, LANES), lambda b, p: (cb0 + b, 2 * n_pairs + p))],
        out_specs=pl.BlockSpec((ctx_len, LANES), lambda b, p: (b, p)),
        out_shape=jax.ShapeDtypeStruct((batch * ctx_len, d), BF16),
        compiler_params=_cparams(2),
        name="ctx_mha",
    )(qkv, qkv, qkv)
    return jnp.concatenate([lat, ctx], axis=0)


def _cattn_kernel(q_ref, k_ref, v_ref, o_ref):
    rows = q_ref.shape[0]
    lane_lo = lax.broadcasted_iota(jnp.int32, (rows, LANES), 1) < HEAD_DIM
    qa, qb = _masked_halves(q_ref[...], lane_lo)
    qs = jnp.concatenate([qa, qb], axis=0)
    o = _softmax_pv([_nt_dot(qs, k_ref[...])], [v_ref[...]], None)
    o_ref[...] = jnp.where(lane_lo, o[:rows], o[rows:]).astype(o_ref.dtype)


def _router_kernel(x_ref, g_ref, sh_ref, sc_ref, wh_ref, wl_ref, b_ref, h_ref, idx_ref, gate_ref):
    h = _norm_mod(x_ref[...], g_ref[...], sh_ref[0], sc_ref[0])
    h_hi = h.astype(BF16)
    h_ref[...] = h_hi
    h_lo = (h - h_hi.astype(F32)).astype(BF16)
    logits = (jnp.dot(h_hi, wh_ref[...], preferred_element_type=F32)
              + jnp.dot(h_hi, wl_ref[...], preferred_element_type=F32)
              + jnp.dot(h_lo, wh_ref[...], preferred_element_type=F32)) + b_ref[...]
    lane = lax.broadcasted_iota(jnp.int32, logits.shape, 1)
    idx_out = jnp.zeros(logits.shape, jnp.int32)
    val_out = jnp.zeros(logits.shape, F32)
    top = None
    denom = 0.0
    for k in range(TOP_K):
        m = logits.max(axis=-1, keepdims=True)
        sel = jnp.min(jnp.where(logits == m, lane, LANES), axis=-1, keepdims=True)
        if top is None:
            top = m
        e = jnp.exp(m - top)
        denom = denom + e
        idx_out = jnp.where(lane == k, sel, idx_out)
        val_out = jnp.where(lane == k, e, val_out)
        logits = jnp.where(lane == sel, -jnp.inf, logits)
    idx_ref[...] = idx_out
    gate_ref[...] = val_out / denom


def router(x, g, shift, scale, w_router, b_router, *, seg_rows, tm=ROW_TILE):
    m, d = x.shape
    n_seg = shift.shape[0]
    n_e = w_router.shape[1]
    w_pad = jnp.zeros((d, LANES), F32).at[:, :n_e].set(w_router)
    w_hi = w_pad.astype(BF16)
    w_lo = (w_pad - w_hi.astype(F32)).astype(BF16)
    b_pad = jnp.full((1, LANES), -jnp.inf, F32).at[0, :n_e].set(b_router)
    seg = lambda i: (jnp.minimum(i * tm // seg_rows, n_seg - 1), 0, 0)
    return pl.pallas_call(
        _router_kernel,
        grid=(m // tm,),
        in_specs=[pl.BlockSpec((tm, d), lambda i: (i, 0)),
                  pl.BlockSpec((1, d), lambda i: (0, 0)),
                  pl.BlockSpec((1, 1, d), seg),
                  pl.BlockSpec((1, 1, d), seg),
                  pl.BlockSpec((d, LANES), lambda i: (0, 0)),
                  pl.BlockSpec((d, LANES), lambda i: (0, 0)),
                  pl.BlockSpec((1, LANES), lambda i: (0, 0))],
        out_specs=[pl.BlockSpec((tm, d), lambda i: (i, 0)),
                   pl.BlockSpec((tm, LANES), lambda i: (i, 0)),
                   pl.BlockSpec((tm, LANES), lambda i: (i, 0))],
        out_shape=[jax.ShapeDtypeStruct((m, d), BF16),
                   jax.ShapeDtypeStruct((m, LANES), jnp.int32),
                   jax.ShapeDtypeStruct((m, LANES), F32)],
        compiler_params=_cparams(1),
        name="router",
    )(x, g, shift, scale, w_hi, w_lo, b_pad)


def _expert_kernel(blk_e_ref, n_used_ref, x_ref, wgu_ref, bgu_ref, wd_ref, bd_ref, sw_ref, o_ref, *, f):
    i = pl.program_id(0)

    @pl.when(i < n_used_ref[0])
    def _():
        gu = jnp.dot(x_ref[...], wgu_ref[0], preferred_element_type=F32) + bgu_ref[0]
        g = jnp.minimum(gu[:, :f], SWIGLU_LIMIT)
        u = jnp.clip(gu[:, f:], -SWIGLU_LIMIT, SWIGLU_LIMIT)
        act = (u + 1) * (g * jax.nn.sigmoid(SWIGLU_ALPHA * g))
        y = jnp.dot(act.astype(BF16), wd_ref[0], preferred_element_type=F32) + bd_ref[0]
        o_ref[...] = (y * sw_ref[...]).astype(o_ref.dtype)

    @pl.when(i >= n_used_ref[0])
    def _():
        o_ref[...] = jnp.zeros_like(o_ref)


def expert_mlp(xs, slot_w, blk_e, n_used, w_gu, b_gu, w_down, b_down, *, bm=MOE_ROWS):
    n_slot, d = xs.shape
    n_e, _, f2 = w_gu.shape
    f = f2 // 2
    return pl.pallas_call(
        functools.partial(_expert_kernel, f=f),
        grid_spec=pltpu.PrefetchScalarGridSpec(
            num_scalar_prefetch=2,
            grid=(n_slot // bm,),
            in_specs=[pl.BlockSpec((bm, d), lambda i, be, nu: (i, 0)),
                      pl.BlockSpec((1, d, f2), lambda i, be, nu: (be[i], 0, 0)),
                      pl.BlockSpec((1, 1, f2), lambda i, be, nu: (be[i], 0, 0)),
                      pl.BlockSpec((1, f, d), lambda i, be, nu: (be[i], 0, 0)),
                      pl.BlockSpec((1, 1, d), lambda i, be, nu: (be[i], 0, 0)),
                      pl.BlockSpec((bm, 1), lambda i, be, nu: (i, 0))],
            out_specs=pl.BlockSpec((bm, d), lambda i, be, nu: (i, 0))),
        out_shape=jax.ShapeDtypeStruct((n_slot, d), BF16),
        compiler_params=_cparams(1),
        name="expert_mlp",
    )(blk_e, n_used, xs, w_gu, b_gu.reshape(n_e, 1, f2), w_down, b_down.reshape(n_e, 1, d), slot_w)


def _combine_kernel(res_ref, y_ref, gate_ref, o_ref):
    d = o_ref.shape[1]
    y = y_ref[:, 0:d].astype(F32)
    for k in range(1, TOP_K):
        y = y + y_ref[:, k * d:(k + 1) * d].astype(F32)
    o_ref[...] = res_ref[...] + gate_ref[0] * y


def moe_layer(x, g, shift, scale, gate, w_router, b_router, w_gu, b_gu, w_down, b_down, *, seg_rows,
              bm=MOE_ROWS):
    m, d = x.shape
    n_e = w_router.shape[1]
    h, idx, gates = router(x, g, shift, scale, w_router, b_router, seg_rows=seg_rows)
    flat_e = idx[:, :TOP_K].reshape(-1)
    flat_w = gates[:, :TOP_K].reshape(-1)
    n_asg = m * TOP_K
    onehot = (flat_e[:, None] == jnp.arange(n_e, dtype=jnp.int32)[None, :]).astype(jnp.int32)
    rank = jnp.take_along_axis(jnp.cumsum(onehot, axis=0), flat_e[:, None], axis=1)[:, 0] - 1
    counts = onehot.sum(axis=0)
    padded = (counts + bm - 1) // bm * bm
    pad_end = jnp.cumsum(padded)
    pad_start = pad_end - padded
    dest = (pad_start[flat_e] + rank).astype(jnp.int32)
    n_blk = -(-(n_asg + n_e * (bm - 1)) // bm)
    n_slot = n_blk * bm
    tok = (jnp.arange(n_asg, dtype=jnp.int32) // TOP_K)
    slot_tok = jnp.zeros((n_slot,), jnp.int32).at[dest].set(tok)
    slot_w = jnp.zeros((n_slot,), F32).at[dest].set(flat_w)
    blk_e = jnp.minimum(jnp.searchsorted(pad_end, jnp.arange(n_blk, dtype=jnp.int32) * bm, side='right'),
                        n_e - 1).astype(jnp.int32)
    n_used = (pad_end[-1] // bm).astype(jnp.int32).reshape(1)
    xs = jnp.take(h, slot_tok, axis=0)
    y_slot = expert_mlp(xs, slot_w.reshape(n_slot, 1), blk_e, n_used, w_gu, b_gu, w_down, b_down, bm=bm)
    y_tok = jnp.take(y_slot, dest, axis=0).reshape(m, TOP_K * d)
    tm = 256
    n_seg = gate.shape[0]
    seg = lambda i: (jnp.minimum(i * tm // seg_rows, n_seg - 1), 0, 0)
    return pl.pallas_call(
        _combine_kernel,
        grid=(m // tm,),
        in_specs=[pl.BlockSpec((tm, d), lambda i: (i, 0)),
                  pl.BlockSpec((tm, TOP_K * d), lambda i: (i, 0)),
                  pl.BlockSpec((1, 1, d), seg)],
        out_specs=pl.BlockSpec((tm, d), lambda i: (i, 0)),
        out_shape=jax.ShapeDtypeStruct((m, d), F32),
        compiler_params=_cparams(1),
        name="moe_combine",
    )(x, y_tok, gate)


def _hyena_filters(n, d, f_w1, f_b1, f_w2, f_b2, f_w3, f_b3, f_freq, f_w4):
    hp = lax.Precision.HIGHEST
    t = jnp.linspace(0.0, 1.0, n, dtype=F32)[:, None]
    ang = (2.0 * math.pi / n) * jnp.arange(n, dtype=F32)[:, None]
    bands = jnp.linspace(1e-4, HY_BANDS - 1, HY_BANDS, dtype=F32)[None, :]
    emb = jnp.concatenate([t, jnp.cos(bands * ang), -jnp.sin(bands * ang)], axis=-1)
    a = jnp.sin(f_freq * (jnp.dot(emb, f_w1, precision=hp) + f_b1))
    a = jnp.sin(f_freq * (jnp.dot(a, f_w2, precision=hp) + f_b2))
    a = jnp.sin(f_freq * (jnp.dot(a, f_w3, precision=hp) + f_b3))
    h = jnp.dot(a, f_w4, precision=hp).reshape(n, HY_ORDER, 2, d)
    deltas = jnp.abs(jnp.linspace(math.log(HY_DECAY_TARGET) / HY_FAST_DECAY,
                                  math.log(HY_DECAY_TARGET) / HY_SLOW_DECAY, d, dtype=F32))
    decay = jnp.exp(-t * deltas[None, :])
    return h * decay[:, None, None, :]


def _long_conv(u, h_fwd, h_bwd, skip):
    n = u.shape[1]
    taps = jnp.concatenate([h_fwd, jnp.zeros((1, h_fwd.shape[1]), F32), h_bwd[:0:-1]], axis=0)
    spec = jnp.fft.rfft(u, n=2 * n, axis=1) * jnp.fft.rfft(taps, n=2 * n, axis=0)[None]
    y = jnp.fft.irfft(spec, n=2 * n, axis=1)[:, :n]
    return y + u * skip


def _hyena_seq(z, filt, skip):
    x1, x2, v = jnp.split(z, 3, axis=-1)
    y = x1 * _long_conv(v, filt[:, 0, 0], filt[:, 0, 1], skip[0])
    y = x2 * _long_conv(y, filt[:, 1, 0], filt[:, 1, 1], skip[1])
    return y


def _final_norm_kernel(x_ref, g_ref, o_ref):
    x = x_ref[...]
    o_ref[...] = x * lax.rsqrt(jnp.mean(x * x, axis=-1, keepdims=True) + NORM_EPS) * g_ref[...]


def final_norm(x, g, *, rows, tm=ROW_TILE):
    d = x.shape[1]
    return pl.pallas_call(
        _final_norm_kernel,
        grid=(rows // tm,),
        in_specs=[pl.BlockSpec((tm, d), lambda i: (i, 0)), pl.BlockSpec((1, d), lambda i: (0, 0))],
        out_specs=pl.BlockSpec((tm, d), lambda i: (i, 0)),
        out_shape=jax.ShapeDtypeStruct((rows, d), F32),
        compiler_params=_cparams(1),
        name="final_norm",
    )(x, g)


def _rope_tables(n_lat, seq_len, n_rows):
    t = jnp.arange(n_rows, dtype=jnp.int32)
    row = ((t % seq_len) // GRID_W).astype(F32)
    col = (t % GRID_W).astype(F32)
    lane = jnp.arange(LANES)
    dd = lane % HEAD_DIM
    quarter = HEAD_DIM // 4
    inv_freq = ROPE_BASE ** (-(dd % quarter).astype(F32) / quarter)
    pos = jnp.where(dd[None, :] < HEAD_DIM // 2, row[:, None], col[:, None])
    ang = pos * inv_freq[None, :]
    sign = jnp.where((dd % (HEAD_DIM // 2)) < quarter, -1.0, 1.0).astype(F32)
    lat = (t < n_lat)[:, None]
    cos = jnp.where(lat, jnp.cos(ang), 1.0)
    sin = jnp.where(lat, jnp.sin(ang) * sign[None, :], 0.0)
    return cos, sin


def _wa_head_perm(n_heads):
    order = []
    for p in range(n_heads // (2 * WA_GROUP)):
        for g in range(WA_GROUP):
            order += [2 * WA_GROUP * p + g, 2 * WA_GROUP * p + WA_GROUP + g]
    cols = jnp.asarray(order, jnp.int32)[:, None] * HEAD_DIM + jnp.arange(HEAD_DIM, dtype=jnp.int32)[None, :]
    return cols.reshape(-1)


def kernel(x, c, ctx, c_ctx, w_ada, b_ada, g_mix, g_ffn, hy_w_in, hy_b_in, hy_w_short, hy_b_short, hy_f_w1, hy_f_b1, hy_f_w2, hy_f_b2, hy_f_w3, hy_f_b3, hy_f_freq, hy_f_w4, hy_skip, hy_w_out, hy_b_out, cf_w_pw1, cf_b_pw1, cf_w_dw, cf_b_dw, cf_ln_g, cf_ln_b, cf_w_pw2, cf_b_pw2, wa_w_qkv, wa_w_o, wa_sinks, na_w_qkv, na_w_o, na_rpb, moe_w_router, moe_b_router, moe_w_gu, moe_b_gu, moe_w_down, moe_b_down, g_final):
    bsz, n, d = x.shape
    n_ctx = ctx.shape[1]
    depth = w_ada.shape[0]
    n_lat = bsz * n
    m_all = n_lat + bsz * n_ctx
    seq_lens = ((n_lat, n), (bsz * n_ctx, n_ctx))

    cond = jnp.zeros((8, d), F32).at[:bsz].set(jax.nn.silu(c)).at[bsz].set(jax.nn.silu(c_ctx))
    mods = adaln(cond, w_ada, b_ada)
    xu = jnp.concatenate([x.reshape(n_lat, d), ctx.reshape(bsz * n_ctx, d)], axis=0)
    zeros_d = jnp.zeros((1, d), F32)

    for i in range(depth):
        kind, j = i % N_MIXERS, i // N_MIXERS
        last = i == depth - 1
        mod = [mods[i, :bsz + 1, k * d:(k + 1) * d].reshape(bsz + 1, 1, d) for k in range(6)]
        m_out = n_lat if last else m_all
        gm = g_mix[i].reshape(1, d)
        need_ctx_in = (not last) or kind >= 2
        m_in = m_all if need_ctx_in else n_lat
        x_in = xu[:m_in]

        if kind == 0:
            z = nm_matmul(x_in, gm, mod[0], mod[1], hy_w_in[j].astype(BF16), hy_b_in[j].reshape(1, -1),
                          seg_rows=n)
            zc = dwconv(z, hy_w_short[j], hy_b_short[j].reshape(1, -1), seq_lens=seq_lens[:1 if m_in == n_lat else 2],
                        tc=512, out_dtype=F32)
            parts = []
            for lo, cnt, ln_ in ((0, n_lat, n), (n_lat, m_in - n_lat, n_ctx)):
                if cnt == 0:
                    continue
                filt = _hyena_filters(ln_, d, hy_f_w1[j], hy_f_b1[j], hy_f_w2[j], hy_f_b2[j], hy_f_w3[j],
                                      hy_f_b3[j], hy_f_freq[j], hy_f_w4[j])
                y = _hyena_seq(zc[lo:lo + cnt].reshape(bsz, ln_, 3 * d), filt, hy_skip[j])
                parts.append(y.reshape(cnt, d))
            y = jnp.concatenate(parts, axis=0).astype(BF16)
            w_o, b_o = hy_w_out[j].astype(BF16), hy_b_out[j].reshape(1, d)
        elif kind == 1:
            w1 = cf_w_pw1[j].astype(BF16).reshape(d, 2, d).transpose(1, 0, 2)
            a = nm_matmul(x_in, gm, mod[0], mod[1], w1, cf_b_pw1[j].reshape(2, 1, d), seg_rows=n, mode="glu")
            y = dwconv(a, cf_w_dw[j], cf_b_dw[j].reshape(1, d), seq_lens=seq_lens[:1 if m_in == n_lat else 2],
                       post="ln_silu", ln=(cf_ln_g[j].reshape(1, d), cf_ln_b[j].reshape(1, d)), rows=64)
            w_o, b_o = cf_w_pw2[j].astype(BF16), cf_b_pw2[j].reshape(1, d)
        elif kind == 2:
            perm = _wa_head_perm(d // HEAD_DIM)
            scale = HEAD_DIM ** -0.5
            w_qkv = jnp.concatenate([wa_w_qkv[j][:, :d][:, perm] * scale, wa_w_qkv[j][:, d:]], axis=1).astype(BF16)
            n_out = w_qkv.shape[1]
            qkv = nm_matmul(x_in, gm, mod[0], mod[1], w_qkv, jnp.zeros((1, n_out), F32), seg_rows=n,
                            mode="rope", rope=_rope_tables(n_lat, n, m_in), n_rope_cols=d + (n_out - d) // 2)
            y = window_attention(qkv, wa_sinks[j], batch=bsz, seq_len=n, ctx_len=n_ctx, d=d)
            w_o, b_o = wa_w_o[j][perm].astype(BF16), zeros_d
        else:
            scale = HEAD_DIM ** -0.5
            w_qkv = jnp.concatenate([na_w_qkv[j][:, :d] * scale, na_w_qkv[j][:, d:]], axis=1).astype(BF16)
            qkv = nm_matmul(x_in, gm, mod[0], mod[1], w_qkv, jnp.zeros((1, 3 * d), F32), seg_rows=n)
            y = neighbourhood_attention(qkv, na_rpb[j], batch=bsz, seq_len=n, ctx_len=n_ctx, d=d,
                                        with_ctx_out=not last)
            w_o, b_o = na_w_o[j].astype(BF16), zeros_d
        xu = mm_res(y[:m_out], w_o, b_o, xu[:m_out], mod[2], seg_rows=n)

        xu = moe_layer(xu, g_ffn[i].reshape(1, d), mod[3], mod[4], mod[5], moe_w_router[i], moe_b_router[i],
                       moe_w_gu[i].astype(BF16), moe_b_gu[i], moe_w_down[i].astype(BF16), moe_b_down[i],
                       seg_rows=n)
    return final_norm(xu, g_final.reshape(1, d), rows=n_lat).reshape(bsz, n, d)
```

```python
import functools
import math

import jax
import jax.numpy as jnp
import numpy as np
from jax import lax
from jax.experimental import pallas as pl
from jax.experimental.pallas import tpu as pltpu

F32 = jnp.float32
BF16 = jnp.bfloat16

GRID_W = 64
N_MIXERS = 4
NORM_EPS = 1e-6
NEG_INF = -1e30
HEAD_DIM = 64
ROPE_BASE = 10000.0

HY_ORDER = 2
HY_BANDS = 16
HY_DECAY_TARGET = 1e-2
HY_FAST_DECAY = 0.3
HY_SLOW_DECAY = 1.5

WA_GROUP = 4
WA_WINDOW = 128
NA_WIN_ROWS = 8
NA_WIN_COLS = 16

TOP_K = 4
SWIGLU_LIMIT = 7.0
SWIGLU_ALPHA = 1.702

LANES = 128
ROW_TILE = 512
MOE_ROWS = 256
VMEM_LIMIT = 56 * 1024 * 1024


def _cparams(n_axes):
    return pltpu.CompilerParams(dimension_semantics=("arbitrary",) * n_axes,
                                vmem_limit_bytes=VMEM_LIMIT)


def _col_tile(n, pref=1024):
    t = min(pref, n)
    while n % t:
        t //= 2
    return t


def _adaln_kernel(c_ref, w_ref, b_ref, o_ref):
    w = w_ref[0]
    w_hi = w.astype(BF16)
    w_lo = (w - w_hi.astype(F32)).astype(BF16)
    c = c_ref[...]
    c_hi = c.astype(BF16)
    c_lo = (c - c_hi.astype(F32)).astype(BF16)
    acc = jnp.dot(c_hi, w_hi, preferred_element_type=F32)
    acc += jnp.dot(c_hi, w_lo, preferred_element_type=F32)
    acc += jnp.dot(c_lo, w_hi, preferred_element_type=F32)
    o_ref[0] = acc + b_ref[0]


def adaln(cond, w_ada, b_ada):
    depth, d, n6 = w_ada.shape
    tn = _col_tile(n6, 1024)
    return pl.pallas_call(
        _adaln_kernel,
        grid=(depth, n6 // tn),
        in_specs=[pl.BlockSpec((8, d), lambda l, j: (0, 0)),
                  pl.BlockSpec((1, d, tn), lambda l, j: (l, 0, j)),
                  pl.BlockSpec((1, 1, tn), lambda l, j: (l, 0, j))],
        out_specs=pl.BlockSpec((1, 8, tn), lambda l, j: (l, 0, j)),
        out_shape=jax.ShapeDtypeStruct((depth, 8, n6), F32),
        compiler_params=_cparams(2),
        name="adaln",
    )(cond, w_ada, b_ada.reshape(depth, 1, n6))


def _norm_mod(x, g, shift, scale):
    y = x * lax.rsqrt(jnp.mean(x * x, axis=-1, keepdims=True) + NORM_EPS)
    return (y * g) * (1 + scale) + shift


def _rope_tile(x, cos, sin):
    lane = lax.broadcasted_iota(jnp.int32, x.shape, 1)
    nxt = pltpu.roll(x, LANES - 16, axis=1)
    prv = pltpu.roll(x, 16, axis=1)
    partner = jnp.where((lane // 16) % 2 == 0, nxt, prv)
    return x * cos + partner * sin


def _nm_mm_kernel(x_ref, g_ref, sh_ref, sc_ref, w_ref, b_ref, *rest, mode, n_rope):
    if mode == "rope":
        cos_ref, sin_ref, o_ref, h_ref = rest
    else:
        o_ref, h_ref = rest
    j = pl.program_id(1)

    @pl.when(j == 0)
    def _():
        h_ref[...] = _norm_mod(x_ref[...], g_ref[...], sh_ref[0], sc_ref[0]).astype(BF16)

    h = h_ref[...]
    if mode == "glu":
        a = jnp.dot(h, w_ref[0], preferred_element_type=F32) + b_ref[0]
        gate = jnp.dot(h, w_ref[1], preferred_element_type=F32) + b_ref[1]
        o_ref[...] = (a * jax.nn.sigmoid(gate)).astype(o_ref.dtype)
        return
    acc = jnp.dot(h, w_ref[...], preferred_element_type=F32) + b_ref[...]
    if mode == "rope":
        @pl.when(j < n_rope)
        def _():
            cos = cos_ref[...]
            sin = sin_ref[...]
            for g in range(acc.shape[1] // LANES):
                sl = slice(g * LANES, (g + 1) * LANES)
                o_ref[:, sl] = _rope_tile(acc[:, sl], cos, sin).astype(o_ref.dtype)

        @pl.when(j >= n_rope)
        def _():
            o_ref[...] = acc.astype(o_ref.dtype)
    else:
        o_ref[...] = acc.astype(o_ref.dtype)


def nm_matmul(x, g, shift, scale, w, b, *, seg_rows, mode="plain", rope=None, n_rope_cols=0,
              tm=ROW_TILE, tn=None):
    m, d = x.shape
    n_seg = shift.shape[0]
    n = w.shape[-1]
    tn = tn or (_col_tile(math.gcd(n, n_rope_cols), 512) if mode == "rope" else _col_tile(n, 1024))
    seg = lambda i, j: (jnp.minimum(i * tm // seg_rows, n_seg - 1), 0, 0)
    in_specs = [pl.BlockSpec((tm, d), lambda i, j: (i, 0)),
                pl.BlockSpec((1, d), lambda i, j: (0, 0)),
                pl.BlockSpec((1, 1, d), seg),
                pl.BlockSpec((1, 1, d), seg)]
    if mode == "glu":
        in_specs += [pl.BlockSpec((2, d, tn), lambda i, j: (0, 0, j)),
                     pl.BlockSpec((2, 1, tn), lambda i, j: (0, 0, j))]
    else:
        in_specs += [pl.BlockSpec((d, tn), lambda i, j: (0, j)),
                     pl.BlockSpec((1, tn), lambda i, j: (0, j))]
    args = [x, g, shift, scale, w, b]
    if mode == "rope":
        in_specs += [pl.BlockSpec((tm, LANES), lambda i, j: (i, 0))] * 2
        args += list(rope)
    return pl.pallas_call(
        functools.partial(_nm_mm_kernel, mode=mode, n_rope=n_rope_cols // tn),
        grid=(m // tm, n // tn),
        in_specs=in_specs,
        out_specs=pl.BlockSpec((tm, tn), lambda i, j: (i, j)),
        out_shape=jax.ShapeDtypeStruct((m, n), BF16),
        scratch_shapes=[pltpu.VMEM((tm, d), BF16)],
        compiler_params=_cparams(2),
        name="nm_matmul_" + mode,
    )(*args)


def _mm_res_kernel(a_ref, w_ref, b_ref, res_ref, gate_ref, o_ref):
    if len(a_ref.shape) == 3:
        kc = a_ref.shape[2]
        acc = b_ref[...]
        for cb in range(a_ref.shape[0]):
            acc = acc + jnp.dot(a_ref[cb], w_ref[cb * kc:(cb + 1) * kc, :], preferred_element_type=F32)
    else:
        acc = jnp.dot(a_ref[...], w_ref[...], preferred_element_type=F32) + b_ref[...]
    o_ref[...] = res_ref[...] + gate_ref[0] * acc


def mm_res(a, w, b, res, gate, *, seg_rows, tm=ROW_TILE, tn=None):
    if a.ndim == 3:
        m = a.shape[1]
        k = a.shape[0] * a.shape[2]
        a_spec = pl.BlockSpec((a.shape[0], tm, a.shape[2]), lambda i, j: (0, i, 0))
    else:
        m, k = a.shape
        a_spec = pl.BlockSpec((tm, k), lambda i, j: (i, 0))
    n = w.shape[1]
    n_seg = gate.shape[0]
    tn = tn or _col_tile(n, 1024)
    seg = lambda i, j: (jnp.minimum(i * tm // seg_rows, n_seg - 1), 0, j)
    return pl.pallas_call(
        _mm_res_kernel,
        grid=(m // tm, n // tn),
        in_specs=[a_spec,
                  pl.BlockSpec((k, tn), lambda i, j: (0, j)),
                  pl.BlockSpec((1, tn), lambda i, j: (0, j)),
                  pl.BlockSpec((tm, tn), lambda i, j: (i, j)),
                  pl.BlockSpec((1, 1, tn), seg)],
        out_specs=pl.BlockSpec((tm, tn), lambda i, j: (i, j)),
        out_shape=jax.ShapeDtypeStruct((m, n), F32),
        compiler_params=_cparams(2),
        name="mm_res",
    )(a, w, b, res, gate)


CONV_HALO = 16


def _dwconv_kernel(prev_ref, x_ref, next_ref, w_ref, b_ref, *rest, taps, blocks_per_seq, post, sub):
    if post == "ln_silu":
        g_ref, beta_ref, o_ref, win_ref = rest
    else:
        o_ref, win_ref = rest
    i = pl.program_id(0)
    pos = i % blocks_per_seq
    rows, c = x_ref.shape
    half = taps // 2
    zero = jnp.zeros((CONV_HALO, c), F32)
    win_ref[CONV_HALO:CONV_HALO + rows, :] = x_ref[...].astype(F32)

    @pl.when(pos == 0)
    def _():
        win_ref[0:CONV_HALO, :] = zero

    @pl.when(pos > 0)
    def _():
        win_ref[0:CONV_HALO, :] = prev_ref[...].astype(F32)

    @pl.when(pos == blocks_per_seq - 1)
    def _():
        win_ref[CONV_HALO + rows:, :] = zero

    @pl.when(pos < blocks_per_seq - 1)
    def _():
        win_ref[CONV_HALO + rows:, :] = next_ref[...].astype(F32)

    bias = b_ref[...]
    for s in range(rows // sub):
        base = CONV_HALO + s * sub - half
        acc = jnp.broadcast_to(bias, (sub, c))
        for t in range(taps):
            acc = acc + w_ref[t] * win_ref[base + t:base + t + sub, :]
        if post == "ln_silu":
            mu = jnp.mean(acc, axis=-1, keepdims=True)
            xc = acc - mu
            var = jnp.mean(xc * xc, axis=-1, keepdims=True)
            y = xc * lax.rsqrt(var + NORM_EPS) * g_ref[...] + beta_ref[...]
            acc = y * jax.nn.sigmoid(y)
        o_ref[s * sub:(s + 1) * sub, :] = acc.astype(o_ref.dtype)


def dwconv(x, w, b, *, start, total, seq_len, post=None, ln=None, rows=256, tc=None, out_dtype=None,
           blocked=False):
    m, c = x.shape
    taps = w.shape[0]
    tc = tc or c
    out_dtype = out_dtype or BF16
    sub = 8
    w = jnp.broadcast_to(w[:, None, :], (taps, sub, c))
    r = min(rows, seq_len)
    bps = seq_len // r
    hb = r // CONV_HALO
    off = start // r
    offh = start // CONV_HALO
    nh = m // CONV_HALO
    in_specs = [pl.BlockSpec((CONV_HALO, tc), lambda i, j: (jnp.maximum(offh + i * hb - 1, 0), j)),
                pl.BlockSpec((r, tc), lambda i, j: (off + i, j)),
                pl.BlockSpec((CONV_HALO, tc), lambda i, j: (jnp.minimum(offh + (i + 1) * hb, nh - 1), j)),
                pl.BlockSpec((taps, sub, tc), lambda i, j: (0, 0, j)),
                pl.BlockSpec((1, tc), lambda i, j: (0, j))]
    args = [x, x, x, w, b]
    if post == "ln_silu":
        in_specs += [pl.BlockSpec((1, tc), lambda i, j: (0, j))] * 2
        args += list(ln)
    if blocked:
        out_spec = pl.BlockSpec((None, r, tc), lambda i, j: (j, i, 0))
        out_shape = jax.ShapeDtypeStruct((c // tc, total, tc), out_dtype)
    else:
        out_spec = pl.BlockSpec((r, tc), lambda i, j: (i, j))
        out_shape = jax.ShapeDtypeStruct((total, c), out_dtype)
    return pl.pallas_call(
        functools.partial(_dwconv_kernel, taps=taps, blocks_per_seq=bps, post=post, sub=sub),
        grid=(total // r, c // tc),
        in_specs=in_specs,
        out_specs=out_spec,
        out_shape=out_shape,
        scratch_shapes=[pltpu.VMEM((r + 2 * CONV_HALO, tc), F32)],
        compiler_params=_cparams(2),
        name="dwconv%d" % taps,
    )(*args)


def _masked_halves(q, lane_lo):
    zero = jnp.zeros_like(q)
    return jnp.where(lane_lo, q, zero), jnp.where(lane_lo, zero, q)


def _softmax_pv(s_parts, v_parts, sink):
    m = s_parts[0].max(axis=-1, keepdims=True)
    for s in s_parts[1:]:
        m = jnp.maximum(m, s.max(axis=-1, keepdims=True))
    if sink is not None:
        m = jnp.maximum(m, sink)
    denom = jnp.exp(sink - m) if sink is not None else 0.0
    o = None
    for s, v in zip(s_parts, v_parts):
        p = jnp.exp(s - m)
        denom = denom + p.sum(axis=-1, keepdims=True)
        pv = jnp.dot(p.astype(BF16), v, preferred_element_type=F32)
        o = pv if o is None else o + pv
    return o / denom


def _nt_dot(a, b):
    return lax.dot_general(a, b, (((1,), (1,)), ((), ())), preferred_element_type=F32)


def _wattn_kernel(sink_ref, q_ref, *rest, local, seq_len, blk):
    if local:
        k0, k1, k2, v0, v1, v2, kc_ref, vc_ref, o_ref = rest
    else:
        kc_ref, vc_ref, o_ref = rest
    p = pl.program_id(2)
    i = pl.program_id(1)
    lane_lo = lax.broadcasted_iota(jnp.int32, (blk, LANES), 1) < HEAD_DIM
    kc = kc_ref[...]
    vc = vc_ref[...]
    if local:
        kl = jnp.concatenate([k0[...], k1[...], k2[...]], axis=0)
        vl = jnp.concatenate([v0[...], v1[...], v2[...]], axis=0)
        qpos = i * blk + lax.broadcasted_iota(jnp.int32, (blk, 3 * blk), 0)
        kpos = (i - 1) * blk + lax.broadcasted_iota(jnp.int32, (blk, 3 * blk), 1)
        valid = (jnp.abs(kpos - qpos) <= WA_WINDOW) & (kpos >= 0) & (kpos < seq_len)
        valid = jnp.concatenate([valid] * WA_GROUP, axis=0)
    qa, qb = [], []
    for g in range(WA_GROUP):
        a, b = _masked_halves(q_ref[:, g * LANES:(g + 1) * LANES], lane_lo)
        qa.append(a)
        qb.append(b)
    outs = []
    for half, qs in enumerate((qa, qb)):
        qs = jnp.concatenate(qs, axis=0)
        sink = jnp.concatenate(
            [jnp.full((blk, 1), sink_ref[8 * p + 4 * half + g], F32) for g in range(WA_GROUP)], axis=0)
        s_parts, v_parts = [], []
        if local:
            s_parts.append(jnp.where(valid, _nt_dot(qs, kl), NEG_INF))
            v_parts.append(vl)
        s_parts.append(_nt_dot(qs, kc))
        v_parts.append(vc)
        outs.append(_softmax_pv(s_parts, v_parts, sink))
    for g in range(WA_GROUP):
        rows = slice(g * blk, (g + 1) * blk)
        o_ref[:, g * LANES:(g + 1) * LANES] = jnp.where(lane_lo, outs[0][rows], outs[1][rows]).astype(o_ref.dtype)


def window_attention(qkv, sinks, *, batch, seq_len, ctx_len, d, blk=128):
    n_pairs = d // (2 * WA_GROUP * HEAD_DIM)
    kcol = d // LANES
    vcol = kcol + n_pairs
    nblk = seq_len // blk
    cb0 = batch * seq_len // ctx_len
    qw = WA_GROUP * LANES
    kern = functools.partial(_wattn_kernel, seq_len=seq_len)
    smem = pl.BlockSpec(memory_space=pltpu.SMEM)

    def kspec(col0, shift):
        return pl.BlockSpec((blk, LANES),
                            lambda b, i, p: (b * nblk + jnp.clip(i + shift, 0, nblk - 1), col0 + p))

    ctx_k = pl.BlockSpec((ctx_len, LANES), lambda b, i, p: (cb0 + b, kcol + p))
    ctx_v = pl.BlockSpec((ctx_len, LANES), lambda b, i, p: (cb0 + b, vcol + p))
    lat = pl.pallas_call(
        functools.partial(kern, local=True, blk=blk),
        grid=(batch, nblk, n_pairs),
        in_specs=[smem, pl.BlockSpec((blk, qw), lambda b, i, p: (b * nblk + i, p))]
        + [kspec(kcol, s) for s in (-1, 0, 1)] + [kspec(vcol, s) for s in (-1, 0, 1)] + [ctx_k, ctx_v],
        out_specs=pl.BlockSpec((blk, qw), lambda b, i, p: (b * nblk + i, p)),
        out_shape=jax.ShapeDtypeStruct((batch * seq_len, d), BF16),
        compiler_params=_cparams(3),
        name="window_attn",
    )(sinks, qkv, *([qkv] * 8))
    ctx = pl.pallas_call(
        functools.partial(kern, local=False, blk=ctx_len),
        grid=(batch, 1, n_pairs),
        in_specs=[smem, pl.BlockSpec((ctx_len, qw), lambda b, i, p: (cb0 + b, p)), ctx_k, ctx_v],
        out_specs=pl.BlockSpec((ctx_len, qw), lambda b, i, p: (b, p)),
        out_shape=jax.ShapeDtypeStruct((batch * ctx_len, d), BF16),
        compiler_params=_cparams(3),
        name="ctx_attn",
    )(sinks, qkv, qkv, qkv)
    return jnp.concatenate([lat, ctx], axis=0)


def _nattn_kernel(q_ref, k0, k1, k2, v0, v1, v2, kc_ref, vc_ref, bias_ref, o_ref, ks_ref, vs_ref,
                  *, grid_rows, rows_per_blk):
    j = pl.program_id(2)
    blk = k0.shape[0]
    for t, (kr, vr) in enumerate(((k0, v0), (k1, v1), (k2, v2))):
        ks_ref[t * blk:(t + 1) * blk, :] = kr[...]
        vs_ref[t * blk:(t + 1) * blk, :] = vr[...]
    kc = kc_ref[...]
    vc = vc_ref[...]
    lane_lo = lax.broadcasted_iota(jnp.int32, (GRID_W, LANES), 1) < HEAD_DIM
    strip = NA_WIN_ROWS * GRID_W
    for r in range(rows_per_blk):
        row = j * rows_per_blk + r
        r0 = jnp.clip(row - NA_WIN_ROWS // 2, 0, grid_rows - NA_WIN_ROWS)
        start = pl.multiple_of((r0 - (j - 1) * rows_per_blk) * GRID_W, GRID_W)
        cls = row - r0
        qa, qb = _masked_halves(q_ref[r * GRID_W:(r + 1) * GRID_W, :], lane_lo)
        qs = jnp.concatenate([qa, qb], axis=0)
        kn = ks_ref[pl.ds(start, strip), :]
        vn = vs_ref[pl.ds(start, strip), :]
        s_nb = _nt_dot(qs, kn) + bias_ref[cls]
        s_cx = _nt_dot(qs, kc)
        o = _softmax_pv([s_nb, s_cx], [vn, vc], None)
        o_ref[r * GRID_W:(r + 1) * GRID_W, :] = jnp.where(lane_lo, o[:GRID_W], o[GRID_W:]).astype(o_ref.dtype)


def _na_bias_table(rpb):
    h = rpb.shape[0]
    n_dcol = 2 * NA_WIN_COLS - 1
    cols = jnp.arange(GRID_W)
    col_start = jnp.clip(cols - NA_WIN_COLS // 2, 0, GRID_W - NA_WIN_COLS)
    inwin = (cols[None, :] >= col_start[:, None]) & (cols[None, :] < col_start[:, None] + NA_WIN_COLS)
    dcol = cols[None, :] - cols[:, None] + NA_WIN_COLS - 1
    pick = (dcol[None] == jnp.arange(n_dcol)[:, None, None]).astype(F32).reshape(n_dcol, -1)
    spread = jnp.dot(rpb.astype(F32).reshape(-1, n_dcol), pick, precision=lax.Precision.HIGHEST)
    spread = spread.reshape(h, 2 * NA_WIN_ROWS - 1, GRID_W, GRID_W)
    spread = jnp.where(inwin[None, None], spread, NEG_INF)
    tab = jnp.stack([spread[:, NA_WIN_ROWS - 1 - cls:2 * NA_WIN_ROWS - 1 - cls]
                     for cls in range(NA_WIN_ROWS)], axis=1)
    tab = tab.transpose(0, 1, 3, 2, 4).reshape(h, NA_WIN_ROWS, GRID_W, NA_WIN_ROWS * GRID_W)
    tab = tab.reshape(h // 2, 2, NA_WIN_ROWS, GRID_W, NA_WIN_ROWS * GRID_W).transpose(0, 2, 1, 3, 4)
    return tab.reshape(h // 2, NA_WIN_ROWS, 2 * GRID_W, NA_WIN_ROWS * GRID_W)


def neighbourhood_attention(qkv, rpb, *, batch, seq_len, ctx_len, d, with_ctx_out):
    n_pairs = d // LANES
    rows_per_blk = NA_WIN_ROWS
    blk = rows_per_blk * GRID_W
    grid_rows = seq_len // GRID_W
    nblk = seq_len // blk
    cb0 = batch * seq_len // ctx_len
    bias = _na_bias_table(rpb)

    def kspec(col0, shift):
        return pl.BlockSpec((blk, LANES),
                            lambda b, p, j: (b * nblk + jnp.clip(j + shift, 0, nblk - 1), col0 + p))

    ctx_k = pl.BlockSpec((ctx_len, LANES), lambda b, p, j: (cb0 + b, n_pairs + p))
    ctx_v = pl.BlockSpec((ctx_len, LANES), lambda b, p, j: (cb0 + b, 2 * n_pairs + p))
    lat = pl.pallas_call(
        functools.partial(_nattn_kernel, grid_rows=grid_rows, rows_per_blk=rows_per_blk),
        grid=(batch, n_pairs, nblk),
        in_specs=[pl.BlockSpec((blk, LANES), lambda b, p, j: (b * nblk + j, p))]
        + [kspec(n_pairs, s) for s in (-1, 0, 1)] + [kspec(2 * n_pairs, s) for s in (-1, 0, 1)]
        + [ctx_k, ctx_v,
           pl.BlockSpec((None, NA_WIN_ROWS, 2 * GRID_W, NA_WIN_ROWS * GRID_W), lambda b, p, j: (p, 0, 0, 0))],
        out_specs=pl.BlockSpec((blk, LANES), lambda b, p, j: (b * nblk + j, p)),
        out_shape=jax.ShapeDtypeStruct((batch * seq_len, d), BF16),
        scratch_shapes=[pltpu.VMEM((3 * blk, LANES), BF16), pltpu.VMEM((3 * blk, LANES), BF16)],
        compiler_params=_cparams(3),
        name="neighbourhood_attn",
    )(qkv, *([qkv] * 8), bias)
    if not with_ctx_out:
        return lat
    ctx = pl.pallas_call(
        _cattn_kernel,
        grid=(batch, n_pairs),
        in_specs=[pl.BlockSpec((ctx_len, LANES), lambda b, p: (cb0 + b, p)),
                  pl.BlockSpec((ctx_len, LANES), lambda b, p: (cb0 + b, n_pairs + p)),
                  pl.BlockSpec((ctx_len, LANES), lambda b, p: (cb0 + b, 2 * n_pairs + p))],
        out_specs=pl.BlockSpec((ctx_len, LANES), lambda b, p: (b, p)),
        out_shape=jax.ShapeDtypeStruct((batch * ctx_len, d), BF16),
        compiler_params=_cparams(2),
        name="ctx_mha",
    )(qkv, qkv, qkv)
    return jnp.concatenate([lat, ctx], axis=0)


def _cattn_kernel(q_ref, k_ref, v_ref, o_ref):
    rows = q_ref.shape[0]
    lane_lo = lax.broadcasted_iota(jnp.int32, (rows, LANES), 1) < HEAD_DIM
    qa, qb = _masked_halves(q_ref[...], lane_lo)
    qs = jnp.concatenate([qa, qb], axis=0)
    o = _softmax_pv([_nt_dot(qs, k_ref[...])], [v_ref[...]], None)
    o_ref[...] = jnp.where(lane_lo, o[:rows], o[rows:]).astype(o_ref.dtype)


def _router_kernel(x_ref, g_ref, sh_ref, sc_ref, wh_ref, wl_ref, b_ref, h_ref, idx_ref, gate_ref, cnt_ref,
                   run_ref):
    @pl.when(pl.program_id(0) == 0)
    def _():
        run_ref[...] = jnp.zeros_like(run_ref)

    h = _norm_mod(x_ref[...], g_ref[...], sh_ref[0], sc_ref[0])
    h_hi = h.astype(BF16)
    h_ref[...] = h_hi
    h_lo = (h - h_hi.astype(F32)).astype(BF16)
    logits = (jnp.dot(h_hi, wh_ref[...], preferred_element_type=F32)
              + jnp.dot(h_hi, wl_ref[...], preferred_element_type=F32)
              + jnp.dot(h_lo, wh_ref[...], preferred_element_type=F32)) + b_ref[...]
    lane = lax.broadcasted_iota(jnp.int32, logits.shape, 1)
    tm = logits.shape[0]
    idx_out = jnp.zeros(logits.shape, jnp.int32)
    val_out = jnp.zeros(logits.shape, F32)
    tri = jnp.where(lax.broadcasted_iota(jnp.int32, (tm, tm), 0) > lax.broadcasted_iota(jnp.int32, (tm, tm), 1),
                    1.0, 0.0).astype(BF16)
    run = run_ref[...]
    top = None
    denom = 0.0
    for k in range(TOP_K):
        m = logits.max(axis=-1, keepdims=True)
        sel = jnp.min(jnp.where(logits == m, lane, LANES), axis=-1, keepdims=True)
        if top is None:
            top = m
        e = jnp.exp(m - top)
        denom = denom + e
        onehot = lane == sel
        before = jnp.dot(tri, jnp.where(onehot, 1.0, 0.0).astype(BF16), preferred_element_type=F32)
        rank = jnp.sum(jnp.where(onehot, before + run, 0.0), axis=-1, keepdims=True).astype(jnp.int32)
        run = run + jnp.sum(jnp.where(onehot, 1.0, 0.0), axis=0, keepdims=True)
        idx_out = jnp.where(lane == k, sel, idx_out)
        idx_out = jnp.where(lane == TOP_K + k, rank, idx_out)
        val_out = jnp.where(lane == k, e, val_out)
        logits = jnp.where(onehot, -jnp.inf, logits)
    run_ref[...] = run
    cnt_ref[...] = run
    idx_ref[...] = idx_out
    gate_ref[...] = val_out / denom


def router(x, g, shift, scale, w_router, b_router, *, seg_rows, tm=ROW_TILE):
    m, d = x.shape
    n_seg = shift.shape[0]
    n_e = w_router.shape[1]
    w_pad = jnp.zeros((d, LANES), F32).at[:, :n_e].set(w_router)
    w_hi = w_pad.astype(BF16)
    w_lo = (w_pad - w_hi.astype(F32)).astype(BF16)
    b_pad = jnp.full((1, LANES), -jnp.inf, F32).at[0, :n_e].set(b_router)
    seg = lambda i: (jnp.minimum(i * tm // seg_rows, n_seg - 1), 0, 0)
    return pl.pallas_call(
        _router_kernel,
        grid=(m // tm,),
        in_specs=[pl.BlockSpec((tm, d), lambda i: (i, 0)),
                  pl.BlockSpec((1, d), lambda i: (0, 0)),
                  pl.BlockSpec((1, 1, d), seg),
                  pl.BlockSpec((1, 1, d), seg),
                  pl.BlockSpec((d, LANES), lambda i: (0, 0)),
                  pl.BlockSpec((d, LANES), lambda i: (0, 0)),
                  pl.BlockSpec((1, LANES), lambda i: (0, 0))],
        out_specs=[pl.BlockSpec((tm, d), lambda i: (i, 0)),
                   pl.BlockSpec((tm, LANES), lambda i: (i, 0)),
                   pl.BlockSpec((tm, LANES), lambda i: (i, 0)),
                   pl.BlockSpec((1, LANES), lambda i: (0, 0))],
        out_shape=[jax.ShapeDtypeStruct((m, d), BF16),
                   jax.ShapeDtypeStruct((m, LANES), jnp.int32),
                   jax.ShapeDtypeStruct((m, LANES), F32),
                   jax.ShapeDtypeStruct((1, LANES), F32)],
        scratch_shapes=[pltpu.VMEM((1, LANES), F32)],
        compiler_params=_cparams(1),
        name="router",
    )(x, g, shift, scale, w_hi, w_lo, b_pad)


def _expert_kernel(blk_e_ref, n_used_ref, x_ref, wgu_ref, bgu_ref, wd_ref, bd_ref, o_ref, *, f):
    i = pl.program_id(0)

    @pl.when(i < n_used_ref[0])
    def _():
        gu = jnp.dot(x_ref[...], wgu_ref[0], preferred_element_type=F32) + bgu_ref[0]
        g = jnp.minimum(gu[:, :f], SWIGLU_LIMIT)
        u = jnp.clip(gu[:, f:], -SWIGLU_LIMIT, SWIGLU_LIMIT)
        act = (u + 1) * (g * jax.nn.sigmoid(SWIGLU_ALPHA * g))
        y = jnp.dot(act.astype(BF16), wd_ref[0], preferred_element_type=F32) + bd_ref[0]
        o_ref[...] = y.astype(o_ref.dtype)

    @pl.when(i >= n_used_ref[0])
    def _():
        o_ref[...] = jnp.zeros_like(o_ref)


def expert_mlp(xs, blk_e, n_used, w_gu, b_gu, w_down, b_down, *, bm=MOE_ROWS):
    n_slot, d = xs.shape
    n_e, _, f2 = w_gu.shape
    f = f2 // 2
    return pl.pallas_call(
        functools.partial(_expert_kernel, f=f),
        grid_spec=pltpu.PrefetchScalarGridSpec(
            num_scalar_prefetch=2,
            grid=(n_slot // bm,),
            in_specs=[pl.BlockSpec((bm, d), lambda i, be, nu: (i, 0)),
                      pl.BlockSpec((1, d, f2), lambda i, be, nu: (be[i], 0, 0)),
                      pl.BlockSpec((1, 1, f2), lambda i, be, nu: (be[i], 0, 0)),
                      pl.BlockSpec((1, f, d), lambda i, be, nu: (be[i], 0, 0)),
                      pl.BlockSpec((1, 1, d), lambda i, be, nu: (be[i], 0, 0))],
            out_specs=pl.BlockSpec((bm, d), lambda i, be, nu: (i, 0))),
        out_shape=jax.ShapeDtypeStruct((n_slot, d), BF16),
        compiler_params=_cparams(1),
        name="expert_mlp",
    )(blk_e, n_used, xs, w_gu, b_gu.reshape(n_e, 1, f2), w_down, b_down.reshape(n_e, 1, d))


def _combine_kernel(res_ref, y_ref, w_ref, gate_ref, o_ref):
    d = o_ref.shape[1]
    w = w_ref[...]
    y = w[:, 0:1] * y_ref[:, 0:d].astype(F32)
    for k in range(1, TOP_K):
        y = y + w[:, k:k + 1] * y_ref[:, k * d:(k + 1) * d].astype(F32)
    o_ref[...] = res_ref[...] + gate_ref[0] * y


def moe_layer(x, g, shift, scale, gate, w_router, b_router, w_gu, b_gu, w_down, b_down, *, seg_rows,
              bm=MOE_ROWS):
    m, d = x.shape
    n_e = w_router.shape[1]
    h, route, gates, counts = router(x, g, shift, scale, w_router, b_router, seg_rows=seg_rows)
    idx = route[:, :TOP_K]
    rank = route[:, TOP_K:2 * TOP_K]
    n_asg = m * TOP_K
    counts = counts[0, :n_e].astype(jnp.int32)
    padded = (counts + bm - 1) // bm * bm
    pad_end = jnp.cumsum(padded)
    pad_start = pad_end - padded
    start_of = jnp.sum(jnp.where(idx[:, :, None] == jnp.arange(n_e, dtype=jnp.int32), pad_start, 0), axis=-1)
    dest = (start_of + rank).reshape(-1)
    n_blk = -(-(n_asg + n_e * (bm - 1)) // bm)
    n_slot = n_blk * bm
    tok = (jnp.arange(n_asg, dtype=jnp.int32) // TOP_K)
    slot_tok = jnp.zeros((n_slot,), jnp.int32).at[dest].set(tok, unique_indices=True)
    blk_start = jnp.arange(n_blk, dtype=jnp.int32) * bm
    blk_e = jnp.minimum(jnp.sum((pad_end[None, :] <= blk_start[:, None]).astype(jnp.int32), axis=1), n_e - 1)
    n_used = (pad_end[-1] // bm).astype(jnp.int32).reshape(1)
    xs = jnp.take(h, slot_tok, axis=0)
    y_slot = expert_mlp(xs, blk_e, n_used, w_gu, b_gu, w_down, b_down, bm=bm)
    y_tok = jnp.take(y_slot, dest, axis=0).reshape(m, TOP_K * d)
    tm = 256
    n_seg = gate.shape[0]
    seg = lambda i: (jnp.minimum(i * tm // seg_rows, n_seg - 1), 0, 0)
    return pl.pallas_call(
        _combine_kernel,
        grid=(m // tm,),
        in_specs=[pl.BlockSpec((tm, d), lambda i: (i, 0)),
                  pl.BlockSpec((tm, TOP_K * d), lambda i: (i, 0)),
                  pl.BlockSpec((tm, LANES), lambda i: (i, 0)),
                  pl.BlockSpec((1, 1, d), seg)],
        out_specs=pl.BlockSpec((tm, d), lambda i: (i, 0)),
        out_shape=jax.ShapeDtypeStruct((m, d), F32),
        compiler_params=_cparams(1),
        name="moe_combine",
    )(x, y_tok, gates, gate)


LC_R2 = 128
LC_C = 256
LC_KB = 8
LC_LANES = 2048


@functools.lru_cache(maxsize=None)
def _lc_mats(n):
    nn = 2 * n
    r1 = nn // LC_R2
    tau = 2.0 * np.pi
    k1 = np.arange(r1)
    ang1 = tau * ((k1[:, None] * np.arange(r1)[None, :]) % r1) / r1
    f1 = np.empty((2 * r1, r1))
    f1[0::2] = np.cos(ang1)
    f1[1::2] = -np.sin(ang1)
    kk = k1[:, None] + r1 * np.arange(LC_R2)[None, :]
    th = tau * ((kk[:, :, None] * np.arange(LC_R2)[None, None, :]) % nn) / nn
    c, s = np.cos(th), np.sin(th)
    f2 = np.concatenate([np.concatenate([c, s], 2), np.concatenate([-s, c], 2)], 1)
    ct, st = c.transpose(0, 2, 1), s.transpose(0, 2, 1)
    f3 = np.concatenate([np.concatenate([ct, -st], 2), np.concatenate([st, ct], 2)], 1)
    ang4 = tau * ((np.arange(r1 // 2)[:, None] * k1[None, :]) % r1) / r1
    f4 = np.concatenate([np.cos(ang4), -np.sin(ang4)], 1) / nn
    return tuple(jnp.asarray(a, BF16) for a in (f1, f2, f3, f4))


@functools.lru_cache(maxsize=None)
def _dft_small_mats(n):
    nn = 2 * n
    ang = 2.0 * np.pi * ((np.arange(nn)[:, None] * np.arange(nn)[None, :]) % nn) / nn
    fwd = np.concatenate([np.cos(ang), -np.sin(ang)], 0)
    inv = np.concatenate([np.cos(ang[:n]), -np.sin(ang[:n])], 1) / nn
    return jnp.asarray(fwd, BF16), jnp.asarray(inv, BF16)


def _lc_first_kernel(x_ref, f_ref, o_ref):
    o_ref[...] = jnp.dot(f_ref[...], x_ref[...].astype(BF16), preferred_element_type=F32).astype(o_ref.dtype)


def lc_first(x, f1, s1, off=0):
    _, s2, nh, l = x.shape
    lc = min(LC_LANES, l)
    r2 = f1.shape[0]
    return pl.pallas_call(
        _lc_first_kernel,
        grid=(s1, s2, l // lc),
        in_specs=[pl.BlockSpec((None, None, nh, lc), lambda a, b, j: (off + a, b, 0, j)),
                  pl.BlockSpec((r2, nh), lambda a, b, j: (0, 0))],
        out_specs=pl.BlockSpec((None, None, r2, lc), lambda a, b, j: (a, b, 0, j)),
        out_shape=jax.ShapeDtypeStruct((s1, s2, r2, l), BF16),
        compiler_params=_cparams(3),
        name="lc_first",
    )(x, f1[:, :nh])


def _lc_mid_kernel(a_ref, f2_ref, *rest, kb, spectrum_only):
    if spectrum_only:
        (o_ref,) = rest
    else:
        h_ref, f3_ref, o_ref = rest
    c = a_ref.shape[-1]
    for kk in range(kb):
        a_in = a_ref[2 * kk:2 * kk + 2].reshape(2 * LC_R2, c)
        x = jnp.dot(f2_ref[kk], a_in, preferred_element_type=F32)
        if spectrum_only:
            o_ref[kk] = x.astype(o_ref.dtype)
            continue
        h = h_ref[kk].astype(F32)
        xr, xi = x[:LC_R2], x[LC_R2:]
        hr, hi = h[:LC_R2], h[LC_R2:]
        prod = jnp.concatenate([xr * hr - xi * hi, xr * hi + xi * hr], axis=0).astype(BF16)
        g = jnp.dot(f3_ref[kk], prod, preferred_element_type=F32)
        o_ref[0, kk] = g[:LC_R2].astype(o_ref.dtype)
        o_ref[1, kk] = g[LC_R2:].astype(o_ref.dtype)


def lc_spectrum(a, f2):
    cb, _, r2, _, c = a.shape
    r1 = r2 // 2
    kb = min(LC_KB, r1)
    return pl.pallas_call(
        functools.partial(_lc_mid_kernel, kb=kb, spectrum_only=True),
        grid=(r1 // kb, cb),
        in_specs=[pl.BlockSpec((None, None, 2 * kb, LC_R2, c), lambda j, i: (i, 0, j, 0, 0)),
                  pl.BlockSpec((kb, 2 * LC_R2, 2 * LC_R2), lambda j, i: (j, 0, 0))],
        out_specs=pl.BlockSpec((None, kb, 2 * LC_R2, c), lambda j, i: (i, j, 0, 0)),
        out_shape=jax.ShapeDtypeStruct((cb, r1, 2 * LC_R2, c), BF16),
        compiler_params=_cparams(2),
        name="lc_spectrum",
    )(a, f2)


def lc_mid(a, h, f2, f3):
    cb, bsz, r2, _, c = a.shape
    r1 = r2 // 2
    kb = min(LC_KB, r1)
    mat = pl.BlockSpec((kb, 2 * LC_R2, 2 * LC_R2), lambda j, i, b: (j, 0, 0))
    return pl.pallas_call(
        functools.partial(_lc_mid_kernel, kb=kb, spectrum_only=False),
        grid=(r1 // kb, cb, bsz),
        in_specs=[pl.BlockSpec((None, None, 2 * kb, LC_R2, c), lambda j, i, b: (i, b, j, 0, 0)),
                  mat,
                  pl.BlockSpec((None, kb, 2 * LC_R2, c), lambda j, i, b: (i, j, 0, 0)),
                  mat],
        out_specs=pl.BlockSpec((None, None, 2, kb, LC_R2, c), lambda j, i, b: (i, b, 0, j, 0, 0)),
        out_shape=jax.ShapeDtypeStruct((cb, bsz, 2, r1, LC_R2, c), BF16),
        compiler_params=_cparams(3),
        name="lc_mid",
    )(a, f2, h, f3)


def _lc_last_kernel(g_ref, f_ref, u_ref, gate_ref, skip_ref, o_ref):
    y = jnp.dot(f_ref[...], g_ref[...], preferred_element_type=F32)
    u = u_ref[...].astype(F32)
    o_ref[...] = (gate_ref[...].astype(F32) * (y + skip_ref[...] * u)).astype(o_ref.dtype)


def lc_last(g, f4, u_arr, u_off, gate_arr, gate_off, skip_row):
    cb, bsz, r2, l = g.shape
    nh = r2 // 4
    lc = min(LC_LANES, l)
    return pl.pallas_call(
        _lc_last_kernel,
        grid=(cb, bsz, l // lc),
        in_specs=[pl.BlockSpec((None, None, r2, lc), lambda i, b, j: (i, b, 0, j)),
                  pl.BlockSpec((nh, r2), lambda i, b, j: (0, 0)),
                  pl.BlockSpec((None, None, nh, lc), lambda i, b, j: (u_off + i, b, 0, j)),
                  pl.BlockSpec((None, None, nh, lc), lambda i, b, j: (gate_off + i, b, 0, j)),
                  pl.BlockSpec((None, 1, lc), lambda i, b, j: (i, 0, j))],
        out_specs=pl.BlockSpec((None, None, nh, lc), lambda i, b, j: (i, b, 0, j)),
        out_shape=jax.ShapeDtypeStruct((cb, bsz, nh, l), BF16),
        compiler_params=_cparams(3),
        name="lc_last",
    )(g, f4, u_arr, gate_arr, skip_row)


def _plain_mm_kernel(a_ref, b_ref, o_ref):
    o_ref[...] = jnp.dot(a_ref[...], b_ref[...].astype(BF16), preferred_element_type=F32)


def plain_mm(a, b, tn=512):
    m, k = a.shape
    n = b.shape[1]
    tn = _col_tile(n, tn)
    return pl.pallas_call(
        _plain_mm_kernel,
        grid=(n // tn,),
        in_specs=[pl.BlockSpec((m, k), lambda j: (0, 0)), pl.BlockSpec((k, tn), lambda j: (0, j))],
        out_specs=pl.BlockSpec((m, tn), lambda j: (0, j)),
        out_shape=jax.ShapeDtypeStruct((m, n), F32),
        compiler_params=_cparams(1),
        name="plain_mm",
    )(a, b)


def _short_conv_kernel(u_ref, gate_ref, h_ref, skip_ref, fwd_ref, inv_ref, o_ref):
    u = u_ref[...]
    x = jnp.dot(fwd_ref[...], u.astype(BF16), preferred_element_type=F32)
    h = h_ref[...]
    nn = x.shape[0] // 2
    xr, xi, hr, hi = x[:nn], x[nn:], h[:nn], h[nn:]
    prod = jnp.concatenate([xr * hr - xi * hi, xr * hi + xi * hr], axis=0).astype(BF16)
    y = jnp.dot(inv_ref[...], prod, preferred_element_type=F32)
    o_ref[...] = (gate_ref[...].astype(F32) * (y + skip_ref[...] * u.astype(F32))).astype(o_ref.dtype)


def short_long_conv(u_arr, u_col, gate_arr, gate_col, spec, skip, fwd, inv, *, batch, seq_len, tl=512):
    d = spec.shape[1]
    tl = _col_tile(d, tl)
    uc, gc = u_col // tl, gate_col // tl
    nn2 = spec.shape[0]
    return pl.pallas_call(
        _short_conv_kernel,
        grid=(batch, d // tl),
        in_specs=[pl.BlockSpec((seq_len, tl), lambda b, j: (b, uc + j)),
                  pl.BlockSpec((seq_len, tl), lambda b, j: (b, gc + j)),
                  pl.BlockSpec((nn2, tl), lambda b, j: (0, j)),
                  pl.BlockSpec((1, tl), lambda b, j: (0, j)),
                  pl.BlockSpec((nn2, seq_len), lambda b, j: (0, 0)),
                  pl.BlockSpec((seq_len, nn2), lambda b, j: (0, 0))],
        out_specs=pl.BlockSpec((seq_len, tl), lambda b, j: (b, j)),
        out_shape=jax.ShapeDtypeStruct((batch * seq_len, d), BF16),
        compiler_params=_cparams(2),
        name="short_long_conv",
    )(u_arr, gate_arr, spec, skip, fwd[:, :seq_len], inv)


def _hyena_taps(n, d, f_w1, f_b1, f_w2, f_b2, f_w3, f_b3, f_freq, f_w4):
    hp = lax.Precision.HIGHEST
    lin = jnp.linspace(0.0, 1.0, n, dtype=F32)
    idx = jnp.arange(n, dtype=F32)
    bands = jnp.linspace(1e-4, HY_BANDS - 1, HY_BANDS, dtype=F32)[None, :]
    deltas = jnp.abs(jnp.linspace(math.log(HY_DECAY_TARGET) / HY_FAST_DECAY,
                                  math.log(HY_DECAY_TARGET) / HY_SLOW_DECAY, d, dtype=F32))

    def branch(t, pos, direction):
        t = t[:, None]
        ang = (2.0 * math.pi / n) * pos[:, None]
        emb = jnp.concatenate([t, jnp.cos(bands * ang), -jnp.sin(bands * ang)], axis=-1)
        a = jnp.sin(f_freq * (jnp.dot(emb, f_w1, precision=hp) + f_b1))
        a = jnp.sin(f_freq * (jnp.dot(a, f_w2, precision=hp) + f_b2))
        a = jnp.sin(f_freq * (jnp.dot(a, f_w3, precision=hp) + f_b3))
        decay = jnp.exp(-t * deltas[None, :])
        w4 = f_w4.reshape(f_w4.shape[0], HY_ORDER, 2, d)[:, :, direction]
        return [jnp.dot(a, w4[:, o], precision=hp) * decay for o in range(HY_ORDER)]

    fwd = branch(lin, idx, 0)
    bwd = branch(lin[:0:-1], idx[:0:-1], 1)
    zero = jnp.zeros((1, d), F32)
    return jnp.stack([jnp.concatenate([fwd[o], zero, bwd[o]], axis=0) for o in range(HY_ORDER)])


def hyena_long_convs(zc, taps, skip, *, batch, seq_len):
    cb3, rows, c = zc.shape
    cb = cb3 // 3
    d = cb * c
    nh = seq_len // LC_R2
    lanes = LC_R2 * c
    f1, f2, f3, f4 = _lc_mats(seq_len)
    zc_v = zc.reshape(cb3, batch, nh, lanes)
    u, u_off = zc_v, 2 * cb
    for o in range(HY_ORDER):
        taps_b = taps[o].reshape(2 * seq_len, cb, c).transpose(1, 0, 2).reshape(cb, 1, 2 * nh, lanes)
        spec = lc_spectrum(lc_first(taps_b, f1, cb).reshape(cb, 1, 4 * nh, LC_R2, c), f2)
        a = lc_first(u, f1, cb, u_off).reshape(cb, batch, 4 * nh, LC_R2, c)
        g = lc_mid(a, spec, f2, f3).reshape(cb, batch, 4 * nh, lanes)
        skip_row = jnp.tile(skip[o].reshape(cb, 1, c), (1, 1, LC_R2))
        u = lc_last(g, f4, u, u_off, zc_v, o * cb, skip_row)
        u_off = 0
    return u.reshape(cb, rows, c)


def _final_norm_kernel(x_ref, g_ref, o_ref):
    x = x_ref[...]
    o_ref[...] = x * lax.rsqrt(jnp.mean(x * x, axis=-1, keepdims=True) + NORM_EPS) * g_ref[...]


def final_norm(x, g, *, rows, tm=ROW_TILE):
    d = x.shape[1]
    return pl.pallas_call(
        _final_norm_kernel,
        grid=(rows // tm,),
        in_specs=[pl.BlockSpec((tm, d), lambda i: (i, 0)), pl.BlockSpec((1, d), lambda i: (0, 0))],
        out_specs=pl.BlockSpec((tm, d), lambda i: (i, 0)),
        out_shape=jax.ShapeDtypeStruct((rows, d), F32),
        compiler_params=_cparams(1),
        name="final_norm",
    )(x, g)


def _rope_tables(n_lat, seq_len, n_rows):
    t = jnp.arange(n_rows, dtype=jnp.int32)
    row = ((t % seq_len) // GRID_W).astype(F32)
    col = (t % GRID_W).astype(F32)
    lane = jnp.arange(LANES)
    dd = lane % HEAD_DIM
    quarter = HEAD_DIM // 4
    inv_freq = ROPE_BASE ** (-(dd % quarter).astype(F32) / quarter)
    pos = jnp.where(dd[None, :] < HEAD_DIM // 2, row[:, None], col[:, None])
    ang = pos * inv_freq[None, :]
    sign = jnp.where((dd % (HEAD_DIM // 2)) < quarter, -1.0, 1.0).astype(F32)
    lat = (t < n_lat)[:, None]
    cos = jnp.where(lat, jnp.cos(ang), 1.0)
    sin = jnp.where(lat, jnp.sin(ang) * sign[None, :], 0.0)
    return cos, sin


def _wa_head_perm(n_heads):
    order = []
    for p in range(n_heads // (2 * WA_GROUP)):
        for g in range(WA_GROUP):
            order += [2 * WA_GROUP * p + g, 2 * WA_GROUP * p + WA_GROUP + g]
    cols = jnp.asarray(order, jnp.int32)[:, None] * HEAD_DIM + jnp.arange(HEAD_DIM, dtype=jnp.int32)[None, :]
    return cols.reshape(-1)


def kernel(x, c, ctx, c_ctx, w_ada, b_ada, g_mix, g_ffn, hy_w_in, hy_b_in, hy_w_short, hy_b_short, hy_f_w1, hy_f_b1, hy_f_w2, hy_f_b2, hy_f_w3, hy_f_b3, hy_f_freq, hy_f_w4, hy_skip, hy_w_out, hy_b_out, cf_w_pw1, cf_b_pw1, cf_w_dw, cf_b_dw, cf_ln_g, cf_ln_b, cf_w_pw2, cf_b_pw2, wa_w_qkv, wa_w_o, wa_sinks, na_w_qkv, na_w_o, na_rpb, moe_w_router, moe_b_router, moe_w_gu, moe_b_gu, moe_w_down, moe_b_down, g_final):
    bsz, n, d = x.shape
    n_ctx = ctx.shape[1]
    depth = w_ada.shape[0]
    n_lat = bsz * n
    m_all = n_lat + bsz * n_ctx

    cond = jnp.zeros((8, d), F32).at[:bsz].set(jax.nn.silu(c)).at[bsz].set(jax.nn.silu(c_ctx))
    mods = adaln(cond, w_ada, b_ada)
    xu = jnp.concatenate([x.reshape(n_lat, d), ctx.reshape(bsz * n_ctx, d)], axis=0)
    zeros_d = jnp.zeros((1, d), F32)

    for i in range(depth):
        kind, j = i % N_MIXERS, i // N_MIXERS
        last = i == depth - 1
        mod = [mods[i, :bsz + 1, k * d:(k + 1) * d].reshape(bsz + 1, 1, d) for k in range(6)]
        m_out = n_lat if last else m_all
        gm = g_mix[i].reshape(1, d)
        need_ctx_in = (not last) or kind >= 2
        m_in = m_all if need_ctx_in else n_lat
        x_in = xu[:m_in]

        if kind == 0:
            z = nm_matmul(x_in, gm, mod[0], mod[1], hy_w_in[j].astype(BF16), hy_b_in[j].reshape(1, -1),
                          seg_rows=n)
            filt_w = (hy_f_w1[j], hy_f_b1[j], hy_f_w2[j], hy_f_b2[j], hy_f_w3[j], hy_f_b3[j], hy_f_freq[j],
                      hy_f_w4[j])
            w_s, b_s = hy_w_short[j], hy_b_short[j].reshape(1, -1)
            w_o, b_o = hy_w_out[j].astype(BF16), hy_b_out[j].reshape(1, d)
            zc = dwconv(z, w_s, b_s, start=0, total=n_lat, seq_len=n, tc=min(LC_C, d), blocked=True)
            y = hyena_long_convs(zc, _hyena_taps(n, d, *filt_w), hy_skip[j], batch=bsz, seq_len=n)
            x_lat = mm_res(y, w_o, b_o, xu, mod[2], seg_rows=n)
            if m_in > n_lat:
                zcc = dwconv(z, w_s, b_s, start=n_lat, total=m_in - n_lat, seq_len=n_ctx, tc=512)
                taps_c = _hyena_taps(n_ctx, d, *filt_w)
                fwd, inv = _dft_small_mats(n_ctx)
                yc = short_long_conv(zcc, 2 * d, zcc, 0, plain_mm(fwd, taps_c[0]), hy_skip[j][0].reshape(1, d),
                                     fwd, inv, batch=bsz, seq_len=n_ctx)
                yc = short_long_conv(yc, 0, zcc, d, plain_mm(fwd, taps_c[1]), hy_skip[j][1].reshape(1, d),
                                     fwd, inv, batch=bsz, seq_len=n_ctx)
                x_ctx = mm_res(yc, w_o, b_o, xu[n_lat:m_in], mod[2][bsz:], seg_rows=n)
                xu = jnp.concatenate([x_lat, x_ctx], axis=0) if not last else x_lat
            else:
                xu = x_lat
            y = None
        elif kind == 1:
            w1 = cf_w_pw1[j].astype(BF16).reshape(d, 2, d).transpose(1, 0, 2)
            a = nm_matmul(x_in, gm, mod[0], mod[1], w1, cf_b_pw1[j].reshape(2, 1, d), seg_rows=n, mode="glu")
            conv = functools.partial(dwconv, a, cf_w_dw[j], cf_b_dw[j].reshape(1, d), post="ln_silu",
                                     ln=(cf_ln_g[j].reshape(1, d), cf_ln_b[j].reshape(1, d)), rows=64)
            y = conv(start=0, total=n_lat, seq_len=n)
            if m_in > n_lat:
                y = jnp.concatenate([y, conv(start=n_lat, total=m_in - n_lat, seq_len=n_ctx)], axis=0)
            w_o, b_o = cf_w_pw2[j].astype(BF16), cf_b_pw2[j].reshape(1, d)
        elif kind == 2:
            perm = _wa_head_perm(d // HEAD_DIM)
            scale = HEAD_DIM ** -0.5
            w_qkv = jnp.concatenate([wa_w_qkv[j][:, :d][:, perm] * scale, wa_w_qkv[j][:, d:]], axis=1).astype(BF16)
            n_out = w_qkv.shape[1]
            qkv = nm_matmul(x_in, gm, mod[0], mod[1], w_qkv, jnp.zeros((1, n_out), F32), seg_rows=n,
                            mode="rope", rope=_rope_tables(n_lat, n, m_in), n_rope_cols=d + (n_out - d) // 2)
            y = window_attention(qkv, wa_sinks[j], batch=bsz, seq_len=n, ctx_len=n_ctx, d=d)
            w_o, b_o = wa_w_o[j][perm].astype(BF16), zeros_d
        else:
            scale = HEAD_DIM ** -0.5
            w_qkv = jnp.concatenate([na_w_qkv[j][:, :d] * scale, na_w_qkv[j][:, d:]], axis=1).astype(BF16)
            qkv = nm_matmul(x_in, gm, mod[0], mod[1], w_qkv, jnp.zeros((1, 3 * d), F32), seg_rows=n)
            y = neighbourhood_attention(qkv, na_rpb[j], batch=bsz, seq_len=n, ctx_len=n_ctx, d=d,
                                        with_ctx_out=not last)
            w_o, b_o = na_w_o[j].astype(BF16), zeros_d
        if y is not None:
            xu = mm_res(y[:m_out], w_o, b_o, xu[:m_out], mod[2], seg_rows=n)

        xu = moe_layer(xu, g_ffn[i].reshape(1, d), mod[3], mod[4], mod[5], moe_w_router[i], moe_b_router[i],
                       moe_w_gu[i].astype(BF16), moe_b_gu[i], moe_w_down[i].astype(BF16), moe_b_down[i],
                       seg_rows=n)
    return final_norm(xu, g_final.reshape(1, d), rows=n_lat).reshape(bsz, n, d)
```

```python
import functools
import math

import jax
import jax.numpy as jnp
import numpy as np
from jax import lax
from jax.experimental import pallas as pl
from jax.experimental.pallas import tpu as pltpu

F32 = jnp.float32
BF16 = jnp.bfloat16

GRID_W = 64
N_MIXERS = 4
NORM_EPS = 1e-6
NEG_INF = -1e30
HEAD_DIM = 64
ROPE_BASE = 10000.0

HY_ORDER = 2
HY_BANDS = 16
HY_DECAY_TARGET = 1e-2
HY_FAST_DECAY = 0.3
HY_SLOW_DECAY = 1.5

WA_GROUP = 4
WA_WINDOW = 128
NA_WIN_ROWS = 8
NA_WIN_COLS = 16

TOP_K = 4
SWIGLU_LIMIT = 7.0
SWIGLU_ALPHA = 1.702

LANES = 128
ROW_TILE = 512
MOE_ROWS = 256
VMEM_LIMIT = 56 * 1024 * 1024
EXPERT_VMEM_LIMIT = 58 * 1024 * 1024


def _cparams(n_axes):
    return pltpu.CompilerParams(dimension_semantics=("arbitrary",) * n_axes,
                                vmem_limit_bytes=VMEM_LIMIT)


def _col_tile(n, pref=1024):
    t = min(pref, n)
    while n % t:
        t //= 2
    return t


def _adaln_kernel(c_ref, w_ref, b_ref, o_ref):
    w = w_ref[0]
    w_hi = w.astype(BF16)
    w_lo = (w - w_hi.astype(F32)).astype(BF16)
    c = c_ref[...]
    c_hi = c.astype(BF16)
    c_lo = (c - c_hi.astype(F32)).astype(BF16)
    acc = jnp.dot(c_hi, w_hi, preferred_element_type=F32)
    acc += jnp.dot(c_hi, w_lo, preferred_element_type=F32)
    acc += jnp.dot(c_lo, w_hi, preferred_element_type=F32)
    o_ref[0] = acc + b_ref[0]


def adaln(cond, w_ada, b_ada):
    depth, d, n6 = w_ada.shape
    tn = _col_tile(n6, 1024)
    return pl.pallas_call(
        _adaln_kernel,
        grid=(depth, n6 // tn),
        in_specs=[pl.BlockSpec((8, d), lambda l, j: (0, 0)),
                  pl.BlockSpec((1, d, tn), lambda l, j: (l, 0, j)),
                  pl.BlockSpec((1, 1, tn), lambda l, j: (l, 0, j))],
        out_specs=pl.BlockSpec((1, 8, tn), lambda l, j: (l, 0, j)),
        out_shape=jax.ShapeDtypeStruct((depth, 8, n6), F32),
        compiler_params=_cparams(2),
        name="adaln",
    )(cond, w_ada, b_ada.reshape(depth, 1, n6))


def _norm_mod(x, g, shift, scale):
    y = x * lax.rsqrt(jnp.mean(x * x, axis=-1, keepdims=True) + NORM_EPS)
    return (y * g) * (1 + scale) + shift


def _rope_tile(x, cos, sin):
    lane = lax.broadcasted_iota(jnp.int32, x.shape, 1)
    nxt = pltpu.roll(x, LANES - 16, axis=1)
    prv = pltpu.roll(x, 16, axis=1)
    partner = jnp.where((lane // 16) % 2 == 0, nxt, prv)
    return x * cos + partner * sin


def _nm_mm_kernel(x_ref, g_ref, sh_ref, sc_ref, w_ref, b_ref, *rest, mode, n_rope):
    if mode == "rope":
        cos_ref, sin_ref, o_ref, h_ref = rest
    else:
        o_ref, h_ref = rest
    j = pl.program_id(1)

    @pl.when(j == 0)
    def _():
        h_ref[...] = _norm_mod(x_ref[...], g_ref[...], sh_ref[0], sc_ref[0]).astype(BF16)

    h = h_ref[...]
    if mode == "glu":
        a = jnp.dot(h, w_ref[0], preferred_element_type=F32) + b_ref[0]
        gate = jnp.dot(h, w_ref[1], preferred_element_type=F32) + b_ref[1]
        o_ref[...] = (a * jax.nn.sigmoid(gate)).astype(o_ref.dtype)
        return
    acc = jnp.dot(h, w_ref[...], preferred_element_type=F32) + b_ref[...]
    if mode == "rope":
        @pl.when(j < n_rope)
        def _():
            cos = cos_ref[...]
            sin = sin_ref[...]
            for g in range(acc.shape[1] // LANES):
                sl = slice(g * LANES, (g + 1) * LANES)
                o_ref[:, sl] = _rope_tile(acc[:, sl], cos, sin).astype(o_ref.dtype)

        @pl.when(j >= n_rope)
        def _():
            o_ref[...] = acc.astype(o_ref.dtype)
    else:
        o_ref[...] = acc.astype(o_ref.dtype)


def nm_matmul(x, g, shift, scale, w, b, *, seg_rows, mode="plain", rope=None, n_rope_cols=0,
              tm=ROW_TILE, tn=None):
    m, d = x.shape
    n_seg = shift.shape[0]
    n = w.shape[-1]
    tn = tn or (_col_tile(math.gcd(n, n_rope_cols), 512) if mode == "rope" else _col_tile(n, 1024))
    seg = lambda i, j: (jnp.minimum(i * tm // seg_rows, n_seg - 1), 0, 0)
    in_specs = [pl.BlockSpec((tm, d), lambda i, j: (i, 0)),
                pl.BlockSpec((1, d), lambda i, j: (0, 0)),
                pl.BlockSpec((1, 1, d), seg),
                pl.BlockSpec((1, 1, d), seg)]
    if mode == "glu":
        in_specs += [pl.BlockSpec((2, d, tn), lambda i, j: (0, 0, j)),
                     pl.BlockSpec((2, 1, tn), lambda i, j: (0, 0, j))]
    else:
        in_specs += [pl.BlockSpec((d, tn), lambda i, j: (0, j)),
                     pl.BlockSpec((1, tn), lambda i, j: (0, j))]
    args = [x, g, shift, scale, w, b]
    if mode == "rope":
        in_specs += [pl.BlockSpec((tm, LANES), lambda i, j: (i, 0))] * 2
        args += list(rope)
    return pl.pallas_call(
        functools.partial(_nm_mm_kernel, mode=mode, n_rope=n_rope_cols // tn),
        grid=(m // tm, n // tn),
        in_specs=in_specs,
        out_specs=pl.BlockSpec((tm, tn), lambda i, j: (i, j)),
        out_shape=jax.ShapeDtypeStruct((m, n), BF16),
        scratch_shapes=[pltpu.VMEM((tm, d), BF16)],
        compiler_params=_cparams(2),
        name="nm_matmul_" + mode,
    )(*args)


def _mm_res_kernel(a_ref, w_ref, b_ref, res_ref, gate_ref, o_ref):
    if len(a_ref.shape) == 3:
        kc = a_ref.shape[2]
        acc = b_ref[...]
        for cb in range(a_ref.shape[0]):
            acc = acc + jnp.dot(a_ref[cb], w_ref[cb * kc:(cb + 1) * kc, :], preferred_element_type=F32)
    else:
        acc = jnp.dot(a_ref[...], w_ref[...], preferred_element_type=F32) + b_ref[...]
    o_ref[...] = res_ref[...] + gate_ref[0] * acc


def mm_res(a, w, b, res, gate, *, seg_rows, row_off=0, rows=None, tm=ROW_TILE, tn=None):
    if a.ndim == 3:
        m = rows or a.shape[1]
        k = a.shape[0] * a.shape[2]
        a_spec = pl.BlockSpec((a.shape[0], tm, a.shape[2]), lambda i, j: (0, i, 0))
    else:
        m, k = rows or a.shape[0], a.shape[1]
        a_spec = pl.BlockSpec((tm, k), lambda i, j: (i, 0))
    n = w.shape[1]
    n_seg = gate.shape[0]
    tn = tn or _col_tile(n, 1024)
    blk_off = row_off // tm
    seg = lambda i, j: (jnp.minimum((blk_off + i) * tm // seg_rows, n_seg - 1), 0, j)
    return pl.pallas_call(
        _mm_res_kernel,
        grid=(m // tm, n // tn),
        in_specs=[a_spec,
                  pl.BlockSpec((k, tn), lambda i, j: (0, j)),
                  pl.BlockSpec((1, tn), lambda i, j: (0, j)),
                  pl.BlockSpec((tm, tn), lambda i, j: (blk_off + i, j)),
                  pl.BlockSpec((1, 1, tn), seg)],
        out_specs=pl.BlockSpec((tm, tn), lambda i, j: (blk_off + i, j)),
        out_shape=jax.ShapeDtypeStruct(res.shape, F32),
        input_output_aliases={3: 0},
        compiler_params=_cparams(2),
        name="mm_res",
    )(a, w, b, res, gate)


CONV_HALO = 16


def _dwconv_kernel(prev_ref, x_ref, next_ref, w_ref, b_ref, *rest, taps, blocks_per_seq, post, sub, n_alias):
    if n_alias:
        rest = rest[:-3] + rest[-2:]
    if post == "ln_silu":
        g_ref, beta_ref, o_ref, win_ref = rest
    else:
        o_ref, win_ref = rest
    i = pl.program_id(0)
    pos = i % blocks_per_seq
    rows, c = x_ref.shape
    half = taps // 2
    zero = jnp.zeros((CONV_HALO, c), F32)
    win_ref[CONV_HALO:CONV_HALO + rows, :] = x_ref[...].astype(F32)

    @pl.when(pos == 0)
    def _():
        win_ref[0:CONV_HALO, :] = zero

    @pl.when(pos > 0)
    def _():
        win_ref[0:CONV_HALO, :] = prev_ref[...].astype(F32)

    @pl.when(pos == blocks_per_seq - 1)
    def _():
        win_ref[CONV_HALO + rows:, :] = zero

    @pl.when(pos < blocks_per_seq - 1)
    def _():
        win_ref[CONV_HALO + rows:, :] = next_ref[...].astype(F32)

    bias = b_ref[...]
    for s in range(rows // sub):
        base = CONV_HALO + s * sub - half
        acc = jnp.broadcast_to(bias, (sub, c))
        for t in range(taps):
            acc = acc + w_ref[t] * win_ref[base + t:base + t + sub, :]
        if post == "ln_silu":
            mu = jnp.mean(acc, axis=-1, keepdims=True)
            xc = acc - mu
            var = jnp.mean(xc * xc, axis=-1, keepdims=True)
            y = xc * lax.rsqrt(var + NORM_EPS) * g_ref[...] + beta_ref[...]
            acc = y * jax.nn.sigmoid(y)
        o_ref[s * sub:(s + 1) * sub, :] = acc.astype(o_ref.dtype)


def dwconv(x, w, b, *, start, total, seq_len, post=None, ln=None, rows=256, tc=None, out_dtype=None,
           blocked=False, same_rows=False, into=None):
    m, c = x.shape
    taps = w.shape[0]
    tc = tc or c
    out_dtype = out_dtype or BF16
    sub = 8
    w = jnp.broadcast_to(w[:, None, :], (taps, sub, c))
    r = min(rows, seq_len)
    bps = seq_len // r
    hb = r // CONV_HALO
    off = start // r
    offh = start // CONV_HALO
    nh = m // CONV_HALO
    in_specs = [pl.BlockSpec((CONV_HALO, tc), lambda i, j: (jnp.maximum(offh + i * hb - 1, 0), j)),
                pl.BlockSpec((r, tc), lambda i, j: (off + i, j)),
                pl.BlockSpec((CONV_HALO, tc), lambda i, j: (jnp.minimum(offh + (i + 1) * hb, nh - 1), j)),
                pl.BlockSpec((taps, sub, tc), lambda i, j: (0, 0, j)),
                pl.BlockSpec((1, tc), lambda i, j: (0, j))]
    args = [x, x, x, w, b]
    if post == "ln_silu":
        in_specs += [pl.BlockSpec((1, tc), lambda i, j: (0, j))] * 2
        args += list(ln)
    aliases = {}
    if blocked:
        out_spec = pl.BlockSpec((None, r, tc), lambda i, j: (j, i, 0))
        out_shape = jax.ShapeDtypeStruct((c // tc, total, tc), out_dtype)
    elif same_rows:
        out_spec = pl.BlockSpec((r, tc), lambda i, j: (off + i, j))
        out_shape = jax.ShapeDtypeStruct((m, c), out_dtype)
        if into is not None:
            in_specs.append(pl.BlockSpec(memory_space=pl.ANY))
            args.append(into)
            aliases = {len(args) - 1: 0}
    else:
        out_spec = pl.BlockSpec((r, tc), lambda i, j: (i, j))
        out_shape = jax.ShapeDtypeStruct((total, c), out_dtype)
    return pl.pallas_call(
        functools.partial(_dwconv_kernel, taps=taps, blocks_per_seq=bps, post=post, sub=sub,
                          n_alias=len(aliases)),
        grid=(total // r, c // tc),
        in_specs=in_specs,
        out_specs=out_spec,
        out_shape=out_shape,
        input_output_aliases=aliases,
        scratch_shapes=[pltpu.VMEM((r + 2 * CONV_HALO, tc), F32)],
        compiler_params=_cparams(2),
        name="dwconv%d" % taps,
    )(*args)


def _masked_halves(q, lane_lo):
    zero = jnp.zeros_like(q)
    return jnp.where(lane_lo, q, zero), jnp.where(lane_lo, zero, q)


def _softmax_pv(s_parts, v_parts, sink):
    m = s_parts[0].max(axis=-1, keepdims=True)
    for s in s_parts[1:]:
        m = jnp.maximum(m, s.max(axis=-1, keepdims=True))
    if sink is not None:
        m = jnp.maximum(m, sink)
    denom = jnp.exp(sink - m) if sink is not None else 0.0
    o = None
    for s, v in zip(s_parts, v_parts):
        p = jnp.exp(s - m)
        denom = denom + p.sum(axis=-1, keepdims=True)
        pv = jnp.dot(p.astype(BF16), v, preferred_element_type=F32)
        o = pv if o is None else o + pv
    return o / denom


def _nt_dot(a, b):
    return lax.dot_general(a, b, (((1,), (1,)), ((), ())), preferred_element_type=F32)


def _wattn_kernel(sink_ref, q_ref, *rest, local, seq_len, blk):
    if local:
        k0, k1, k2, v0, v1, v2, kc_ref, vc_ref, o_ref = rest
    else:
        kc_ref, vc_ref, _, o_ref = rest
    p = pl.program_id(2)
    i = pl.program_id(1)
    lane_lo = lax.broadcasted_iota(jnp.int32, (blk, LANES), 1) < HEAD_DIM
    kc = kc_ref[...]
    vc = vc_ref[...]
    if local:
        kl = jnp.concatenate([k0[...], k1[...], k2[...]], axis=0)
        vl = jnp.concatenate([v0[...], v1[...], v2[...]], axis=0)
        qpos = i * blk + lax.broadcasted_iota(jnp.int32, (blk, 3 * blk), 0)
        kpos = (i - 1) * blk + lax.broadcasted_iota(jnp.int32, (blk, 3 * blk), 1)
        valid = (jnp.abs(kpos - qpos) <= WA_WINDOW) & (kpos >= 0) & (kpos < seq_len)
        valid = jnp.concatenate([valid] * WA_GROUP, axis=0)
    qa, qb = [], []
    for g in range(WA_GROUP):
        a, b = _masked_halves(q_ref[:, g * LANES:(g + 1) * LANES], lane_lo)
        qa.append(a)
        qb.append(b)
    outs = []
    for half, qs in enumerate((qa, qb)):
        qs = jnp.concatenate(qs, axis=0)
        sink = jnp.concatenate(
            [jnp.full((blk, 1), sink_ref[8 * p + 4 * half + g], F32) for g in range(WA_GROUP)], axis=0)
        s_parts, v_parts = [], []
        if local:
            s_parts.append(jnp.where(valid, _nt_dot(qs, kl), NEG_INF))
            v_parts.append(vl)
        s_parts.append(_nt_dot(qs, kc))
        v_parts.append(vc)
        outs.append(_softmax_pv(s_parts, v_parts, sink))
    for g in range(WA_GROUP):
        rows = slice(g * blk, (g + 1) * blk)
        o_ref[:, g * LANES:(g + 1) * LANES] = jnp.where(lane_lo, outs[0][rows], outs[1][rows]).astype(o_ref.dtype)


def window_attention(qkv, sinks, *, batch, seq_len, ctx_len, d, blk=128):
    n_pairs = d // (2 * WA_GROUP * HEAD_DIM)
    kcol = d // LANES
    vcol = kcol + n_pairs
    nblk = seq_len // blk
    cb0 = batch * seq_len // ctx_len
    qw = WA_GROUP * LANES
    kern = functools.partial(_wattn_kernel, seq_len=seq_len)
    smem = pl.BlockSpec(memory_space=pltpu.SMEM)

    def kspec(col0, shift):
        return pl.BlockSpec((blk, LANES),
                            lambda b, i, p: (b * nblk + jnp.clip(i + shift, 0, nblk - 1), col0 + p))

    ctx_k = pl.BlockSpec((ctx_len, LANES), lambda b, i, p: (cb0 + b, kcol + p))
    ctx_v = pl.BlockSpec((ctx_len, LANES), lambda b, i, p: (cb0 + b, vcol + p))
    lat = pl.pallas_call(
        functools.partial(kern, local=True, blk=blk),
        grid=(batch, nblk, n_pairs),
        in_specs=[smem, pl.BlockSpec((blk, qw), lambda b, i, p: (b * nblk + i, p))]
        + [kspec(kcol, s) for s in (-1, 0, 1)] + [kspec(vcol, s) for s in (-1, 0, 1)] + [ctx_k, ctx_v],
        out_specs=pl.BlockSpec((blk, qw), lambda b, i, p: (b * nblk + i, p)),
        out_shape=jax.ShapeDtypeStruct((qkv.shape[0], d), BF16),
        compiler_params=_cparams(3),
        name="window_attn",
    )(sinks, qkv, *([qkv] * 8))
    return pl.pallas_call(
        functools.partial(kern, local=False, blk=ctx_len),
        grid=(batch, 1, n_pairs),
        in_specs=[smem, pl.BlockSpec((ctx_len, qw), lambda b, i, p: (cb0 + b, p)), ctx_k, ctx_v,
                  pl.BlockSpec(memory_space=pl.ANY)],
        out_specs=pl.BlockSpec((ctx_len, qw), lambda b, i, p: (cb0 + b, p)),
        out_shape=jax.ShapeDtypeStruct((qkv.shape[0], d), BF16),
        input_output_aliases={4: 0},
        compiler_params=_cparams(3),
        name="ctx_attn",
    )(sinks, qkv, qkv, qkv, lat)


def _nattn_kernel(q_ref, k0, k1, k2, v0, v1, v2, kc_ref, vc_ref, bias_ref, o_ref, ks_ref, vs_ref,
                  *, grid_rows, rows_per_blk):
    j = pl.program_id(2)
    blk = k0.shape[0]
    for t, (kr, vr) in enumerate(((k0, v0), (k1, v1), (k2, v2))):
        ks_ref[t * blk:(t + 1) * blk, :] = kr[...]
        vs_ref[t * blk:(t + 1) * blk, :] = vr[...]
    kc = kc_ref[...]
    vc = vc_ref[...]
    lane_lo = lax.broadcasted_iota(jnp.int32, (GRID_W, LANES), 1) < HEAD_DIM
    strip = NA_WIN_ROWS * GRID_W
    for r in range(rows_per_blk):
        row = j * rows_per_blk + r
        r0 = jnp.clip(row - NA_WIN_ROWS // 2, 0, grid_rows - NA_WIN_ROWS)
        start = pl.multiple_of((r0 - (j - 1) * rows_per_blk) * GRID_W, GRID_W)
        cls = row - r0
        qa, qb = _masked_halves(q_ref[r * GRID_W:(r + 1) * GRID_W, :], lane_lo)
        qs = jnp.concatenate([qa, qb], axis=0)
        kn = ks_ref[pl.ds(start, strip), :]
        vn = vs_ref[pl.ds(start, strip), :]
        s_nb = _nt_dot(qs, kn) + bias_ref[cls]
        s_cx = _nt_dot(qs, kc)
        o = _softmax_pv([s_nb, s_cx], [vn, vc], None)
        o_ref[r * GRID_W:(r + 1) * GRID_W, :] = jnp.where(lane_lo, o[:GRID_W], o[GRID_W:]).astype(o_ref.dtype)


def _na_bias_table(rpb):
    h = rpb.shape[0]
    n_dcol = 2 * NA_WIN_COLS - 1
    cols = jnp.arange(GRID_W)
    col_start = jnp.clip(cols - NA_WIN_COLS // 2, 0, GRID_W - NA_WIN_COLS)
    inwin = (cols[None, :] >= col_start[:, None]) & (cols[None, :] < col_start[:, None] + NA_WIN_COLS)
    dcol = cols[None, :] - cols[:, None] + NA_WIN_COLS - 1
    pick = (dcol[None] == jnp.arange(n_dcol)[:, None, None]).astype(F32).reshape(n_dcol, -1)
    spread = jnp.dot(rpb.astype(F32).reshape(-1, n_dcol), pick, precision=lax.Precision.HIGHEST)
    spread = spread.reshape(h, 2 * NA_WIN_ROWS - 1, GRID_W, GRID_W)
    spread = jnp.where(inwin[None, None], spread, NEG_INF)
    tab = jnp.stack([spread[:, NA_WIN_ROWS - 1 - cls:2 * NA_WIN_ROWS - 1 - cls]
                     for cls in range(NA_WIN_ROWS)], axis=1)
    tab = tab.transpose(0, 1, 3, 2, 4).reshape(h, NA_WIN_ROWS, GRID_W, NA_WIN_ROWS * GRID_W)
    tab = tab.reshape(h // 2, 2, NA_WIN_ROWS, GRID_W, NA_WIN_ROWS * GRID_W).transpose(0, 2, 1, 3, 4)
    return tab.reshape(h // 2, NA_WIN_ROWS, 2 * GRID_W, NA_WIN_ROWS * GRID_W)


def neighbourhood_attention(qkv, rpb, *, batch, seq_len, ctx_len, d, with_ctx_out):
    n_pairs = d // LANES
    rows_per_blk = NA_WIN_ROWS
    blk = rows_per_blk * GRID_W
    grid_rows = seq_len // GRID_W
    nblk = seq_len // blk
    cb0 = batch * seq_len // ctx_len
    bias = _na_bias_table(rpb)

    def kspec(col0, shift):
        return pl.BlockSpec((blk, LANES),
                            lambda b, p, j: (b * nblk + jnp.clip(j + shift, 0, nblk - 1), col0 + p))

    ctx_k = pl.BlockSpec((ctx_len, LANES), lambda b, p, j: (cb0 + b, n_pairs + p))
    ctx_v = pl.BlockSpec((ctx_len, LANES), lambda b, p, j: (cb0 + b, 2 * n_pairs + p))
    lat = pl.pallas_call(
        functools.partial(_nattn_kernel, grid_rows=grid_rows, rows_per_blk=rows_per_blk),
        grid=(batch, n_pairs, nblk),
        in_specs=[pl.BlockSpec((blk, LANES), lambda b, p, j: (b * nblk + j, p))]
        + [kspec(n_pairs, s) for s in (-1, 0, 1)] + [kspec(2 * n_pairs, s) for s in (-1, 0, 1)]
        + [ctx_k, ctx_v,
           pl.BlockSpec((None, NA_WIN_ROWS, 2 * GRID_W, NA_WIN_ROWS * GRID_W), lambda b, p, j: (p, 0, 0, 0))],
        out_specs=pl.BlockSpec((blk, LANES), lambda b, p, j: (b * nblk + j, p)),
        out_shape=jax.ShapeDtypeStruct((batch * seq_len, d), BF16),
        scratch_shapes=[pltpu.VMEM((3 * blk, LANES), BF16), pltpu.VMEM((3 * blk, LANES), BF16)],
        compiler_params=_cparams(3),
        name="neighbourhood_attn",
    )(qkv, *([qkv] * 8), bias)
    if not with_ctx_out:
        return lat
    ctx = pl.pallas_call(
        _cattn_kernel,
        grid=(batch, n_pairs),
        in_specs=[pl.BlockSpec((ctx_len, LANES), lambda b, p: (cb0 + b, p)),
                  pl.BlockSpec((ctx_len, LANES), lambda b, p: (cb0 + b, n_pairs + p)),
                  pl.BlockSpec((ctx_len, LANES), lambda b, p: (cb0 + b, 2 * n_pairs + p))],
        out_specs=pl.BlockSpec((ctx_len, LANES), lambda b, p: (b, p)),
        out_shape=jax.ShapeDtypeStruct((batch * ctx_len, d), BF16),
        compiler_params=_cparams(2),
        name="ctx_mha",
    )(qkv, qkv, qkv)
    return jnp.concatenate([lat, ctx], axis=0)


def _cattn_kernel(q_ref, k_ref, v_ref, o_ref):
    rows = q_ref.shape[0]
    lane_lo = lax.broadcasted_iota(jnp.int32, (rows, LANES), 1) < HEAD_DIM
    qa, qb = _masked_halves(q_ref[...], lane_lo)
    qs = jnp.concatenate([qa, qb], axis=0)
    o = _softmax_pv([_nt_dot(qs, k_ref[...])], [v_ref[...]], None)
    o_ref[...] = jnp.where(lane_lo, o[:rows], o[rows:]).astype(o_ref.dtype)


def _router_kernel(x_ref, g_ref, sh_ref, sc_ref, wh_ref, wl_ref, b_ref, h_ref, idx_ref, gate_ref, cnt_ref,
                   run_ref):
    @pl.when(pl.program_id(0) == 0)
    def _():
        run_ref[...] = jnp.zeros_like(run_ref)

    h = _norm_mod(x_ref[...], g_ref[...], sh_ref[0], sc_ref[0])
    h_hi = h.astype(BF16)
    h_ref[...] = h_hi
    h_lo = (h - h_hi.astype(F32)).astype(BF16)
    logits = (jnp.dot(h_hi, wh_ref[...], preferred_element_type=F32)
              + jnp.dot(h_hi, wl_ref[...], preferred_element_type=F32)
              + jnp.dot(h_lo, wh_ref[...], preferred_element_type=F32)) + b_ref[...]
    lane = lax.broadcasted_iota(jnp.int32, logits.shape, 1)
    tm = logits.shape[0]
    idx_out = jnp.zeros(logits.shape, jnp.int32)
    val_out = jnp.zeros(logits.shape, F32)
    tri = jnp.where(lax.broadcasted_iota(jnp.int32, (tm, tm), 0) > lax.broadcasted_iota(jnp.int32, (tm, tm), 1),
                    1.0, 0.0).astype(BF16)
    run = run_ref[...]
    top = None
    denom = 0.0
    for k in range(TOP_K):
        m = logits.max(axis=-1, keepdims=True)
        sel = jnp.min(jnp.where(logits == m, lane, LANES), axis=-1, keepdims=True)
        if top is None:
            top = m
        e = jnp.exp(m - top)
        denom = denom + e
        onehot = lane == sel
        before = jnp.dot(tri, jnp.where(onehot, 1.0, 0.0).astype(BF16), preferred_element_type=F32)
        rank = jnp.sum(jnp.where(onehot, before + run, 0.0), axis=-1, keepdims=True).astype(jnp.int32)
        run = run + jnp.sum(jnp.where(onehot, 1.0, 0.0), axis=0, keepdims=True)
        idx_out = jnp.where(lane == k, sel, idx_out)
        idx_out = jnp.where(lane == TOP_K + k, rank, idx_out)
        val_out = jnp.where(lane == k, e, val_out)
        logits = jnp.where(onehot, -jnp.inf, logits)
    run_ref[...] = run
    cnt_ref[...] = run
    idx_ref[...] = idx_out
    gate_ref[...] = val_out / denom


def router(x, g, shift, scale, w_router, b_router, *, seg_rows, rows, tm=ROW_TILE):
    m, d = rows, x.shape[1]
    n_seg = shift.shape[0]
    n_e = w_router.shape[1]
    w_pad = jnp.zeros((d, LANES), F32).at[:, :n_e].set(w_router)
    w_hi = w_pad.astype(BF16)
    w_lo = (w_pad - w_hi.astype(F32)).astype(BF16)
    b_pad = jnp.full((1, LANES), -jnp.inf, F32).at[0, :n_e].set(b_router)
    seg = lambda i: (jnp.minimum(i * tm // seg_rows, n_seg - 1), 0, 0)
    return pl.pallas_call(
        _router_kernel,
        grid=(m // tm,),
        in_specs=[pl.BlockSpec((tm, d), lambda i: (i, 0)),
                  pl.BlockSpec((1, d), lambda i: (0, 0)),
                  pl.BlockSpec((1, 1, d), seg),
                  pl.BlockSpec((1, 1, d), seg),
                  pl.BlockSpec((d, LANES), lambda i: (0, 0)),
                  pl.BlockSpec((d, LANES), lambda i: (0, 0)),
                  pl.BlockSpec((1, LANES), lambda i: (0, 0))],
        out_specs=[pl.BlockSpec((tm, d), lambda i: (i, 0)),
                   pl.BlockSpec((tm, LANES), lambda i: (i, 0)),
                   pl.BlockSpec((tm, LANES), lambda i: (i, 0)),
                   pl.BlockSpec((1, LANES), lambda i: (0, 0))],
        out_shape=[jax.ShapeDtypeStruct((m, d), BF16),
                   jax.ShapeDtypeStruct((m, LANES), jnp.int32),
                   jax.ShapeDtypeStruct((m, LANES), F32),
                   jax.ShapeDtypeStruct((1, LANES), F32)],
        scratch_shapes=[pltpu.VMEM((1, LANES), F32)],
        compiler_params=_cparams(1),
        name="router",
    )(x, g, shift, scale, w_hi, w_lo, b_pad)


def _expert_kernel(blk_e_ref, n_used_ref, x_ref, wgu_ref, bgu_ref, wd_ref, bd_ref, o_ref, wgu_s, wd_s, *, f):
    i = pl.program_id(0)
    used = i < n_used_ref[0]

    @pl.when(used & ((i == 0) | (blk_e_ref[i] != blk_e_ref[jnp.maximum(i - 1, 0)])))
    def _():
        wgu_s[...] = wgu_ref[0].astype(BF16)
        wd_s[...] = wd_ref[0].astype(BF16)

    @pl.when(used)
    def _():
        gu = jnp.dot(x_ref[...], wgu_s[...], preferred_element_type=F32) + bgu_ref[0]
        g = jnp.minimum(gu[:, :f], SWIGLU_LIMIT)
        u = jnp.clip(gu[:, f:], -SWIGLU_LIMIT, SWIGLU_LIMIT)
        act = (u + 1) * (g * jax.nn.sigmoid(SWIGLU_ALPHA * g))
        y = jnp.dot(act.astype(BF16), wd_s[...], preferred_element_type=F32) + bd_ref[0]
        o_ref[...] = y.astype(o_ref.dtype)

    @pl.when(i >= n_used_ref[0])
    def _():
        o_ref[...] = jnp.zeros_like(o_ref)


def expert_mlp(xs, blk_e, n_used, w_gu, b_gu, w_down, b_down, *, bm=MOE_ROWS):
    n_slot, d = xs.shape
    n_e, _, f2 = w_gu.shape
    f = f2 // 2
    return pl.pallas_call(
        functools.partial(_expert_kernel, f=f),
        grid_spec=pltpu.PrefetchScalarGridSpec(
            num_scalar_prefetch=2,
            grid=(n_slot // bm,),
            in_specs=[pl.BlockSpec((bm, d), lambda i, be, nu: (i, 0)),
                      pl.BlockSpec((1, d, f2), lambda i, be, nu: (be[i], 0, 0)),
                      pl.BlockSpec((1, 1, f2), lambda i, be, nu: (be[i], 0, 0)),
                      pl.BlockSpec((1, f, d), lambda i, be, nu: (be[i], 0, 0),
                                   pipeline_mode=pl.Buffered(1)),
                      pl.BlockSpec((1, 1, d), lambda i, be, nu: (be[i], 0, 0))],
            out_specs=pl.BlockSpec((bm, d), lambda i, be, nu: (i, 0)),
            scratch_shapes=[pltpu.VMEM((d, f2), BF16), pltpu.VMEM((f, d), BF16)]),
        out_shape=jax.ShapeDtypeStruct((n_slot, d), BF16),
        compiler_params=pltpu.CompilerParams(dimension_semantics=("arbitrary",),
                                             vmem_limit_bytes=EXPERT_VMEM_LIMIT),
        name="expert_mlp",
    )(blk_e, n_used, xs, w_gu, b_gu.reshape(n_e, 1, f2), w_down, b_down.reshape(n_e, 1, d))


def _combine_kernel(res_ref, y_ref, w_ref, gate_ref, o_ref):
    d = o_ref.shape[1]
    w = w_ref[...]
    y = w[:, 0:1] * y_ref[:, 0:d].astype(F32)
    for k in range(1, TOP_K):
        y = y + w[:, k:k + 1] * y_ref[:, k * d:(k + 1) * d].astype(F32)
    o_ref[...] = res_ref[...] + gate_ref[0] * y


def moe_layer(x, g, shift, scale, gate, w_router, b_router, w_gu, b_gu, w_down, b_down, *, seg_rows, rows,
              bm=MOE_ROWS):
    m, d = rows, x.shape[1]
    n_e = w_router.shape[1]
    h, route, gates, counts = router(x, g, shift, scale, w_router, b_router, seg_rows=seg_rows, rows=rows)
    idx = route[:, :TOP_K]
    rank = route[:, TOP_K:2 * TOP_K]
    n_asg = m * TOP_K
    counts = counts[0, :n_e].astype(jnp.int32)
    padded = (counts + bm - 1) // bm * bm
    pad_end = jnp.cumsum(padded)
    pad_start = pad_end - padded
    start_of = jnp.sum(jnp.where(idx[:, :, None] == jnp.arange(n_e, dtype=jnp.int32), pad_start, 0), axis=-1)
    dest = (start_of + rank).reshape(-1)
    n_blk = -(-(n_asg + n_e * (bm - 1)) // bm)
    n_slot = n_blk * bm
    tok = (jnp.arange(n_asg, dtype=jnp.int32) // TOP_K)
    slot_tok = jnp.zeros((n_slot,), jnp.int32).at[dest].set(tok, unique_indices=True)
    blk_start = jnp.arange(n_blk, dtype=jnp.int32) * bm
    blk_e = jnp.minimum(jnp.sum((pad_end[None, :] <= blk_start[:, None]).astype(jnp.int32), axis=1), n_e - 1)
    n_used = (pad_end[-1] // bm).astype(jnp.int32).reshape(1)
    xs = h.at[slot_tok].get(mode="promise_in_bounds")
    y_slot = expert_mlp(xs, blk_e, n_used, w_gu, b_gu, w_down, b_down, bm=bm)
    y_tok = y_slot.at[dest].get(mode="promise_in_bounds").reshape(m, TOP_K * d)
    tm = 256
    n_seg = gate.shape[0]
    seg = lambda i: (jnp.minimum(i * tm // seg_rows, n_seg - 1), 0, 0)
    return pl.pallas_call(
        _combine_kernel,
        grid=(m // tm,),
        in_specs=[pl.BlockSpec((tm, d), lambda i: (i, 0)),
                  pl.BlockSpec((tm, TOP_K * d), lambda i: (i, 0)),
                  pl.BlockSpec((tm, LANES), lambda i: (i, 0)),
                  pl.BlockSpec((1, 1, d), seg)],
        out_specs=pl.BlockSpec((tm, d), lambda i: (i, 0)),
        out_shape=jax.ShapeDtypeStruct(x.shape, F32),
        input_output_aliases={0: 0},
        compiler_params=_cparams(1),
        name="moe_combine",
    )(x, y_tok, gates, gate)


LC_R2 = 128
LC_C = 256
LC_KB = 8
LC_LANES = 2048


@functools.lru_cache(maxsize=None)
def _lc_mats(n):
    nn = 2 * n
    r1 = nn // LC_R2
    tau = 2.0 * np.pi
    k1 = np.arange(r1)
    ang1 = tau * ((k1[:, None] * np.arange(r1)[None, :]) % r1) / r1
    f1 = np.empty((2 * r1, r1))
    f1[0::2] = np.cos(ang1)
    f1[1::2] = -np.sin(ang1)
    kk = k1[:, None] + r1 * np.arange(LC_R2)[None, :]
    th = tau * ((kk[:, :, None] * np.arange(LC_R2)[None, None, :]) % nn) / nn
    c, s = np.cos(th), np.sin(th)
    f2 = np.concatenate([np.concatenate([c, s], 2), np.concatenate([-s, c], 2)], 1)
    ct, st = c.transpose(0, 2, 1), s.transpose(0, 2, 1)
    f3 = np.concatenate([np.concatenate([ct, -st], 2), np.concatenate([st, ct], 2)], 1)
    ang4 = tau * ((np.arange(r1 // 2)[:, None] * k1[None, :]) % r1) / r1
    f4 = np.concatenate([np.cos(ang4), -np.sin(ang4)], 1) / nn
    return tuple(jnp.asarray(a, BF16) for a in (f1, f2, f3, f4))


@functools.lru_cache(maxsize=None)
def _dft_small_mats(n):
    nn = 2 * n
    ang = 2.0 * np.pi * ((np.arange(nn)[:, None] * np.arange(nn)[None, :]) % nn) / nn
    fwd = np.concatenate([np.cos(ang), -np.sin(ang)], 0)
    inv = np.concatenate([np.cos(ang[:n]), -np.sin(ang[:n])], 1) / nn
    return jnp.asarray(fwd, BF16), jnp.asarray(inv, BF16)


def _lc_first_kernel(x_ref, f_ref, o_ref):
    o_ref[...] = jnp.dot(f_ref[...], x_ref[...].astype(BF16), preferred_element_type=F32).astype(o_ref.dtype)


def lc_first(x, f1, s1, off=0):
    _, s2, nh, l = x.shape
    lc = min(LC_LANES, l)
    r2 = f1.shape[0]
    return pl.pallas_call(
        _lc_first_kernel,
        grid=(s1, s2, l // lc),
        in_specs=[pl.BlockSpec((None, None, nh, lc), lambda a, b, j: (off + a, b, 0, j)),
                  pl.BlockSpec((r2, nh), lambda a, b, j: (0, 0))],
        out_specs=pl.BlockSpec((None, None, r2, lc), lambda a, b, j: (a, b, 0, j)),
        out_shape=jax.ShapeDtypeStruct((s1, s2, r2, l), BF16),
        compiler_params=_cparams(3),
        name="lc_first",
    )(x, f1[:, :nh])


def _lc_mid_kernel(a_ref, f2_ref, *rest, kb, spectrum_only):
    if spectrum_only:
        (o_ref,) = rest
    else:
        h_ref, f3_ref, o_ref = rest
    c = a_ref.shape[-1]
    for kk in range(kb):
        a_in = a_ref[2 * kk:2 * kk + 2].reshape(2 * LC_R2, c)
        x = jnp.dot(f2_ref[kk], a_in, preferred_element_type=F32)
        if spectrum_only:
            o_ref[kk] = x.astype(o_ref.dtype)
            continue
        h = h_ref[kk].astype(F32)
        xr, xi = x[:LC_R2], x[LC_R2:]
        hr, hi = h[:LC_R2], h[LC_R2:]
        prod = jnp.concatenate([xr * hr - xi * hi, xr * hi + xi * hr], axis=0).astype(BF16)
        g = jnp.dot(f3_ref[kk], prod, preferred_element_type=F32)
        o_ref[0, kk] = g[:LC_R2].astype(o_ref.dtype)
        o_ref[1, kk] = g[LC_R2:].astype(o_ref.dtype)


def lc_spectrum(a, f2):
    cb, _, r2, _, c = a.shape
    r1 = r2 // 2
    kb = min(LC_KB, r1)
    return pl.pallas_call(
        functools.partial(_lc_mid_kernel, kb=kb, spectrum_only=True),
        grid=(r1 // kb, cb),
        in_specs=[pl.BlockSpec((None, None, 2 * kb, LC_R2, c), lambda j, i: (i, 0, j, 0, 0)),
                  pl.BlockSpec((kb, 2 * LC_R2, 2 * LC_R2), lambda j, i: (j, 0, 0))],
        out_specs=pl.BlockSpec((None, kb, 2 * LC_R2, c), lambda j, i: (i, j, 0, 0)),
        out_shape=jax.ShapeDtypeStruct((cb, r1, 2 * LC_R2, c), BF16),
        compiler_params=_cparams(2),
        name="lc_spectrum",
    )(a, f2)


def lc_mid(a, h, f2, f3):
    cb, bsz, r2, _, c = a.shape
    r1 = r2 // 2
    kb = min(LC_KB, r1)
    mat = pl.BlockSpec((kb, 2 * LC_R2, 2 * LC_R2), lambda j, i, b: (j, 0, 0))
    return pl.pallas_call(
        functools.partial(_lc_mid_kernel, kb=kb, spectrum_only=False),
        grid=(r1 // kb, cb, bsz),
        in_specs=[pl.BlockSpec((None, None, 2 * kb, LC_R2, c), lambda j, i, b: (i, b, j, 0, 0)),
                  mat,
                  pl.BlockSpec((None, kb, 2 * LC_R2, c), lambda j, i, b: (i, j, 0, 0)),
                  mat],
        out_specs=pl.BlockSpec((None, None, 2, kb, LC_R2, c), lambda j, i, b: (i, b, 0, j, 0, 0)),
        out_shape=jax.ShapeDtypeStruct((cb, bsz, 2, r1, LC_R2, c), BF16),
        compiler_params=_cparams(3),
        name="lc_mid",
    )(a, f2, h, f3)


def _lc_last_kernel(g_ref, f_ref, u_ref, gate_ref, skip_ref, o_ref):
    y = jnp.dot(f_ref[...], g_ref[...], preferred_element_type=F32)
    u = u_ref[...].astype(F32)
    o_ref[...] = (gate_ref[...].astype(F32) * (y + skip_ref[...] * u)).astype(o_ref.dtype)


def lc_last(g, f4, u_arr, u_off, gate_arr, gate_off, skip_row):
    cb, bsz, r2, l = g.shape
    nh = r2 // 4
    lc = min(LC_LANES, l)
    return pl.pallas_call(
        _lc_last_kernel,
        grid=(cb, bsz, l // lc),
        in_specs=[pl.BlockSpec((None, None, r2, lc), lambda i, b, j: (i, b, 0, j)),
                  pl.BlockSpec((nh, r2), lambda i, b, j: (0, 0)),
                  pl.BlockSpec((None, None, nh, lc), lambda i, b, j: (u_off + i, b, 0, j)),
                  pl.BlockSpec((None, None, nh, lc), lambda i, b, j: (gate_off + i, b, 0, j)),
                  pl.BlockSpec((None, 1, lc), lambda i, b, j: (i, 0, j))],
        out_specs=pl.BlockSpec((None, None, nh, lc), lambda i, b, j: (i, b, 0, j)),
        out_shape=jax.ShapeDtypeStruct((cb, bsz, nh, l), BF16),
        compiler_params=_cparams(3),
        name="lc_last",
    )(g, f4, u_arr, gate_arr, skip_row)


def _plain_mm_kernel(a_ref, b_ref, o_ref):
    o_ref[...] = jnp.dot(a_ref[...], b_ref[...].astype(BF16), preferred_element_type=F32)


def plain_mm(a, b, tn=512):
    m, k = a.shape
    n = b.shape[1]
    tn = _col_tile(n, tn)
    return pl.pallas_call(
        _plain_mm_kernel,
        grid=(n // tn,),
        in_specs=[pl.BlockSpec((m, k), lambda j: (0, 0)), pl.BlockSpec((k, tn), lambda j: (0, j))],
        out_specs=pl.BlockSpec((m, tn), lambda j: (0, j)),
        out_shape=jax.ShapeDtypeStruct((m, n), F32),
        compiler_params=_cparams(1),
        name="plain_mm",
    )(a, b)


def _short_conv_kernel(u_ref, gate_ref, h_ref, skip_ref, fwd_ref, inv_ref, o_ref):
    u = u_ref[...]
    x = jnp.dot(fwd_ref[...], u.astype(BF16), preferred_element_type=F32)
    h = h_ref[...]
    nn = x.shape[0] // 2
    xr, xi, hr, hi = x[:nn], x[nn:], h[:nn], h[nn:]
    prod = jnp.concatenate([xr * hr - xi * hi, xr * hi + xi * hr], axis=0).astype(BF16)
    y = jnp.dot(inv_ref[...], prod, preferred_element_type=F32)
    o_ref[...] = (gate_ref[...].astype(F32) * (y + skip_ref[...] * u.astype(F32))).astype(o_ref.dtype)


def short_long_conv(u_arr, u_col, gate_arr, gate_col, spec, skip, fwd, inv, *, batch, seq_len, tl=512):
    d = spec.shape[1]
    tl = _col_tile(d, tl)
    uc, gc = u_col // tl, gate_col // tl
    nn2 = spec.shape[0]
    return pl.pallas_call(
        _short_conv_kernel,
        grid=(batch, d // tl),
        in_specs=[pl.BlockSpec((seq_len, tl), lambda b, j: (b, uc + j)),
                  pl.BlockSpec((seq_len, tl), lambda b, j: (b, gc + j)),
                  pl.BlockSpec((nn2, tl), lambda b, j: (0, j)),
                  pl.BlockSpec((1, tl), lambda b, j: (0, j)),
                  pl.BlockSpec((nn2, seq_len), lambda b, j: (0, 0)),
                  pl.BlockSpec((seq_len, nn2), lambda b, j: (0, 0))],
        out_specs=pl.BlockSpec((seq_len, tl), lambda b, j: (b, j)),
        out_shape=jax.ShapeDtypeStruct((batch * seq_len, d), BF16),
        compiler_params=_cparams(2),
        name="short_long_conv",
    )(u_arr, gate_arr, spec, skip, fwd[:, :seq_len], inv)


def _hyena_taps(n, d, f_w1, f_b1, f_w2, f_b2, f_w3, f_b3, f_freq, f_w4):
    hp = lax.Precision.HIGHEST
    lin = jnp.linspace(0.0, 1.0, n, dtype=F32)
    idx = jnp.arange(n, dtype=F32)
    bands = jnp.linspace(1e-4, HY_BANDS - 1, HY_BANDS, dtype=F32)[None, :]
    deltas = jnp.abs(jnp.linspace(math.log(HY_DECAY_TARGET) / HY_FAST_DECAY,
                                  math.log(HY_DECAY_TARGET) / HY_SLOW_DECAY, d, dtype=F32))

    def branch(t, pos, direction):
        t = t[:, None]
        ang = (2.0 * math.pi / n) * pos[:, None]
        emb = jnp.concatenate([t, jnp.cos(bands * ang), -jnp.sin(bands * ang)], axis=-1)
        a = jnp.sin(f_freq * (jnp.dot(emb, f_w1, precision=hp) + f_b1))
        a = jnp.sin(f_freq * (jnp.dot(a, f_w2, precision=hp) + f_b2))
        a = jnp.sin(f_freq * (jnp.dot(a, f_w3, precision=hp) + f_b3))
        decay = jnp.exp(-t * deltas[None, :])
        w4 = f_w4.reshape(f_w4.shape[0], HY_ORDER, 2, d)[:, :, direction]
        return [jnp.dot(a, w4[:, o], precision=hp) * decay for o in range(HY_ORDER)]

    fwd = branch(lin, idx, 0)
    bwd = branch(lin[:0:-1], idx[:0:-1], 1)
    zero = jnp.zeros((1, d), F32)
    return [jnp.concatenate([fwd[o], zero, bwd[o]], axis=0) for o in range(HY_ORDER)]


def hyena_long_convs(zc, taps, skip, *, batch, seq_len):
    cb3, rows, c = zc.shape
    cb = cb3 // 3
    d = cb * c
    nh = seq_len // LC_R2
    lanes = LC_R2 * c
    f1, f2, f3, f4 = _lc_mats(seq_len)
    zc_v = zc.reshape(cb3, batch, nh, lanes)
    u, u_off = zc_v, 2 * cb
    for o in range(HY_ORDER):
        taps_b = taps[o].reshape(2 * seq_len, cb, c).transpose(1, 0, 2).reshape(cb, 1, 2 * nh, lanes)
        spec = lc_spectrum(lc_first(taps_b, f1, cb).reshape(cb, 1, 4 * nh, LC_R2, c), f2)
        a = lc_first(u, f1, cb, u_off).reshape(cb, batch, 4 * nh, LC_R2, c)
        g = lc_mid(a, spec, f2, f3).reshape(cb, batch, 4 * nh, lanes)
        skip_row = jnp.tile(skip[o].reshape(cb, 1, c), (1, 1, LC_R2))
        u = lc_last(g, f4, u, u_off, zc_v, o * cb, skip_row)
        u_off = 0
    return u.reshape(cb, rows, c)


def _final_norm_kernel(x_ref, g_ref, o_ref):
    x = x_ref[...]
    o_ref[...] = x * lax.rsqrt(jnp.mean(x * x, axis=-1, keepdims=True) + NORM_EPS) * g_ref[...]


def final_norm(x, g, *, rows, tm=ROW_TILE):
    d = x.shape[1]
    return pl.pallas_call(
        _final_norm_kernel,
        grid=(rows // tm,),
        in_specs=[pl.BlockSpec((tm, d), lambda i: (i, 0)), pl.BlockSpec((1, d), lambda i: (0, 0))],
        out_specs=pl.BlockSpec((tm, d), lambda i: (i, 0)),
        out_shape=jax.ShapeDtypeStruct((rows, d), F32),
        compiler_params=_cparams(1),
        name="final_norm",
    )(x, g)


def _rope_tables(n_lat, seq_len, n_rows):
    t = jnp.arange(n_rows, dtype=jnp.int32)
    row = ((t % seq_len) // GRID_W).astype(F32)
    col = (t % GRID_W).astype(F32)
    lane = jnp.arange(LANES)
    dd = lane % HEAD_DIM
    quarter = HEAD_DIM // 4
    inv_freq = ROPE_BASE ** (-(dd % quarter).astype(F32) / quarter)
    pos = jnp.where(dd[None, :] < HEAD_DIM // 2, row[:, None], col[:, None])
    ang = pos * inv_freq[None, :]
    sign = jnp.where((dd % (HEAD_DIM // 2)) < quarter, -1.0, 1.0).astype(F32)
    lat = (t < n_lat)[:, None]
    cos = jnp.where(lat, jnp.cos(ang), 1.0)
    sin = jnp.where(lat, jnp.sin(ang) * sign[None, :], 0.0)
    return cos, sin


def _wa_head_perm(n_heads):
    order = []
    for p in range(n_heads // (2 * WA_GROUP)):
        for g in range(WA_GROUP):
            order += [2 * WA_GROUP * p + g, 2 * WA_GROUP * p + WA_GROUP + g]
    cols = jnp.asarray(order, jnp.int32)[:, None] * HEAD_DIM + jnp.arange(HEAD_DIM, dtype=jnp.int32)[None, :]
    return cols.reshape(-1)


def kernel(x, c, ctx, c_ctx, w_ada, b_ada, g_mix, g_ffn, hy_w_in, hy_b_in, hy_w_short, hy_b_short, hy_f_w1, hy_f_b1, hy_f_w2, hy_f_b2, hy_f_w3, hy_f_b3, hy_f_freq, hy_f_w4, hy_skip, hy_w_out, hy_b_out, cf_w_pw1, cf_b_pw1, cf_w_dw, cf_b_dw, cf_ln_g, cf_ln_b, cf_w_pw2, cf_b_pw2, wa_w_qkv, wa_w_o, wa_sinks, na_w_qkv, na_w_o, na_rpb, moe_w_router, moe_b_router, moe_w_gu, moe_b_gu, moe_w_down, moe_b_down, g_final):
    bsz, n, d = x.shape
    n_ctx = ctx.shape[1]
    depth = w_ada.shape[0]
    n_lat = bsz * n
    m_all = n_lat + bsz * n_ctx

    cond = jnp.zeros((8, d), F32).at[:bsz].set(jax.nn.silu(c)).at[bsz].set(jax.nn.silu(c_ctx))
    mods = adaln(cond, w_ada, b_ada)
    xu = jnp.concatenate([x.reshape(n_lat, d), ctx.reshape(bsz * n_ctx, d)], axis=0)
    zeros_d = jnp.zeros((1, d), F32)

    for i in range(depth):
        kind, j = i % N_MIXERS, i // N_MIXERS
        last = i == depth - 1
        mod = [mods[i, :bsz + 1, k * d:(k + 1) * d].reshape(bsz + 1, 1, d) for k in range(6)]
        m_out = n_lat if last else m_all
        gm = g_mix[i].reshape(1, d)
        need_ctx_in = (not last) or kind >= 2
        m_in = m_all if need_ctx_in else n_lat
        x_in = xu[:m_in]

        if kind == 0:
            z = nm_matmul(x_in, gm, mod[0], mod[1], hy_w_in[j].astype(BF16), hy_b_in[j].reshape(1, -1),
                          seg_rows=n)
            filt_w = (hy_f_w1[j], hy_f_b1[j], hy_f_w2[j], hy_f_b2[j], hy_f_w3[j], hy_f_b3[j], hy_f_freq[j],
                      hy_f_w4[j])
            w_s, b_s = hy_w_short[j], hy_b_short[j].reshape(1, -1)
            w_o, b_o = hy_w_out[j].astype(BF16), hy_b_out[j].reshape(1, d)
            zc = dwconv(z, w_s, b_s, start=0, total=n_lat, seq_len=n, tc=min(LC_C, d), blocked=True)
            y = hyena_long_convs(zc, _hyena_taps(n, d, *filt_w), hy_skip[j], batch=bsz, seq_len=n)
            xu = mm_res(y, w_o, b_o, xu, mod[2], seg_rows=n)
            if m_in > n_lat:
                zcc = dwconv(z, w_s, b_s, start=n_lat, total=m_in - n_lat, seq_len=n_ctx, tc=512)
                taps_c = _hyena_taps(n_ctx, d, *filt_w)
                fwd, inv = _dft_small_mats(n_ctx)
                yc = short_long_conv(zcc, 2 * d, zcc, 0, plain_mm(fwd, taps_c[0]), hy_skip[j][0].reshape(1, d),
                                     fwd, inv, batch=bsz, seq_len=n_ctx)
                yc = short_long_conv(yc, 0, zcc, d, plain_mm(fwd, taps_c[1]), hy_skip[j][1].reshape(1, d),
                                     fwd, inv, batch=bsz, seq_len=n_ctx)
                if not last:
                    xu = mm_res(yc, w_o, b_o, xu, mod[2], seg_rows=n, row_off=n_lat)
            y = None
        elif kind == 1:
            w1 = cf_w_pw1[j].astype(BF16).reshape(d, 2, d).transpose(1, 0, 2)
            a = nm_matmul(x_in, gm, mod[0], mod[1], w1, cf_b_pw1[j].reshape(2, 1, d), seg_rows=n, mode="glu")
            conv = functools.partial(dwconv, a, cf_w_dw[j], cf_b_dw[j].reshape(1, d), post="ln_silu",
                                     ln=(cf_ln_g[j].reshape(1, d), cf_ln_b[j].reshape(1, d)), rows=64)
            y = conv(start=0, total=n_lat, seq_len=n, same_rows=True)
            if m_in > n_lat:
                y = conv(start=n_lat, total=m_in - n_lat, seq_len=n_ctx, same_rows=True, into=y)
            w_o, b_o = cf_w_pw2[j].astype(BF16), cf_b_pw2[j].reshape(1, d)
        elif kind == 2:
            perm = _wa_head_perm(d // HEAD_DIM)
            scale = HEAD_DIM ** -0.5
            w_qkv = jnp.concatenate([wa_w_qkv[j][:, :d][:, perm] * scale, wa_w_qkv[j][:, d:]], axis=1).astype(BF16)
            n_out = w_qkv.shape[1]
            qkv = nm_matmul(x_in, gm, mod[0], mod[1], w_qkv, jnp.zeros((1, n_out), F32), seg_rows=n,
                            mode="rope", rope=_rope_tables(n_lat, n, m_in), n_rope_cols=d + (n_out - d) // 2)
            y = window_attention(qkv, wa_sinks[j], batch=bsz, seq_len=n, ctx_len=n_ctx, d=d)
            w_o, b_o = wa_w_o[j][perm].astype(BF16), zeros_d
        else:
            scale = HEAD_DIM ** -0.5
            w_qkv = jnp.concatenate([na_w_qkv[j][:, :d] * scale, na_w_qkv[j][:, d:]], axis=1).astype(BF16)
            qkv = nm_matmul(x_in, gm, mod[0], mod[1], w_qkv, jnp.zeros((1, 3 * d), F32), seg_rows=n)
            y = neighbourhood_attention(qkv, na_rpb[j], batch=bsz, seq_len=n, ctx_len=n_ctx, d=d,
                                        with_ctx_out=not last)
            w_o, b_o = na_w_o[j].astype(BF16), zeros_d
        if y is not None:
            xu = mm_res(y, w_o, b_o, xu, mod[2], seg_rows=n, rows=m_out)

        xu = moe_layer(xu, g_ffn[i].reshape(1, d), mod[3], mod[4], mod[5], moe_w_router[i], moe_b_router[i],
                       moe_w_gu[i], moe_b_gu[i], moe_w_down[i], moe_b_down[i], seg_rows=n, rows=m_out)
    return final_norm(xu, g_final.reshape(1, d), rows=n_lat).reshape(bsz, n, d)
```

```python
import functools
import math

import jax
import jax.numpy as jnp
import numpy as np
from jax import lax
from jax.experimental import pallas as pl
from jax.experimental.pallas import tpu as pltpu

F32 = jnp.float32
BF16 = jnp.bfloat16

GRID_W = 64
N_MIXERS = 4
NORM_EPS = 1e-6
NEG_INF = -1e30
HEAD_DIM = 64
ROPE_BASE = 10000.0

HY_ORDER = 2
HY_BANDS = 16
HY_DECAY_TARGET = 1e-2
HY_FAST_DECAY = 0.3
HY_SLOW_DECAY = 1.5

WA_GROUP = 4
WA_WINDOW = 128
NA_WIN_ROWS = 8
NA_WIN_COLS = 16

TOP_K = 4
SWIGLU_LIMIT = 7.0
SWIGLU_ALPHA = 1.702

LANES = 128
ROW_TILE = 512
MOE_ROWS = 256
VMEM_LIMIT = 56 * 1024 * 1024
EXPERT_VMEM_LIMIT = 58 * 1024 * 1024


def _cparams(n_axes):
    return pltpu.CompilerParams(dimension_semantics=("arbitrary",) * n_axes,
                                vmem_limit_bytes=VMEM_LIMIT)


def _col_tile(n, pref=1024):
    t = min(pref, n)
    while n % t:
        t //= 2
    return t


def _adaln_kernel(c_ref, w_ref, b_ref, o_ref):
    w = w_ref[0]
    w_hi = w.astype(BF16)
    w_lo = (w - w_hi.astype(F32)).astype(BF16)
    c = c_ref[...]
    c_hi = c.astype(BF16)
    c_lo = (c - c_hi.astype(F32)).astype(BF16)
    acc = jnp.dot(c_hi, w_hi, preferred_element_type=F32)
    acc += jnp.dot(c_hi, w_lo, preferred_element_type=F32)
    acc += jnp.dot(c_lo, w_hi, preferred_element_type=F32)
    o_ref[0] = acc + b_ref[0]


def adaln(cond, w_ada, b_ada):
    depth, d, n6 = w_ada.shape
    tn = _col_tile(n6, 1024)
    return pl.pallas_call(
        _adaln_kernel,
        grid=(depth, n6 // tn),
        in_specs=[pl.BlockSpec((8, d), lambda l, j: (0, 0)),
                  pl.BlockSpec((1, d, tn), lambda l, j: (l, 0, j)),
                  pl.BlockSpec((1, 1, tn), lambda l, j: (l, 0, j))],
        out_specs=pl.BlockSpec((1, 8, tn), lambda l, j: (l, 0, j)),
        out_shape=jax.ShapeDtypeStruct((depth, 8, n6), F32),
        compiler_params=_cparams(2),
        name="adaln",
    )(cond, w_ada, b_ada.reshape(depth, 1, n6))


def _norm_mod(x, g, shift, scale):
    y = x * lax.rsqrt(jnp.mean(x * x, axis=-1, keepdims=True) + NORM_EPS)
    return (y * g) * (1 + scale) + shift


def _rope_tile(x, cos, sin):
    lane = lax.broadcasted_iota(jnp.int32, x.shape, 1)
    nxt = pltpu.roll(x, LANES - 16, axis=1)
    prv = pltpu.roll(x, 16, axis=1)
    partner = jnp.where((lane // 16) % 2 == 0, nxt, prv)
    return x * cos + partner * sin


def _nm_mm_kernel(x_ref, g_ref, sh_ref, sc_ref, w_ref, b_ref, *rest, mode, n_rope):
    if mode == "rope":
        cos_ref, sin_ref, o_ref, h_ref = rest
    else:
        o_ref, h_ref = rest
    j = pl.program_id(1)

    @pl.when(j == 0)
    def _():
        h_ref[...] = _norm_mod(x_ref[...], g_ref[...], sh_ref[0], sc_ref[0]).astype(BF16)

    h = h_ref[...]
    if mode == "glu":
        a = jnp.dot(h, w_ref[0], preferred_element_type=F32) + b_ref[0]
        gate = jnp.dot(h, w_ref[1], preferred_element_type=F32) + b_ref[1]
        o_ref[...] = (a * jax.nn.sigmoid(gate)).astype(o_ref.dtype)
        return
    acc = jnp.dot(h, w_ref[...], preferred_element_type=F32) + b_ref[...]
    if mode == "rope":
        @pl.when(j < n_rope)
        def _():
            cos = cos_ref[...]
            sin = sin_ref[...]
            for g in range(acc.shape[1] // LANES):
                sl = slice(g * LANES, (g + 1) * LANES)
                o_ref[:, sl] = _rope_tile(acc[:, sl], cos, sin).astype(o_ref.dtype)

        @pl.when(j >= n_rope)
        def _():
            o_ref[...] = acc.astype(o_ref.dtype)
    else:
        o_ref[...] = acc.astype(o_ref.dtype)


def nm_matmul(x, g, shift, scale, w, b, *, seg_rows, mode="plain", rope=None, n_rope_cols=0,
              tm=ROW_TILE, tn=None):
    m, d = x.shape
    n_seg = shift.shape[0]
    n = w.shape[-1]
    tn = tn or (_col_tile(math.gcd(n, n_rope_cols), 512) if mode == "rope" else _col_tile(n, 1024))
    seg = lambda i, j: (jnp.minimum(i * tm // seg_rows, n_seg - 1), 0, 0)
    in_specs = [pl.BlockSpec((tm, d), lambda i, j: (i, 0)),
                pl.BlockSpec((1, d), lambda i, j: (0, 0)),
                pl.BlockSpec((1, 1, d), seg),
                pl.BlockSpec((1, 1, d), seg)]
    if mode == "glu":
        in_specs += [pl.BlockSpec((2, d, tn), lambda i, j: (0, 0, j)),
                     pl.BlockSpec((2, 1, tn), lambda i, j: (0, 0, j))]
    else:
        in_specs += [pl.BlockSpec((d, tn), lambda i, j: (0, j)),
                     pl.BlockSpec((1, tn), lambda i, j: (0, j))]
    args = [x, g, shift, scale, w, b]
    if mode == "rope":
        in_specs += [pl.BlockSpec((tm, LANES), lambda i, j: (i, 0))] * 2
        args += list(rope)
    return pl.pallas_call(
        functools.partial(_nm_mm_kernel, mode=mode, n_rope=n_rope_cols // tn),
        grid=(m // tm, n // tn),
        in_specs=in_specs,
        out_specs=pl.BlockSpec((tm, tn), lambda i, j: (i, j)),
        out_shape=jax.ShapeDtypeStruct((m, n), BF16),
        scratch_shapes=[pltpu.VMEM((tm, d), BF16)],
        compiler_params=_cparams(2),
        name="nm_matmul_" + mode,
    )(*args)


def _mm_res_kernel(a_ref, w_ref, b_ref, res_ref, gate_ref, o_ref):
    if len(a_ref.shape) == 3:
        kc = a_ref.shape[2]
        acc = b_ref[...]
        for cb in range(a_ref.shape[0]):
            acc = acc + jnp.dot(a_ref[cb], w_ref[cb * kc:(cb + 1) * kc, :], preferred_element_type=F32)
    else:
        acc = jnp.dot(a_ref[...], w_ref[...], preferred_element_type=F32) + b_ref[...]
    o_ref[...] = res_ref[...] + gate_ref[0] * acc


def mm_res(a, w, b, res, gate, *, seg_rows, row_off=0, rows=None, tm=ROW_TILE, tn=None):
    if a.ndim == 3:
        m = rows or a.shape[1]
        k = a.shape[0] * a.shape[2]
        a_spec = pl.BlockSpec((a.shape[0], tm, a.shape[2]), lambda i, j: (0, i, 0))
    else:
        m, k = rows or a.shape[0], a.shape[1]
        a_spec = pl.BlockSpec((tm, k), lambda i, j: (i, 0))
    n = w.shape[1]
    n_seg = gate.shape[0]
    tn = tn or _col_tile(n, 1024)
    blk_off = row_off // tm
    seg = lambda i, j: (jnp.minimum((blk_off + i) * tm // seg_rows, n_seg - 1), 0, j)
    return pl.pallas_call(
        _mm_res_kernel,
        grid=(m // tm, n // tn),
        in_specs=[a_spec,
                  pl.BlockSpec((k, tn), lambda i, j: (0, j)),
                  pl.BlockSpec((1, tn), lambda i, j: (0, j)),
                  pl.BlockSpec((tm, tn), lambda i, j: (blk_off + i, j)),
                  pl.BlockSpec((1, 1, tn), seg)],
        out_specs=pl.BlockSpec((tm, tn), lambda i, j: (blk_off + i, j)),
        out_shape=jax.ShapeDtypeStruct(res.shape, F32),
        input_output_aliases={3: 0},
        compiler_params=_cparams(2),
        name="mm_res",
    )(a, w, b, res, gate)


CONV_HALO = 16


def _dwconv_kernel(prev_ref, x_ref, next_ref, w_ref, b_ref, *rest, taps, blocks_per_seq, post, sub, n_alias):
    if n_alias:
        rest = rest[:-3] + rest[-2:]
    if post == "ln_silu":
        g_ref, beta_ref, o_ref, win_ref = rest
    else:
        o_ref, win_ref = rest
    i = pl.program_id(0)
    pos = i % blocks_per_seq
    rows, c = x_ref.shape
    half = taps // 2
    zero = jnp.zeros((CONV_HALO, c), F32)
    win_ref[CONV_HALO:CONV_HALO + rows, :] = x_ref[...].astype(F32)

    @pl.when(pos == 0)
    def _():
        win_ref[0:CONV_HALO, :] = zero

    @pl.when(pos > 0)
    def _():
        win_ref[0:CONV_HALO, :] = prev_ref[...].astype(F32)

    @pl.when(pos == blocks_per_seq - 1)
    def _():
        win_ref[CONV_HALO + rows:, :] = zero

    @pl.when(pos < blocks_per_seq - 1)
    def _():
        win_ref[CONV_HALO + rows:, :] = next_ref[...].astype(F32)

    bias = b_ref[...]
    for s in range(rows // sub):
        base = CONV_HALO + s * sub - half
        acc = jnp.broadcast_to(bias, (sub, c))
        for t in range(taps):
            acc = acc + w_ref[t] * win_ref[base + t:base + t + sub, :]
        if post == "ln_silu":
            mu = jnp.mean(acc, axis=-1, keepdims=True)
            xc = acc - mu
            var = jnp.mean(xc * xc, axis=-1, keepdims=True)
            y = xc * lax.rsqrt(var + NORM_EPS) * g_ref[...] + beta_ref[...]
            acc = y * jax.nn.sigmoid(y)
        o_ref[s * sub:(s + 1) * sub, :] = acc.astype(o_ref.dtype)


def dwconv(x, w, b, *, start, total, seq_len, post=None, ln=None, rows=256, tc=None, out_dtype=None,
           blocked=False, same_rows=False, into=None):
    m, c = x.shape
    taps = w.shape[0]
    tc = tc or c
    out_dtype = out_dtype or BF16
    sub = 8
    w = jnp.broadcast_to(w[:, None, :], (taps, sub, c))
    r = min(rows, seq_len)
    bps = seq_len // r
    hb = r // CONV_HALO
    off = start // r
    offh = start // CONV_HALO
    nh = m // CONV_HALO
    in_specs = [pl.BlockSpec((CONV_HALO, tc), lambda i, j: (jnp.maximum(offh + i * hb - 1, 0), j)),
                pl.BlockSpec((r, tc), lambda i, j: (off + i, j)),
                pl.BlockSpec((CONV_HALO, tc), lambda i, j: (jnp.minimum(offh + (i + 1) * hb, nh - 1), j)),
                pl.BlockSpec((taps, sub, tc), lambda i, j: (0, 0, j)),
                pl.BlockSpec((1, tc), lambda i, j: (0, j))]
    args = [x, x, x, w, b]
    if post == "ln_silu":
        in_specs += [pl.BlockSpec((1, tc), lambda i, j: (0, j))] * 2
        args += list(ln)
    aliases = {}
    if blocked:
        out_spec = pl.BlockSpec((None, r, tc), lambda i, j: (j, i, 0))
        out_shape = jax.ShapeDtypeStruct((c // tc, total, tc), out_dtype)
    elif same_rows:
        out_spec = pl.BlockSpec((r, tc), lambda i, j: (off + i, j))
        out_shape = jax.ShapeDtypeStruct((m, c), out_dtype)
        if into is not None:
            in_specs.append(pl.BlockSpec(memory_space=pl.ANY))
            args.append(into)
            aliases = {len(args) - 1: 0}
    else:
        out_spec = pl.BlockSpec((r, tc), lambda i, j: (i, j))
        out_shape = jax.ShapeDtypeStruct((total, c), out_dtype)
    return pl.pallas_call(
        functools.partial(_dwconv_kernel, taps=taps, blocks_per_seq=bps, post=post, sub=sub,
                          n_alias=len(aliases)),
        grid=(total // r, c // tc),
        in_specs=in_specs,
        out_specs=out_spec,
        out_shape=out_shape,
        input_output_aliases=aliases,
        scratch_shapes=[pltpu.VMEM((r + 2 * CONV_HALO, tc), F32)],
        compiler_params=_cparams(2),
        name="dwconv%d" % taps,
    )(*args)


def _masked_halves(q, lane_lo):
    zero = jnp.zeros_like(q)
    return jnp.where(lane_lo, q, zero), jnp.where(lane_lo, zero, q)


def _softmax_pv(s_parts, v_parts, sink):
    m = s_parts[0].max(axis=-1, keepdims=True)
    for s in s_parts[1:]:
        m = jnp.maximum(m, s.max(axis=-1, keepdims=True))
    if sink is not None:
        m = jnp.maximum(m, sink)
    denom = jnp.exp(sink - m) if sink is not None else 0.0
    o = None
    for s, v in zip(s_parts, v_parts):
        p = jnp.exp(s - m)
        denom = denom + p.sum(axis=-1, keepdims=True)
        pv = jnp.dot(p.astype(BF16), v, preferred_element_type=F32)
        o = pv if o is None else o + pv
    return o / denom


def _nt_dot(a, b):
    return lax.dot_general(a, b, (((1,), (1,)), ((), ())), preferred_element_type=F32)


def _wattn_kernel(sink_ref, q_ref, *rest, local, seq_len, blk):
    if local:
        k0, k1, k2, v0, v1, v2, kc_ref, vc_ref, o_ref = rest
    else:
        kc_ref, vc_ref, _, o_ref = rest
    p = pl.program_id(2)
    i = pl.program_id(1)
    lane_lo = lax.broadcasted_iota(jnp.int32, (blk, LANES), 1) < HEAD_DIM
    kc = kc_ref[...]
    vc = vc_ref[...]
    if local:
        kl = jnp.concatenate([k0[...], k1[...], k2[...]], axis=0)
        vl = jnp.concatenate([v0[...], v1[...], v2[...]], axis=0)
        qpos = i * blk + lax.broadcasted_iota(jnp.int32, (blk, 3 * blk), 0)
        kpos = (i - 1) * blk + lax.broadcasted_iota(jnp.int32, (blk, 3 * blk), 1)
        valid = (jnp.abs(kpos - qpos) <= WA_WINDOW) & (kpos >= 0) & (kpos < seq_len)
        valid = jnp.concatenate([valid] * WA_GROUP, axis=0)
    qa, qb = [], []
    for g in range(WA_GROUP):
        a, b = _masked_halves(q_ref[:, g * LANES:(g + 1) * LANES], lane_lo)
        qa.append(a)
        qb.append(b)
    outs = []
    for half, qs in enumerate((qa, qb)):
        qs = jnp.concatenate(qs, axis=0)
        sink = jnp.concatenate(
            [jnp.full((blk, 1), sink_ref[8 * p + 4 * half + g], F32) for g in range(WA_GROUP)], axis=0)
        s_parts, v_parts = [], []
        if local:
            s_parts.append(jnp.where(valid, _nt_dot(qs, kl), NEG_INF))
            v_parts.append(vl)
        s_parts.append(_nt_dot(qs, kc))
        v_parts.append(vc)
        outs.append(_softmax_pv(s_parts, v_parts, sink))
    for g in range(WA_GROUP):
        rows = slice(g * blk, (g + 1) * blk)
        o_ref[:, g * LANES:(g + 1) * LANES] = jnp.where(lane_lo, outs[0][rows], outs[1][rows]).astype(o_ref.dtype)


def window_attention(qkv, sinks, *, batch, seq_len, ctx_len, d, blk=128):
    n_pairs = d // (2 * WA_GROUP * HEAD_DIM)
    kcol = d // LANES
    vcol = kcol + n_pairs
    nblk = seq_len // blk
    cb0 = batch * seq_len // ctx_len
    qw = WA_GROUP * LANES
    kern = functools.partial(_wattn_kernel, seq_len=seq_len)
    smem = pl.BlockSpec(memory_space=pltpu.SMEM)

    def kspec(col0, shift):
        return pl.BlockSpec((blk, LANES),
                            lambda b, i, p: (b * nblk + jnp.clip(i + shift, 0, nblk - 1), col0 + p))

    ctx_k = pl.BlockSpec((ctx_len, LANES), lambda b, i, p: (cb0 + b, kcol + p))
    ctx_v = pl.BlockSpec((ctx_len, LANES), lambda b, i, p: (cb0 + b, vcol + p))
    lat = pl.pallas_call(
        functools.partial(kern, local=True, blk=blk),
        grid=(batch, nblk, n_pairs),
        in_specs=[smem, pl.BlockSpec((blk, qw), lambda b, i, p: (b * nblk + i, p))]
        + [kspec(kcol, s) for s in (-1, 0, 1)] + [kspec(vcol, s) for s in (-1, 0, 1)] + [ctx_k, ctx_v],
        out_specs=pl.BlockSpec((blk, qw), lambda b, i, p: (b * nblk + i, p)),
        out_shape=jax.ShapeDtypeStruct((qkv.shape[0], d), BF16),
        compiler_params=_cparams(3),
        name="window_attn",
    )(sinks, qkv, *([qkv] * 8))
    return pl.pallas_call(
        functools.partial(kern, local=False, blk=ctx_len),
        grid=(batch, 1, n_pairs),
        in_specs=[smem, pl.BlockSpec((ctx_len, qw), lambda b, i, p: (cb0 + b, p)), ctx_k, ctx_v,
                  pl.BlockSpec(memory_space=pl.ANY)],
        out_specs=pl.BlockSpec((ctx_len, qw), lambda b, i, p: (cb0 + b, p)),
        out_shape=jax.ShapeDtypeStruct((qkv.shape[0], d), BF16),
        input_output_aliases={4: 0},
        compiler_params=_cparams(3),
        name="ctx_attn",
    )(sinks, qkv, qkv, qkv, lat)


def _nattn_kernel(q_ref, k0, k1, k2, v0, v1, v2, kc_ref, vc_ref, bias_ref, o_ref, ks_ref, vs_ref,
                  *, grid_rows, rows_per_blk):
    j = pl.program_id(2)
    blk = k0.shape[0]
    for t, (kr, vr) in enumerate(((k0, v0), (k1, v1), (k2, v2))):
        ks_ref[t * blk:(t + 1) * blk, :] = kr[...]
        vs_ref[t * blk:(t + 1) * blk, :] = vr[...]
    kc = kc_ref[...]
    vc = vc_ref[...]
    lane_lo = lax.broadcasted_iota(jnp.int32, (GRID_W, LANES), 1) < HEAD_DIM
    strip = NA_WIN_ROWS * GRID_W
    for r in range(rows_per_blk):
        row = j * rows_per_blk + r
        r0 = jnp.clip(row - NA_WIN_ROWS // 2, 0, grid_rows - NA_WIN_ROWS)
        start = pl.multiple_of((r0 - (j - 1) * rows_per_blk) * GRID_W, GRID_W)
        cls = row - r0
        qa, qb = _masked_halves(q_ref[r * GRID_W:(r + 1) * GRID_W, :], lane_lo)
        qs = jnp.concatenate([qa, qb], axis=0)
        kn = ks_ref[pl.ds(start, strip), :]
        vn = vs_ref[pl.ds(start, strip), :]
        s_nb = _nt_dot(qs, kn) + bias_ref[cls]
        s_cx = _nt_dot(qs, kc)
        o = _softmax_pv([s_nb, s_cx], [vn, vc], None)
        o_ref[r * GRID_W:(r + 1) * GRID_W, :] = jnp.where(lane_lo, o[:GRID_W], o[GRID_W:]).astype(o_ref.dtype)


def _na_bias_table(rpb):
    h = rpb.shape[0]
    n_dcol = 2 * NA_WIN_COLS - 1
    cols = jnp.arange(GRID_W)
    col_start = jnp.clip(cols - NA_WIN_COLS // 2, 0, GRID_W - NA_WIN_COLS)
    inwin = (cols[None, :] >= col_start[:, None]) & (cols[None, :] < col_start[:, None] + NA_WIN_COLS)
    dcol = cols[None, :] - cols[:, None] + NA_WIN_COLS - 1
    pick = (dcol[None] == jnp.arange(n_dcol)[:, None, None]).astype(F32).reshape(n_dcol, -1)
    spread = jnp.dot(rpb.astype(F32).reshape(-1, n_dcol), pick, precision=lax.Precision.HIGHEST)
    spread = spread.reshape(h, 2 * NA_WIN_ROWS - 1, GRID_W, GRID_W)
    spread = jnp.where(inwin[None, None], spread, NEG_INF)
    tab = jnp.stack([spread[:, NA_WIN_ROWS - 1 - cls:2 * NA_WIN_ROWS - 1 - cls]
                     for cls in range(NA_WIN_ROWS)], axis=1)
    tab = tab.transpose(0, 1, 3, 2, 4).reshape(h, NA_WIN_ROWS, GRID_W, NA_WIN_ROWS * GRID_W)
    tab = tab.reshape(h // 2, 2, NA_WIN_ROWS, GRID_W, NA_WIN_ROWS * GRID_W).transpose(0, 2, 1, 3, 4)
    return tab.reshape(h // 2, NA_WIN_ROWS, 2 * GRID_W, NA_WIN_ROWS * GRID_W)


def neighbourhood_attention(qkv, rpb, *, batch, seq_len, ctx_len, d, with_ctx_out):
    n_pairs = d // LANES
    rows_per_blk = NA_WIN_ROWS
    blk = rows_per_blk * GRID_W
    grid_rows = seq_len // GRID_W
    nblk = seq_len // blk
    cb0 = batch * seq_len // ctx_len
    bias = _na_bias_table(rpb)

    def kspec(col0, shift):
        return pl.BlockSpec((blk, LANES),
                            lambda b, p, j: (b * nblk + jnp.clip(j + shift, 0, nblk - 1), col0 + p))

    ctx_k = pl.BlockSpec((ctx_len, LANES), lambda b, p, j: (cb0 + b, n_pairs + p))
    ctx_v = pl.BlockSpec((ctx_len, LANES), lambda b, p, j: (cb0 + b, 2 * n_pairs + p))
    lat = pl.pallas_call(
        functools.partial(_nattn_kernel, grid_rows=grid_rows, rows_per_blk=rows_per_blk),
        grid=(batch, n_pairs, nblk),
        in_specs=[pl.BlockSpec((blk, LANES), lambda b, p, j: (b * nblk + j, p))]
        + [kspec(n_pairs, s) for s in (-1, 0, 1)] + [kspec(2 * n_pairs, s) for s in (-1, 0, 1)]
        + [ctx_k, ctx_v,
           pl.BlockSpec((None, NA_WIN_ROWS, 2 * GRID_W, NA_WIN_ROWS * GRID_W), lambda b, p, j: (p, 0, 0, 0))],
        out_specs=pl.BlockSpec((blk, LANES), lambda b, p, j: (b * nblk + j, p)),
        out_shape=jax.ShapeDtypeStruct((batch * seq_len, d), BF16),
        scratch_shapes=[pltpu.VMEM((3 * blk, LANES), BF16), pltpu.VMEM((3 * blk, LANES), BF16)],
        compiler_params=_cparams(3),
        name="neighbourhood_attn",
    )(qkv, *([qkv] * 8), bias)
    if not with_ctx_out:
        return lat
    ctx = pl.pallas_call(
        _cattn_kernel,
        grid=(batch, n_pairs),
        in_specs=[pl.BlockSpec((ctx_len, LANES), lambda b, p: (cb0 + b, p)),
                  pl.BlockSpec((ctx_len, LANES), lambda b, p: (cb0 + b, n_pairs + p)),
                  pl.BlockSpec((ctx_len, LANES), lambda b, p: (cb0 + b, 2 * n_pairs + p))],
        out_specs=pl.BlockSpec((ctx_len, LANES), lambda b, p: (b, p)),
        out_shape=jax.ShapeDtypeStruct((batch * ctx_len, d), BF16),
        compiler_params=_cparams(2),
        name="ctx_mha",
    )(qkv, qkv, qkv)
    return jnp.concatenate([lat, ctx], axis=0)


def _cattn_kernel(q_ref, k_ref, v_ref, o_ref):
    rows = q_ref.shape[0]
    lane_lo = lax.broadcasted_iota(jnp.int32, (rows, LANES), 1) < HEAD_DIM
    qa, qb = _masked_halves(q_ref[...], lane_lo)
    qs = jnp.concatenate([qa, qb], axis=0)
    o = _softmax_pv([_nt_dot(qs, k_ref[...])], [v_ref[...]], None)
    o_ref[...] = jnp.where(lane_lo, o[:rows], o[rows:]).astype(o_ref.dtype)


def _router_kernel(x_ref, g_ref, sh_ref, sc_ref, wh_ref, wl_ref, b_ref, h_ref, idx_ref, gate_ref, cnt_ref,
                   run_ref):
    @pl.when(pl.program_id(0) == 0)
    def _():
        run_ref[...] = jnp.zeros_like(run_ref)

    h = _norm_mod(x_ref[...], g_ref[...], sh_ref[0], sc_ref[0])
    h_hi = h.astype(BF16)
    h_ref[...] = h_hi
    h_lo = (h - h_hi.astype(F32)).astype(BF16)
    logits = (jnp.dot(h_hi, wh_ref[...], preferred_element_type=F32)
              + jnp.dot(h_hi, wl_ref[...], preferred_element_type=F32)
              + jnp.dot(h_lo, wh_ref[...], preferred_element_type=F32)) + b_ref[...]
    lane = lax.broadcasted_iota(jnp.int32, logits.shape, 1)
    tm = logits.shape[0]
    idx_out = jnp.zeros(logits.shape, jnp.int32)
    val_out = jnp.zeros(logits.shape, F32)
    tri = jnp.where(lax.broadcasted_iota(jnp.int32, (tm, tm), 0) > lax.broadcasted_iota(jnp.int32, (tm, tm), 1),
                    1.0, 0.0).astype(BF16)
    run = run_ref[...]
    top = None
    denom = 0.0
    for k in range(TOP_K):
        m = logits.max(axis=-1, keepdims=True)
        sel = jnp.min(jnp.where(logits == m, lane, LANES), axis=-1, keepdims=True)
        if top is None:
            top = m
        e = jnp.exp(m - top)
        denom = denom + e
        onehot = lane == sel
        before = jnp.dot(tri, jnp.where(onehot, 1.0, 0.0).astype(BF16), preferred_element_type=F32)
        rank = jnp.sum(jnp.where(onehot, before + run, 0.0), axis=-1, keepdims=True).astype(jnp.int32)
        run = run + jnp.sum(jnp.where(onehot, 1.0, 0.0), axis=0, keepdims=True)
        idx_out = jnp.where(lane == k, sel, idx_out)
        idx_out = jnp.where(lane == TOP_K + k, rank, idx_out)
        val_out = jnp.where(lane == k, e, val_out)
        logits = jnp.where(onehot, -jnp.inf, logits)
    run_ref[...] = run
    cnt_ref[...] = run
    idx_ref[...] = idx_out
    gate_ref[...] = val_out / denom


def router(x, g, shift, scale, w_router, b_router, *, seg_rows, rows, tm=ROW_TILE):
    m, d = rows, x.shape[1]
    n_seg = shift.shape[0]
    n_e = w_router.shape[1]
    w_pad = jnp.zeros((d, LANES), F32).at[:, :n_e].set(w_router)
    w_hi = w_pad.astype(BF16)
    w_lo = (w_pad - w_hi.astype(F32)).astype(BF16)
    b_pad = jnp.full((1, LANES), -jnp.inf, F32).at[0, :n_e].set(b_router)
    seg = lambda i: (jnp.minimum(i * tm // seg_rows, n_seg - 1), 0, 0)
    return pl.pallas_call(
        _router_kernel,
        grid=(m // tm,),
        in_specs=[pl.BlockSpec((tm, d), lambda i: (i, 0)),
                  pl.BlockSpec((1, d), lambda i: (0, 0)),
                  pl.BlockSpec((1, 1, d), seg),
                  pl.BlockSpec((1, 1, d), seg),
                  pl.BlockSpec((d, LANES), lambda i: (0, 0)),
                  pl.BlockSpec((d, LANES), lambda i: (0, 0)),
                  pl.BlockSpec((1, LANES), lambda i: (0, 0))],
        out_specs=[pl.BlockSpec((tm, d), lambda i: (i, 0)),
                   pl.BlockSpec((tm, LANES), lambda i: (i, 0)),
                   pl.BlockSpec((tm, LANES), lambda i: (i, 0)),
                   pl.BlockSpec((1, LANES), lambda i: (0, 0))],
        out_shape=[jax.ShapeDtypeStruct((m, d), BF16),
                   jax.ShapeDtypeStruct((m, LANES), jnp.int32),
                   jax.ShapeDtypeStruct((m, LANES), F32),
                   jax.ShapeDtypeStruct((1, LANES), F32)],
        scratch_shapes=[pltpu.VMEM((1, LANES), F32)],
        compiler_params=_cparams(1),
        name="router",
    )(x, g, shift, scale, w_hi, w_lo, b_pad)


def _expert_kernel(blk_e_ref, n_used_ref, x_ref, wgu_ref, bgu_ref, wd_ref, bd_ref, o_ref, wgu_s, wd_s, *, f):
    i = pl.program_id(0)
    used = i < n_used_ref[0]

    @pl.when(used & ((i == 0) | (blk_e_ref[i] != blk_e_ref[jnp.maximum(i - 1, 0)])))
    def _():
        wgu_s[...] = wgu_ref[0].astype(BF16)
        wd_s[...] = wd_ref[0].astype(BF16)

    @pl.when(used)
    def _():
        gu = jnp.dot(x_ref[...], wgu_s[...], preferred_element_type=F32) + bgu_ref[0]
        g = jnp.minimum(gu[:, :f], SWIGLU_LIMIT)
        u = jnp.clip(gu[:, f:], -SWIGLU_LIMIT, SWIGLU_LIMIT)
        act = (u + 1) * (g * jax.nn.sigmoid(SWIGLU_ALPHA * g))
        y = jnp.dot(act.astype(BF16), wd_s[...], preferred_element_type=F32) + bd_ref[0]
        o_ref[...] = y.astype(o_ref.dtype)

    @pl.when(i >= n_used_ref[0])
    def _():
        o_ref[...] = jnp.zeros_like(o_ref)


def expert_mlp(xs, blk_e, n_used, w_gu, b_gu, w_down, b_down, *, layer, bm=MOE_ROWS):
    n_slot, d = xs.shape
    _, n_e, _, f2 = w_gu.shape
    f = f2 // 2
    return pl.pallas_call(
        functools.partial(_expert_kernel, f=f),
        grid_spec=pltpu.PrefetchScalarGridSpec(
            num_scalar_prefetch=2,
            grid=(n_slot // bm,),
            in_specs=[pl.BlockSpec((bm, d), lambda i, be, nu: (i, 0)),
                      pl.BlockSpec((None, 1, d, f2), lambda i, be, nu: (layer, be[i], 0, 0)),
                      pl.BlockSpec((1, 1, f2), lambda i, be, nu: (be[i], 0, 0)),
                      pl.BlockSpec((None, 1, f, d), lambda i, be, nu: (layer, be[i], 0, 0)),
                      pl.BlockSpec((1, 1, d), lambda i, be, nu: (be[i], 0, 0))],
            out_specs=pl.BlockSpec((bm, d), lambda i, be, nu: (i, 0)),
            scratch_shapes=[pltpu.VMEM((d, f2), BF16), pltpu.VMEM((f, d), BF16)]),
        out_shape=jax.ShapeDtypeStruct((n_slot, d), BF16),
        compiler_params=pltpu.CompilerParams(dimension_semantics=("arbitrary",),
                                             vmem_limit_bytes=EXPERT_VMEM_LIMIT),
        name="expert_mlp",
    )(blk_e, n_used, xs, w_gu, b_gu.reshape(n_e, 1, f2), w_down, b_down.reshape(n_e, 1, d))


def _combine_kernel(res_ref, y_ref, w_ref, gate_ref, o_ref):
    d = o_ref.shape[1]
    w = w_ref[...]
    y = w[:, 0:1] * y_ref[:, 0:d].astype(F32)
    for k in range(1, TOP_K):
        y = y + w[:, k:k + 1] * y_ref[:, k * d:(k + 1) * d].astype(F32)
    o_ref[...] = res_ref[...] + gate_ref[0] * y


def moe_layer(x, g, shift, scale, gate, w_router, b_router, w_gu, b_gu, w_down, b_down, *, layer, seg_rows,
              rows, bm=MOE_ROWS):
    m, d = rows, x.shape[1]
    n_e = w_router.shape[1]
    h, route, gates, counts = router(x, g, shift, scale, w_router, b_router, seg_rows=seg_rows, rows=rows)
    idx = route[:, :TOP_K]
    rank = route[:, TOP_K:2 * TOP_K]
    n_asg = m * TOP_K
    counts = counts[0, :n_e].astype(jnp.int32)
    padded = (counts + bm - 1) // bm * bm
    pad_end = jnp.cumsum(padded)
    pad_start = pad_end - padded
    start_of = jnp.sum(jnp.where(idx[:, :, None] == jnp.arange(n_e, dtype=jnp.int32), pad_start, 0), axis=-1)
    dest = (start_of + rank).reshape(-1)
    n_blk = -(-(n_asg + n_e * (bm - 1)) // bm)
    n_slot = n_blk * bm
    tok = (jnp.arange(n_asg, dtype=jnp.int32) // TOP_K)
    slot_tok = jnp.zeros((n_slot,), jnp.int32).at[dest].set(tok, unique_indices=True)
    blk_start = jnp.arange(n_blk, dtype=jnp.int32) * bm
    blk_e = jnp.minimum(jnp.sum((pad_end[None, :] <= blk_start[:, None]).astype(jnp.int32), axis=1), n_e - 1)
    n_used = (pad_end[-1] // bm).astype(jnp.int32).reshape(1)
    xs = h.at[slot_tok].get(mode="promise_in_bounds")
    y_slot = expert_mlp(xs, blk_e, n_used, w_gu, b_gu, w_down, b_down, layer=layer, bm=bm)
    y_tok = y_slot.at[dest].get(mode="promise_in_bounds").reshape(m, TOP_K * d)
    tm = 256
    n_seg = gate.shape[0]
    seg = lambda i: (jnp.minimum(i * tm // seg_rows, n_seg - 1), 0, 0)
    return pl.pallas_call(
        _combine_kernel,
        grid=(m // tm,),
        in_specs=[pl.BlockSpec((tm, d), lambda i: (i, 0)),
                  pl.BlockSpec((tm, TOP_K * d), lambda i: (i, 0)),
                  pl.BlockSpec((tm, LANES), lambda i: (i, 0)),
                  pl.BlockSpec((1, 1, d), seg)],
        out_specs=pl.BlockSpec((tm, d), lambda i: (i, 0)),
        out_shape=jax.ShapeDtypeStruct(x.shape, F32),
        input_output_aliases={0: 0},
        compiler_params=_cparams(1),
        name="moe_combine",
    )(x, y_tok, gates, gate)


LC_R2 = 128
LC_C = 256
LC_KB = 8
LC_LANES = 2048


@functools.lru_cache(maxsize=None)
def _lc_mats(n):
    nn = 2 * n
    r1 = nn // LC_R2
    tau = 2.0 * np.pi
    k1 = np.arange(r1)
    ang1 = tau * ((k1[:, None] * np.arange(r1)[None, :]) % r1) / r1
    f1 = np.empty((2 * r1, r1))
    f1[0::2] = np.cos(ang1)
    f1[1::2] = -np.sin(ang1)
    kk = k1[:, None] + r1 * np.arange(LC_R2)[None, :]
    th = tau * ((kk[:, :, None] * np.arange(LC_R2)[None, None, :]) % nn) / nn
    c, s = np.cos(th), np.sin(th)
    f2 = np.concatenate([np.concatenate([c, s], 2), np.concatenate([-s, c], 2)], 1)
    ct, st = c.transpose(0, 2, 1), s.transpose(0, 2, 1)
    f3 = np.concatenate([np.concatenate([ct, -st], 2), np.concatenate([st, ct], 2)], 1)
    ang4 = tau * ((np.arange(r1 // 2)[:, None] * k1[None, :]) % r1) / r1
    f4 = np.concatenate([np.cos(ang4), -np.sin(ang4)], 1) / nn
    return tuple(jnp.asarray(a, BF16) for a in (f1, f2, f3, f4))


@functools.lru_cache(maxsize=None)
def _dft_small_mats(n):
    nn = 2 * n
    ang = 2.0 * np.pi * ((np.arange(nn)[:, None] * np.arange(nn)[None, :]) % nn) / nn
    fwd = np.concatenate([np.cos(ang), -np.sin(ang)], 0)
    inv = np.concatenate([np.cos(ang[:n]), -np.sin(ang[:n])], 1) / nn
    return jnp.asarray(fwd, BF16), jnp.asarray(inv, BF16)


def _lc_first_kernel(x_ref, f_ref, o_ref):
    o_ref[...] = jnp.dot(f_ref[...], x_ref[...].astype(BF16), preferred_element_type=F32).astype(o_ref.dtype)


def lc_first(x, f1, s1, off=0):
    _, s2, nh, l = x.shape
    lc = min(LC_LANES, l)
    r2 = f1.shape[0]
    return pl.pallas_call(
        _lc_first_kernel,
        grid=(s1, s2, l // lc),
        in_specs=[pl.BlockSpec((None, None, nh, lc), lambda a, b, j: (off + a, b, 0, j)),
                  pl.BlockSpec((r2, nh), lambda a, b, j: (0, 0))],
        out_specs=pl.BlockSpec((None, None, r2, lc), lambda a, b, j: (a, b, 0, j)),
        out_shape=jax.ShapeDtypeStruct((s1, s2, r2, l), BF16),
        compiler_params=_cparams(3),
        name="lc_first",
    )(x, f1[:, :nh])


def _lc_mid_kernel(a_ref, f2_ref, *rest, kb, spectrum_only):
    if spectrum_only:
        (o_ref,) = rest
    else:
        h_ref, f3_ref, o_ref = rest
    c = a_ref.shape[-1]
    for kk in range(kb):
        a_in = a_ref[2 * kk:2 * kk + 2].reshape(2 * LC_R2, c)
        x = jnp.dot(f2_ref[kk], a_in, preferred_element_type=F32)
        if spectrum_only:
            o_ref[kk] = x.astype(o_ref.dtype)
            continue
        h = h_ref[kk].astype(F32)
        xr, xi = x[:LC_R2], x[LC_R2:]
        hr, hi = h[:LC_R2], h[LC_R2:]
        prod = jnp.concatenate([xr * hr - xi * hi, xr * hi + xi * hr], axis=0).astype(BF16)
        g = jnp.dot(f3_ref[kk], prod, preferred_element_type=F32)
        o_ref[0, kk] = g[:LC_R2].astype(o_ref.dtype)
        o_ref[1, kk] = g[LC_R2:].astype(o_ref.dtype)


def lc_spectrum(a, f2):
    cb, _, r2, _, c = a.shape
    r1 = r2 // 2
    kb = min(LC_KB, r1)
    return pl.pallas_call(
        functools.partial(_lc_mid_kernel, kb=kb, spectrum_only=True),
        grid=(r1 // kb, cb),
        in_specs=[pl.BlockSpec((None, None, 2 * kb, LC_R2, c), lambda j, i: (i, 0, j, 0, 0)),
                  pl.BlockSpec((kb, 2 * LC_R2, 2 * LC_R2), lambda j, i: (j, 0, 0))],
        out_specs=pl.BlockSpec((None, kb, 2 * LC_R2, c), lambda j, i: (i, j, 0, 0)),
        out_shape=jax.ShapeDtypeStruct((cb, r1, 2 * LC_R2, c), BF16),
        compiler_params=_cparams(2),
        name="lc_spectrum",
    )(a, f2)


def lc_mid(a, h, f2, f3):
    cb, bsz, r2, _, c = a.shape
    r1 = r2 // 2
    kb = min(LC_KB, r1)
    mat = pl.BlockSpec((kb, 2 * LC_R2, 2 * LC_R2), lambda j, i, b: (j, 0, 0))
    return pl.pallas_call(
        functools.partial(_lc_mid_kernel, kb=kb, spectrum_only=False),
        grid=(r1 // kb, cb, bsz),
        in_specs=[pl.BlockSpec((None, None, 2 * kb, LC_R2, c), lambda j, i, b: (i, b, j, 0, 0)),
                  mat,
                  pl.BlockSpec((None, kb, 2 * LC_R2, c), lambda j, i, b: (i, j, 0, 0)),
                  mat],
        out_specs=pl.BlockSpec((None, None, 2, kb, LC_R2, c), lambda j, i, b: (i, b, 0, j, 0, 0)),
        out_shape=jax.ShapeDtypeStruct((cb, bsz, 2, r1, LC_R2, c), BF16),
        compiler_params=_cparams(3),
        name="lc_mid",
    )(a, f2, h, f3)


def _lc_last_kernel(g_ref, f_ref, u_ref, gate_ref, skip_ref, o_ref):
    y = jnp.dot(f_ref[...], g_ref[...], preferred_element_type=F32)
    u = u_ref[...].astype(F32)
    o_ref[...] = (gate_ref[...].astype(F32) * (y + skip_ref[...] * u)).astype(o_ref.dtype)


def lc_last(g, f4, u_arr, u_off, gate_arr, gate_off, skip_row):
    cb, bsz, r2, l = g.shape
    nh = r2 // 4
    lc = min(LC_LANES, l)
    return pl.pallas_call(
        _lc_last_kernel,
        grid=(cb, bsz, l // lc),
        in_specs=[pl.BlockSpec((None, None, r2, lc), lambda i, b, j: (i, b, 0, j)),
                  pl.BlockSpec((nh, r2), lambda i, b, j: (0, 0)),
                  pl.BlockSpec((None, None, nh, lc), lambda i, b, j: (u_off + i, b, 0, j)),
                  pl.BlockSpec((None, None, nh, lc), lambda i, b, j: (gate_off + i, b, 0, j)),
                  pl.BlockSpec((None, 1, lc), lambda i, b, j: (i, 0, j))],
        out_specs=pl.BlockSpec((None, None, nh, lc), lambda i, b, j: (i, b, 0, j)),
        out_shape=jax.ShapeDtypeStruct((cb, bsz, nh, l), BF16),
        compiler_params=_cparams(3),
        name="lc_last",
    )(g, f4, u_arr, gate_arr, skip_row)


def _plain_mm_kernel(a_ref, b_ref, o_ref):
    o_ref[...] = jnp.dot(a_ref[...], b_ref[...].astype(BF16), preferred_element_type=F32)


def plain_mm(a, b, tn=512):
    m, k = a.shape
    n = b.shape[1]
    tn = _col_tile(n, tn)
    return pl.pallas_call(
        _plain_mm_kernel,
        grid=(n // tn,),
        in_specs=[pl.BlockSpec((m, k), lambda j: (0, 0)), pl.BlockSpec((k, tn), lambda j: (0, j))],
        out_specs=pl.BlockSpec((m, tn), lambda j: (0, j)),
        out_shape=jax.ShapeDtypeStruct((m, n), F32),
        compiler_params=_cparams(1),
        name="plain_mm",
    )(a, b)


def _short_conv_kernel(u_ref, gate_ref, h_ref, skip_ref, fwd_ref, inv_ref, o_ref):
    u = u_ref[...]
    x = jnp.dot(fwd_ref[...], u.astype(BF16), preferred_element_type=F32)
    h = h_ref[...]
    nn = x.shape[0] // 2
    xr, xi, hr, hi = x[:nn], x[nn:], h[:nn], h[nn:]
    prod = jnp.concatenate([xr * hr - xi * hi, xr * hi + xi * hr], axis=0).astype(BF16)
    y = jnp.dot(inv_ref[...], prod, preferred_element_type=F32)
    o_ref[...] = (gate_ref[...].astype(F32) * (y + skip_ref[...] * u.astype(F32))).astype(o_ref.dtype)


def short_long_conv(u_arr, u_col, gate_arr, gate_col, spec, skip, fwd, inv, *, batch, seq_len, tl=512):
    d = spec.shape[1]
    tl = _col_tile(d, tl)
    uc, gc = u_col // tl, gate_col // tl
    nn2 = spec.shape[0]
    return pl.pallas_call(
        _short_conv_kernel,
        grid=(batch, d // tl),
        in_specs=[pl.BlockSpec((seq_len, tl), lambda b, j: (b, uc + j)),
                  pl.BlockSpec((seq_len, tl), lambda b, j: (b, gc + j)),
                  pl.BlockSpec((nn2, tl), lambda b, j: (0, j)),
                  pl.BlockSpec((1, tl), lambda b, j: (0, j)),
                  pl.BlockSpec((nn2, seq_len), lambda b, j: (0, 0)),
                  pl.BlockSpec((seq_len, nn2), lambda b, j: (0, 0))],
        out_specs=pl.BlockSpec((seq_len, tl), lambda b, j: (b, j)),
        out_shape=jax.ShapeDtypeStruct((batch * seq_len, d), BF16),
        compiler_params=_cparams(2),
        name="short_long_conv",
    )(u_arr, gate_arr, spec, skip, fwd[:, :seq_len], inv)


def _hyena_taps(n, d, f_w1, f_b1, f_w2, f_b2, f_w3, f_b3, f_freq, f_w4):
    hp = lax.Precision.HIGHEST
    lin = jnp.linspace(0.0, 1.0, n, dtype=F32)
    idx = jnp.arange(n, dtype=F32)
    bands = jnp.linspace(1e-4, HY_BANDS - 1, HY_BANDS, dtype=F32)[None, :]
    deltas = jnp.abs(jnp.linspace(math.log(HY_DECAY_TARGET) / HY_FAST_DECAY,
                                  math.log(HY_DECAY_TARGET) / HY_SLOW_DECAY, d, dtype=F32))

    def branch(t, pos, direction):
        t = t[:, None]
        ang = (2.0 * math.pi / n) * pos[:, None]
        emb = jnp.concatenate([t, jnp.cos(bands * ang), -jnp.sin(bands * ang)], axis=-1)
        a = jnp.sin(f_freq * (jnp.dot(emb, f_w1, precision=hp) + f_b1))
        a = jnp.sin(f_freq * (jnp.dot(a, f_w2, precision=hp) + f_b2))
        a = jnp.sin(f_freq * (jnp.dot(a, f_w3, precision=hp) + f_b3))
        decay = jnp.exp(-t * deltas[None, :])
        w4 = f_w4.reshape(f_w4.shape[0], HY_ORDER, 2, d)[:, :, direction]
        return [jnp.dot(a, w4[:, o], precision=hp) * decay for o in range(HY_ORDER)]

    fwd = branch(lin, idx, 0)
    bwd = branch(lin[:0:-1], idx[:0:-1], 1)
    zero = jnp.zeros((1, d), F32)
    return [jnp.concatenate([fwd[o], zero, bwd[o]], axis=0) for o in range(HY_ORDER)]


def hyena_long_convs(zc, taps, skip, *, batch, seq_len):
    cb3, rows, c = zc.shape
    cb = cb3 // 3
    d = cb * c
    nh = seq_len // LC_R2
    lanes = LC_R2 * c
    f1, f2, f3, f4 = _lc_mats(seq_len)
    zc_v = zc.reshape(cb3, batch, nh, lanes)
    u, u_off = zc_v, 2 * cb
    for o in range(HY_ORDER):
        taps_b = taps[o].reshape(2 * seq_len, cb, c).transpose(1, 0, 2).reshape(cb, 1, 2 * nh, lanes)
        spec = lc_spectrum(lc_first(taps_b, f1, cb).reshape(cb, 1, 4 * nh, LC_R2, c), f2)
        a = lc_first(u, f1, cb, u_off).reshape(cb, batch, 4 * nh, LC_R2, c)
        g = lc_mid(a, spec, f2, f3).reshape(cb, batch, 4 * nh, lanes)
        skip_row = jnp.tile(skip[o].reshape(cb, 1, c), (1, 1, LC_R2))
        u = lc_last(g, f4, u, u_off, zc_v, o * cb, skip_row)
        u_off = 0
    return u.reshape(cb, rows, c)


def _final_norm_kernel(x_ref, g_ref, o_ref):
    x = x_ref[...]
    o_ref[...] = x * lax.rsqrt(jnp.mean(x * x, axis=-1, keepdims=True) + NORM_EPS) * g_ref[...]


def final_norm(x, g, *, rows, tm=ROW_TILE):
    d = x.shape[1]
    return pl.pallas_call(
        _final_norm_kernel,
        grid=(rows // tm,),
        in_specs=[pl.BlockSpec((tm, d), lambda i: (i, 0)), pl.BlockSpec((1, d), lambda i: (0, 0))],
        out_specs=pl.BlockSpec((tm, d), lambda i: (i, 0)),
        out_shape=jax.ShapeDtypeStruct((rows, d), F32),
        compiler_params=_cparams(1),
        name="final_norm",
    )(x, g)


def _rope_tables(n_lat, seq_len, n_rows):
    t = jnp.arange(n_rows, dtype=jnp.int32)
    row = ((t % seq_len) // GRID_W).astype(F32)
    col = (t % GRID_W).astype(F32)
    lane = jnp.arange(LANES)
    dd = lane % HEAD_DIM
    quarter = HEAD_DIM // 4
    inv_freq = ROPE_BASE ** (-(dd % quarter).astype(F32) / quarter)
    pos = jnp.where(dd[None, :] < HEAD_DIM // 2, row[:, None], col[:, None])
    ang = pos * inv_freq[None, :]
    sign = jnp.where((dd % (HEAD_DIM // 2)) < quarter, -1.0, 1.0).astype(F32)
    lat = (t < n_lat)[:, None]
    cos = jnp.where(lat, jnp.cos(ang), 1.0)
    sin = jnp.where(lat, jnp.sin(ang) * sign[None, :], 0.0)
    return cos, sin


def _wa_head_perm(n_heads):
    order = []
    for p in range(n_heads // (2 * WA_GROUP)):
        for g in range(WA_GROUP):
            order += [2 * WA_GROUP * p + g, 2 * WA_GROUP * p + WA_GROUP + g]
    cols = jnp.asarray(order, jnp.int32)[:, None] * HEAD_DIM + jnp.arange(HEAD_DIM, dtype=jnp.int32)[None, :]
    return cols.reshape(-1)


def kernel(x, c, ctx, c_ctx, w_ada, b_ada, g_mix, g_ffn, hy_w_in, hy_b_in, hy_w_short, hy_b_short, hy_f_w1, hy_f_b1, hy_f_w2, hy_f_b2, hy_f_w3, hy_f_b3, hy_f_freq, hy_f_w4, hy_skip, hy_w_out, hy_b_out, cf_w_pw1, cf_b_pw1, cf_w_dw, cf_b_dw, cf_ln_g, cf_ln_b, cf_w_pw2, cf_b_pw2, wa_w_qkv, wa_w_o, wa_sinks, na_w_qkv, na_w_o, na_rpb, moe_w_router, moe_b_router, moe_w_gu, moe_b_gu, moe_w_down, moe_b_down, g_final):
    bsz, n, d = x.shape
    n_ctx = ctx.shape[1]
    depth = w_ada.shape[0]
    n_lat = bsz * n
    m_all = n_lat + bsz * n_ctx

    cond = jnp.zeros((8, d), F32).at[:bsz].set(jax.nn.silu(c)).at[bsz].set(jax.nn.silu(c_ctx))
    mods = adaln(cond, w_ada, b_ada)
    xu = jnp.concatenate([x.reshape(n_lat, d), ctx.reshape(bsz * n_ctx, d)], axis=0)
    zeros_d = jnp.zeros((1, d), F32)

    for i in range(depth):
        kind, j = i % N_MIXERS, i // N_MIXERS
        last = i == depth - 1
        mod = [mods[i, :bsz + 1, k * d:(k + 1) * d].reshape(bsz + 1, 1, d) for k in range(6)]
        m_out = n_lat if last else m_all
        gm = g_mix[i].reshape(1, d)
        need_ctx_in = (not last) or kind >= 2
        m_in = m_all if need_ctx_in else n_lat
        x_in = xu[:m_in]

        if kind == 0:
            z = nm_matmul(x_in, gm, mod[0], mod[1], hy_w_in[j].astype(BF16), hy_b_in[j].reshape(1, -1),
                          seg_rows=n)
            filt_w = (hy_f_w1[j], hy_f_b1[j], hy_f_w2[j], hy_f_b2[j], hy_f_w3[j], hy_f_b3[j], hy_f_freq[j],
                      hy_f_w4[j])
            w_s, b_s = hy_w_short[j], hy_b_short[j].reshape(1, -1)
            w_o, b_o = hy_w_out[j].astype(BF16), hy_b_out[j].reshape(1, d)
            zc = dwconv(z, w_s, b_s, start=0, total=n_lat, seq_len=n, tc=min(LC_C, d), blocked=True,
                        rows=1024)
            y = hyena_long_convs(zc, _hyena_taps(n, d, *filt_w), hy_skip[j], batch=bsz, seq_len=n)
            xu = mm_res(y, w_o, b_o, xu, mod[2], seg_rows=n)
            if m_in > n_lat:
                zcc = dwconv(z, w_s, b_s, start=n_lat, total=m_in - n_lat, seq_len=n_ctx, tc=512)
                taps_c = _hyena_taps(n_ctx, d, *filt_w)
                fwd, inv = _dft_small_mats(n_ctx)
                yc = short_long_conv(zcc, 2 * d, zcc, 0, plain_mm(fwd, taps_c[0]), hy_skip[j][0].reshape(1, d),
                                     fwd, inv, batch=bsz, seq_len=n_ctx)
                yc = short_long_conv(yc, 0, zcc, d, plain_mm(fwd, taps_c[1]), hy_skip[j][1].reshape(1, d),
                                     fwd, inv, batch=bsz, seq_len=n_ctx)
                if not last:
                    xu = mm_res(yc, w_o, b_o, xu, mod[2], seg_rows=n, row_off=n_lat)
            y = None
        elif kind == 1:
            w1 = cf_w_pw1[j].astype(BF16).reshape(d, 2, d).transpose(1, 0, 2)
            a = nm_matmul(x_in, gm, mod[0], mod[1], w1, cf_b_pw1[j].reshape(2, 1, d), seg_rows=n, mode="glu")
            conv = functools.partial(dwconv, a, cf_w_dw[j], cf_b_dw[j].reshape(1, d), post="ln_silu",
                                     ln=(cf_ln_g[j].reshape(1, d), cf_ln_b[j].reshape(1, d)), rows=64)
            y = conv(start=0, total=n_lat, seq_len=n, same_rows=True)
            if m_in > n_lat:
                y = conv(start=n_lat, total=m_in - n_lat, seq_len=n_ctx, same_rows=True, into=y)
            w_o, b_o = cf_w_pw2[j].astype(BF16), cf_b_pw2[j].reshape(1, d)
        elif kind == 2:
            perm = _wa_head_perm(d // HEAD_DIM)
            scale = HEAD_DIM ** -0.5
            w_qkv = jnp.concatenate([wa_w_qkv[j][:, :d][:, perm] * scale, wa_w_qkv[j][:, d:]], axis=1).astype(BF16)
            n_out = w_qkv.shape[1]
            qkv = nm_matmul(x_in, gm, mod[0], mod[1], w_qkv, jnp.zeros((1, n_out), F32), seg_rows=n,
                            mode="rope", rope=_rope_tables(n_lat, n, m_in), n_rope_cols=d + (n_out - d) // 2)
            y = window_attention(qkv, wa_sinks[j], batch=bsz, seq_len=n, ctx_len=n_ctx, d=d)
            w_o, b_o = wa_w_o[j][perm].astype(BF16), zeros_d
        else:
            scale = HEAD_DIM ** -0.5
            w_qkv = jnp.concatenate([na_w_qkv[j][:, :d] * scale, na_w_qkv[j][:, d:]], axis=1).astype(BF16)
            qkv = nm_matmul(x_in, gm, mod[0], mod[1], w_qkv, jnp.zeros((1, 3 * d), F32), seg_rows=n)
            y = neighbourhood_attention(qkv, na_rpb[j], batch=bsz, seq_len=n, ctx_len=n_ctx, d=d,
                                        with_ctx_out=not last)
            w_o, b_o = na_w_o[j].astype(BF16), zeros_d
        if y is not None:
            xu = mm_res(y, w_o, b_o, xu, mod[2], seg_rows=n, rows=m_out)

        xu = moe_layer(xu, g_ffn[i].reshape(1, d), mod[3], mod[4], mod[5], moe_w_router[i], moe_b_router[i],
                       moe_w_gu, moe_b_gu[i], moe_w_down, moe_b_down[i], layer=i, seg_rows=n, rows=m_out)
    return final_norm(xu, g_final.reshape(1, d), rows=n_lat).reshape(bsz, n, d)
```

```python
import functools
import math

import jax
import jax.numpy as jnp
import numpy as np
from jax import lax
from jax.experimental import pallas as pl
from jax.experimental.pallas import tpu as pltpu

F32 = jnp.float32
BF16 = jnp.bfloat16

GRID_W = 64
N_MIXERS = 4
NORM_EPS = 1e-6
NEG_INF = -1e30
HEAD_DIM = 64
ROPE_BASE = 10000.0

HY_ORDER = 2
HY_BANDS = 16
HY_DECAY_TARGET = 1e-2
HY_FAST_DECAY = 0.3
HY_SLOW_DECAY = 1.5

WA_GROUP = 4
WA_WINDOW = 128
NA_WIN_ROWS = 8
NA_WIN_COLS = 16

TOP_K = 4
SWIGLU_LIMIT = 7.0
SWIGLU_ALPHA = 1.702

LANES = 128
ROW_TILE = 512
MOE_ROWS = 256
VMEM_LIMIT = 56 * 1024 * 1024
EXPERT_VMEM_LIMIT = 58 * 1024 * 1024


def _cparams(n_axes):
    return pltpu.CompilerParams(dimension_semantics=("arbitrary",) * n_axes,
                                vmem_limit_bytes=VMEM_LIMIT)


def _col_tile(n, pref=1024):
    t = min(pref, n)
    while n % t:
        t //= 2
    return t


def _adaln_kernel(c_ref, w_ref, b_ref, o_ref):
    w = w_ref[0]
    w_hi = w.astype(BF16)
    w_lo = (w - w_hi.astype(F32)).astype(BF16)
    c = c_ref[...]
    c_hi = c.astype(BF16)
    c_lo = (c - c_hi.astype(F32)).astype(BF16)
    acc = jnp.dot(c_hi, w_hi, preferred_element_type=F32)
    acc += jnp.dot(c_hi, w_lo, preferred_element_type=F32)
    acc += jnp.dot(c_lo, w_hi, preferred_element_type=F32)
    o_ref[0] = acc + b_ref[0]


def adaln(cond, w_ada, b_ada):
    depth, d, n6 = w_ada.shape
    tn = _col_tile(n6, 1024)
    return pl.pallas_call(
        _adaln_kernel,
        grid=(depth, n6 // tn),
        in_specs=[pl.BlockSpec((8, d), lambda l, j: (0, 0)),
                  pl.BlockSpec((1, d, tn), lambda l, j: (l, 0, j)),
                  pl.BlockSpec((1, 1, tn), lambda l, j: (l, 0, j))],
        out_specs=pl.BlockSpec((1, 8, tn), lambda l, j: (l, 0, j)),
        out_shape=jax.ShapeDtypeStruct((depth, 8, n6), F32),
        compiler_params=_cparams(2),
        name="adaln",
    )(cond, w_ada, b_ada.reshape(depth, 1, n6))


def _norm_mod(x, g, shift, scale):
    y = x * lax.rsqrt(jnp.mean(x * x, axis=-1, keepdims=True) + NORM_EPS)
    return (y * g) * (1 + scale) + shift


def _rope_tile(x, cos, sin):
    lane = lax.broadcasted_iota(jnp.int32, x.shape, 1)
    nxt = pltpu.roll(x, LANES - 16, axis=1)
    prv = pltpu.roll(x, 16, axis=1)
    partner = jnp.where((lane // 16) % 2 == 0, nxt, prv)
    return x * cos + partner * sin


def _nm_mm_kernel(x_ref, g_ref, sh_ref, sc_ref, w_ref, b_ref, *rest, mode, n_rope):
    if mode == "rope":
        cos_ref, sin_ref, o_ref, h_ref = rest
    else:
        o_ref, h_ref = rest
    j = pl.program_id(1)

    @pl.when(j == 0)
    def _():
        h_ref[...] = _norm_mod(x_ref[...], g_ref[...], sh_ref[0], sc_ref[0]).astype(BF16)

    h = h_ref[...]
    if mode == "glu":
        a = jnp.dot(h, w_ref[0], preferred_element_type=F32) + b_ref[0]
        gate = jnp.dot(h, w_ref[1], preferred_element_type=F32) + b_ref[1]
        o_ref[...] = (a * jax.nn.sigmoid(gate)).astype(o_ref.dtype)
        return
    acc = jnp.dot(h, w_ref[...], preferred_element_type=F32) + b_ref[...]
    if mode == "rope":
        @pl.when(j < n_rope)
        def _():
            cos = cos_ref[...]
            sin = sin_ref[...]
            for g in range(acc.shape[1] // LANES):
                sl = slice(g * LANES, (g + 1) * LANES)
                o_ref[:, sl] = _rope_tile(acc[:, sl], cos, sin).astype(o_ref.dtype)

        @pl.when(j >= n_rope)
        def _():
            o_ref[...] = acc.astype(o_ref.dtype)
    else:
        o_ref[...] = acc.astype(o_ref.dtype)


def nm_matmul(x, g, shift, scale, w, b, *, seg_rows, mode="plain", rope=None, n_rope_cols=0,
              tm=ROW_TILE, tn=None):
    m, d = x.shape
    n_seg = shift.shape[0]
    n = w.shape[-1]
    tn = tn or (_col_tile(math.gcd(n, n_rope_cols), 512) if mode == "rope" else _col_tile(n, 1024))
    seg = lambda i, j: (jnp.minimum(i * tm // seg_rows, n_seg - 1), 0, 0)
    in_specs = [pl.BlockSpec((tm, d), lambda i, j: (i, 0)),
                pl.BlockSpec((1, d), lambda i, j: (0, 0)),
                pl.BlockSpec((1, 1, d), seg),
                pl.BlockSpec((1, 1, d), seg)]
    if mode == "glu":
        in_specs += [pl.BlockSpec((2, d, tn), lambda i, j: (0, 0, j)),
                     pl.BlockSpec((2, 1, tn), lambda i, j: (0, 0, j))]
    else:
        in_specs += [pl.BlockSpec((d, tn), lambda i, j: (0, j)),
                     pl.BlockSpec((1, tn), lambda i, j: (0, j))]
    args = [x, g, shift, scale, w, b]
    if mode == "rope":
        in_specs += [pl.BlockSpec((tm, LANES), lambda i, j: (i, 0))] * 2
        args += list(rope)
    return pl.pallas_call(
        functools.partial(_nm_mm_kernel, mode=mode, n_rope=n_rope_cols // tn),
        grid=(m // tm, n // tn),
        in_specs=in_specs,
        out_specs=pl.BlockSpec((tm, tn), lambda i, j: (i, j)),
        out_shape=jax.ShapeDtypeStruct((m, n), BF16),
        scratch_shapes=[pltpu.VMEM((tm, d), BF16)],
        compiler_params=_cparams(2),
        name="nm_matmul_" + mode,
    )(*args)


def _mm_res_kernel(a_ref, w_ref, b_ref, res_ref, gate_ref, o_ref):
    if len(a_ref.shape) == 3:
        kc = a_ref.shape[2]
        acc = b_ref[...]
        for cb in range(a_ref.shape[0]):
            acc = acc + jnp.dot(a_ref[cb], w_ref[cb * kc:(cb + 1) * kc, :], preferred_element_type=F32)
    else:
        acc = jnp.dot(a_ref[...], w_ref[...], preferred_element_type=F32) + b_ref[...]
    o_ref[...] = res_ref[...] + gate_ref[0] * acc


def mm_res(a, w, b, res, gate, *, seg_rows, row_off=0, rows=None, tm=ROW_TILE, tn=None):
    if a.ndim == 3:
        m = rows or a.shape[1]
        k = a.shape[0] * a.shape[2]
        a_spec = pl.BlockSpec((a.shape[0], tm, a.shape[2]), lambda i, j: (0, i, 0))
    else:
        m, k = rows or a.shape[0], a.shape[1]
        a_spec = pl.BlockSpec((tm, k), lambda i, j: (i, 0))
    n = w.shape[1]
    n_seg = gate.shape[0]
    tn = tn or _col_tile(n, 1024)
    blk_off = row_off // tm
    seg = lambda i, j: (jnp.minimum((blk_off + i) * tm // seg_rows, n_seg - 1), 0, j)
    return pl.pallas_call(
        _mm_res_kernel,
        grid=(m // tm, n // tn),
        in_specs=[a_spec,
                  pl.BlockSpec((k, tn), lambda i, j: (0, j)),
                  pl.BlockSpec((1, tn), lambda i, j: (0, j)),
                  pl.BlockSpec((tm, tn), lambda i, j: (blk_off + i, j)),
                  pl.BlockSpec((1, 1, tn), seg)],
        out_specs=pl.BlockSpec((tm, tn), lambda i, j: (blk_off + i, j)),
        out_shape=jax.ShapeDtypeStruct(res.shape, F32),
        input_output_aliases={3: 0},
        compiler_params=_cparams(2),
        name="mm_res",
    )(a, w, b, res, gate)


CONV_HALO = 16


def _dwconv_kernel(prev_ref, x_ref, next_ref, w_ref, b_ref, *rest, taps, blocks_per_seq, post, sub, n_alias):
    if n_alias:
        rest = rest[:-3] + rest[-2:]
    if post == "ln_silu":
        g_ref, beta_ref, o_ref, win_ref = rest
    else:
        o_ref, win_ref = rest
    i = pl.program_id(0)
    pos = i % blocks_per_seq
    rows, c = x_ref.shape
    half = taps // 2
    zero = jnp.zeros((CONV_HALO, c), F32)
    win_ref[CONV_HALO:CONV_HALO + rows, :] = x_ref[...].astype(F32)

    @pl.when(pos == 0)
    def _():
        win_ref[0:CONV_HALO, :] = zero

    @pl.when(pos > 0)
    def _():
        win_ref[0:CONV_HALO, :] = prev_ref[...].astype(F32)

    @pl.when(pos == blocks_per_seq - 1)
    def _():
        win_ref[CONV_HALO + rows:, :] = zero

    @pl.when(pos < blocks_per_seq - 1)
    def _():
        win_ref[CONV_HALO + rows:, :] = next_ref[...].astype(F32)

    bias = b_ref[...]
    for s in range(rows // sub):
        base = CONV_HALO + s * sub - half
        acc = jnp.broadcast_to(bias, (sub, c))
        for t in range(taps):
            acc = acc + w_ref[t] * win_ref[base + t:base + t + sub, :]
        if post == "ln_silu":
            mu = jnp.mean(acc, axis=-1, keepdims=True)
            xc = acc - mu
            var = jnp.mean(xc * xc, axis=-1, keepdims=True)
            y = xc * lax.rsqrt(var + NORM_EPS) * g_ref[...] + beta_ref[...]
            acc = y * jax.nn.sigmoid(y)
        o_ref[s * sub:(s + 1) * sub, :] = acc.astype(o_ref.dtype)


def dwconv(x, w, b, *, start, total, seq_len, post=None, ln=None, rows=256, tc=None, out_dtype=None,
           blocked=False, same_rows=False, into=None):
    m, c = x.shape
    taps = w.shape[0]
    tc = tc or c
    out_dtype = out_dtype or BF16
    sub = 8
    w = jnp.broadcast_to(w[:, None, :], (taps, sub, c))
    r = min(rows, seq_len)
    bps = seq_len // r
    hb = r // CONV_HALO
    off = start // r
    offh = start // CONV_HALO
    nh = m // CONV_HALO
    in_specs = [pl.BlockSpec((CONV_HALO, tc), lambda i, j: (jnp.maximum(offh + i * hb - 1, 0), j)),
                pl.BlockSpec((r, tc), lambda i, j: (off + i, j)),
                pl.BlockSpec((CONV_HALO, tc), lambda i, j: (jnp.minimum(offh + (i + 1) * hb, nh - 1), j)),
                pl.BlockSpec((taps, sub, tc), lambda i, j: (0, 0, j)),
                pl.BlockSpec((1, tc), lambda i, j: (0, j))]
    args = [x, x, x, w, b]
    if post == "ln_silu":
        in_specs += [pl.BlockSpec((1, tc), lambda i, j: (0, j))] * 2
        args += list(ln)
    aliases = {}
    if blocked:
        out_spec = pl.BlockSpec((None, r, tc), lambda i, j: (j, i, 0))
        out_shape = jax.ShapeDtypeStruct((c // tc, total, tc), out_dtype)
    elif same_rows:
        out_spec = pl.BlockSpec((r, tc), lambda i, j: (off + i, j))
        out_shape = jax.ShapeDtypeStruct((m, c), out_dtype)
        if into is not None:
            in_specs.append(pl.BlockSpec(memory_space=pl.ANY))
            args.append(into)
            aliases = {len(args) - 1: 0}
    else:
        out_spec = pl.BlockSpec((r, tc), lambda i, j: (i, j))
        out_shape = jax.ShapeDtypeStruct((total, c), out_dtype)
    return pl.pallas_call(
        functools.partial(_dwconv_kernel, taps=taps, blocks_per_seq=bps, post=post, sub=sub,
                          n_alias=len(aliases)),
        grid=(total // r, c // tc),
        in_specs=in_specs,
        out_specs=out_spec,
        out_shape=out_shape,
        input_output_aliases=aliases,
        scratch_shapes=[pltpu.VMEM((r + 2 * CONV_HALO, tc), F32)],
        compiler_params=_cparams(2),
        name="dwconv%d" % taps,
    )(*args)


def _masked_halves(q, lane_lo):
    zero = jnp.zeros_like(q)
    return jnp.where(lane_lo, q, zero), jnp.where(lane_lo, zero, q)


def _softmax_pv(s_parts, v_parts, sink):
    m = s_parts[0].max(axis=-1, keepdims=True)
    for s in s_parts[1:]:
        m = jnp.maximum(m, s.max(axis=-1, keepdims=True))
    if sink is not None:
        m = jnp.maximum(m, sink)
    denom = jnp.exp(sink - m) if sink is not None else 0.0
    o = None
    for s, v in zip(s_parts, v_parts):
        p = jnp.exp(s - m)
        denom = denom + p.sum(axis=-1, keepdims=True)
        pv = jnp.dot(p.astype(BF16), v, preferred_element_type=F32)
        o = pv if o is None else o + pv
    return o / denom


def _nt_dot(a, b):
    return lax.dot_general(a, b, (((1,), (1,)), ((), ())), preferred_element_type=F32)


def _wattn_kernel(sink_ref, q_ref, *rest, local, seq_len, blk):
    if local:
        k0, k1, k2, v0, v1, v2, kc_ref, vc_ref, o_ref = rest
    else:
        kc_ref, vc_ref, _, o_ref = rest
    p = pl.program_id(2)
    i = pl.program_id(1)
    lane_lo = lax.broadcasted_iota(jnp.int32, (blk, LANES), 1) < HEAD_DIM
    kc = kc_ref[...]
    vc = vc_ref[...]
    if local:
        kl = jnp.concatenate([k0[...], k1[...], k2[...]], axis=0)
        vl = jnp.concatenate([v0[...], v1[...], v2[...]], axis=0)
        qpos = i * blk + lax.broadcasted_iota(jnp.int32, (blk, 3 * blk), 0)
        kpos = (i - 1) * blk + lax.broadcasted_iota(jnp.int32, (blk, 3 * blk), 1)
        valid = (jnp.abs(kpos - qpos) <= WA_WINDOW) & (kpos >= 0) & (kpos < seq_len)
        valid = jnp.concatenate([valid] * WA_GROUP, axis=0)
    qa, qb = [], []
    for g in range(WA_GROUP):
        a, b = _masked_halves(q_ref[:, g * LANES:(g + 1) * LANES], lane_lo)
        qa.append(a)
        qb.append(b)
    outs = []
    for half, qs in enumerate((qa, qb)):
        qs = jnp.concatenate(qs, axis=0)
        sink = jnp.concatenate(
            [jnp.full((blk, 1), sink_ref[8 * p + 4 * half + g], F32) for g in range(WA_GROUP)], axis=0)
        s_parts, v_parts = [], []
        if local:
            s_parts.append(jnp.where(valid, _nt_dot(qs, kl), NEG_INF))
            v_parts.append(vl)
        s_parts.append(_nt_dot(qs, kc))
        v_parts.append(vc)
        outs.append(_softmax_pv(s_parts, v_parts, sink))
    for g in range(WA_GROUP):
        rows = slice(g * blk, (g + 1) * blk)
        o_ref[:, g * LANES:(g + 1) * LANES] = jnp.where(lane_lo, outs[0][rows], outs[1][rows]).astype(o_ref.dtype)


def window_attention(qkv, sinks, *, batch, seq_len, ctx_len, d, blk=128):
    n_pairs = d // (2 * WA_GROUP * HEAD_DIM)
    kcol = d // LANES
    vcol = kcol + n_pairs
    nblk = seq_len // blk
    cb0 = batch * seq_len // ctx_len
    qw = WA_GROUP * LANES
    kern = functools.partial(_wattn_kernel, seq_len=seq_len)
    smem = pl.BlockSpec(memory_space=pltpu.SMEM)

    def kspec(col0, shift):
        return pl.BlockSpec((blk, LANES),
                            lambda b, i, p: (b * nblk + jnp.clip(i + shift, 0, nblk - 1), col0 + p))

    ctx_k = pl.BlockSpec((ctx_len, LANES), lambda b, i, p: (cb0 + b, kcol + p))
    ctx_v = pl.BlockSpec((ctx_len, LANES), lambda b, i, p: (cb0 + b, vcol + p))
    lat = pl.pallas_call(
        functools.partial(kern, local=True, blk=blk),
        grid=(batch, nblk, n_pairs),
        in_specs=[smem, pl.BlockSpec((blk, qw), lambda b, i, p: (b * nblk + i, p))]
        + [kspec(kcol, s) for s in (-1, 0, 1)] + [kspec(vcol, s) for s in (-1, 0, 1)] + [ctx_k, ctx_v],
        out_specs=pl.BlockSpec((blk, qw), lambda b, i, p: (b * nblk + i, p)),
        out_shape=jax.ShapeDtypeStruct((qkv.shape[0], d), BF16),
        compiler_params=_cparams(3),
        name="window_attn",
    )(sinks, qkv, *([qkv] * 8))
    return pl.pallas_call(
        functools.partial(kern, local=False, blk=ctx_len),
        grid=(batch, 1, n_pairs),
        in_specs=[smem, pl.BlockSpec((ctx_len, qw), lambda b, i, p: (cb0 + b, p)), ctx_k, ctx_v,
                  pl.BlockSpec(memory_space=pl.ANY)],
        out_specs=pl.BlockSpec((ctx_len, qw), lambda b, i, p: (cb0 + b, p)),
        out_shape=jax.ShapeDtypeStruct((qkv.shape[0], d), BF16),
        input_output_aliases={4: 0},
        compiler_params=_cparams(3),
        name="ctx_attn",
    )(sinks, qkv, qkv, qkv, lat)


def _nattn_kernel(q_ref, k0, k1, k2, v0, v1, v2, kc_ref, vc_ref, bias_ref, o_ref, ks_ref, vs_ref,
                  *, grid_rows, rows_per_blk):
    j = pl.program_id(2)
    blk = k0.shape[0]
    for t, (kr, vr) in enumerate(((k0, v0), (k1, v1), (k2, v2))):
        ks_ref[t * blk:(t + 1) * blk, :] = kr[...]
        vs_ref[t * blk:(t + 1) * blk, :] = vr[...]
    kc = kc_ref[...]
    vc = vc_ref[...]
    lane_lo = lax.broadcasted_iota(jnp.int32, (GRID_W, LANES), 1) < HEAD_DIM
    strip = NA_WIN_ROWS * GRID_W
    for r in range(rows_per_blk):
        row = j * rows_per_blk + r
        r0 = jnp.clip(row - NA_WIN_ROWS // 2, 0, grid_rows - NA_WIN_ROWS)
        start = pl.multiple_of((r0 - (j - 1) * rows_per_blk) * GRID_W, GRID_W)
        cls = row - r0
        qa, qb = _masked_halves(q_ref[r * GRID_W:(r + 1) * GRID_W, :], lane_lo)
        qs = jnp.concatenate([qa, qb], axis=0)
        kn = ks_ref[pl.ds(start, strip), :]
        vn = vs_ref[pl.ds(start, strip), :]
        s_nb = _nt_dot(qs, kn) + bias_ref[cls]
        s_cx = _nt_dot(qs, kc)
        o = _softmax_pv([s_nb, s_cx], [vn, vc], None)
        o_ref[r * GRID_W:(r + 1) * GRID_W, :] = jnp.where(lane_lo, o[:GRID_W], o[GRID_W:]).astype(o_ref.dtype)


def _na_bias_table(rpb):
    h = rpb.shape[0]
    n_dcol = 2 * NA_WIN_COLS - 1
    cols = jnp.arange(GRID_W)
    col_start = jnp.clip(cols - NA_WIN_COLS // 2, 0, GRID_W - NA_WIN_COLS)
    inwin = (cols[None, :] >= col_start[:, None]) & (cols[None, :] < col_start[:, None] + NA_WIN_COLS)
    dcol = cols[None, :] - cols[:, None] + NA_WIN_COLS - 1
    pick = (dcol[None] == jnp.arange(n_dcol)[:, None, None]).astype(F32).reshape(n_dcol, -1)
    spread = jnp.dot(rpb.astype(F32).reshape(-1, n_dcol), pick, precision=lax.Precision.HIGHEST)
    spread = spread.reshape(h, 2 * NA_WIN_ROWS - 1, GRID_W, GRID_W)
    spread = jnp.where(inwin[None, None], spread, NEG_INF)
    tab = jnp.stack([spread[:, NA_WIN_ROWS - 1 - cls:2 * NA_WIN_ROWS - 1 - cls]
                     for cls in range(NA_WIN_ROWS)], axis=1)
    tab = tab.transpose(0, 1, 3, 2, 4).reshape(h, NA_WIN_ROWS, GRID_W, NA_WIN_ROWS * GRID_W)
    tab = tab.reshape(h // 2, 2, NA_WIN_ROWS, GRID_W, NA_WIN_ROWS * GRID_W).transpose(0, 2, 1, 3, 4)
    return tab.reshape(h // 2, NA_WIN_ROWS, 2 * GRID_W, NA_WIN_ROWS * GRID_W)


def neighbourhood_attention(qkv, rpb, *, batch, seq_len, ctx_len, d, with_ctx_out):
    n_pairs = d // LANES
    rows_per_blk = NA_WIN_ROWS
    blk = rows_per_blk * GRID_W
    grid_rows = seq_len // GRID_W
    nblk = seq_len // blk
    cb0 = batch * seq_len // ctx_len
    bias = _na_bias_table(rpb)

    def kspec(col0, shift):
        return pl.BlockSpec((blk, LANES),
                            lambda b, p, j: (b * nblk + jnp.clip(j + shift, 0, nblk - 1), col0 + p))

    ctx_k = pl.BlockSpec((ctx_len, LANES), lambda b, p, j: (cb0 + b, n_pairs + p))
    ctx_v = pl.BlockSpec((ctx_len, LANES), lambda b, p, j: (cb0 + b, 2 * n_pairs + p))
    lat = pl.pallas_call(
        functools.partial(_nattn_kernel, grid_rows=grid_rows, rows_per_blk=rows_per_blk),
        grid=(batch, n_pairs, nblk),
        in_specs=[pl.BlockSpec((blk, LANES), lambda b, p, j: (b * nblk + j, p))]
        + [kspec(n_pairs, s) for s in (-1, 0, 1)] + [kspec(2 * n_pairs, s) for s in (-1, 0, 1)]
        + [ctx_k, ctx_v,
           pl.BlockSpec((None, NA_WIN_ROWS, 2 * GRID_W, NA_WIN_ROWS * GRID_W), lambda b, p, j: (p, 0, 0, 0))],
        out_specs=pl.BlockSpec((blk, LANES), lambda b, p, j: (b * nblk + j, p)),
        out_shape=jax.ShapeDtypeStruct((batch * seq_len, d), BF16),
        scratch_shapes=[pltpu.VMEM((3 * blk, LANES), BF16), pltpu.VMEM((3 * blk, LANES), BF16)],
        compiler_params=_cparams(3),
        name="neighbourhood_attn",
    )(qkv, *([qkv] * 8), bias)
    if not with_ctx_out:
        return lat
    ctx = pl.pallas_call(
        _cattn_kernel,
        grid=(batch, n_pairs),
        in_specs=[pl.BlockSpec((ctx_len, LANES), lambda b, p: (cb0 + b, p)),
                  pl.BlockSpec((ctx_len, LANES), lambda b, p: (cb0 + b, n_pairs + p)),
                  pl.BlockSpec((ctx_len, LANES), lambda b, p: (cb0 + b, 2 * n_pairs + p))],
        out_specs=pl.BlockSpec((ctx_len, LANES), lambda b, p: (b, p)),
        out_shape=jax.ShapeDtypeStruct((batch * ctx_len, d), BF16),
        compiler_params=_cparams(2),
        name="ctx_mha",
    )(qkv, qkv, qkv)
    return jnp.concatenate([lat, ctx], axis=0)


def _cattn_kernel(q_ref, k_ref, v_ref, o_ref):
    rows = q_ref.shape[0]
    lane_lo = lax.broadcasted_iota(jnp.int32, (rows, LANES), 1) < HEAD_DIM
    qa, qb = _masked_halves(q_ref[...], lane_lo)
    qs = jnp.concatenate([qa, qb], axis=0)
    o = _softmax_pv([_nt_dot(qs, k_ref[...])], [v_ref[...]], None)
    o_ref[...] = jnp.where(lane_lo, o[:rows], o[rows:]).astype(o_ref.dtype)


def _router_kernel(x_ref, g_ref, sh_ref, sc_ref, wh_ref, wl_ref, b_ref, h_ref, idx_ref, gate_ref, cnt_ref,
                   run_ref):
    @pl.when(pl.program_id(0) == 0)
    def _():
        run_ref[...] = jnp.zeros_like(run_ref)

    h = _norm_mod(x_ref[...], g_ref[...], sh_ref[0], sc_ref[0])
    h_hi = h.astype(BF16)
    h_ref[...] = h_hi
    h_lo = (h - h_hi.astype(F32)).astype(BF16)
    logits = (jnp.dot(h_hi, wh_ref[...], preferred_element_type=F32)
              + jnp.dot(h_hi, wl_ref[...], preferred_element_type=F32)
              + jnp.dot(h_lo, wh_ref[...], preferred_element_type=F32)) + b_ref[...]
    lane = lax.broadcasted_iota(jnp.int32, logits.shape, 1)
    tm = logits.shape[0]
    idx_out = jnp.zeros(logits.shape, jnp.int32)
    val_out = jnp.zeros(logits.shape, F32)
    tri = jnp.where(lax.broadcasted_iota(jnp.int32, (tm, tm), 0) > lax.broadcasted_iota(jnp.int32, (tm, tm), 1),
                    1.0, 0.0).astype(BF16)
    run = run_ref[...]
    top = None
    denom = 0.0
    for k in range(TOP_K):
        m = logits.max(axis=-1, keepdims=True)
        sel = jnp.min(jnp.where(logits == m, lane, LANES), axis=-1, keepdims=True)
        if top is None:
            top = m
        e = jnp.exp(m - top)
        denom = denom + e
        onehot = lane == sel
        before = jnp.dot(tri, jnp.where(onehot, 1.0, 0.0).astype(BF16), preferred_element_type=F32)
        rank = jnp.sum(jnp.where(onehot, before + run, 0.0), axis=-1, keepdims=True).astype(jnp.int32)
        run = run + jnp.sum(jnp.where(onehot, 1.0, 0.0), axis=0, keepdims=True)
        idx_out = jnp.where(lane == k, sel, idx_out)
        idx_out = jnp.where(lane == TOP_K + k, rank, idx_out)
        val_out = jnp.where(lane == k, e, val_out)
        logits = jnp.where(onehot, -jnp.inf, logits)
    run_ref[...] = run
    cnt_ref[...] = run
    idx_ref[...] = idx_out
    gate_ref[...] = val_out / denom


def router(x, g, shift, scale, w_router, b_router, *, seg_rows, rows, tm=ROW_TILE):
    m, d = rows, x.shape[1]
    n_seg = shift.shape[0]
    n_e = w_router.shape[1]
    w_pad = jnp.zeros((d, LANES), F32).at[:, :n_e].set(w_router)
    w_hi = w_pad.astype(BF16)
    w_lo = (w_pad - w_hi.astype(F32)).astype(BF16)
    b_pad = jnp.full((1, LANES), -jnp.inf, F32).at[0, :n_e].set(b_router)
    seg = lambda i: (jnp.minimum(i * tm // seg_rows, n_seg - 1), 0, 0)
    return pl.pallas_call(
        _router_kernel,
        grid=(m // tm,),
        in_specs=[pl.BlockSpec((tm, d), lambda i: (i, 0)),
                  pl.BlockSpec((1, d), lambda i: (0, 0)),
                  pl.BlockSpec((1, 1, d), seg),
                  pl.BlockSpec((1, 1, d), seg),
                  pl.BlockSpec((d, LANES), lambda i: (0, 0)),
                  pl.BlockSpec((d, LANES), lambda i: (0, 0)),
                  pl.BlockSpec((1, LANES), lambda i: (0, 0))],
        out_specs=[pl.BlockSpec((tm, d), lambda i: (i, 0)),
                   pl.BlockSpec((tm, LANES), lambda i: (i, 0)),
                   pl.BlockSpec((tm, LANES), lambda i: (i, 0)),
                   pl.BlockSpec((1, LANES), lambda i: (0, 0))],
        out_shape=[jax.ShapeDtypeStruct((m, d), BF16),
                   jax.ShapeDtypeStruct((m, LANES), jnp.int32),
                   jax.ShapeDtypeStruct((m, LANES), F32),
                   jax.ShapeDtypeStruct((1, LANES), F32)],
        scratch_shapes=[pltpu.VMEM((1, LANES), F32)],
        compiler_params=_cparams(1),
        name="router",
    )(x, g, shift, scale, w_hi, w_lo, b_pad)


def _expert_kernel(blk_e_ref, n_used_ref, x_ref, wgu_ref, bgu_ref, wd_ref, bd_ref, o_ref, wgu_s, wd_s, *, f):
    i = pl.program_id(0)
    used = i < n_used_ref[0]

    @pl.when(used & ((i == 0) | (blk_e_ref[i] != blk_e_ref[jnp.maximum(i - 1, 0)])))
    def _():
        wgu_s[...] = wgu_ref[0].astype(BF16)
        wd_s[...] = wd_ref[0].astype(BF16)

    @pl.when(used)
    def _():
        gu = jnp.dot(x_ref[...], wgu_s[...], preferred_element_type=F32) + bgu_ref[0]
        g = jnp.minimum(gu[:, :f], SWIGLU_LIMIT)
        u = jnp.clip(gu[:, f:], -SWIGLU_LIMIT, SWIGLU_LIMIT)
        act = (u + 1) * (g * jax.nn.sigmoid(SWIGLU_ALPHA * g))
        y = jnp.dot(act.astype(BF16), wd_s[...], preferred_element_type=F32) + bd_ref[0]
        o_ref[...] = y.astype(o_ref.dtype)

    @pl.when(i >= n_used_ref[0])
    def _():
        o_ref[...] = jnp.zeros_like(o_ref)


def expert_mlp(xs, blk_e, n_used, w_gu, b_gu, w_down, b_down, *, layer, bm=MOE_ROWS):
    n_slot, d = xs.shape
    _, n_e, _, f2 = w_gu.shape
    f = f2 // 2
    return pl.pallas_call(
        functools.partial(_expert_kernel, f=f),
        grid_spec=pltpu.PrefetchScalarGridSpec(
            num_scalar_prefetch=2,
            grid=(n_slot // bm,),
            in_specs=[pl.BlockSpec((bm, d), lambda i, be, nu: (i, 0)),
                      pl.BlockSpec((None, 1, d, f2), lambda i, be, nu: (layer, be[i], 0, 0)),
                      pl.BlockSpec((1, 1, f2), lambda i, be, nu: (be[i], 0, 0)),
                      pl.BlockSpec((None, 1, f, d), lambda i, be, nu: (layer, be[i], 0, 0)),
                      pl.BlockSpec((1, 1, d), lambda i, be, nu: (be[i], 0, 0))],
            out_specs=pl.BlockSpec((bm, d), lambda i, be, nu: (i, 0)),
            scratch_shapes=[pltpu.VMEM((d, f2), BF16), pltpu.VMEM((f, d), BF16)]),
        out_shape=jax.ShapeDtypeStruct((n_slot, d), BF16),
        compiler_params=pltpu.CompilerParams(dimension_semantics=("arbitrary",),
                                             vmem_limit_bytes=EXPERT_VMEM_LIMIT),
        name="expert_mlp",
    )(blk_e, n_used, xs, w_gu, b_gu.reshape(n_e, 1, f2), w_down, b_down.reshape(n_e, 1, d))


def _combine_kernel(res_ref, y_ref, w_ref, gate_ref, o_ref):
    d = o_ref.shape[1]
    w = w_ref[...]
    y = w[:, 0:1] * y_ref[:, 0:d].astype(F32)
    for k in range(1, TOP_K):
        y = y + w[:, k:k + 1] * y_ref[:, k * d:(k + 1) * d].astype(F32)
    o_ref[...] = res_ref[...] + gate_ref[0] * y


def moe_layer(x, g, shift, scale, gate, w_router, b_router, w_gu, b_gu, w_down, b_down, *, layer, seg_rows,
              rows, bm=MOE_ROWS):
    m, d = rows, x.shape[1]
    n_e = w_router.shape[1]
    h, route, gates, counts = router(x, g, shift, scale, w_router, b_router, seg_rows=seg_rows, rows=rows)
    idx = route[:, :TOP_K]
    rank = route[:, TOP_K:2 * TOP_K]
    n_asg = m * TOP_K
    counts = counts[0, :n_e].astype(jnp.int32)
    padded = (counts + bm - 1) // bm * bm
    pad_end = jnp.cumsum(padded)
    pad_start = pad_end - padded
    start_of = jnp.sum(jnp.where(idx[:, :, None] == jnp.arange(n_e, dtype=jnp.int32), pad_start, 0), axis=-1)
    dest = (start_of + rank).reshape(-1)
    n_blk = -(-(n_asg + n_e * (bm - 1)) // bm)
    n_slot = n_blk * bm
    tok = (jnp.arange(n_asg, dtype=jnp.int32) // TOP_K)
    slot_tok = jnp.zeros((n_slot,), jnp.int32).at[dest].set(tok, unique_indices=True)
    blk_start = jnp.arange(n_blk, dtype=jnp.int32) * bm
    blk_e = jnp.minimum(jnp.sum((pad_end[None, :] <= blk_start[:, None]).astype(jnp.int32), axis=1), n_e - 1)
    n_used = (pad_end[-1] // bm).astype(jnp.int32).reshape(1)
    xs = h.at[slot_tok].get(mode="promise_in_bounds")
    y_slot = expert_mlp(xs, blk_e, n_used, w_gu, b_gu, w_down, b_down, layer=layer, bm=bm)
    y_tok = y_slot.at[dest].get(mode="promise_in_bounds").reshape(m, TOP_K * d)
    tm = 256
    n_seg = gate.shape[0]
    seg = lambda i: (jnp.minimum(i * tm // seg_rows, n_seg - 1), 0, 0)
    return pl.pallas_call(
        _combine_kernel,
        grid=(m // tm,),
        in_specs=[pl.BlockSpec((tm, d), lambda i: (i, 0)),
                  pl.BlockSpec((tm, TOP_K * d), lambda i: (i, 0)),
                  pl.BlockSpec((tm, LANES), lambda i: (i, 0)),
                  pl.BlockSpec((1, 1, d), seg)],
        out_specs=pl.BlockSpec((tm, d), lambda i: (i, 0)),
        out_shape=jax.ShapeDtypeStruct(x.shape, F32),
        input_output_aliases={0: 0},
        compiler_params=_cparams(1),
        name="moe_combine",
    )(x, y_tok, gates, gate)


LC_R2 = 128
LC_C = 256
LC_KB = 8
LC_LANES = 2048


@functools.lru_cache(maxsize=None)
def _lc_mats(n):
    nn = 2 * n
    r1 = nn // LC_R2
    tau = 2.0 * np.pi
    k1 = np.arange(r1)
    ang1 = tau * ((k1[:, None] * np.arange(r1)[None, :]) % r1) / r1
    f1 = np.empty((2 * r1, r1))
    f1[0::2] = np.cos(ang1)
    f1[1::2] = -np.sin(ang1)
    kk = k1[:, None] + r1 * np.arange(LC_R2)[None, :]
    th = tau * ((kk[:, :, None] * np.arange(LC_R2)[None, None, :]) % nn) / nn
    c, s = np.cos(th), np.sin(th)
    f2 = np.concatenate([np.concatenate([c, s], 2), np.concatenate([-s, c], 2)], 1)
    ct, st = c.transpose(0, 2, 1), s.transpose(0, 2, 1)
    f3 = np.concatenate([np.concatenate([ct, -st], 2), np.concatenate([st, ct], 2)], 1)
    ang4 = tau * ((np.arange(r1 // 2)[:, None] * k1[None, :]) % r1) / r1
    f4 = np.concatenate([np.cos(ang4), -np.sin(ang4)], 1) / nn
    return tuple(jnp.asarray(a, BF16) for a in (f1, f2, f3, f4))


@functools.lru_cache(maxsize=None)
def _dft_small_mats(n):
    nn = 2 * n
    ang = 2.0 * np.pi * ((np.arange(nn)[:, None] * np.arange(nn)[None, :]) % nn) / nn
    fwd = np.concatenate([np.cos(ang), -np.sin(ang)], 0)
    inv = np.concatenate([np.cos(ang[:n]), -np.sin(ang[:n])], 1) / nn
    return jnp.asarray(fwd, BF16), jnp.asarray(inv, BF16)


def _lc_first_kernel(x_ref, f_ref, o_ref):
    o_ref[...] = jnp.dot(f_ref[...], x_ref[...].astype(BF16), preferred_element_type=F32).astype(o_ref.dtype)


def lc_first(x, f1, s1, off=0):
    _, s2, nh, l = x.shape
    lc = min(LC_LANES, l)
    r2 = f1.shape[0]
    return pl.pallas_call(
        _lc_first_kernel,
        grid=(s1, s2, l // lc),
        in_specs=[pl.BlockSpec((None, None, nh, lc), lambda a, b, j: (off + a, b, 0, j)),
                  pl.BlockSpec((r2, nh), lambda a, b, j: (0, 0))],
        out_specs=pl.BlockSpec((None, None, r2, lc), lambda a, b, j: (a, b, 0, j)),
        out_shape=jax.ShapeDtypeStruct((s1, s2, r2, l), BF16),
        compiler_params=_cparams(3),
        name="lc_first",
    )(x, f1[:, :nh])


def _lc_mid_kernel(a_ref, f2_ref, *rest, kb, spectrum_only):
    if spectrum_only:
        (o_ref,) = rest
    else:
        h_ref, f3_ref, o_ref = rest
    c = a_ref.shape[-1]
    for kk in range(kb):
        a_in = a_ref[2 * kk:2 * kk + 2].reshape(2 * LC_R2, c)
        x = jnp.dot(f2_ref[kk], a_in, preferred_element_type=F32)
        if spectrum_only:
            o_ref[kk] = x.astype(o_ref.dtype)
            continue
        h = h_ref[kk].astype(F32)
        xr, xi = x[:LC_R2], x[LC_R2:]
        hr, hi = h[:LC_R2], h[LC_R2:]
        prod = jnp.concatenate([xr * hr - xi * hi, xr * hi + xi * hr], axis=0).astype(BF16)
        g = jnp.dot(f3_ref[kk], prod, preferred_element_type=F32)
        o_ref[0, kk] = g[:LC_R2].astype(o_ref.dtype)
        o_ref[1, kk] = g[LC_R2:].astype(o_ref.dtype)


def lc_spectrum(a, f2):
    cb, _, r2, _, c = a.shape
    r1 = r2 // 2
    kb = min(LC_KB, r1)
    return pl.pallas_call(
        functools.partial(_lc_mid_kernel, kb=kb, spectrum_only=True),
        grid=(r1 // kb, cb),
        in_specs=[pl.BlockSpec((None, None, 2 * kb, LC_R2, c), lambda j, i: (i, 0, j, 0, 0)),
                  pl.BlockSpec((kb, 2 * LC_R2, 2 * LC_R2), lambda j, i: (j, 0, 0))],
        out_specs=pl.BlockSpec((None, kb, 2 * LC_R2, c), lambda j, i: (i, j, 0, 0)),
        out_shape=jax.ShapeDtypeStruct((cb, r1, 2 * LC_R2, c), BF16),
        compiler_params=_cparams(2),
        name="lc_spectrum",
    )(a, f2)


def lc_mid(a, h, f2, f3):
    cb, bsz, r2, _, c = a.shape
    r1 = r2 // 2
    kb = min(LC_KB, r1)
    mat = pl.BlockSpec((kb, 2 * LC_R2, 2 * LC_R2), lambda j, i, b: (j, 0, 0))
    return pl.pallas_call(
        functools.partial(_lc_mid_kernel, kb=kb, spectrum_only=False),
        grid=(r1 // kb, cb, bsz),
        in_specs=[pl.BlockSpec((None, None, 2 * kb, LC_R2, c), lambda j, i, b: (i, b, j, 0, 0)),
                  mat,
                  pl.BlockSpec((None, kb, 2 * LC_R2, c), lambda j, i, b: (i, j, 0, 0)),
                  mat],
        out_specs=pl.BlockSpec((None, None, 2, kb, LC_R2, c), lambda j, i, b: (i, b, 0, j, 0, 0)),
        out_shape=jax.ShapeDtypeStruct((cb, bsz, 2, r1, LC_R2, c), BF16),
        compiler_params=_cparams(3),
        name="lc_mid",
    )(a, f2, h, f3)


def _lc_last_kernel(g_ref, f_ref, u_ref, gate_ref, skip_ref, o_ref):
    y = jnp.dot(f_ref[...], g_ref[...], preferred_element_type=F32)
    u = u_ref[...].astype(F32)
    o_ref[...] = (gate_ref[...].astype(F32) * (y + skip_ref[...] * u)).astype(o_ref.dtype)


def lc_last(g, f4, u_arr, u_off, gate_arr, gate_off, skip_row):
    cb, bsz, r2, l = g.shape
    nh = r2 // 4
    lc = min(LC_LANES, l)
    return pl.pallas_call(
        _lc_last_kernel,
        grid=(cb, bsz, l // lc),
        in_specs=[pl.BlockSpec((None, None, r2, lc), lambda i, b, j: (i, b, 0, j)),
                  pl.BlockSpec((nh, r2), lambda i, b, j: (0, 0)),
                  pl.BlockSpec((None, None, nh, lc), lambda i, b, j: (u_off + i, b, 0, j)),
                  pl.BlockSpec((None, None, nh, lc), lambda i, b, j: (gate_off + i, b, 0, j)),
                  pl.BlockSpec((None, 1, lc), lambda i, b, j: (i, 0, j))],
        out_specs=pl.BlockSpec((None, None, nh, lc), lambda i, b, j: (i, b, 0, j)),
        out_shape=jax.ShapeDtypeStruct((cb, bsz, nh, l), BF16),
        compiler_params=_cparams(3),
        name="lc_last",
    )(g, f4, u_arr, gate_arr, skip_row)


def _plain_mm_kernel(a_ref, b_ref, o_ref):
    o_ref[...] = jnp.dot(a_ref[...], b_ref[...].astype(BF16), preferred_element_type=F32)


def plain_mm(a, b, tn=512):
    m, k = a.shape
    n = b.shape[1]
    tn = _col_tile(n, tn)
    return pl.pallas_call(
        _plain_mm_kernel,
        grid=(n // tn,),
        in_specs=[pl.BlockSpec((m, k), lambda j: (0, 0)), pl.BlockSpec((k, tn), lambda j: (0, j))],
        out_specs=pl.BlockSpec((m, tn), lambda j: (0, j)),
        out_shape=jax.ShapeDtypeStruct((m, n), F32),
        compiler_params=_cparams(1),
        name="plain_mm",
    )(a, b)


def _short_conv_kernel(u_ref, gate_ref, h_ref, skip_ref, fwd_ref, inv_ref, o_ref):
    u = u_ref[...]
    x = jnp.dot(fwd_ref[...], u.astype(BF16), preferred_element_type=F32)
    h = h_ref[...]
    nn = x.shape[0] // 2
    xr, xi, hr, hi = x[:nn], x[nn:], h[:nn], h[nn:]
    prod = jnp.concatenate([xr * hr - xi * hi, xr * hi + xi * hr], axis=0).astype(BF16)
    y = jnp.dot(inv_ref[...], prod, preferred_element_type=F32)
    o_ref[...] = (gate_ref[...].astype(F32) * (y + skip_ref[...] * u.astype(F32))).astype(o_ref.dtype)


def short_long_conv(u_arr, u_col, gate_arr, gate_col, spec, skip, fwd, inv, *, batch, seq_len, tl=512):
    d = spec.shape[1]
    tl = _col_tile(d, tl)
    uc, gc = u_col // tl, gate_col // tl
    nn2 = spec.shape[0]
    return pl.pallas_call(
        _short_conv_kernel,
        grid=(batch, d // tl),
        in_specs=[pl.BlockSpec((seq_len, tl), lambda b, j: (b, uc + j)),
                  pl.BlockSpec((seq_len, tl), lambda b, j: (b, gc + j)),
                  pl.BlockSpec((nn2, tl), lambda b, j: (0, j)),
                  pl.BlockSpec((1, tl), lambda b, j: (0, j)),
                  pl.BlockSpec((nn2, seq_len), lambda b, j: (0, 0)),
                  pl.BlockSpec((seq_len, nn2), lambda b, j: (0, 0))],
        out_specs=pl.BlockSpec((seq_len, tl), lambda b, j: (b, j)),
        out_shape=jax.ShapeDtypeStruct((batch * seq_len, d), BF16),
        compiler_params=_cparams(2),
        name="short_long_conv",
    )(u_arr, gate_arr, spec, skip, fwd[:, :seq_len], inv)


def _hyena_taps(n, d, f_w1, f_b1, f_w2, f_b2, f_w3, f_b3, f_freq, f_w4):
    hp = lax.Precision.HIGHEST
    lin = jnp.linspace(0.0, 1.0, n, dtype=F32)
    idx = jnp.arange(n, dtype=F32)
    bands = jnp.linspace(1e-4, HY_BANDS - 1, HY_BANDS, dtype=F32)[None, :]
    deltas = jnp.abs(jnp.linspace(math.log(HY_DECAY_TARGET) / HY_FAST_DECAY,
                                  math.log(HY_DECAY_TARGET) / HY_SLOW_DECAY, d, dtype=F32))

    def branch(t, pos, direction):
        t = t[:, None]
        ang = (2.0 * math.pi / n) * pos[:, None]
        emb = jnp.concatenate([t, jnp.cos(bands * ang), -jnp.sin(bands * ang)], axis=-1)
        a = jnp.sin(f_freq * (jnp.dot(emb, f_w1, precision=hp) + f_b1))
        a = jnp.sin(f_freq * (jnp.dot(a, f_w2, precision=hp) + f_b2))
        a = jnp.sin(f_freq * (jnp.dot(a, f_w3, precision=hp) + f_b3))
        decay = jnp.exp(-t * deltas[None, :])
        w4 = f_w4.reshape(f_w4.shape[0], HY_ORDER, 2, d)[:, :, direction]
        return [jnp.dot(a, w4[:, o], precision=hp) * decay for o in range(HY_ORDER)]

    fwd = branch(lin, idx, 0)
    bwd = branch(lin[:0:-1], idx[:0:-1], 1)
    zero = jnp.zeros((1, d), F32)
    return [jnp.concatenate([fwd[o], zero, bwd[o]], axis=0) for o in range(HY_ORDER)]


def hyena_long_convs(zc, taps, skip, *, batch, seq_len):
    cb3, rows, c = zc.shape
    cb = cb3 // 3
    d = cb * c
    nh = seq_len // LC_R2
    lanes = LC_R2 * c
    f1, f2, f3, f4 = _lc_mats(seq_len)
    zc_v = zc.reshape(cb3, batch, nh, lanes)
    u, u_off = zc_v, 2 * cb
    for o in range(HY_ORDER):
        taps_b = taps[o].reshape(2 * seq_len, cb, c).transpose(1, 0, 2).reshape(cb, 1, 2 * nh, lanes)
        spec = lc_spectrum(lc_first(taps_b, f1, cb).reshape(cb, 1, 4 * nh, LC_R2, c), f2)
        a = lc_first(u, f1, cb, u_off).reshape(cb, batch, 4 * nh, LC_R2, c)
        g = lc_mid(a, spec, f2, f3).reshape(cb, batch, 4 * nh, lanes)
        skip_row = jnp.tile(skip[o].reshape(cb, 1, c), (1, 1, LC_R2))
        u = lc_last(g, f4, u, u_off, zc_v, o * cb, skip_row)
        u_off = 0
    return u.reshape(cb, rows, c)


HC = 128
HC_PAIR = 2
LC_TL = 16


@functools.lru_cache(maxsize=None)
def _hy_mats(n):
    nn = 2 * n
    r1 = nn // LC_R2
    tau = 2.0 * np.pi
    k1 = np.arange(r1)
    t = LC_R2 * np.arange(r1)[None, :] + np.arange(LC_R2)[:, None]
    ang1 = tau * ((k1[None, :, None] * t[:, None, :]) % nn) / nn
    f1 = np.empty((LC_R2, 2 * r1, r1))
    f1[:, 0::2] = np.cos(ang1)
    f1[:, 1::2] = -np.sin(ang1)
    ang2 = tau * ((np.arange(LC_R2)[:, None] * np.arange(LC_R2)[None, :]) % LC_R2) / LC_R2
    c, s = np.cos(ang2), np.sin(ang2)
    f2 = np.empty((2 * LC_R2, 2 * LC_R2))
    f2[:LC_R2, 0::2], f2[:LC_R2, 1::2] = c, s
    f2[LC_R2:, 0::2], f2[LC_R2:, 1::2] = -s, c
    f3 = np.empty((2 * LC_R2, 2 * LC_R2))
    f3[0::2, :LC_R2], f3[0::2, LC_R2:] = c, -s
    f3[1::2, :LC_R2], f3[1::2, LC_R2:] = s, c
    m = LC_R2 * np.arange(r1 // 2)[None, :] + np.arange(LC_R2)[:, None]
    ang4 = tau * ((m[:, :, None] * k1[None, None, :]) % nn) / nn
    f4 = np.empty((LC_R2, r1 // 2, 2 * r1))
    f4[:, :, 0::2] = np.cos(ang4) / nn
    f4[:, :, 1::2] = -np.sin(ang4) / nn
    return tuple(jnp.asarray(a, BF16) for a in (f1, f2, f3, f4))


def _hy1_kernel(x_ref, f_ref, o_ref, *, tl_n, nh):
    c = pl.program_id(2)
    for tl in range(tl_n):
        xs = jnp.concatenate([x_ref[h, pl.ds(c * tl_n + tl, nh, stride=LC_R2), :] for h in range(HC_PAIR)],
                             axis=1).astype(BF16)
        a = jnp.dot(f_ref[tl], xs, preferred_element_type=F32)
        packed = pltpu.bitcast(a.astype(BF16), jnp.uint32)
        for h in range(HC_PAIR):
            for j in range(packed.shape[0] // 8):
                o_ref[h, j, tl * 8:(tl + 1) * 8, :] = packed[j * 8:(j + 1) * 8, h * HC:(h + 1) * HC]


def hy_stage1(x, f1, *, blk_off, n_blk, batch, rows_hi):
    r2 = f1.shape[1]
    rows = rows_hi * LC_R2
    tl_n = LC_TL
    return pl.pallas_call(
        functools.partial(_hy1_kernel, tl_n=tl_n, nh=rows_hi),
        grid=(n_blk // HC_PAIR, batch, LC_R2 // tl_n),
        in_specs=[pl.BlockSpec((HC_PAIR, rows, HC), lambda p, b, c: (blk_off // HC_PAIR + p, b, 0)),
                  pl.BlockSpec((tl_n, r2, rows_hi), lambda p, b, c: (c, 0, 0))],
        out_specs=pl.BlockSpec((HC_PAIR, None, r2 // 16, tl_n * 8, HC), lambda p, b, c: (p, b, 0, c, 0)),
        out_shape=jax.ShapeDtypeStruct((n_blk, batch, r2 // 16, LC_R2 * 8, HC), jnp.uint32),
        compiler_params=_cparams(3),
        name="hy_stage1",
    )(x, f1[:, :, :rows_hi])


def _hy2_kernel(a_ref, f2_ref, *rest, spectrum_only):
    if spectrum_only:
        (o_ref,) = rest
    else:
        h_ref, f3_ref, o_ref = rest
    for kk in range(8):
        w = jnp.concatenate([a_ref[h, pl.ds(kk, LC_R2, stride=8), :] for h in range(HC_PAIR)], axis=1)
        a_in = pltpu.bitcast(w, BF16)
        x = jnp.dot(f2_ref[...], a_in, preferred_element_type=F32)
        if spectrum_only:
            o_ref[kk] = x.astype(o_ref.dtype)
            continue
        hs = h_ref[kk].astype(F32)
        xr, xi = x[:LC_R2], x[LC_R2:]
        hr, hi = hs[:LC_R2], hs[LC_R2:]
        prod = jnp.concatenate([xr * hr - xi * hi, xr * hi + xi * hr], axis=0).astype(BF16)
        g = jnp.dot(f3_ref[...], prod, preferred_element_type=F32)
        packed = pltpu.bitcast(g.astype(BF16), jnp.uint32)
        for h in range(HC_PAIR):
            o_ref[h, pl.ds(kk, LC_R2, stride=8), :] = packed[:, h * HC:(h + 1) * HC]


def hy_spectrum(a, f2):
    cb, _, nj, rows, _ = a.shape
    mat = pl.BlockSpec((2 * LC_R2, 2 * LC_R2), lambda j, p: (0, 0))
    return pl.pallas_call(
        functools.partial(_hy2_kernel, spectrum_only=True),
        grid=(nj, cb // HC_PAIR),
        in_specs=[pl.BlockSpec((HC_PAIR, None, None, rows, HC), lambda j, p: (p, 0, j, 0, 0)), mat],
        out_specs=pl.BlockSpec((None, 8, 2 * LC_R2, HC_PAIR * HC), lambda j, p: (p, j, 0, 0)),
        out_shape=jax.ShapeDtypeStruct((cb // HC_PAIR, nj * 8, 2 * LC_R2, HC_PAIR * HC), BF16),
        compiler_params=_cparams(2),
        name="hy_spectrum",
    )(a, f2)


def hy_stage23(a, spec, f2, f3):
    cb, bsz, nj, rows, _ = a.shape
    mat = pl.BlockSpec((2 * LC_R2, 2 * LC_R2), lambda j, p, b: (0, 0))
    blk = pl.BlockSpec((HC_PAIR, None, None, rows, HC), lambda j, p, b: (p, b, j, 0, 0))
    return pl.pallas_call(
        functools.partial(_hy2_kernel, spectrum_only=False),
        grid=(nj, cb // HC_PAIR, bsz),
        in_specs=[blk, mat,
                  pl.BlockSpec((None, 8, 2 * LC_R2, HC_PAIR * HC), lambda j, p, b: (p, j, 0, 0)), mat],
        out_specs=blk,
        out_shape=jax.ShapeDtypeStruct(a.shape, jnp.uint32),
        compiler_params=_cparams(3),
        name="hy_stage23",
    )(a, f2, spec, f3)


def _hy4_kernel(g_ref, f_ref, o_ref, *, tl_n, nh):
    c = pl.program_id(2)
    for tl in range(tl_n):
        w = jnp.concatenate([g_ref[h, :, tl * 8:(tl + 1) * 8, :].reshape(-1, HC) for h in range(HC_PAIR)], axis=1)
        g_in = pltpu.bitcast(w, BF16)
        y = jnp.dot(f_ref[tl], g_in, preferred_element_type=F32)
        for h in range(HC_PAIR):
            o_ref[h, pl.ds(c * tl_n + tl, nh, stride=LC_R2), :] = y[:, h * HC:(h + 1) * HC]


def hy_stage4(g, f4, *, seq_len):
    cb, bsz, nj, _, _ = g.shape
    nh = seq_len // LC_R2
    tl_n = LC_TL
    return pl.pallas_call(
        functools.partial(_hy4_kernel, tl_n=tl_n, nh=nh),
        grid=(cb // HC_PAIR, bsz, LC_R2 // tl_n),
        in_specs=[pl.BlockSpec((HC_PAIR, None, nj, tl_n * 8, HC), lambda p, b, c: (p, b, 0, c, 0)),
                  pl.BlockSpec((tl_n, nh, 16 * nj), lambda p, b, c: (c, 0, 0))],
        out_specs=pl.BlockSpec((HC_PAIR, seq_len, HC), lambda p, b, c: (p, b, 0)),
        out_shape=jax.ShapeDtypeStruct((cb, bsz * seq_len, HC), F32),
        compiler_params=_cparams(3),
        name="hy_stage4",
    )(g, f4)


def _hy_gate_kernel(y_ref, u_ref, gate_ref, skip_ref, o_ref):
    o_ref[...] = (gate_ref[...] * (y_ref[...] + skip_ref[...] * u_ref[...])).astype(o_ref.dtype)


def hy_gate(y, u_arr, u_off, gate_arr, gate_off, skip, *, natural, tr=1024):
    cb, rows, _ = y.shape
    if natural:
        out_spec = pl.BlockSpec((tr, HC), lambda i, r: (r, i))
        out_shape = jax.ShapeDtypeStruct((rows, cb * HC), BF16)
    else:
        out_spec = pl.BlockSpec((None, tr, HC), lambda i, r: (i, r, 0))
        out_shape = jax.ShapeDtypeStruct((cb, rows, HC), F32)
    return pl.pallas_call(
        _hy_gate_kernel,
        grid=(cb, rows // tr),
        in_specs=[pl.BlockSpec((None, tr, HC), lambda i, r: (i, r, 0)),
                  pl.BlockSpec((None, tr, HC), lambda i, r: (u_off + i, r, 0)),
                  pl.BlockSpec((None, tr, HC), lambda i, r: (gate_off + i, r, 0)),
                  pl.BlockSpec((None, 1, HC), lambda i, r: (i, 0, 0))],
        out_specs=out_spec,
        out_shape=out_shape,
        compiler_params=_cparams(2),
        name="hy_gate",
    )(y, u_arr, gate_arr, skip.reshape(cb, 1, HC))


def _hy_taps_kernel(a_ref, t_ref, w_ref, dl_ref, o_ref):
    a = a_ref[...]
    a_hi = a.astype(BF16)
    a_lo = (a - a_hi.astype(F32)).astype(BF16)
    w = w_ref[...]
    w_hi = w.astype(BF16)
    w_lo = (w - w_hi.astype(F32)).astype(BF16)
    h = (jnp.dot(a_hi, w_hi, preferred_element_type=F32) + jnp.dot(a_hi, w_lo, preferred_element_type=F32)
         + jnp.dot(a_lo, w_hi, preferred_element_type=F32))
    o_ref[...] = h * jnp.exp(-t_ref[...] * dl_ref[...])


def hy_taps(a_all, t_all, w4, deltas, *, seq_len, order, tr=1024):
    nrow, width = a_all.shape
    d = deltas.shape[0]
    per_dir = seq_len // tr
    return pl.pallas_call(
        _hy_taps_kernel,
        grid=(d // HC, nrow // tr),
        in_specs=[pl.BlockSpec((tr, width), lambda i, r: (r, 0)),
                  pl.BlockSpec((tr, 1), lambda i, r: (r, 0)),
                  pl.BlockSpec((None, None, width, HC), lambda i, r: (order, r // per_dir, 0, i)),
                  pl.BlockSpec((1, HC), lambda i, r: (0, i))],
        out_specs=pl.BlockSpec((None, tr, HC), lambda i, r: (i, r, 0)),
        out_shape=jax.ShapeDtypeStruct((d // HC, nrow, HC), F32),
        compiler_params=_cparams(2),
        name="hy_taps",
    )(a_all, t_all, w4.reshape(width, HY_ORDER, 2, d).transpose(1, 2, 0, 3), deltas.reshape(1, d))


def _hyena_filter_hidden(n, f_w1, f_b1, f_w2, f_b2, f_w3, f_b3, f_freq):
    hp = lax.Precision.HIGHEST
    lin = jnp.linspace(0.0, 1.0, n, dtype=F32)
    idx = jnp.arange(n, dtype=F32)
    bands = jnp.linspace(1e-4, HY_BANDS - 1, HY_BANDS, dtype=F32)[None, :]
    t = jnp.concatenate([lin, jnp.zeros((1,), F32), lin[:0:-1]])[:, None]
    pos = jnp.concatenate([idx, jnp.zeros((1,), F32), idx[:0:-1]])[:, None]
    ang = (2.0 * math.pi / n) * pos
    emb = jnp.concatenate([t, jnp.cos(bands * ang), -jnp.sin(bands * ang)], axis=-1)
    a = jnp.sin(f_freq * (jnp.dot(emb, f_w1, precision=hp) + f_b1))
    a = jnp.sin(f_freq * (jnp.dot(a, f_w2, precision=hp) + f_b2))
    a = jnp.sin(f_freq * (jnp.dot(a, f_w3, precision=hp) + f_b3))
    keep = (jnp.arange(2 * n) != n)[:, None]
    return jnp.where(keep, a, 0.0), t


def _hyena_deltas(d):
    return jnp.abs(jnp.linspace(math.log(HY_DECAY_TARGET) / HY_FAST_DECAY,
                                math.log(HY_DECAY_TARGET) / HY_SLOW_DECAY, d, dtype=F32))


def hyena_long_convs2(zc, a_all, t_all, w4, skip, *, batch, seq_len, d):
    cb = d // HC
    f1, f2, f3, f4 = _hy_mats(seq_len)
    nh = seq_len // LC_R2
    deltas = _hyena_deltas(d)
    u, u_off = zc, 2 * cb
    for o in range(HY_ORDER):
        taps = hy_taps(a_all, t_all, w4, deltas, seq_len=seq_len, order=o)
        spec = hy_spectrum(hy_stage1(taps, f1, blk_off=0, n_blk=cb, batch=1, rows_hi=2 * nh), f2)
        a = hy_stage1(u, f1, blk_off=u_off, n_blk=cb, batch=batch, rows_hi=nh)
        y = hy_stage4(hy_stage23(a, spec, f2, f3), f4, seq_len=seq_len)
        u = hy_gate(y, u, u_off, zc, o * cb, skip[o], natural=(o == HY_ORDER - 1))
        u_off = 0
    return u


def _final_norm_kernel(x_ref, g_ref, o_ref):
    x = x_ref[...]
    o_ref[...] = x * lax.rsqrt(jnp.mean(x * x, axis=-1, keepdims=True) + NORM_EPS) * g_ref[...]


def final_norm(x, g, *, rows, tm=ROW_TILE):
    d = x.shape[1]
    return pl.pallas_call(
        _final_norm_kernel,
        grid=(rows // tm,),
        in_specs=[pl.BlockSpec((tm, d), lambda i: (i, 0)), pl.BlockSpec((1, d), lambda i: (0, 0))],
        out_specs=pl.BlockSpec((tm, d), lambda i: (i, 0)),
        out_shape=jax.ShapeDtypeStruct((rows, d), F32),
        compiler_params=_cparams(1),
        name="final_norm",
    )(x, g)


def _rope_tables(n_lat, seq_len, n_rows):
    t = jnp.arange(n_rows, dtype=jnp.int32)
    row = ((t % seq_len) // GRID_W).astype(F32)
    col = (t % GRID_W).astype(F32)
    lane = jnp.arange(LANES)
    dd = lane % HEAD_DIM
    quarter = HEAD_DIM // 4
    inv_freq = ROPE_BASE ** (-(dd % quarter).astype(F32) / quarter)
    pos = jnp.where(dd[None, :] < HEAD_DIM // 2, row[:, None], col[:, None])
    ang = pos * inv_freq[None, :]
    sign = jnp.where((dd % (HEAD_DIM // 2)) < quarter, -1.0, 1.0).astype(F32)
    lat = (t < n_lat)[:, None]
    cos = jnp.where(lat, jnp.cos(ang), 1.0)
    sin = jnp.where(lat, jnp.sin(ang) * sign[None, :], 0.0)
    return cos, sin


def _wa_head_perm(n_heads):
    order = []
    for p in range(n_heads // (2 * WA_GROUP)):
        for g in range(WA_GROUP):
            order += [2 * WA_GROUP * p + g, 2 * WA_GROUP * p + WA_GROUP + g]
    cols = jnp.asarray(order, jnp.int32)[:, None] * HEAD_DIM + jnp.arange(HEAD_DIM, dtype=jnp.int32)[None, :]
    return cols.reshape(-1)


def kernel(x, c, ctx, c_ctx, w_ada, b_ada, g_mix, g_ffn, hy_w_in, hy_b_in, hy_w_short, hy_b_short, hy_f_w1, hy_f_b1, hy_f_w2, hy_f_b2, hy_f_w3, hy_f_b3, hy_f_freq, hy_f_w4, hy_skip, hy_w_out, hy_b_out, cf_w_pw1, cf_b_pw1, cf_w_dw, cf_b_dw, cf_ln_g, cf_ln_b, cf_w_pw2, cf_b_pw2, wa_w_qkv, wa_w_o, wa_sinks, na_w_qkv, na_w_o, na_rpb, moe_w_router, moe_b_router, moe_w_gu, moe_b_gu, moe_w_down, moe_b_down, g_final):
    bsz, n, d = x.shape
    n_ctx = ctx.shape[1]
    depth = w_ada.shape[0]
    n_lat = bsz * n
    m_all = n_lat + bsz * n_ctx

    cond = jnp.zeros((8, d), F32).at[:bsz].set(jax.nn.silu(c)).at[bsz].set(jax.nn.silu(c_ctx))
    mods = adaln(cond, w_ada, b_ada)
    xu = jnp.concatenate([x.reshape(n_lat, d), ctx.reshape(bsz * n_ctx, d)], axis=0)
    zeros_d = jnp.zeros((1, d), F32)

    for i in range(depth):
        kind, j = i % N_MIXERS, i // N_MIXERS
        last = i == depth - 1
        mod = [mods[i, :bsz + 1, k * d:(k + 1) * d].reshape(bsz + 1, 1, d) for k in range(6)]
        m_out = n_lat if last else m_all
        gm = g_mix[i].reshape(1, d)
        need_ctx_in = (not last) or kind >= 2
        m_in = m_all if need_ctx_in else n_lat
        x_in = xu[:m_in]

        if kind == 0:
            z = nm_matmul(x_in, gm, mod[0], mod[1], hy_w_in[j].astype(BF16), hy_b_in[j].reshape(1, -1),
                          seg_rows=n)
            filt_w = (hy_f_w1[j], hy_f_b1[j], hy_f_w2[j], hy_f_b2[j], hy_f_w3[j], hy_f_b3[j], hy_f_freq[j],
                      hy_f_w4[j])
            w_s, b_s = hy_w_short[j], hy_b_short[j].reshape(1, -1)
            w_o, b_o = hy_w_out[j].astype(BF16), hy_b_out[j].reshape(1, d)
            zc = dwconv(z, w_s, b_s, start=0, total=n_lat, seq_len=n, tc=HC, blocked=True, rows=2048,
                        out_dtype=F32)
            a_all, t_all = _hyena_filter_hidden(n, *filt_w[:-1])
            y = hyena_long_convs2(zc, a_all, t_all, hy_f_w4[j], hy_skip[j], batch=bsz, seq_len=n, d=d)
            xu = mm_res(y, w_o, b_o, xu, mod[2], seg_rows=n)
            if m_in > n_lat:
                zcc = dwconv(z, w_s, b_s, start=n_lat, total=m_in - n_lat, seq_len=n_ctx, tc=512)
                taps_c = _hyena_taps(n_ctx, d, *filt_w)
                fwd, inv = _dft_small_mats(n_ctx)
                yc = short_long_conv(zcc, 2 * d, zcc, 0, plain_mm(fwd, taps_c[0]), hy_skip[j][0].reshape(1, d),
                                     fwd, inv, batch=bsz, seq_len=n_ctx)
                yc = short_long_conv(yc, 0, zcc, d, plain_mm(fwd, taps_c[1]), hy_skip[j][1].reshape(1, d),
                                     fwd, inv, batch=bsz, seq_len=n_ctx)
                if not last:
                    xu = mm_res(yc, w_o, b_o, xu, mod[2], seg_rows=n, row_off=n_lat)
            y = None
        elif kind == 1:
            w1 = cf_w_pw1[j].astype(BF16).reshape(d, 2, d).transpose(1, 0, 2)
            a = nm_matmul(x_in, gm, mod[0], mod[1], w1, cf_b_pw1[j].reshape(2, 1, d), seg_rows=n, mode="glu")
            conv = functools.partial(dwconv, a, cf_w_dw[j], cf_b_dw[j].reshape(1, d), post="ln_silu",
                                     ln=(cf_ln_g[j].reshape(1, d), cf_ln_b[j].reshape(1, d)), rows=64)
            y = conv(start=0, total=n_lat, seq_len=n, same_rows=True)
            if m_in > n_lat:
                y = conv(start=n_lat, total=m_in - n_lat, seq_len=n_ctx, same_rows=True, into=y)
            w_o, b_o = cf_w_pw2[j].astype(BF16), cf_b_pw2[j].reshape(1, d)
        elif kind == 2:
            perm = _wa_head_perm(d // HEAD_DIM)
            scale = HEAD_DIM ** -0.5
            w_qkv = jnp.concatenate([wa_w_qkv[j][:, :d][:, perm] * scale, wa_w_qkv[j][:, d:]], axis=1).astype(BF16)
            n_out = w_qkv.shape[1]
            qkv = nm_matmul(x_in, gm, mod[0], mod[1], w_qkv, jnp.zeros((1, n_out), F32), seg_rows=n,
                            mode="rope", rope=_rope_tables(n_lat, n, m_in), n_rope_cols=d + (n_out - d) // 2)
            y = window_attention(qkv, wa_sinks[j], batch=bsz, seq_len=n, ctx_len=n_ctx, d=d)
            w_o, b_o = wa_w_o[j][perm].astype(BF16), zeros_d
        else:
            scale = HEAD_DIM ** -0.5
            w_qkv = jnp.concatenate([na_w_qkv[j][:, :d] * scale, na_w_qkv[j][:, d:]], axis=1).astype(BF16)
            qkv = nm_matmul(x_in, gm, mod[0], mod[1], w_qkv, jnp.zeros((1, 3 * d), F32), seg_rows=n)
            y = neighbourhood_attention(qkv, na_rpb[j], batch=bsz, seq_len=n, ctx_len=n_ctx, d=d,
                                        with_ctx_out=not last)
            w_o, b_o = na_w_o[j].astype(BF16), zeros_d
        if y is not None:
            xu = mm_res(y, w_o, b_o, xu, mod[2], seg_rows=n, rows=m_out)

        xu = moe_layer(xu, g_ffn[i].reshape(1, d), mod[3], mod[4], mod[5], moe_w_router[i], moe_b_router[i],
                       moe_w_gu, moe_b_gu[i], moe_w_down, moe_b_down[i], layer=i, seg_rows=n, rows=m_out)
    return final_norm(xu, g_final.reshape(1, d), rows=n_lat).reshape(bsz, n, d)
```

```python
import functools
import math

import jax
import jax.numpy as jnp
import numpy as np
from jax import lax
from jax.experimental import pallas as pl
from jax.experimental.pallas import tpu as pltpu

F32 = jnp.float32
BF16 = jnp.bfloat16

GRID_W = 64
N_MIXERS = 4
NORM_EPS = 1e-6
NEG_INF = -1e30
HEAD_DIM = 64
ROPE_BASE = 10000.0

HY_ORDER = 2
HY_BANDS = 16
HY_DECAY_TARGET = 1e-2
HY_FAST_DECAY = 0.3
HY_SLOW_DECAY = 1.5

WA_GROUP = 4
WA_WINDOW = 128
NA_WIN_ROWS = 8
NA_WIN_COLS = 16

TOP_K = 4
SWIGLU_LIMIT = 7.0
SWIGLU_ALPHA = 1.702

LANES = 128
ROW_TILE = 512
MOE_ROWS = 256
VMEM_LIMIT = 56 * 1024 * 1024
EXPERT_VMEM_LIMIT = 58 * 1024 * 1024


def _cparams(n_axes):
    return pltpu.CompilerParams(dimension_semantics=("arbitrary",) * n_axes,
                                vmem_limit_bytes=VMEM_LIMIT)


def _col_tile(n, pref=1024):
    t = min(pref, n)
    while n % t:
        t //= 2
    return t


def _adaln_kernel(c_ref, w_ref, b_ref, o_ref):
    w = w_ref[0]
    w_hi = w.astype(BF16)
    w_lo = (w - w_hi.astype(F32)).astype(BF16)
    c = c_ref[...]
    c_hi = c.astype(BF16)
    c_lo = (c - c_hi.astype(F32)).astype(BF16)
    acc = jnp.dot(c_hi, w_hi, preferred_element_type=F32)
    acc += jnp.dot(c_hi, w_lo, preferred_element_type=F32)
    acc += jnp.dot(c_lo, w_hi, preferred_element_type=F32)
    o_ref[0] = acc + b_ref[0]


def adaln(cond, w_ada, b_ada):
    depth, d, n6 = w_ada.shape
    tn = _col_tile(n6, 1024)
    return pl.pallas_call(
        _adaln_kernel,
        grid=(depth, n6 // tn),
        in_specs=[pl.BlockSpec((8, d), lambda l, j: (0, 0)),
                  pl.BlockSpec((1, d, tn), lambda l, j: (l, 0, j)),
                  pl.BlockSpec((1, 1, tn), lambda l, j: (l, 0, j))],
        out_specs=pl.BlockSpec((1, 8, tn), lambda l, j: (l, 0, j)),
        out_shape=jax.ShapeDtypeStruct((depth, 8, n6), F32),
        compiler_params=_cparams(2),
        name="adaln",
    )(cond, w_ada, b_ada.reshape(depth, 1, n6))


def _norm_mod(x, g, shift, scale):
    y = x * lax.rsqrt(jnp.mean(x * x, axis=-1, keepdims=True) + NORM_EPS)
    return (y * g) * (1 + scale) + shift


def _rope_tile(x, cos, sin):
    lane = lax.broadcasted_iota(jnp.int32, x.shape, 1)
    nxt = pltpu.roll(x, LANES - 16, axis=1)
    prv = pltpu.roll(x, 16, axis=1)
    partner = jnp.where((lane // 16) % 2 == 0, nxt, prv)
    return x * cos + partner * sin


def _nm_mm_kernel(x_ref, g_ref, sh_ref, sc_ref, w_ref, b_ref, *rest, mode, n_rope):
    if mode == "rope":
        cos_ref, sin_ref, o_ref, h_ref = rest
    else:
        o_ref, h_ref = rest
    j = pl.program_id(1)

    @pl.when(j == 0)
    def _():
        h_ref[...] = _norm_mod(x_ref[...], g_ref[...], sh_ref[0], sc_ref[0]).astype(BF16)

    h = h_ref[...]
    if mode == "glu":
        a = jnp.dot(h, w_ref[0], preferred_element_type=F32) + b_ref[0]
        gate = jnp.dot(h, w_ref[1], preferred_element_type=F32) + b_ref[1]
        o_ref[...] = (a * jax.nn.sigmoid(gate)).astype(o_ref.dtype)
        return
    acc = jnp.dot(h, w_ref[...], preferred_element_type=F32) + b_ref[...]
    if mode == "rope":
        @pl.when(j < n_rope)
        def _():
            cos = cos_ref[...]
            sin = sin_ref[...]
            for g in range(acc.shape[1] // LANES):
                sl = slice(g * LANES, (g + 1) * LANES)
                o_ref[:, sl] = _rope_tile(acc[:, sl], cos, sin).astype(o_ref.dtype)

        @pl.when(j >= n_rope)
        def _():
            o_ref[...] = acc.astype(o_ref.dtype)
    else:
        o_ref[...] = acc.astype(o_ref.dtype)


def nm_matmul(x, g, shift, scale, w, b, *, seg_rows, mode="plain", rope=None, n_rope_cols=0,
              tm=ROW_TILE, tn=None):
    m, d = x.shape
    n_seg = shift.shape[0]
    n = w.shape[-1]
    tn = tn or (_col_tile(math.gcd(n, n_rope_cols), 512) if mode == "rope" else _col_tile(n, 1024))
    seg = lambda i, j: (jnp.minimum(i * tm // seg_rows, n_seg - 1), 0, 0)
    in_specs = [pl.BlockSpec((tm, d), lambda i, j: (i, 0)),
                pl.BlockSpec((1, d), lambda i, j: (0, 0)),
                pl.BlockSpec((1, 1, d), seg),
                pl.BlockSpec((1, 1, d), seg)]
    if mode == "glu":
        in_specs += [pl.BlockSpec((2, d, tn), lambda i, j: (0, 0, j)),
                     pl.BlockSpec((2, 1, tn), lambda i, j: (0, 0, j))]
    else:
        in_specs += [pl.BlockSpec((d, tn), lambda i, j: (0, j)),
                     pl.BlockSpec((1, tn), lambda i, j: (0, j))]
    args = [x, g, shift, scale, w, b]
    if mode == "rope":
        in_specs += [pl.BlockSpec((tm, LANES), lambda i, j: (i, 0))] * 2
        args += list(rope)
    return pl.pallas_call(
        functools.partial(_nm_mm_kernel, mode=mode, n_rope=n_rope_cols // tn),
        grid=(m // tm, n // tn),
        in_specs=in_specs,
        out_specs=pl.BlockSpec((tm, tn), lambda i, j: (i, j)),
        out_shape=jax.ShapeDtypeStruct((m, n), BF16),
        scratch_shapes=[pltpu.VMEM((tm, d), BF16)],
        compiler_params=_cparams(2),
        name="nm_matmul_" + mode,
    )(*args)


def _mm_res_kernel(a_ref, w_ref, b_ref, res_ref, gate_ref, o_ref):
    if len(a_ref.shape) == 3:
        kc = a_ref.shape[2]
        acc = b_ref[...]
        for cb in range(a_ref.shape[0]):
            acc = acc + jnp.dot(a_ref[cb], w_ref[cb * kc:(cb + 1) * kc, :], preferred_element_type=F32)
    else:
        acc = jnp.dot(a_ref[...], w_ref[...], preferred_element_type=F32) + b_ref[...]
    o_ref[...] = res_ref[...] + gate_ref[0] * acc


def mm_res(a, w, b, res, gate, *, seg_rows, row_off=0, rows=None, tm=ROW_TILE, tn=None):
    if a.ndim == 3:
        m = rows or a.shape[1]
        k = a.shape[0] * a.shape[2]
        a_spec = pl.BlockSpec((a.shape[0], tm, a.shape[2]), lambda i, j: (0, i, 0))
    else:
        m, k = rows or a.shape[0], a.shape[1]
        a_spec = pl.BlockSpec((tm, k), lambda i, j: (i, 0))
    n = w.shape[1]
    n_seg = gate.shape[0]
    tn = tn or _col_tile(n, 1024)
    blk_off = row_off // tm
    seg = lambda i, j: (jnp.minimum((blk_off + i) * tm // seg_rows, n_seg - 1), 0, j)
    return pl.pallas_call(
        _mm_res_kernel,
        grid=(m // tm, n // tn),
        in_specs=[a_spec,
                  pl.BlockSpec((k, tn), lambda i, j: (0, j)),
                  pl.BlockSpec((1, tn), lambda i, j: (0, j)),
                  pl.BlockSpec((tm, tn), lambda i, j: (blk_off + i, j)),
                  pl.BlockSpec((1, 1, tn), seg)],
        out_specs=pl.BlockSpec((tm, tn), lambda i, j: (blk_off + i, j)),
        out_shape=jax.ShapeDtypeStruct(res.shape, F32),
        input_output_aliases={3: 0},
        compiler_params=_cparams(2),
        name="mm_res",
    )(a, w, b, res, gate)


CONV_HALO = 16


def _dwconv_kernel(prev_ref, x_ref, next_ref, w_ref, b_ref, *rest, taps, blocks_per_seq, post, sub, n_alias):
    if n_alias:
        rest = rest[:-3] + rest[-2:]
    if post == "ln_silu":
        g_ref, beta_ref, o_ref, win_ref = rest
    else:
        o_ref, win_ref = rest
    i = pl.program_id(0)
    pos = i % blocks_per_seq
    rows, c = x_ref.shape
    half = taps // 2
    zero = jnp.zeros((CONV_HALO, c), F32)
    win_ref[CONV_HALO:CONV_HALO + rows, :] = x_ref[...].astype(F32)

    @pl.when(pos == 0)
    def _():
        win_ref[0:CONV_HALO, :] = zero

    @pl.when(pos > 0)
    def _():
        win_ref[0:CONV_HALO, :] = prev_ref[...].astype(F32)

    @pl.when(pos == blocks_per_seq - 1)
    def _():
        win_ref[CONV_HALO + rows:, :] = zero

    @pl.when(pos < blocks_per_seq - 1)
    def _():
        win_ref[CONV_HALO + rows:, :] = next_ref[...].astype(F32)

    bias = b_ref[...]
    for s in range(rows // sub):
        base = CONV_HALO + s * sub - half
        acc = jnp.broadcast_to(bias, (sub, c))
        for t in range(taps):
            acc = acc + w_ref[t] * win_ref[base + t:base + t + sub, :]
        if post == "ln_silu":
            mu = jnp.mean(acc, axis=-1, keepdims=True)
            xc = acc - mu
            var = jnp.mean(xc * xc, axis=-1, keepdims=True)
            y = xc * lax.rsqrt(var + NORM_EPS) * g_ref[...] + beta_ref[...]
            acc = y * jax.nn.sigmoid(y)
        o_ref[s * sub:(s + 1) * sub, :] = acc.astype(o_ref.dtype)


def dwconv(x, w, b, *, start, total, seq_len, post=None, ln=None, rows=256, tc=None, out_dtype=None,
           blocked=False, same_rows=False, into=None):
    m, c = x.shape
    taps = w.shape[0]
    tc = tc or c
    out_dtype = out_dtype or BF16
    sub = 8
    w = jnp.broadcast_to(w[:, None, :], (taps, sub, c))
    r = min(rows, seq_len)
    bps = seq_len // r
    hb = r // CONV_HALO
    off = start // r
    offh = start // CONV_HALO
    nh = m // CONV_HALO
    in_specs = [pl.BlockSpec((CONV_HALO, tc), lambda i, j: (jnp.maximum(offh + i * hb - 1, 0), j)),
                pl.BlockSpec((r, tc), lambda i, j: (off + i, j)),
                pl.BlockSpec((CONV_HALO, tc), lambda i, j: (jnp.minimum(offh + (i + 1) * hb, nh - 1), j)),
                pl.BlockSpec((taps, sub, tc), lambda i, j: (0, 0, j)),
                pl.BlockSpec((1, tc), lambda i, j: (0, j))]
    args = [x, x, x, w, b]
    if post == "ln_silu":
        in_specs += [pl.BlockSpec((1, tc), lambda i, j: (0, j))] * 2
        args += list(ln)
    aliases = {}
    if blocked:
        out_spec = pl.BlockSpec((None, r, tc), lambda i, j: (j, i, 0))
        out_shape = jax.ShapeDtypeStruct((c // tc, total, tc), out_dtype)
    elif same_rows:
        out_spec = pl.BlockSpec((r, tc), lambda i, j: (off + i, j))
        out_shape = jax.ShapeDtypeStruct((m, c), out_dtype)
        if into is not None:
            in_specs.append(pl.BlockSpec(memory_space=pl.ANY))
            args.append(into)
            aliases = {len(args) - 1: 0}
    else:
        out_spec = pl.BlockSpec((r, tc), lambda i, j: (i, j))
        out_shape = jax.ShapeDtypeStruct((total, c), out_dtype)
    return pl.pallas_call(
        functools.partial(_dwconv_kernel, taps=taps, blocks_per_seq=bps, post=post, sub=sub,
                          n_alias=len(aliases)),
        grid=(total // r, c // tc),
        in_specs=in_specs,
        out_specs=out_spec,
        out_shape=out_shape,
        input_output_aliases=aliases,
        scratch_shapes=[pltpu.VMEM((r + 2 * CONV_HALO, tc), F32)],
        compiler_params=_cparams(2),
        name="dwconv%d" % taps,
    )(*args)


def _masked_halves(q, lane_lo):
    zero = jnp.zeros_like(q)
    return jnp.where(lane_lo, q, zero), jnp.where(lane_lo, zero, q)


def _softmax_pv(s_parts, v_parts, sink):
    m = s_parts[0].max(axis=-1, keepdims=True)
    for s in s_parts[1:]:
        m = jnp.maximum(m, s.max(axis=-1, keepdims=True))
    if sink is not None:
        m = jnp.maximum(m, sink)
    denom = jnp.exp(sink - m) if sink is not None else 0.0
    o = None
    for s, v in zip(s_parts, v_parts):
        p = jnp.exp(s - m)
        denom = denom + p.sum(axis=-1, keepdims=True)
        pv = jnp.dot(p.astype(BF16), v, preferred_element_type=F32)
        o = pv if o is None else o + pv
    return o / denom


def _nt_dot(a, b):
    return lax.dot_general(a, b, (((1,), (1,)), ((), ())), preferred_element_type=F32)


def _wattn_kernel(sink_ref, q_ref, *rest, local, seq_len, blk):
    if local:
        k0, k1, k2, v0, v1, v2, kc_ref, vc_ref, o_ref = rest
    else:
        kc_ref, vc_ref, _, o_ref = rest
    p = pl.program_id(2)
    i = pl.program_id(1)
    lane_lo = lax.broadcasted_iota(jnp.int32, (blk, LANES), 1) < HEAD_DIM
    kc = kc_ref[...]
    vc = vc_ref[...]
    if local:
        kl = jnp.concatenate([k0[...], k1[...], k2[...]], axis=0)
        vl = jnp.concatenate([v0[...], v1[...], v2[...]], axis=0)
        qpos = i * blk + lax.broadcasted_iota(jnp.int32, (blk, 3 * blk), 0)
        kpos = (i - 1) * blk + lax.broadcasted_iota(jnp.int32, (blk, 3 * blk), 1)
        valid = (jnp.abs(kpos - qpos) <= WA_WINDOW) & (kpos >= 0) & (kpos < seq_len)
        valid = jnp.concatenate([valid] * WA_GROUP, axis=0)
    qa, qb = [], []
    for g in range(WA_GROUP):
        a, b = _masked_halves(q_ref[:, g * LANES:(g + 1) * LANES], lane_lo)
        qa.append(a)
        qb.append(b)
    outs = []
    for half, qs in enumerate((qa, qb)):
        qs = jnp.concatenate(qs, axis=0)
        sink = jnp.concatenate(
            [jnp.full((blk, 1), sink_ref[8 * p + 4 * half + g], F32) for g in range(WA_GROUP)], axis=0)
        s_parts, v_parts = [], []
        if local:
            s_parts.append(jnp.where(valid, _nt_dot(qs, kl), NEG_INF))
            v_parts.append(vl)
        s_parts.append(_nt_dot(qs, kc))
        v_parts.append(vc)
        outs.append(_softmax_pv(s_parts, v_parts, sink))
    for g in range(WA_GROUP):
        rows = slice(g * blk, (g + 1) * blk)
        o_ref[:, g * LANES:(g + 1) * LANES] = jnp.where(lane_lo, outs[0][rows], outs[1][rows]).astype(o_ref.dtype)


def window_attention(qkv, sinks, *, batch, seq_len, ctx_len, d, blk=128):
    n_pairs = d // (2 * WA_GROUP * HEAD_DIM)
    kcol = d // LANES
    vcol = kcol + n_pairs
    nblk = seq_len // blk
    cb0 = batch * seq_len // ctx_len
    qw = WA_GROUP * LANES
    kern = functools.partial(_wattn_kernel, seq_len=seq_len)
    smem = pl.BlockSpec(memory_space=pltpu.SMEM)

    def kspec(col0, shift):
        return pl.BlockSpec((blk, LANES),
                            lambda b, i, p: (b * nblk + jnp.clip(i + shift, 0, nblk - 1), col0 + p))

    ctx_k = pl.BlockSpec((ctx_len, LANES), lambda b, i, p: (cb0 + b, kcol + p))
    ctx_v = pl.BlockSpec((ctx_len, LANES), lambda b, i, p: (cb0 + b, vcol + p))
    lat = pl.pallas_call(
        functools.partial(kern, local=True, blk=blk),
        grid=(batch, nblk, n_pairs),
        in_specs=[smem, pl.BlockSpec((blk, qw), lambda b, i, p: (b * nblk + i, p))]
        + [kspec(kcol, s) for s in (-1, 0, 1)] + [kspec(vcol, s) for s in (-1, 0, 1)] + [ctx_k, ctx_v],
        out_specs=pl.BlockSpec((blk, qw), lambda b, i, p: (b * nblk + i, p)),
        out_shape=jax.ShapeDtypeStruct((qkv.shape[0], d), BF16),
        compiler_params=_cparams(3),
        name="window_attn",
    )(sinks, qkv, *([qkv] * 8))
    return pl.pallas_call(
        functools.partial(kern, local=False, blk=ctx_len),
        grid=(batch, 1, n_pairs),
        in_specs=[smem, pl.BlockSpec((ctx_len, qw), lambda b, i, p: (cb0 + b, p)), ctx_k, ctx_v,
                  pl.BlockSpec(memory_space=pl.ANY)],
        out_specs=pl.BlockSpec((ctx_len, qw), lambda b, i, p: (cb0 + b, p)),
        out_shape=jax.ShapeDtypeStruct((qkv.shape[0], d), BF16),
        input_output_aliases={4: 0},
        compiler_params=_cparams(3),
        name="ctx_attn",
    )(sinks, qkv, qkv, qkv, lat)


def _nattn_kernel(q_ref, k0, k1, k2, v0, v1, v2, kc_ref, vc_ref, bias_ref, o_ref, ks_ref, vs_ref,
                  *, grid_rows, rows_per_blk):
    j = pl.program_id(2)
    blk = k0.shape[0]
    for t, (kr, vr) in enumerate(((k0, v0), (k1, v1), (k2, v2))):
        ks_ref[t * blk:(t + 1) * blk, :] = kr[...]
        vs_ref[t * blk:(t + 1) * blk, :] = vr[...]
    kc = kc_ref[...]
    vc = vc_ref[...]
    lane_lo = lax.broadcasted_iota(jnp.int32, (GRID_W, LANES), 1) < HEAD_DIM
    strip = NA_WIN_ROWS * GRID_W
    for r in range(rows_per_blk):
        row = j * rows_per_blk + r
        r0 = jnp.clip(row - NA_WIN_ROWS // 2, 0, grid_rows - NA_WIN_ROWS)
        start = pl.multiple_of((r0 - (j - 1) * rows_per_blk) * GRID_W, GRID_W)
        cls = row - r0
        qa, qb = _masked_halves(q_ref[r * GRID_W:(r + 1) * GRID_W, :], lane_lo)
        qs = jnp.concatenate([qa, qb], axis=0)
        kn = ks_ref[pl.ds(start, strip), :]
        vn = vs_ref[pl.ds(start, strip), :]
        s_nb = _nt_dot(qs, kn) + bias_ref[cls]
        s_cx = _nt_dot(qs, kc)
        o = _softmax_pv([s_nb, s_cx], [vn, vc], None)
        o_ref[r * GRID_W:(r + 1) * GRID_W, :] = jnp.where(lane_lo, o[:GRID_W], o[GRID_W:]).astype(o_ref.dtype)


def _na_bias_table(rpb):
    h = rpb.shape[0]
    n_dcol = 2 * NA_WIN_COLS - 1
    cols = jnp.arange(GRID_W)
    col_start = jnp.clip(cols - NA_WIN_COLS // 2, 0, GRID_W - NA_WIN_COLS)
    inwin = (cols[None, :] >= col_start[:, None]) & (cols[None, :] < col_start[:, None] + NA_WIN_COLS)
    dcol = cols[None, :] - cols[:, None] + NA_WIN_COLS - 1
    pick = (dcol[None] == jnp.arange(n_dcol)[:, None, None]).astype(F32).reshape(n_dcol, -1)
    spread = jnp.dot(rpb.astype(F32).reshape(-1, n_dcol), pick, precision=lax.Precision.HIGHEST)
    spread = spread.reshape(h, 2 * NA_WIN_ROWS - 1, GRID_W, GRID_W)
    spread = jnp.where(inwin[None, None], spread, NEG_INF)
    tab = jnp.stack([spread[:, NA_WIN_ROWS - 1 - cls:2 * NA_WIN_ROWS - 1 - cls]
                     for cls in range(NA_WIN_ROWS)], axis=1)
    tab = tab.transpose(0, 1, 3, 2, 4).reshape(h, NA_WIN_ROWS, GRID_W, NA_WIN_ROWS * GRID_W)
    tab = tab.reshape(h // 2, 2, NA_WIN_ROWS, GRID_W, NA_WIN_ROWS * GRID_W).transpose(0, 2, 1, 3, 4)
    return tab.reshape(h // 2, NA_WIN_ROWS, 2 * GRID_W, NA_WIN_ROWS * GRID_W)


def neighbourhood_attention(qkv, rpb, *, batch, seq_len, ctx_len, d, with_ctx_out):
    n_pairs = d // LANES
    rows_per_blk = NA_WIN_ROWS
    blk = rows_per_blk * GRID_W
    grid_rows = seq_len // GRID_W
    nblk = seq_len // blk
    cb0 = batch * seq_len // ctx_len
    bias = _na_bias_table(rpb)

    def kspec(col0, shift):
        return pl.BlockSpec((blk, LANES),
                            lambda b, p, j: (b * nblk + jnp.clip(j + shift, 0, nblk - 1), col0 + p))

    ctx_k = pl.BlockSpec((ctx_len, LANES), lambda b, p, j: (cb0 + b, n_pairs + p))
    ctx_v = pl.BlockSpec((ctx_len, LANES), lambda b, p, j: (cb0 + b, 2 * n_pairs + p))
    lat = pl.pallas_call(
        functools.partial(_nattn_kernel, grid_rows=grid_rows, rows_per_blk=rows_per_blk),
        grid=(batch, n_pairs, nblk),
        in_specs=[pl.BlockSpec((blk, LANES), lambda b, p, j: (b * nblk + j, p))]
        + [kspec(n_pairs, s) for s in (-1, 0, 1)] + [kspec(2 * n_pairs, s) for s in (-1, 0, 1)]
        + [ctx_k, ctx_v,
           pl.BlockSpec((None, NA_WIN_ROWS, 2 * GRID_W, NA_WIN_ROWS * GRID_W), lambda b, p, j: (p, 0, 0, 0))],
        out_specs=pl.BlockSpec((blk, LANES), lambda b, p, j: (b * nblk + j, p)),
        out_shape=jax.ShapeDtypeStruct((batch * seq_len, d), BF16),
        scratch_shapes=[pltpu.VMEM((3 * blk, LANES), BF16), pltpu.VMEM((3 * blk, LANES), BF16)],
        compiler_params=_cparams(3),
        name="neighbourhood_attn",
    )(qkv, *([qkv] * 8), bias)
    if not with_ctx_out:
        return lat
    ctx = pl.pallas_call(
        _cattn_kernel,
        grid=(batch, n_pairs),
        in_specs=[pl.BlockSpec((ctx_len, LANES), lambda b, p: (cb0 + b, p)),
                  pl.BlockSpec((ctx_len, LANES), lambda b, p: (cb0 + b, n_pairs + p)),
                  pl.BlockSpec((ctx_len, LANES), lambda b, p: (cb0 + b, 2 * n_pairs + p))],
        out_specs=pl.BlockSpec((ctx_len, LANES), lambda b, p: (b, p)),
        out_shape=jax.ShapeDtypeStruct((batch * ctx_len, d), BF16),
        compiler_params=_cparams(2),
        name="ctx_mha",
    )(qkv, qkv, qkv)
    return jnp.concatenate([lat, ctx], axis=0)


def _cattn_kernel(q_ref, k_ref, v_ref, o_ref):
    rows = q_ref.shape[0]
    lane_lo = lax.broadcasted_iota(jnp.int32, (rows, LANES), 1) < HEAD_DIM
    qa, qb = _masked_halves(q_ref[...], lane_lo)
    qs = jnp.concatenate([qa, qb], axis=0)
    o = _softmax_pv([_nt_dot(qs, k_ref[...])], [v_ref[...]], None)
    o_ref[...] = jnp.where(lane_lo, o[:rows], o[rows:]).astype(o_ref.dtype)


def _router_kernel(x_ref, g_ref, sh_ref, sc_ref, wh_ref, wl_ref, b_ref, h_ref, idx_ref, gate_ref, cnt_ref,
                   run_ref):
    @pl.when(pl.program_id(0) == 0)
    def _():
        run_ref[...] = jnp.zeros_like(run_ref)

    h = _norm_mod(x_ref[...], g_ref[...], sh_ref[0], sc_ref[0])
    h_hi = h.astype(BF16)
    h_ref[...] = h_hi
    h_lo = (h - h_hi.astype(F32)).astype(BF16)
    logits = (jnp.dot(h_hi, wh_ref[...], preferred_element_type=F32)
              + jnp.dot(h_hi, wl_ref[...], preferred_element_type=F32)
              + jnp.dot(h_lo, wh_ref[...], preferred_element_type=F32)) + b_ref[...]
    lane = lax.broadcasted_iota(jnp.int32, logits.shape, 1)
    tm = logits.shape[0]
    idx_out = jnp.zeros(logits.shape, jnp.int32)
    val_out = jnp.zeros(logits.shape, F32)
    tri = jnp.where(lax.broadcasted_iota(jnp.int32, (tm, tm), 0) > lax.broadcasted_iota(jnp.int32, (tm, tm), 1),
                    1.0, 0.0).astype(BF16)
    run = run_ref[...]
    top = None
    denom = 0.0
    for k in range(TOP_K):
        m = logits.max(axis=-1, keepdims=True)
        sel = jnp.min(jnp.where(logits == m, lane, LANES), axis=-1, keepdims=True)
        if top is None:
            top = m
        e = jnp.exp(m - top)
        denom = denom + e
        onehot = lane == sel
        before = jnp.dot(tri, jnp.where(onehot, 1.0, 0.0).astype(BF16), preferred_element_type=F32)
        rank = jnp.sum(jnp.where(onehot, before + run, 0.0), axis=-1, keepdims=True).astype(jnp.int32)
        run = run + jnp.sum(jnp.where(onehot, 1.0, 0.0), axis=0, keepdims=True)
        idx_out = jnp.where(lane == k, sel, idx_out)
        idx_out = jnp.where(lane == TOP_K + k, rank, idx_out)
        val_out = jnp.where(lane == k, e, val_out)
        logits = jnp.where(onehot, -jnp.inf, logits)
    run_ref[...] = run
    cnt_ref[...] = run
    idx_ref[...] = idx_out
    gate_ref[...] = val_out / denom


def router(x, g, shift, scale, w_router, b_router, *, seg_rows, rows, tm=ROW_TILE):
    m, d = rows, x.shape[1]
    n_seg = shift.shape[0]
    n_e = w_router.shape[1]
    w_pad = jnp.zeros((d, LANES), F32).at[:, :n_e].set(w_router)
    w_hi = w_pad.astype(BF16)
    w_lo = (w_pad - w_hi.astype(F32)).astype(BF16)
    b_pad = jnp.full((1, LANES), -jnp.inf, F32).at[0, :n_e].set(b_router)
    seg = lambda i: (jnp.minimum(i * tm // seg_rows, n_seg - 1), 0, 0)
    return pl.pallas_call(
        _router_kernel,
        grid=(m // tm,),
        in_specs=[pl.BlockSpec((tm, d), lambda i: (i, 0)),
                  pl.BlockSpec((1, d), lambda i: (0, 0)),
                  pl.BlockSpec((1, 1, d), seg),
                  pl.BlockSpec((1, 1, d), seg),
                  pl.BlockSpec((d, LANES), lambda i: (0, 0)),
                  pl.BlockSpec((d, LANES), lambda i: (0, 0)),
                  pl.BlockSpec((1, LANES), lambda i: (0, 0))],
        out_specs=[pl.BlockSpec((tm, d), lambda i: (i, 0)),
                   pl.BlockSpec((tm, LANES), lambda i: (i, 0)),
                   pl.BlockSpec((tm, LANES), lambda i: (i, 0)),
                   pl.BlockSpec((1, LANES), lambda i: (0, 0))],
        out_shape=[jax.ShapeDtypeStruct((m, d), BF16),
                   jax.ShapeDtypeStruct((m, LANES), jnp.int32),
                   jax.ShapeDtypeStruct((m, LANES), F32),
                   jax.ShapeDtypeStruct((1, LANES), F32)],
        scratch_shapes=[pltpu.VMEM((1, LANES), F32)],
        compiler_params=_cparams(1),
        name="router",
    )(x, g, shift, scale, w_hi, w_lo, b_pad)


def _expert_kernel(blk_e_ref, n_used_ref, x_ref, wgu_ref, bgu_ref, wd_ref, bd_ref, o_ref, wgu_s, wd_s, *, f):
    i = pl.program_id(0)
    used = i < n_used_ref[0]

    @pl.when(used & ((i == 0) | (blk_e_ref[i] != blk_e_ref[jnp.maximum(i - 1, 0)])))
    def _():
        wgu_s[...] = wgu_ref[0].astype(BF16)
        wd_s[...] = wd_ref[0].astype(BF16)

    @pl.when(used)
    def _():
        gu = jnp.dot(x_ref[...], wgu_s[...], preferred_element_type=F32) + bgu_ref[0]
        g = jnp.minimum(gu[:, :f], SWIGLU_LIMIT)
        u = jnp.clip(gu[:, f:], -SWIGLU_LIMIT, SWIGLU_LIMIT)
        act = (u + 1) * (g * jax.nn.sigmoid(SWIGLU_ALPHA * g))
        y = jnp.dot(act.astype(BF16), wd_s[...], preferred_element_type=F32) + bd_ref[0]
        o_ref[...] = y.astype(o_ref.dtype)

    @pl.when(i >= n_used_ref[0])
    def _():
        o_ref[...] = jnp.zeros_like(o_ref)


def expert_mlp(xs, blk_e, n_used, w_gu, b_gu, w_down, b_down, *, layer, bm=MOE_ROWS):
    n_slot, d = xs.shape
    _, n_e, _, f2 = w_gu.shape
    f = f2 // 2
    return pl.pallas_call(
        functools.partial(_expert_kernel, f=f),
        grid_spec=pltpu.PrefetchScalarGridSpec(
            num_scalar_prefetch=2,
            grid=(n_slot // bm,),
            in_specs=[pl.BlockSpec((bm, d), lambda i, be, nu: (i, 0)),
                      pl.BlockSpec((None, 1, d, f2), lambda i, be, nu: (layer, be[i], 0, 0)),
                      pl.BlockSpec((1, 1, f2), lambda i, be, nu: (be[i], 0, 0)),
                      pl.BlockSpec((None, 1, f, d), lambda i, be, nu: (layer, be[i], 0, 0)),
                      pl.BlockSpec((1, 1, d), lambda i, be, nu: (be[i], 0, 0))],
            out_specs=pl.BlockSpec((bm, d), lambda i, be, nu: (i, 0)),
            scratch_shapes=[pltpu.VMEM((d, f2), BF16), pltpu.VMEM((f, d), BF16)]),
        out_shape=jax.ShapeDtypeStruct((n_slot, d), BF16),
        compiler_params=pltpu.CompilerParams(dimension_semantics=("arbitrary",),
                                             vmem_limit_bytes=EXPERT_VMEM_LIMIT),
        name="expert_mlp",
    )(blk_e, n_used, xs, w_gu, b_gu.reshape(n_e, 1, f2), w_down, b_down.reshape(n_e, 1, d))


def _combine_kernel(res_ref, y_ref, w_ref, gate_ref, o_ref):
    w = w_ref[...]
    y = w[:, 0:1] * y_ref[0].astype(F32)
    for k in range(1, TOP_K):
        y = y + w[:, k:k + 1] * y_ref[k].astype(F32)
    o_ref[...] = res_ref[...] + gate_ref[0] * y


def moe_layer(x, g, shift, scale, gate, w_router, b_router, w_gu, b_gu, w_down, b_down, *, layer, seg_rows,
              rows, bm=MOE_ROWS):
    m, d = rows, x.shape[1]
    n_e = w_router.shape[1]
    h, route, gates, counts = router(x, g, shift, scale, w_router, b_router, seg_rows=seg_rows, rows=rows)
    idx = route[:, :TOP_K]
    rank = route[:, TOP_K:2 * TOP_K]
    n_asg = m * TOP_K
    counts = counts[0, :n_e].astype(jnp.int32)
    padded = (counts + bm - 1) // bm * bm
    pad_end = jnp.cumsum(padded)
    pad_start = pad_end - padded
    start_of = jnp.sum(jnp.where(idx[:, :, None] == jnp.arange(n_e, dtype=jnp.int32), pad_start, 0), axis=-1)
    dest = (start_of + rank).reshape(-1)
    n_blk = -(-(n_asg + n_e * (bm - 1)) // bm)
    n_slot = n_blk * bm
    tok = (jnp.arange(n_asg, dtype=jnp.int32) // TOP_K)
    slot_tok = (jnp.arange(n_slot, dtype=jnp.int32) % m).at[dest].set(tok, unique_indices=True)
    blk_start = jnp.arange(n_blk, dtype=jnp.int32) * bm
    blk_e = jnp.minimum(jnp.sum((pad_end[None, :] <= blk_start[:, None]).astype(jnp.int32), axis=1), n_e - 1)
    n_used = (pad_end[-1] // bm).astype(jnp.int32).reshape(1)
    xs = h.at[slot_tok].get(mode="promise_in_bounds")
    y_slot = expert_mlp(xs, blk_e, n_used, w_gu, b_gu, w_down, b_down, layer=layer, bm=bm)
    dest_km = dest.reshape(m, TOP_K).T.reshape(-1)
    y_tok = y_slot.at[dest_km].get(mode="promise_in_bounds").reshape(TOP_K, m, d)
    tm = 256
    n_seg = gate.shape[0]
    seg = lambda i: (jnp.minimum(i * tm // seg_rows, n_seg - 1), 0, 0)
    return pl.pallas_call(
        _combine_kernel,
        grid=(m // tm,),
        in_specs=[pl.BlockSpec((tm, d), lambda i: (i, 0)),
                  pl.BlockSpec((TOP_K, tm, d), lambda i: (0, i, 0)),
                  pl.BlockSpec((tm, LANES), lambda i: (i, 0)),
                  pl.BlockSpec((1, 1, d), seg)],
        out_specs=pl.BlockSpec((tm, d), lambda i: (i, 0)),
        out_shape=jax.ShapeDtypeStruct(x.shape, F32),
        input_output_aliases={0: 0},
        compiler_params=_cparams(1),
        name="moe_combine",
    )(x, y_tok, gates, gate)


LC_R2 = 128
LC_C = 256
LC_KB = 8
LC_LANES = 2048


@functools.lru_cache(maxsize=None)
def _lc_mats(n):
    nn = 2 * n
    r1 = nn // LC_R2
    tau = 2.0 * np.pi
    k1 = np.arange(r1)
    ang1 = tau * ((k1[:, None] * np.arange(r1)[None, :]) % r1) / r1
    f1 = np.empty((2 * r1, r1))
    f1[0::2] = np.cos(ang1)
    f1[1::2] = -np.sin(ang1)
    kk = k1[:, None] + r1 * np.arange(LC_R2)[None, :]
    th = tau * ((kk[:, :, None] * np.arange(LC_R2)[None, None, :]) % nn) / nn
    c, s = np.cos(th), np.sin(th)
    f2 = np.concatenate([np.concatenate([c, s], 2), np.concatenate([-s, c], 2)], 1)
    ct, st = c.transpose(0, 2, 1), s.transpose(0, 2, 1)
    f3 = np.concatenate([np.concatenate([ct, -st], 2), np.concatenate([st, ct], 2)], 1)
    ang4 = tau * ((np.arange(r1 // 2)[:, None] * k1[None, :]) % r1) / r1
    f4 = np.concatenate([np.cos(ang4), -np.sin(ang4)], 1) / nn
    return tuple(jnp.asarray(a, BF16) for a in (f1, f2, f3, f4))


@functools.lru_cache(maxsize=None)
def _dft_small_mats(n):
    nn = 2 * n
    ang = 2.0 * np.pi * ((np.arange(nn)[:, None] * np.arange(nn)[None, :]) % nn) / nn
    fwd = np.concatenate([np.cos(ang), -np.sin(ang)], 0)
    inv = np.concatenate([np.cos(ang[:n]), -np.sin(ang[:n])], 1) / nn
    return jnp.asarray(fwd, BF16), jnp.asarray(inv, BF16)


def _lc_first_kernel(x_ref, f_ref, o_ref):
    o_ref[...] = jnp.dot(f_ref[...], x_ref[...].astype(BF16), preferred_element_type=F32).astype(o_ref.dtype)


def lc_first(x, f1, s1, off=0):
    _, s2, nh, l = x.shape
    lc = min(LC_LANES, l)
    r2 = f1.shape[0]
    return pl.pallas_call(
        _lc_first_kernel,
        grid=(s1, s2, l // lc),
        in_specs=[pl.BlockSpec((None, None, nh, lc), lambda a, b, j: (off + a, b, 0, j)),
                  pl.BlockSpec((r2, nh), lambda a, b, j: (0, 0))],
        out_specs=pl.BlockSpec((None, None, r2, lc), lambda a, b, j: (a, b, 0, j)),
        out_shape=jax.ShapeDtypeStruct((s1, s2, r2, l), BF16),
        compiler_params=_cparams(3),
        name="lc_first",
    )(x, f1[:, :nh])


def _lc_mid_kernel(a_ref, f2_ref, *rest, kb, spectrum_only):
    if spectrum_only:
        (o_ref,) = rest
    else:
        h_ref, f3_ref, o_ref = rest
    c = a_ref.shape[-1]
    for kk in range(kb):
        a_in = a_ref[2 * kk:2 * kk + 2].reshape(2 * LC_R2, c)
        x = jnp.dot(f2_ref[kk], a_in, preferred_element_type=F32)
        if spectrum_only:
            o_ref[kk] = x.astype(o_ref.dtype)
            continue
        h = h_ref[kk].astype(F32)
        xr, xi = x[:LC_R2], x[LC_R2:]
        hr, hi = h[:LC_R2], h[LC_R2:]
        prod = jnp.concatenate([xr * hr - xi * hi, xr * hi + xi * hr], axis=0).astype(BF16)
        g = jnp.dot(f3_ref[kk], prod, preferred_element_type=F32)
        o_ref[0, kk] = g[:LC_R2].astype(o_ref.dtype)
        o_ref[1, kk] = g[LC_R2:].astype(o_ref.dtype)


def lc_spectrum(a, f2):
    cb, _, r2, _, c = a.shape
    r1 = r2 // 2
    kb = min(LC_KB, r1)
    return pl.pallas_call(
        functools.partial(_lc_mid_kernel, kb=kb, spectrum_only=True),
        grid=(r1 // kb, cb),
        in_specs=[pl.BlockSpec((None, None, 2 * kb, LC_R2, c), lambda j, i: (i, 0, j, 0, 0)),
                  pl.BlockSpec((kb, 2 * LC_R2, 2 * LC_R2), lambda j, i: (j, 0, 0))],
        out_specs=pl.BlockSpec((None, kb, 2 * LC_R2, c), lambda j, i: (i, j, 0, 0)),
        out_shape=jax.ShapeDtypeStruct((cb, r1, 2 * LC_R2, c), BF16),
        compiler_params=_cparams(2),
        name="lc_spectrum",
    )(a, f2)


def lc_mid(a, h, f2, f3):
    cb, bsz, r2, _, c = a.shape
    r1 = r2 // 2
    kb = min(LC_KB, r1)
    mat = pl.BlockSpec((kb, 2 * LC_R2, 2 * LC_R2), lambda j, i, b: (j, 0, 0))
    return pl.pallas_call(
        functools.partial(_lc_mid_kernel, kb=kb, spectrum_only=False),
        grid=(r1 // kb, cb, bsz),
        in_specs=[pl.BlockSpec((None, None, 2 * kb, LC_R2, c), lambda j, i, b: (i, b, j, 0, 0)),
                  mat,
                  pl.BlockSpec((None, kb, 2 * LC_R2, c), lambda j, i, b: (i, j, 0, 0)),
                  mat],
        out_specs=pl.BlockSpec((None, None, 2, kb, LC_R2, c), lambda j, i, b: (i, b, 0, j, 0, 0)),
        out_shape=jax.ShapeDtypeStruct((cb, bsz, 2, r1, LC_R2, c), BF16),
        compiler_params=_cparams(3),
        name="lc_mid",
    )(a, f2, h, f3)


def _lc_last_kernel(g_ref, f_ref, u_ref, gate_ref, skip_ref, o_ref):
    y = jnp.dot(f_ref[...], g_ref[...], preferred_element_type=F32)
    u = u_ref[...].astype(F32)
    o_ref[...] = (gate_ref[...].astype(F32) * (y + skip_ref[...] * u)).astype(o_ref.dtype)


def lc_last(g, f4, u_arr, u_off, gate_arr, gate_off, skip_row):
    cb, bsz, r2, l = g.shape
    nh = r2 // 4
    lc = min(LC_LANES, l)
    return pl.pallas_call(
        _lc_last_kernel,
        grid=(cb, bsz, l // lc),
        in_specs=[pl.BlockSpec((None, None, r2, lc), lambda i, b, j: (i, b, 0, j)),
                  pl.BlockSpec((nh, r2), lambda i, b, j: (0, 0)),
                  pl.BlockSpec((None, None, nh, lc), lambda i, b, j: (u_off + i, b, 0, j)),
                  pl.BlockSpec((None, None, nh, lc), lambda i, b, j: (gate_off + i, b, 0, j)),
                  pl.BlockSpec((None, 1, lc), lambda i, b, j: (i, 0, j))],
        out_specs=pl.BlockSpec((None, None, nh, lc), lambda i, b, j: (i, b, 0, j)),
        out_shape=jax.ShapeDtypeStruct((cb, bsz, nh, l), BF16),
        compiler_params=_cparams(3),
        name="lc_last",
    )(g, f4, u_arr, gate_arr, skip_row)


def _plain_mm_kernel(a_ref, b_ref, o_ref):
    o_ref[...] = jnp.dot(a_ref[...], b_ref[...].astype(BF16), preferred_element_type=F32)


def plain_mm(a, b, tn=512):
    m, k = a.shape
    n = b.shape[1]
    tn = _col_tile(n, tn)
    return pl.pallas_call(
        _plain_mm_kernel,
        grid=(n // tn,),
        in_specs=[pl.BlockSpec((m, k), lambda j: (0, 0)), pl.BlockSpec((k, tn), lambda j: (0, j))],
        out_specs=pl.BlockSpec((m, tn), lambda j: (0, j)),
        out_shape=jax.ShapeDtypeStruct((m, n), F32),
        compiler_params=_cparams(1),
        name="plain_mm",
    )(a, b)


def _short_conv_kernel(u_ref, gate_ref, h_ref, skip_ref, fwd_ref, inv_ref, o_ref):
    u = u_ref[...]
    x = jnp.dot(fwd_ref[...], u.astype(BF16), preferred_element_type=F32)
    h = h_ref[...]
    nn = x.shape[0] // 2
    xr, xi, hr, hi = x[:nn], x[nn:], h[:nn], h[nn:]
    prod = jnp.concatenate([xr * hr - xi * hi, xr * hi + xi * hr], axis=0).astype(BF16)
    y = jnp.dot(inv_ref[...], prod, preferred_element_type=F32)
    o_ref[...] = (gate_ref[...].astype(F32) * (y + skip_ref[...] * u.astype(F32))).astype(o_ref.dtype)


def short_long_conv(u_arr, u_col, gate_arr, gate_col, spec, skip, fwd, inv, *, batch, seq_len, tl=512):
    d = spec.shape[1]
    tl = _col_tile(d, tl)
    uc, gc = u_col // tl, gate_col // tl
    nn2 = spec.shape[0]
    return pl.pallas_call(
        _short_conv_kernel,
        grid=(batch, d // tl),
        in_specs=[pl.BlockSpec((seq_len, tl), lambda b, j: (b, uc + j)),
                  pl.BlockSpec((seq_len, tl), lambda b, j: (b, gc + j)),
                  pl.BlockSpec((nn2, tl), lambda b, j: (0, j)),
                  pl.BlockSpec((1, tl), lambda b, j: (0, j)),
                  pl.BlockSpec((nn2, seq_len), lambda b, j: (0, 0)),
                  pl.BlockSpec((seq_len, nn2), lambda b, j: (0, 0))],
        out_specs=pl.BlockSpec((seq_len, tl), lambda b, j: (b, j)),
        out_shape=jax.ShapeDtypeStruct((batch * seq_len, d), BF16),
        compiler_params=_cparams(2),
        name="short_long_conv",
    )(u_arr, gate_arr, spec, skip, fwd[:, :seq_len], inv)


def _hyena_taps(n, d, f_w1, f_b1, f_w2, f_b2, f_w3, f_b3, f_freq, f_w4):
    hp = lax.Precision.HIGHEST
    lin = jnp.linspace(0.0, 1.0, n, dtype=F32)
    idx = jnp.arange(n, dtype=F32)
    bands = jnp.linspace(1e-4, HY_BANDS - 1, HY_BANDS, dtype=F32)[None, :]
    deltas = jnp.abs(jnp.linspace(math.log(HY_DECAY_TARGET) / HY_FAST_DECAY,
                                  math.log(HY_DECAY_TARGET) / HY_SLOW_DECAY, d, dtype=F32))

    def branch(t, pos, direction):
        t = t[:, None]
        ang = (2.0 * math.pi / n) * pos[:, None]
        emb = jnp.concatenate([t, jnp.cos(bands * ang), -jnp.sin(bands * ang)], axis=-1)
        a = jnp.sin(f_freq * (jnp.dot(emb, f_w1, precision=hp) + f_b1))
        a = jnp.sin(f_freq * (jnp.dot(a, f_w2, precision=hp) + f_b2))
        a = jnp.sin(f_freq * (jnp.dot(a, f_w3, precision=hp) + f_b3))
        decay = jnp.exp(-t * deltas[None, :])
        w4 = f_w4.reshape(f_w4.shape[0], HY_ORDER, 2, d)[:, :, direction]
        return [jnp.dot(a, w4[:, o], precision=hp) * decay for o in range(HY_ORDER)]

    fwd = branch(lin, idx, 0)
    bwd = branch(lin[:0:-1], idx[:0:-1], 1)
    zero = jnp.zeros((1, d), F32)
    return [jnp.concatenate([fwd[o], zero, bwd[o]], axis=0) for o in range(HY_ORDER)]


def hyena_long_convs(zc, taps, skip, *, batch, seq_len):
    cb3, rows, c = zc.shape
    cb = cb3 // 3
    d = cb * c
    nh = seq_len // LC_R2
    lanes = LC_R2 * c
    f1, f2, f3, f4 = _lc_mats(seq_len)
    zc_v = zc.reshape(cb3, batch, nh, lanes)
    u, u_off = zc_v, 2 * cb
    for o in range(HY_ORDER):
        taps_b = taps[o].reshape(2 * seq_len, cb, c).transpose(1, 0, 2).reshape(cb, 1, 2 * nh, lanes)
        spec = lc_spectrum(lc_first(taps_b, f1, cb).reshape(cb, 1, 4 * nh, LC_R2, c), f2)
        a = lc_first(u, f1, cb, u_off).reshape(cb, batch, 4 * nh, LC_R2, c)
        g = lc_mid(a, spec, f2, f3).reshape(cb, batch, 4 * nh, lanes)
        skip_row = jnp.tile(skip[o].reshape(cb, 1, c), (1, 1, LC_R2))
        u = lc_last(g, f4, u, u_off, zc_v, o * cb, skip_row)
        u_off = 0
    return u.reshape(cb, rows, c)


HC = 128
HC_PAIR = 2
LC_TL = 16


@functools.lru_cache(maxsize=None)
def _hy_mats(n):
    nn = 2 * n
    r1 = nn // LC_R2
    tau = 2.0 * np.pi
    k1 = np.arange(r1)
    t = LC_R2 * np.arange(r1)[None, :] + np.arange(LC_R2)[:, None]
    ang1 = tau * ((k1[None, :, None] * t[:, None, :]) % nn) / nn
    f1 = np.empty((LC_R2, 2 * r1, r1))
    f1[:, 0::2] = np.cos(ang1)
    f1[:, 1::2] = -np.sin(ang1)
    ang2 = tau * ((np.arange(LC_R2)[:, None] * np.arange(LC_R2)[None, :]) % LC_R2) / LC_R2
    c, s = np.cos(ang2), np.sin(ang2)
    f2 = np.empty((2 * LC_R2, 2 * LC_R2))
    f2[:LC_R2, 0::2], f2[:LC_R2, 1::2] = c, s
    f2[LC_R2:, 0::2], f2[LC_R2:, 1::2] = -s, c
    f3 = np.empty((2 * LC_R2, 2 * LC_R2))
    f3[0::2, :LC_R2], f3[0::2, LC_R2:] = c, -s
    f3[1::2, :LC_R2], f3[1::2, LC_R2:] = s, c
    m = LC_R2 * np.arange(r1 // 2)[None, :] + np.arange(LC_R2)[:, None]
    ang4 = tau * ((m[:, :, None] * k1[None, None, :]) % nn) / nn
    f4 = np.empty((LC_R2, r1 // 2, 2 * r1))
    f4[:, :, 0::2] = np.cos(ang4) / nn
    f4[:, :, 1::2] = -np.sin(ang4) / nn
    return tuple(jnp.asarray(a, BF16) for a in (f1, f2, f3, f4))


def _hy1_kernel(x_ref, f_ref, o_ref, *, tl_n, nh):
    c = pl.program_id(2)
    for tl in range(tl_n):
        xs = jnp.concatenate([x_ref[h, pl.ds(c * tl_n + tl, nh, stride=LC_R2), :] for h in range(HC_PAIR)],
                             axis=1).astype(BF16)
        a = jnp.dot(f_ref[tl], xs, preferred_element_type=F32)
        packed = pltpu.bitcast(a.astype(BF16), jnp.uint32)
        for h in range(HC_PAIR):
            for j in range(packed.shape[0] // 8):
                o_ref[h, j, tl * 8:(tl + 1) * 8, :] = packed[j * 8:(j + 1) * 8, h * HC:(h + 1) * HC]


def hy_stage1(x, f1, *, blk_off, n_blk, batch, rows_hi):
    r2 = f1.shape[1]
    rows = rows_hi * LC_R2
    tl_n = LC_TL
    return pl.pallas_call(
        functools.partial(_hy1_kernel, tl_n=tl_n, nh=rows_hi),
        grid=(n_blk // HC_PAIR, batch, LC_R2 // tl_n),
        in_specs=[pl.BlockSpec((HC_PAIR, rows, HC), lambda p, b, c: (blk_off // HC_PAIR + p, b, 0)),
                  pl.BlockSpec((tl_n, r2, rows_hi), lambda p, b, c: (c, 0, 0))],
        out_specs=pl.BlockSpec((HC_PAIR, None, r2 // 16, tl_n * 8, HC), lambda p, b, c: (p, b, 0, c, 0)),
        out_shape=jax.ShapeDtypeStruct((n_blk, batch, r2 // 16, LC_R2 * 8, HC), jnp.uint32),
        compiler_params=_cparams(3),
        name="hy_stage1",
    )(x, f1[:, :, :rows_hi])


def _hy2_kernel(a_ref, f2_ref, *rest, spectrum_only):
    if spectrum_only:
        (o_ref,) = rest
    else:
        h_ref, f3_ref, o_ref = rest
    for kk in range(8):
        w = jnp.concatenate([a_ref[h, pl.ds(kk, LC_R2, stride=8), :] for h in range(HC_PAIR)], axis=1)
        a_in = pltpu.bitcast(w, BF16)
        x = jnp.dot(f2_ref[...], a_in, preferred_element_type=F32)
        if spectrum_only:
            o_ref[kk] = x.astype(o_ref.dtype)
            continue
        hs = h_ref[kk].astype(F32)
        xr, xi = x[:LC_R2], x[LC_R2:]
        hr, hi = hs[:LC_R2], hs[LC_R2:]
        prod = jnp.concatenate([xr * hr - xi * hi, xr * hi + xi * hr], axis=0).astype(BF16)
        g = jnp.dot(f3_ref[...], prod, preferred_element_type=F32)
        packed = pltpu.bitcast(g.astype(BF16), jnp.uint32)
        for h in range(HC_PAIR):
            o_ref[h, pl.ds(kk, LC_R2, stride=8), :] = packed[:, h * HC:(h + 1) * HC]


def hy_spectrum(a, f2):
    cb, _, nj, rows, _ = a.shape
    mat = pl.BlockSpec((2 * LC_R2, 2 * LC_R2), lambda j, p: (0, 0))
    return pl.pallas_call(
        functools.partial(_hy2_kernel, spectrum_only=True),
        grid=(nj, cb // HC_PAIR),
        in_specs=[pl.BlockSpec((HC_PAIR, None, None, rows, HC), lambda j, p: (p, 0, j, 0, 0)), mat],
        out_specs=pl.BlockSpec((None, 8, 2 * LC_R2, HC_PAIR * HC), lambda j, p: (p, j, 0, 0)),
        out_shape=jax.ShapeDtypeStruct((cb // HC_PAIR, nj * 8, 2 * LC_R2, HC_PAIR * HC), BF16),
        compiler_params=_cparams(2),
        name="hy_spectrum",
    )(a, f2)


def hy_stage23(a, spec, f2, f3):
    cb, bsz, nj, rows, _ = a.shape
    mat = pl.BlockSpec((2 * LC_R2, 2 * LC_R2), lambda j, p, b: (0, 0))
    blk = pl.BlockSpec((HC_PAIR, None, None, rows, HC), lambda j, p, b: (p, b, j, 0, 0))
    return pl.pallas_call(
        functools.partial(_hy2_kernel, spectrum_only=False),
        grid=(nj, cb // HC_PAIR, bsz),
        in_specs=[blk, mat,
                  pl.BlockSpec((None, 8, 2 * LC_R2, HC_PAIR * HC), lambda j, p, b: (p, j, 0, 0)), mat],
        out_specs=blk,
        out_shape=jax.ShapeDtypeStruct(a.shape, jnp.uint32),
        compiler_params=_cparams(3),
        name="hy_stage23",
    )(a, f2, spec, f3)


def _hy4_kernel(g_ref, f_ref, o_ref, *, tl_n, nh):
    c = pl.program_id(2)
    for tl in range(tl_n):
        w = jnp.concatenate([g_ref[h, :, tl * 8:(tl + 1) * 8, :].reshape(-1, HC) for h in range(HC_PAIR)], axis=1)
        g_in = pltpu.bitcast(w, BF16)
        y = jnp.dot(f_ref[tl], g_in, preferred_element_type=F32)
        for h in range(HC_PAIR):
            o_ref[h, pl.ds(c * tl_n + tl, nh, stride=LC_R2), :] = y[:, h * HC:(h + 1) * HC]


def hy_stage4(g, f4, *, seq_len):
    cb, bsz, nj, _, _ = g.shape
    nh = seq_len // LC_R2
    tl_n = LC_TL
    return pl.pallas_call(
        functools.partial(_hy4_kernel, tl_n=tl_n, nh=nh),
        grid=(cb // HC_PAIR, bsz, LC_R2 // tl_n),
        in_specs=[pl.BlockSpec((HC_PAIR, None, nj, tl_n * 8, HC), lambda p, b, c: (p, b, 0, c, 0)),
                  pl.BlockSpec((tl_n, nh, 16 * nj), lambda p, b, c: (c, 0, 0))],
        out_specs=pl.BlockSpec((HC_PAIR, seq_len, HC), lambda p, b, c: (p, b, 0)),
        out_shape=jax.ShapeDtypeStruct((cb, bsz * seq_len, HC), F32),
        compiler_params=_cparams(3),
        name="hy_stage4",
    )(g, f4)


def _hy_gate_kernel(y_ref, u_ref, gate_ref, skip_ref, o_ref):
    o_ref[...] = (gate_ref[...] * (y_ref[...] + skip_ref[...] * u_ref[...])).astype(o_ref.dtype)


def hy_gate(y, u_arr, u_off, gate_arr, gate_off, skip, *, natural, tr=4096):
    cb, rows, _ = y.shape
    tr = min(tr, rows)
    if natural:
        out_spec = pl.BlockSpec((tr, HC), lambda i, r: (r, i))
        out_shape = jax.ShapeDtypeStruct((rows, cb * HC), BF16)
    else:
        out_spec = pl.BlockSpec((None, tr, HC), lambda i, r: (i, r, 0))
        out_shape = jax.ShapeDtypeStruct((cb, rows, HC), F32)
    return pl.pallas_call(
        _hy_gate_kernel,
        grid=(cb, rows // tr),
        in_specs=[pl.BlockSpec((None, tr, HC), lambda i, r: (i, r, 0)),
                  pl.BlockSpec((None, tr, HC), lambda i, r: (u_off + i, r, 0)),
                  pl.BlockSpec((None, tr, HC), lambda i, r: (gate_off + i, r, 0)),
                  pl.BlockSpec((None, 1, HC), lambda i, r: (i, 0, 0))],
        out_specs=out_spec,
        out_shape=out_shape,
        compiler_params=_cparams(2),
        name="hy_gate",
    )(y, u_arr, gate_arr, skip.reshape(cb, 1, HC))


def _hy_taps_kernel(a_ref, t_ref, w_ref, dl_ref, o_ref):
    a = a_ref[...]
    a_hi = a.astype(BF16)
    a_lo = (a - a_hi.astype(F32)).astype(BF16)
    w = w_ref[...]
    w_hi = w.astype(BF16)
    w_lo = (w - w_hi.astype(F32)).astype(BF16)
    h = (jnp.dot(a_hi, w_hi, preferred_element_type=F32) + jnp.dot(a_hi, w_lo, preferred_element_type=F32)
         + jnp.dot(a_lo, w_hi, preferred_element_type=F32))
    o_ref[...] = h * jnp.exp(-t_ref[...] * dl_ref[...])


def hy_taps(a_all, t_all, w4, deltas, *, seq_len, order, tr=1024):
    nrow, width = a_all.shape
    d = deltas.shape[0]
    per_dir = seq_len // tr
    return pl.pallas_call(
        _hy_taps_kernel,
        grid=(d // HC, nrow // tr),
        in_specs=[pl.BlockSpec((tr, width), lambda i, r: (r, 0)),
                  pl.BlockSpec((tr, 1), lambda i, r: (r, 0)),
                  pl.BlockSpec((None, None, width, HC), lambda i, r: (order, r // per_dir, 0, i)),
                  pl.BlockSpec((1, HC), lambda i, r: (0, i))],
        out_specs=pl.BlockSpec((None, tr, HC), lambda i, r: (i, r, 0)),
        out_shape=jax.ShapeDtypeStruct((d // HC, nrow, HC), F32),
        compiler_params=_cparams(2),
        name="hy_taps",
    )(a_all, t_all, w4.reshape(width, HY_ORDER, 2, d).transpose(1, 2, 0, 3), deltas.reshape(1, d))


def _hyena_filter_hidden(n, f_w1, f_b1, f_w2, f_b2, f_w3, f_b3, f_freq):
    hp = lax.Precision.HIGHEST
    lin = jnp.linspace(0.0, 1.0, n, dtype=F32)
    idx = jnp.arange(n, dtype=F32)
    bands = jnp.linspace(1e-4, HY_BANDS - 1, HY_BANDS, dtype=F32)[None, :]
    t = jnp.concatenate([lin, jnp.zeros((1,), F32), lin[:0:-1]])[:, None]
    pos = jnp.concatenate([idx, jnp.zeros((1,), F32), idx[:0:-1]])[:, None]
    ang = (2.0 * math.pi / n) * pos
    emb = jnp.concatenate([t, jnp.cos(bands * ang), -jnp.sin(bands * ang)], axis=-1)
    a = jnp.sin(f_freq * (jnp.dot(emb, f_w1, precision=hp) + f_b1))
    a = jnp.sin(f_freq * (jnp.dot(a, f_w2, precision=hp) + f_b2))
    a = jnp.sin(f_freq * (jnp.dot(a, f_w3, precision=hp) + f_b3))
    keep = (jnp.arange(2 * n) != n)[:, None]
    return jnp.where(keep, a, 0.0), t


def _hyena_deltas(d):
    return jnp.abs(jnp.linspace(math.log(HY_DECAY_TARGET) / HY_FAST_DECAY,
                                math.log(HY_DECAY_TARGET) / HY_SLOW_DECAY, d, dtype=F32))


def hyena_long_convs2(zc, a_all, t_all, w4, skip, *, batch, seq_len, d):
    cb = d // HC
    f1, f2, f3, f4 = _hy_mats(seq_len)
    nh = seq_len // LC_R2
    deltas = _hyena_deltas(d)
    u, u_off = zc, 2 * cb
    for o in range(HY_ORDER):
        taps = hy_taps(a_all, t_all, w4, deltas, seq_len=seq_len, order=o)
        spec = hy_spectrum(hy_stage1(taps, f1, blk_off=0, n_blk=cb, batch=1, rows_hi=2 * nh), f2)
        a = hy_stage1(u, f1, blk_off=u_off, n_blk=cb, batch=batch, rows_hi=nh)
        y = hy_stage4(hy_stage23(a, spec, f2, f3), f4, seq_len=seq_len)
        u = hy_gate(y, u, u_off, zc, o * cb, skip[o], natural=(o == HY_ORDER - 1))
        u_off = 0
    return u


def _final_norm_kernel(x_ref, g_ref, o_ref):
    x = x_ref[...]
    o_ref[...] = x * lax.rsqrt(jnp.mean(x * x, axis=-1, keepdims=True) + NORM_EPS) * g_ref[...]


def final_norm(x, g, *, rows, tm=ROW_TILE):
    d = x.shape[1]
    return pl.pallas_call(
        _final_norm_kernel,
        grid=(rows // tm,),
        in_specs=[pl.BlockSpec((tm, d), lambda i: (i, 0)), pl.BlockSpec((1, d), lambda i: (0, 0))],
        out_specs=pl.BlockSpec((tm, d), lambda i: (i, 0)),
        out_shape=jax.ShapeDtypeStruct((rows, d), F32),
        compiler_params=_cparams(1),
        name="final_norm",
    )(x, g)


def _rope_tables(n_lat, seq_len, n_rows):
    t = jnp.arange(n_rows, dtype=jnp.int32)
    row = ((t % seq_len) // GRID_W).astype(F32)
    col = (t % GRID_W).astype(F32)
    lane = jnp.arange(LANES)
    dd = lane % HEAD_DIM
    quarter = HEAD_DIM // 4
    inv_freq = ROPE_BASE ** (-(dd % quarter).astype(F32) / quarter)
    pos = jnp.where(dd[None, :] < HEAD_DIM // 2, row[:, None], col[:, None])
    ang = pos * inv_freq[None, :]
    sign = jnp.where((dd % (HEAD_DIM // 2)) < quarter, -1.0, 1.0).astype(F32)
    lat = (t < n_lat)[:, None]
    cos = jnp.where(lat, jnp.cos(ang), 1.0)
    sin = jnp.where(lat, jnp.sin(ang) * sign[None, :], 0.0)
    return cos, sin


def _wa_head_perm(n_heads):
    order = []
    for p in range(n_heads // (2 * WA_GROUP)):
        for g in range(WA_GROUP):
            order += [2 * WA_GROUP * p + g, 2 * WA_GROUP * p + WA_GROUP + g]
    cols = jnp.asarray(order, jnp.int32)[:, None] * HEAD_DIM + jnp.arange(HEAD_DIM, dtype=jnp.int32)[None, :]
    return cols.reshape(-1)


def kernel(x, c, ctx, c_ctx, w_ada, b_ada, g_mix, g_ffn, hy_w_in, hy_b_in, hy_w_short, hy_b_short, hy_f_w1, hy_f_b1, hy_f_w2, hy_f_b2, hy_f_w3, hy_f_b3, hy_f_freq, hy_f_w4, hy_skip, hy_w_out, hy_b_out, cf_w_pw1, cf_b_pw1, cf_w_dw, cf_b_dw, cf_ln_g, cf_ln_b, cf_w_pw2, cf_b_pw2, wa_w_qkv, wa_w_o, wa_sinks, na_w_qkv, na_w_o, na_rpb, moe_w_router, moe_b_router, moe_w_gu, moe_b_gu, moe_w_down, moe_b_down, g_final):
    bsz, n, d = x.shape
    n_ctx = ctx.shape[1]
    depth = w_ada.shape[0]
    n_lat = bsz * n
    m_all = n_lat + bsz * n_ctx

    cond = jnp.zeros((8, d), F32).at[:bsz].set(jax.nn.silu(c)).at[bsz].set(jax.nn.silu(c_ctx))
    mods = adaln(cond, w_ada, b_ada)
    xu = jnp.concatenate([x.reshape(n_lat, d), ctx.reshape(bsz * n_ctx, d)], axis=0)
    zeros_d = jnp.zeros((1, d), F32)

    for i in range(depth):
        kind, j = i % N_MIXERS, i // N_MIXERS
        last = i == depth - 1
        mod = [mods[i, :bsz + 1, k * d:(k + 1) * d].reshape(bsz + 1, 1, d) for k in range(6)]
        m_out = n_lat if last else m_all
        gm = g_mix[i].reshape(1, d)
        need_ctx_in = (not last) or kind >= 2
        m_in = m_all if need_ctx_in else n_lat
        x_in = xu[:m_in]

        if kind == 0:
            z = nm_matmul(x_in, gm, mod[0], mod[1], hy_w_in[j].astype(BF16), hy_b_in[j].reshape(1, -1),
                          seg_rows=n)
            filt_w = (hy_f_w1[j], hy_f_b1[j], hy_f_w2[j], hy_f_b2[j], hy_f_w3[j], hy_f_b3[j], hy_f_freq[j],
                      hy_f_w4[j])
            w_s, b_s = hy_w_short[j], hy_b_short[j].reshape(1, -1)
            w_o, b_o = hy_w_out[j].astype(BF16), hy_b_out[j].reshape(1, d)
            zc = dwconv(z, w_s, b_s, start=0, total=n_lat, seq_len=n, tc=HC, blocked=True, rows=2048,
                        out_dtype=F32)
            a_all, t_all = _hyena_filter_hidden(n, *filt_w[:-1])
            y = hyena_long_convs2(zc, a_all, t_all, hy_f_w4[j], hy_skip[j], batch=bsz, seq_len=n, d=d)
            xu = mm_res(y, w_o, b_o, xu, mod[2], seg_rows=n)
            if m_in > n_lat:
                zcc = dwconv(z, w_s, b_s, start=n_lat, total=m_in - n_lat, seq_len=n_ctx, tc=512)
                taps_c = _hyena_taps(n_ctx, d, *filt_w)
                fwd, inv = _dft_small_mats(n_ctx)
                yc = short_long_conv(zcc, 2 * d, zcc, 0, plain_mm(fwd, taps_c[0]), hy_skip[j][0].reshape(1, d),
                                     fwd, inv, batch=bsz, seq_len=n_ctx)
                yc = short_long_conv(yc, 0, zcc, d, plain_mm(fwd, taps_c[1]), hy_skip[j][1].reshape(1, d),
                                     fwd, inv, batch=bsz, seq_len=n_ctx)
                if not last:
                    xu = mm_res(yc, w_o, b_o, xu, mod[2], seg_rows=n, row_off=n_lat)
            y = None
        elif kind == 1:
            w1 = cf_w_pw1[j].astype(BF16).reshape(d, 2, d).transpose(1, 0, 2)
            a = nm_matmul(x_in, gm, mod[0], mod[1], w1, cf_b_pw1[j].reshape(2, 1, d), seg_rows=n, mode="glu")
            conv = functools.partial(dwconv, a, cf_w_dw[j], cf_b_dw[j].reshape(1, d), post="ln_silu",
                                     ln=(cf_ln_g[j].reshape(1, d), cf_ln_b[j].reshape(1, d)), rows=64)
            y = conv(start=0, total=n_lat, seq_len=n, same_rows=True)
            if m_in > n_lat:
                y = conv(start=n_lat, total=m_in - n_lat, seq_len=n_ctx, same_rows=True, into=y)
            w_o, b_o = cf_w_pw2[j].astype(BF16), cf_b_pw2[j].reshape(1, d)
        elif kind == 2:
            perm = _wa_head_perm(d // HEAD_DIM)
            scale = HEAD_DIM ** -0.5
            w_qkv = jnp.concatenate([wa_w_qkv[j][:, :d][:, perm] * scale, wa_w_qkv[j][:, d:]], axis=1).astype(BF16)
            n_out = w_qkv.shape[1]
            qkv = nm_matmul(x_in, gm, mod[0], mod[1], w_qkv, jnp.zeros((1, n_out), F32), seg_rows=n,
                            mode="rope", rope=_rope_tables(n_lat, n, m_in), n_rope_cols=d + (n_out - d) // 2)
            y = window_attention(qkv, wa_sinks[j], batch=bsz, seq_len=n, ctx_len=n_ctx, d=d)
            w_o, b_o = wa_w_o[j][perm].astype(BF16), zeros_d
        else:
            scale = HEAD_DIM ** -0.5
            w_qkv = jnp.concatenate([na_w_qkv[j][:, :d] * scale, na_w_qkv[j][:, d:]], axis=1).astype(BF16)
            qkv = nm_matmul(x_in, gm, mod[0], mod[1], w_qkv, jnp.zeros((1, 3 * d), F32), seg_rows=n)
            y = neighbourhood_attention(qkv, na_rpb[j], batch=bsz, seq_len=n, ctx_len=n_ctx, d=d,
                                        with_ctx_out=not last)
            w_o, b_o = na_w_o[j].astype(BF16), zeros_d
        if y is not None:
            xu = mm_res(y, w_o, b_o, xu, mod[2], seg_rows=n, rows=m_out)

        xu = moe_layer(xu, g_ffn[i].reshape(1, d), mod[3], mod[4], mod[5], moe_w_router[i], moe_b_router[i],
                       moe_w_gu, moe_b_gu[i], moe_w_down, moe_b_down[i], layer=i, seg_rows=n, rows=m_out)
    return final_norm(xu, g_final.reshape(1, d), rows=n_lat).reshape(bsz, n, d)
```

```python
import functools
import math

import jax
import jax.numpy as jnp
import numpy as np
from jax import lax
from jax.experimental import pallas as pl
from jax.experimental.pallas import tpu as pltpu

F32 = jnp.float32
BF16 = jnp.bfloat16

GRID_W = 64
N_MIXERS = 4
NORM_EPS = 1e-6
NEG_INF = -1e30
HEAD_DIM = 64
ROPE_BASE = 10000.0

HY_ORDER = 2
HY_BANDS = 16
HY_DECAY_TARGET = 1e-2
HY_FAST_DECAY = 0.3
HY_SLOW_DECAY = 1.5

WA_GROUP = 4
WA_WINDOW = 128
NA_WIN_ROWS = 8
NA_WIN_COLS = 16

TOP_K = 4
SWIGLU_LIMIT = 7.0
SWIGLU_ALPHA = 1.702

LANES = 128
ROW_TILE = 512
MOE_ROWS = 512
VMEM_LIMIT = 56 * 1024 * 1024
EXPERT_VMEM_LIMIT = 58 * 1024 * 1024


def _cparams(n_axes):
    return pltpu.CompilerParams(dimension_semantics=("arbitrary",) * n_axes,
                                vmem_limit_bytes=VMEM_LIMIT)


def _col_tile(n, pref=1024):
    t = min(pref, n)
    while n % t:
        t //= 2
    return t


def _adaln_kernel(c_ref, w_ref, b_ref, o_ref):
    w = w_ref[0]
    w_hi = w.astype(BF16)
    w_lo = (w - w_hi.astype(F32)).astype(BF16)
    c = c_ref[...]
    c_hi = c.astype(BF16)
    c_lo = (c - c_hi.astype(F32)).astype(BF16)
    acc = jnp.dot(c_hi, w_hi, preferred_element_type=F32)
    acc += jnp.dot(c_hi, w_lo, preferred_element_type=F32)
    acc += jnp.dot(c_lo, w_hi, preferred_element_type=F32)
    o_ref[0] = acc + b_ref[0]


def adaln(cond, w_ada, b_ada):
    depth, d, n6 = w_ada.shape
    tn = _col_tile(n6, 1024)
    return pl.pallas_call(
        _adaln_kernel,
        grid=(depth, n6 // tn),
        in_specs=[pl.BlockSpec((8, d), lambda l, j: (0, 0)),
                  pl.BlockSpec((1, d, tn), lambda l, j: (l, 0, j)),
                  pl.BlockSpec((1, 1, tn), lambda l, j: (l, 0, j))],
        out_specs=pl.BlockSpec((1, 8, tn), lambda l, j: (l, 0, j)),
        out_shape=jax.ShapeDtypeStruct((depth, 8, n6), F32),
        compiler_params=_cparams(2),
        name="adaln",
    )(cond, w_ada, b_ada.reshape(depth, 1, n6))


def _norm_mod(x, g, shift, scale):
    y = x * lax.rsqrt(jnp.mean(x * x, axis=-1, keepdims=True) + NORM_EPS)
    return (y * g) * (1 + scale) + shift


def _rope_tile(x, cos, sin):
    lane = lax.broadcasted_iota(jnp.int32, x.shape, 1)
    nxt = pltpu.roll(x, LANES - 16, axis=1)
    prv = pltpu.roll(x, 16, axis=1)
    partner = jnp.where((lane // 16) % 2 == 0, nxt, prv)
    return x * cos + partner * sin


def _nm_mm_kernel(x_ref, g_ref, sh_ref, sc_ref, w_ref, b_ref, *rest, mode, n_rope):
    if mode == "rope":
        cos_ref, sin_ref, o_ref, h_ref = rest
    else:
        o_ref, h_ref = rest
    j = pl.program_id(1)

    @pl.when(j == 0)
    def _():
        h_ref[...] = _norm_mod(x_ref[...], g_ref[...], sh_ref[0], sc_ref[0]).astype(BF16)

    h = h_ref[...]
    if mode == "glu":
        a = jnp.dot(h, w_ref[0], preferred_element_type=F32) + b_ref[0]
        gate = jnp.dot(h, w_ref[1], preferred_element_type=F32) + b_ref[1]
        o_ref[...] = (a * jax.nn.sigmoid(gate)).astype(o_ref.dtype)
        return
    acc = jnp.dot(h, w_ref[...], preferred_element_type=F32) + b_ref[...]
    if mode == "rope":
        @pl.when(j < n_rope)
        def _():
            cos = cos_ref[...]
            sin = sin_ref[...]
            for g in range(acc.shape[1] // LANES):
                sl = slice(g * LANES, (g + 1) * LANES)
                o_ref[:, sl] = _rope_tile(acc[:, sl], cos, sin).astype(o_ref.dtype)

        @pl.when(j >= n_rope)
        def _():
            o_ref[...] = acc.astype(o_ref.dtype)
    else:
        o_ref[...] = acc.astype(o_ref.dtype)


def nm_matmul(x, g, shift, scale, w, b, *, seg_rows, mode="plain", rope=None, n_rope_cols=0,
              tm=ROW_TILE, tn=None):
    m, d = x.shape
    n_seg = shift.shape[0]
    n = w.shape[-1]
    tn = tn or (_col_tile(math.gcd(n, n_rope_cols), 512) if mode == "rope" else _col_tile(n, 1024))
    seg = lambda i, j: (jnp.minimum(i * tm // seg_rows, n_seg - 1), 0, 0)
    in_specs = [pl.BlockSpec((tm, d), lambda i, j: (i, 0)),
                pl.BlockSpec((1, d), lambda i, j: (0, 0)),
                pl.BlockSpec((1, 1, d), seg),
                pl.BlockSpec((1, 1, d), seg)]
    if mode == "glu":
        in_specs += [pl.BlockSpec((2, d, tn), lambda i, j: (0, 0, j)),
                     pl.BlockSpec((2, 1, tn), lambda i, j: (0, 0, j))]
    else:
        in_specs += [pl.BlockSpec((d, tn), lambda i, j: (0, j)),
                     pl.BlockSpec((1, tn), lambda i, j: (0, j))]
    args = [x, g, shift, scale, w, b]
    if mode == "rope":
        in_specs += [pl.BlockSpec((tm, LANES), lambda i, j: (i, 0))] * 2
        args += list(rope)
    return pl.pallas_call(
        functools.partial(_nm_mm_kernel, mode=mode, n_rope=n_rope_cols // tn),
        grid=(m // tm, n // tn),
        in_specs=in_specs,
        out_specs=pl.BlockSpec((tm, tn), lambda i, j: (i, j)),
        out_shape=jax.ShapeDtypeStruct((m, n), BF16),
        scratch_shapes=[pltpu.VMEM((tm, d), BF16)],
        compiler_params=_cparams(2),
        name="nm_matmul_" + mode,
    )(*args)


def _mm_res_kernel(a_ref, w_ref, b_ref, res_ref, gate_ref, o_ref):
    if len(a_ref.shape) == 3:
        kc = a_ref.shape[2]
        acc = b_ref[...]
        for cb in range(a_ref.shape[0]):
            acc = acc + jnp.dot(a_ref[cb], w_ref[cb * kc:(cb + 1) * kc, :], preferred_element_type=F32)
    else:
        acc = jnp.dot(a_ref[...], w_ref[...], preferred_element_type=F32) + b_ref[...]
    o_ref[...] = res_ref[...] + gate_ref[0] * acc


def mm_res(a, w, b, res, gate, *, seg_rows, row_off=0, rows=None, tm=ROW_TILE, tn=None):
    if a.ndim == 3:
        m = rows or a.shape[1]
        k = a.shape[0] * a.shape[2]
        a_spec = pl.BlockSpec((a.shape[0], tm, a.shape[2]), lambda i, j: (0, i, 0))
    else:
        m, k = rows or a.shape[0], a.shape[1]
        a_spec = pl.BlockSpec((tm, k), lambda i, j: (i, 0))
    n = w.shape[1]
    n_seg = gate.shape[0]
    tn = tn or _col_tile(n, 1024)
    blk_off = row_off // tm
    seg = lambda i, j: (jnp.minimum((blk_off + i) * tm // seg_rows, n_seg - 1), 0, j)
    return pl.pallas_call(
        _mm_res_kernel,
        grid=(m // tm, n // tn),
        in_specs=[a_spec,
                  pl.BlockSpec((k, tn), lambda i, j: (0, j)),
                  pl.BlockSpec((1, tn), lambda i, j: (0, j)),
                  pl.BlockSpec((tm, tn), lambda i, j: (blk_off + i, j)),
                  pl.BlockSpec((1, 1, tn), seg)],
        out_specs=pl.BlockSpec((tm, tn), lambda i, j: (blk_off + i, j)),
        out_shape=jax.ShapeDtypeStruct(res.shape, F32),
        input_output_aliases={3: 0},
        compiler_params=_cparams(2),
        name="mm_res",
    )(a, w, b, res, gate)


CONV_HALO = 16


def _dwconv_kernel(prev_ref, x_ref, next_ref, w_ref, b_ref, *rest, taps, blocks_per_seq, post, sub, n_alias):
    if n_alias:
        n_tail = 3 if post == "ln_silu" else 2
        rest = rest[:-n_tail - 1] + rest[-n_tail:]
    if post == "ln_silu":
        g_ref, beta_ref, o_ref, win_ref, conv_ref = rest
    else:
        o_ref, win_ref = rest
    i = pl.program_id(0)
    pos = i % blocks_per_seq
    rows, c = x_ref.shape
    half = taps // 2
    zero = jnp.zeros((CONV_HALO, c), F32)
    win_ref[CONV_HALO:CONV_HALO + rows, :] = x_ref[...].astype(F32)

    @pl.when(pos == 0)
    def _():
        win_ref[0:CONV_HALO, :] = zero

    @pl.when(pos > 0)
    def _():
        win_ref[0:CONV_HALO, :] = prev_ref[...].astype(F32)

    @pl.when(pos == blocks_per_seq - 1)
    def _():
        win_ref[CONV_HALO + rows:, :] = zero

    @pl.when(pos < blocks_per_seq - 1)
    def _():
        win_ref[CONV_HALO + rows:, :] = next_ref[...].astype(F32)

    bias = b_ref[...]
    if post == "ln_silu":
        ext = rows + sub
        groups = {}
        for t in range(taps):
            off = CONV_HALO - half + t
            groups.setdefault(off % sub, []).append((t, off // sub))
        lw = min(2 * LANES, c)
        for lc in range(0, c, lw):
            lanes = slice(lc, lc + lw)
            y = jnp.broadcast_to(bias[:, lanes], (rows, lw))
            for s, members in sorted(groups.items()):
                acc = None
                for t, a in members:
                    term = pltpu.repeat(w_ref[t, :, lanes], ext // sub, axis=0) * win_ref[sub * a:sub * a + ext, lanes]
                    acc = term if acc is None else acc + term
                y = y + acc[s:s + rows]
            conv_ref[:, lanes] = y
        for s in range(rows // sub):
            acc = conv_ref[s * sub:(s + 1) * sub, :]
            mu = jnp.mean(acc, axis=-1, keepdims=True)
            xc = acc - mu
            var = jnp.mean(xc * xc, axis=-1, keepdims=True)
            y = xc * lax.rsqrt(var + NORM_EPS) * g_ref[...] + beta_ref[...]
            o_ref[s * sub:(s + 1) * sub, :] = (y * jax.nn.sigmoid(y)).astype(o_ref.dtype)
        return
    for s in range(rows // sub):
        base = CONV_HALO + s * sub - half
        acc = jnp.broadcast_to(bias, (sub, c))
        for t in range(taps):
            acc = acc + w_ref[t] * win_ref[base + t:base + t + sub, :]
        o_ref[s * sub:(s + 1) * sub, :] = acc.astype(o_ref.dtype)


def dwconv(x, w, b, *, start, total, seq_len, post=None, ln=None, rows=256, tc=None, out_dtype=None,
           blocked=False, same_rows=False, into=None):
    m, c = x.shape
    taps = w.shape[0]
    tc = tc or c
    out_dtype = out_dtype or BF16
    sub = 8
    w = jnp.broadcast_to(w[:, None, :], (taps, sub, c))
    r = min(rows, seq_len)
    bps = seq_len // r
    hb = r // CONV_HALO
    off = start // r
    offh = start // CONV_HALO
    nh = m // CONV_HALO
    in_specs = [pl.BlockSpec((CONV_HALO, tc), lambda i, j: (jnp.maximum(offh + i * hb - 1, 0), j)),
                pl.BlockSpec((r, tc), lambda i, j: (off + i, j)),
                pl.BlockSpec((CONV_HALO, tc), lambda i, j: (jnp.minimum(offh + (i + 1) * hb, nh - 1), j)),
                pl.BlockSpec((taps, sub, tc), lambda i, j: (0, 0, j)),
                pl.BlockSpec((1, tc), lambda i, j: (0, j))]
    args = [x, x, x, w, b]
    if post == "ln_silu":
        in_specs += [pl.BlockSpec((1, tc), lambda i, j: (0, j))] * 2
        args += list(ln)
    aliases = {}
    if blocked:
        out_spec = pl.BlockSpec((None, r, tc), lambda i, j: (j, i, 0))
        out_shape = jax.ShapeDtypeStruct((c // tc, total, tc), out_dtype)
    elif same_rows:
        out_spec = pl.BlockSpec((r, tc), lambda i, j: (off + i, j))
        out_shape = jax.ShapeDtypeStruct((m, c), out_dtype)
        if into is not None:
            in_specs.append(pl.BlockSpec(memory_space=pl.ANY))
            args.append(into)
            aliases = {len(args) - 1: 0}
    else:
        out_spec = pl.BlockSpec((r, tc), lambda i, j: (i, j))
        out_shape = jax.ShapeDtypeStruct((total, c), out_dtype)
    return pl.pallas_call(
        functools.partial(_dwconv_kernel, taps=taps, blocks_per_seq=bps, post=post, sub=sub,
                          n_alias=len(aliases)),
        grid=(total // r, c // tc),
        in_specs=in_specs,
        out_specs=out_spec,
        out_shape=out_shape,
        input_output_aliases=aliases,
        scratch_shapes=[pltpu.VMEM((r + 2 * CONV_HALO, tc), F32)]
        + ([pltpu.VMEM((r, tc), F32)] if post == "ln_silu" else []),
        compiler_params=_cparams(2),
        name="dwconv%d" % taps,
    )(*args)


def _masked_halves(q, lane_lo):
    zero = jnp.zeros_like(q)
    return jnp.where(lane_lo, q, zero), jnp.where(lane_lo, zero, q)


def _softmax_pv(s_parts, v_parts, sink):
    m = s_parts[0].max(axis=-1, keepdims=True)
    for s in s_parts[1:]:
        m = jnp.maximum(m, s.max(axis=-1, keepdims=True))
    if sink is not None:
        m = jnp.maximum(m, sink)
    denom = jnp.exp(sink - m) if sink is not None else 0.0
    o = None
    for s, v in zip(s_parts, v_parts):
        p = jnp.exp(s - m)
        denom = denom + p.sum(axis=-1, keepdims=True)
        pv = jnp.dot(p.astype(BF16), v, preferred_element_type=F32)
        o = pv if o is None else o + pv
    return o / denom


def _nt_dot(a, b):
    return lax.dot_general(a, b, (((1,), (1,)), ((), ())), preferred_element_type=F32)


def _wattn_kernel(sink_ref, q_ref, *rest, local, seq_len, blk):
    if local:
        k0, k1, k2, v0, v1, v2, kc_ref, vc_ref, o_ref = rest
    else:
        kc_ref, vc_ref, _, o_ref = rest
    p = pl.program_id(2)
    i = pl.program_id(1)
    lane_lo = lax.broadcasted_iota(jnp.int32, (blk, LANES), 1) < HEAD_DIM
    kc = kc_ref[...]
    vc = vc_ref[...]
    if local:
        kl = jnp.concatenate([k0[...], k1[...], k2[...]], axis=0)
        vl = jnp.concatenate([v0[...], v1[...], v2[...]], axis=0)
        qpos = i * blk + lax.broadcasted_iota(jnp.int32, (blk, 3 * blk), 0)
        kpos = (i - 1) * blk + lax.broadcasted_iota(jnp.int32, (blk, 3 * blk), 1)
        valid = (jnp.abs(kpos - qpos) <= WA_WINDOW) & (kpos >= 0) & (kpos < seq_len)
        valid = jnp.concatenate([valid] * WA_GROUP, axis=0)
    qa, qb = [], []
    for g in range(WA_GROUP):
        a, b = _masked_halves(q_ref[:, g * LANES:(g + 1) * LANES], lane_lo)
        qa.append(a)
        qb.append(b)
    outs = []
    for half, qs in enumerate((qa, qb)):
        qs = jnp.concatenate(qs, axis=0)
        sink = jnp.concatenate(
            [jnp.full((blk, 1), sink_ref[8 * p + 4 * half + g], F32) for g in range(WA_GROUP)], axis=0)
        s_parts, v_parts = [], []
        if local:
            s_parts.append(jnp.where(valid, _nt_dot(qs, kl), NEG_INF))
            v_parts.append(vl)
        s_parts.append(_nt_dot(qs, kc))
        v_parts.append(vc)
        outs.append(_softmax_pv(s_parts, v_parts, sink))
    for g in range(WA_GROUP):
        rows = slice(g * blk, (g + 1) * blk)
        o_ref[:, g * LANES:(g + 1) * LANES] = jnp.where(lane_lo, outs[0][rows], outs[1][rows]).astype(o_ref.dtype)


def window_attention(qkv, sinks, *, batch, seq_len, ctx_len, d, blk=128):
    n_pairs = d // (2 * WA_GROUP * HEAD_DIM)
    kcol = d // LANES
    vcol = kcol + n_pairs
    nblk = seq_len // blk
    cb0 = batch * seq_len // ctx_len
    qw = WA_GROUP * LANES
    kern = functools.partial(_wattn_kernel, seq_len=seq_len)
    smem = pl.BlockSpec(memory_space=pltpu.SMEM)

    def kspec(col0, shift):
        return pl.BlockSpec((blk, LANES),
                            lambda b, i, p: (b * nblk + jnp.clip(i + shift, 0, nblk - 1), col0 + p))

    ctx_k = pl.BlockSpec((ctx_len, LANES), lambda b, i, p: (cb0 + b, kcol + p))
    ctx_v = pl.BlockSpec((ctx_len, LANES), lambda b, i, p: (cb0 + b, vcol + p))
    lat = pl.pallas_call(
        functools.partial(kern, local=True, blk=blk),
        grid=(batch, nblk, n_pairs),
        in_specs=[smem, pl.BlockSpec((blk, qw), lambda b, i, p: (b * nblk + i, p))]
        + [kspec(kcol, s) for s in (-1, 0, 1)] + [kspec(vcol, s) for s in (-1, 0, 1)] + [ctx_k, ctx_v],
        out_specs=pl.BlockSpec((blk, qw), lambda b, i, p: (b * nblk + i, p)),
        out_shape=jax.ShapeDtypeStruct((qkv.shape[0], d), BF16),
        compiler_params=_cparams(3),
        name="window_attn",
    )(sinks, qkv, *([qkv] * 8))
    return pl.pallas_call(
        functools.partial(kern, local=False, blk=ctx_len),
        grid=(batch, 1, n_pairs),
        in_specs=[smem, pl.BlockSpec((ctx_len, qw), lambda b, i, p: (cb0 + b, p)), ctx_k, ctx_v,
                  pl.BlockSpec(memory_space=pl.ANY)],
        out_specs=pl.BlockSpec((ctx_len, qw), lambda b, i, p: (cb0 + b, p)),
        out_shape=jax.ShapeDtypeStruct((qkv.shape[0], d), BF16),
        input_output_aliases={4: 0},
        compiler_params=_cparams(3),
        name="ctx_attn",
    )(sinks, qkv, qkv, qkv, lat)


def _nattn_kernel(q_ref, k0, k1, k2, v0, v1, v2, kc_ref, vc_ref, bias_ref, o_ref, ks_ref, vs_ref,
                  *, grid_rows, rows_per_blk):
    j = pl.program_id(2)
    blk = k0.shape[0]
    for t, (kr, vr) in enumerate(((k0, v0), (k1, v1), (k2, v2))):
        ks_ref[t * blk:(t + 1) * blk, :] = kr[...]
        vs_ref[t * blk:(t + 1) * blk, :] = vr[...]
    kc = kc_ref[...]
    vc = vc_ref[...]
    lane_lo = lax.broadcasted_iota(jnp.int32, (GRID_W, LANES), 1) < HEAD_DIM
    strip = NA_WIN_ROWS * GRID_W
    for r in range(rows_per_blk):
        row = j * rows_per_blk + r
        r0 = jnp.clip(row - NA_WIN_ROWS // 2, 0, grid_rows - NA_WIN_ROWS)
        start = pl.multiple_of((r0 - (j - 1) * rows_per_blk) * GRID_W, GRID_W)
        cls = row - r0
        qa, qb = _masked_halves(q_ref[r * GRID_W:(r + 1) * GRID_W, :], lane_lo)
        qs = jnp.concatenate([qa, qb], axis=0)
        kn = ks_ref[pl.ds(start, strip), :]
        vn = vs_ref[pl.ds(start, strip), :]
        s_nb = _nt_dot(qs, kn) + bias_ref[cls]
        s_cx = _nt_dot(qs, kc)
        o = _softmax_pv([s_nb, s_cx], [vn, vc], None)
        o_ref[r * GRID_W:(r + 1) * GRID_W, :] = jnp.where(lane_lo, o[:GRID_W], o[GRID_W:]).astype(o_ref.dtype)


def _na_bias_table(rpb):
    h = rpb.shape[0]
    n_dcol = 2 * NA_WIN_COLS - 1
    cols = jnp.arange(GRID_W)
    col_start = jnp.clip(cols - NA_WIN_COLS // 2, 0, GRID_W - NA_WIN_COLS)
    inwin = (cols[None, :] >= col_start[:, None]) & (cols[None, :] < col_start[:, None] + NA_WIN_COLS)
    dcol = cols[None, :] - cols[:, None] + NA_WIN_COLS - 1
    pick = (dcol[None] == jnp.arange(n_dcol)[:, None, None]).astype(F32).reshape(n_dcol, -1)
    spread = jnp.dot(rpb.astype(F32).reshape(-1, n_dcol), pick, precision=lax.Precision.HIGHEST)
    spread = spread.reshape(h, 2 * NA_WIN_ROWS - 1, GRID_W, GRID_W)
    spread = jnp.where(inwin[None, None], spread, NEG_INF)
    tab = jnp.stack([spread[:, NA_WIN_ROWS - 1 - cls:2 * NA_WIN_ROWS - 1 - cls]
                     for cls in range(NA_WIN_ROWS)], axis=1)
    tab = tab.transpose(0, 1, 3, 2, 4).reshape(h, NA_WIN_ROWS, GRID_W, NA_WIN_ROWS * GRID_W)
    tab = tab.reshape(h // 2, 2, NA_WIN_ROWS, GRID_W, NA_WIN_ROWS * GRID_W).transpose(0, 2, 1, 3, 4)
    return tab.reshape(h // 2, NA_WIN_ROWS, 2 * GRID_W, NA_WIN_ROWS * GRID_W)


def neighbourhood_attention(qkv, rpb, *, batch, seq_len, ctx_len, d, with_ctx_out):
    n_pairs = d // LANES
    rows_per_blk = NA_WIN_ROWS
    blk = rows_per_blk * GRID_W
    grid_rows = seq_len // GRID_W
    nblk = seq_len // blk
    cb0 = batch * seq_len // ctx_len
    bias = _na_bias_table(rpb)

    def kspec(col0, shift):
        return pl.BlockSpec((blk, LANES),
                            lambda b, p, j: (b * nblk + jnp.clip(j + shift, 0, nblk - 1), col0 + p))

    ctx_k = pl.BlockSpec((ctx_len, LANES), lambda b, p, j: (cb0 + b, n_pairs + p))
    ctx_v = pl.BlockSpec((ctx_len, LANES), lambda b, p, j: (cb0 + b, 2 * n_pairs + p))
    lat = pl.pallas_call(
        functools.partial(_nattn_kernel, grid_rows=grid_rows, rows_per_blk=rows_per_blk),
        grid=(batch, n_pairs, nblk),
        in_specs=[pl.BlockSpec((blk, LANES), lambda b, p, j: (b * nblk + j, p))]
        + [kspec(n_pairs, s) for s in (-1, 0, 1)] + [kspec(2 * n_pairs, s) for s in (-1, 0, 1)]
        + [ctx_k, ctx_v,
           pl.BlockSpec((None, NA_WIN_ROWS, 2 * GRID_W, NA_WIN_ROWS * GRID_W), lambda b, p, j: (p, 0, 0, 0))],
        out_specs=pl.BlockSpec((blk, LANES), lambda b, p, j: (b * nblk + j, p)),
        out_shape=jax.ShapeDtypeStruct((batch * seq_len, d), BF16),
        scratch_shapes=[pltpu.VMEM((3 * blk, LANES), BF16), pltpu.VMEM((3 * blk, LANES), BF16)],
        compiler_params=_cparams(3),
        name="neighbourhood_attn",
    )(qkv, *([qkv] * 8), bias)
    if not with_ctx_out:
        return lat
    ctx = pl.pallas_call(
        _cattn_kernel,
        grid=(batch, n_pairs),
        in_specs=[pl.BlockSpec((ctx_len, LANES), lambda b, p: (cb0 + b, p)),
                  pl.BlockSpec((ctx_len, LANES), lambda b, p: (cb0 + b, n_pairs + p)),
                  pl.BlockSpec((ctx_len, LANES), lambda b, p: (cb0 + b, 2 * n_pairs + p))],
        out_specs=pl.BlockSpec((ctx_len, LANES), lambda b, p: (b, p)),
        out_shape=jax.ShapeDtypeStruct((batch * ctx_len, d), BF16),
        compiler_params=_cparams(2),
        name="ctx_mha",
    )(qkv, qkv, qkv)
    return jnp.concatenate([lat, ctx], axis=0)


def _cattn_kernel(q_ref, k_ref, v_ref, o_ref):
    rows = q_ref.shape[0]
    lane_lo = lax.broadcasted_iota(jnp.int32, (rows, LANES), 1) < HEAD_DIM
    qa, qb = _masked_halves(q_ref[...], lane_lo)
    qs = jnp.concatenate([qa, qb], axis=0)
    o = _softmax_pv([_nt_dot(qs, k_ref[...])], [v_ref[...]], None)
    o_ref[...] = jnp.where(lane_lo, o[:rows], o[rows:]).astype(o_ref.dtype)


def _router_kernel(x_ref, g_ref, sh_ref, sc_ref, wh_ref, wl_ref, b_ref, h_ref, idx_ref, gate_ref, cnt_ref,
                   run_ref):
    @pl.when(pl.program_id(0) == 0)
    def _():
        run_ref[...] = jnp.zeros_like(run_ref)

    h = _norm_mod(x_ref[...], g_ref[...], sh_ref[0], sc_ref[0])
    h_hi = h.astype(BF16)
    h_ref[...] = h_hi
    h_lo = (h - h_hi.astype(F32)).astype(BF16)
    logits = (jnp.dot(h_hi, wh_ref[...], preferred_element_type=F32)
              + jnp.dot(h_hi, wl_ref[...], preferred_element_type=F32)
              + jnp.dot(h_lo, wh_ref[...], preferred_element_type=F32)) + b_ref[...]
    lane = lax.broadcasted_iota(jnp.int32, logits.shape, 1)
    tm = logits.shape[0]
    idx_out = jnp.zeros(logits.shape, jnp.int32)
    val_out = jnp.zeros(logits.shape, F32)
    tri = jnp.where(lax.broadcasted_iota(jnp.int32, (tm, tm), 0) > lax.broadcasted_iota(jnp.int32, (tm, tm), 1),
                    1.0, 0.0).astype(BF16)
    run = run_ref[...]
    top = None
    denom = 0.0
    for k in range(TOP_K):
        m = logits.max(axis=-1, keepdims=True)
        sel = jnp.min(jnp.where(logits == m, lane, LANES), axis=-1, keepdims=True)
        if top is None:
            top = m
        e = jnp.exp(m - top)
        denom = denom + e
        onehot = lane == sel
        before = jnp.dot(tri, jnp.where(onehot, 1.0, 0.0).astype(BF16), preferred_element_type=F32)
        rank = jnp.sum(jnp.where(onehot, before + run, 0.0), axis=-1, keepdims=True).astype(jnp.int32)
        run = run + jnp.sum(jnp.where(onehot, 1.0, 0.0), axis=0, keepdims=True)
        idx_out = jnp.where(lane == k, sel, idx_out)
        idx_out = jnp.where(lane == TOP_K + k, rank, idx_out)
        val_out = jnp.where(lane == k, e, val_out)
        logits = jnp.where(onehot, -jnp.inf, logits)
    run_ref[...] = run
    cnt_ref[...] = run
    idx_ref[...] = idx_out
    gate_ref[...] = val_out / denom


def router(x, g, shift, scale, w_router, b_router, *, seg_rows, rows, tm=ROW_TILE):
    m, d = rows, x.shape[1]
    n_seg = shift.shape[0]
    n_e = w_router.shape[1]
    w_pad = jnp.zeros((d, LANES), F32).at[:, :n_e].set(w_router)
    w_hi = w_pad.astype(BF16)
    w_lo = (w_pad - w_hi.astype(F32)).astype(BF16)
    b_pad = jnp.full((1, LANES), -jnp.inf, F32).at[0, :n_e].set(b_router)
    seg = lambda i: (jnp.minimum(i * tm // seg_rows, n_seg - 1), 0, 0)
    return pl.pallas_call(
        _router_kernel,
        grid=(m // tm,),
        in_specs=[pl.BlockSpec((tm, d), lambda i: (i, 0)),
                  pl.BlockSpec((1, d), lambda i: (0, 0)),
                  pl.BlockSpec((1, 1, d), seg),
                  pl.BlockSpec((1, 1, d), seg),
                  pl.BlockSpec((d, LANES), lambda i: (0, 0)),
                  pl.BlockSpec((d, LANES), lambda i: (0, 0)),
                  pl.BlockSpec((1, LANES), lambda i: (0, 0))],
        out_specs=[pl.BlockSpec((tm, d), lambda i: (i, 0)),
                   pl.BlockSpec((tm, LANES), lambda i: (i, 0)),
                   pl.BlockSpec((tm, LANES), lambda i: (i, 0)),
                   pl.BlockSpec((1, LANES), lambda i: (0, 0))],
        out_shape=[jax.ShapeDtypeStruct((m, d), BF16),
                   jax.ShapeDtypeStruct((m, LANES), jnp.int32),
                   jax.ShapeDtypeStruct((m, LANES), F32),
                   jax.ShapeDtypeStruct((1, LANES), F32)],
        scratch_shapes=[pltpu.VMEM((1, LANES), F32)],
        compiler_params=_cparams(1),
        name="router",
    )(x, g, shift, scale, w_hi, w_lo, b_pad)


def _expert_kernel(blk_e_ref, n_used_ref, x_ref, wgu_ref, bgu_ref, wd_ref, bd_ref, o_ref, wgu_s, wd_s, *, f):
    i = pl.program_id(0)
    used = i < n_used_ref[0]

    @pl.when(used & ((i == 0) | (blk_e_ref[i] != blk_e_ref[jnp.maximum(i - 1, 0)])))
    def _():
        wgu_s[...] = wgu_ref[0].astype(BF16)
        wd_s[...] = wd_ref[0].astype(BF16)

    @pl.when(used)
    def _():
        gu = jnp.dot(x_ref[...], wgu_s[...], preferred_element_type=F32) + bgu_ref[0]
        g = jnp.minimum(gu[:, :f], SWIGLU_LIMIT)
        u = jnp.clip(gu[:, f:], -SWIGLU_LIMIT, SWIGLU_LIMIT)
        act = (u + 1) * (g * jax.nn.sigmoid(SWIGLU_ALPHA * g))
        y = jnp.dot(act.astype(BF16), wd_s[...], preferred_element_type=F32) + bd_ref[0]
        o_ref[...] = y.astype(o_ref.dtype)

    @pl.when(i >= n_used_ref[0])
    def _():
        o_ref[...] = jnp.zeros_like(o_ref)


def expert_mlp(xs, blk_e, n_used, w_gu, b_gu, w_down, b_down, *, layer, bm=MOE_ROWS):
    n_slot, d = xs.shape
    _, n_e, _, f2 = w_gu.shape
    f = f2 // 2
    return pl.pallas_call(
        functools.partial(_expert_kernel, f=f),
        grid_spec=pltpu.PrefetchScalarGridSpec(
            num_scalar_prefetch=2,
            grid=(n_slot // bm,),
            in_specs=[pl.BlockSpec((bm, d), lambda i, be, nu: (i, 0)),
                      pl.BlockSpec((None, 1, d, f2), lambda i, be, nu: (layer, be[i], 0, 0)),
                      pl.BlockSpec((1, 1, f2), lambda i, be, nu: (be[i], 0, 0)),
                      pl.BlockSpec((None, 1, f, d), lambda i, be, nu: (layer, be[i], 0, 0)),
                      pl.BlockSpec((1, 1, d), lambda i, be, nu: (be[i], 0, 0))],
            out_specs=pl.BlockSpec((bm, d), lambda i, be, nu: (i, 0)),
            scratch_shapes=[pltpu.VMEM((d, f2), BF16), pltpu.VMEM((f, d), BF16)]),
        out_shape=jax.ShapeDtypeStruct((n_slot, d), BF16),
        compiler_params=pltpu.CompilerParams(dimension_semantics=("arbitrary",),
                                             vmem_limit_bytes=EXPERT_VMEM_LIMIT),
        name="expert_mlp",
    )(blk_e, n_used, xs, w_gu, b_gu.reshape(n_e, 1, f2), w_down, b_down.reshape(n_e, 1, d))


def _combine_kernel(res_ref, y_ref, w_ref, gate_ref, o_ref):
    w = w_ref[...]
    y = w[:, 0:1] * y_ref[0].astype(F32)
    for k in range(1, TOP_K):
        y = y + w[:, k:k + 1] * y_ref[k].astype(F32)
    o_ref[...] = res_ref[...] + gate_ref[0] * y


def moe_layer(x, g, shift, scale, gate, w_router, b_router, w_gu, b_gu, w_down, b_down, *, layer, seg_rows,
              rows, bm=MOE_ROWS):
    m, d = rows, x.shape[1]
    n_e = w_router.shape[1]
    h, route, gates, counts = router(x, g, shift, scale, w_router, b_router, seg_rows=seg_rows, rows=rows)
    idx = route[:, :TOP_K]
    rank = route[:, TOP_K:2 * TOP_K]
    n_asg = m * TOP_K
    counts = counts[0, :n_e].astype(jnp.int32)
    padded = (counts + bm - 1) // bm * bm
    pad_end = jnp.cumsum(padded)
    pad_start = pad_end - padded
    start_of = jnp.sum(jnp.where(idx[:, :, None] == jnp.arange(n_e, dtype=jnp.int32), pad_start, 0), axis=-1)
    dest = (start_of + rank).reshape(-1)
    n_blk = -(-(n_asg + n_e * (bm - 1)) // bm)
    n_slot = n_blk * bm
    tok = (jnp.arange(n_asg, dtype=jnp.int32) // TOP_K)
    slot_tok = (jnp.arange(n_slot, dtype=jnp.int32) % m).at[dest].set(tok, unique_indices=True)
    blk_start = jnp.arange(n_blk, dtype=jnp.int32) * bm
    blk_e = jnp.minimum(jnp.sum((pad_end[None, :] <= blk_start[:, None]).astype(jnp.int32), axis=1), n_e - 1)
    n_used = (pad_end[-1] // bm).astype(jnp.int32).reshape(1)
    xs = h.at[slot_tok].get(mode="promise_in_bounds")
    y_slot = expert_mlp(xs, blk_e, n_used, w_gu, b_gu, w_down, b_down, layer=layer, bm=bm)
    dest_km = dest.reshape(m, TOP_K).T.reshape(-1)
    y_tok = y_slot.at[dest_km].get(mode="promise_in_bounds").reshape(TOP_K, m, d)
    tm = 256
    n_seg = gate.shape[0]
    seg = lambda i: (jnp.minimum(i * tm // seg_rows, n_seg - 1), 0, 0)
    return pl.pallas_call(
        _combine_kernel,
        grid=(m // tm,),
        in_specs=[pl.BlockSpec((tm, d), lambda i: (i, 0)),
                  pl.BlockSpec((TOP_K, tm, d), lambda i: (0, i, 0)),
                  pl.BlockSpec((tm, LANES), lambda i: (i, 0)),
                  pl.BlockSpec((1, 1, d), seg)],
        out_specs=pl.BlockSpec((tm, d), lambda i: (i, 0)),
        out_shape=jax.ShapeDtypeStruct(x.shape, F32),
        input_output_aliases={0: 0},
        compiler_params=_cparams(1),
        name="moe_combine",
    )(x, y_tok, gates, gate)


LC_R2 = 128


@functools.lru_cache(maxsize=None)
def _dft_small_mats(n):
    nn = 2 * n
    ang = 2.0 * np.pi * ((np.arange(nn)[:, None] * np.arange(nn)[None, :]) % nn) / nn
    fwd = np.concatenate([np.cos(ang), -np.sin(ang)], 0)
    inv = np.concatenate([np.cos(ang[:n]), -np.sin(ang[:n])], 1) / nn
    return jnp.asarray(fwd, BF16), jnp.asarray(inv, BF16)


def _plain_mm_kernel(a_ref, b_ref, o_ref):
    o_ref[...] = jnp.dot(a_ref[...], b_ref[...].astype(BF16), preferred_element_type=F32)


def plain_mm(a, b, tn=512):
    m, k = a.shape
    n = b.shape[1]
    tn = _col_tile(n, tn)
    return pl.pallas_call(
        _plain_mm_kernel,
        grid=(n // tn,),
        in_specs=[pl.BlockSpec((m, k), lambda j: (0, 0)), pl.BlockSpec((k, tn), lambda j: (0, j))],
        out_specs=pl.BlockSpec((m, tn), lambda j: (0, j)),
        out_shape=jax.ShapeDtypeStruct((m, n), F32),
        compiler_params=_cparams(1),
        name="plain_mm",
    )(a, b)


def _short_conv_kernel(u_ref, gate_ref, h_ref, skip_ref, fwd_ref, inv_ref, o_ref):
    u = u_ref[...]
    x = jnp.dot(fwd_ref[...], u.astype(BF16), preferred_element_type=F32)
    h = h_ref[...]
    nn = x.shape[0] // 2
    xr, xi, hr, hi = x[:nn], x[nn:], h[:nn], h[nn:]
    prod = jnp.concatenate([xr * hr - xi * hi, xr * hi + xi * hr], axis=0).astype(BF16)
    y = jnp.dot(inv_ref[...], prod, preferred_element_type=F32)
    o_ref[...] = (gate_ref[...].astype(F32) * (y + skip_ref[...] * u.astype(F32))).astype(o_ref.dtype)


def short_long_conv(u_arr, u_col, gate_arr, gate_col, spec, skip, fwd, inv, *, batch, seq_len, tl=512):
    d = spec.shape[1]
    tl = _col_tile(d, tl)
    uc, gc = u_col // tl, gate_col // tl
    nn2 = spec.shape[0]
    return pl.pallas_call(
        _short_conv_kernel,
        grid=(batch, d // tl),
        in_specs=[pl.BlockSpec((seq_len, tl), lambda b, j: (b, uc + j)),
                  pl.BlockSpec((seq_len, tl), lambda b, j: (b, gc + j)),
                  pl.BlockSpec((nn2, tl), lambda b, j: (0, j)),
                  pl.BlockSpec((1, tl), lambda b, j: (0, j)),
                  pl.BlockSpec((nn2, seq_len), lambda b, j: (0, 0)),
                  pl.BlockSpec((seq_len, nn2), lambda b, j: (0, 0))],
        out_specs=pl.BlockSpec((seq_len, tl), lambda b, j: (b, j)),
        out_shape=jax.ShapeDtypeStruct((batch * seq_len, d), BF16),
        compiler_params=_cparams(2),
        name="short_long_conv",
    )(u_arr, gate_arr, spec, skip, fwd[:, :seq_len], inv)


def _hyena_taps(n, d, f_w1, f_b1, f_w2, f_b2, f_w3, f_b3, f_freq, f_w4):
    hp = lax.Precision.HIGHEST
    lin = jnp.linspace(0.0, 1.0, n, dtype=F32)
    idx = jnp.arange(n, dtype=F32)
    bands = jnp.linspace(1e-4, HY_BANDS - 1, HY_BANDS, dtype=F32)[None, :]
    deltas = jnp.abs(jnp.linspace(math.log(HY_DECAY_TARGET) / HY_FAST_DECAY,
                                  math.log(HY_DECAY_TARGET) / HY_SLOW_DECAY, d, dtype=F32))

    def branch(t, pos, direction):
        t = t[:, None]
        ang = (2.0 * math.pi / n) * pos[:, None]
        emb = jnp.concatenate([t, jnp.cos(bands * ang), -jnp.sin(bands * ang)], axis=-1)
        a = jnp.sin(f_freq * (jnp.dot(emb, f_w1, precision=hp) + f_b1))
        a = jnp.sin(f_freq * (jnp.dot(a, f_w2, precision=hp) + f_b2))
        a = jnp.sin(f_freq * (jnp.dot(a, f_w3, precision=hp) + f_b3))
        decay = jnp.exp(-t * deltas[None, :])
        w4 = f_w4.reshape(f_w4.shape[0], HY_ORDER, 2, d)[:, :, direction]
        return [jnp.dot(a, w4[:, o], precision=hp) * decay for o in range(HY_ORDER)]

    fwd = branch(lin, idx, 0)
    bwd = branch(lin[:0:-1], idx[:0:-1], 1)
    zero = jnp.zeros((1, d), F32)
    return [jnp.concatenate([fwd[o], zero, bwd[o]], axis=0) for o in range(HY_ORDER)]


HC = 128
HC_PAIR = 2
LC_TL = 16


@functools.lru_cache(maxsize=None)
def _hy_mats(n):
    nn = 2 * n
    r1 = nn // LC_R2
    tau = 2.0 * np.pi
    k1 = np.arange(r1)
    t = LC_R2 * np.arange(r1)[None, :] + np.arange(LC_R2)[:, None]
    ang1 = tau * ((k1[None, :, None] * t[:, None, :]) % nn) / nn
    f1 = np.empty((LC_R2, 2 * r1, r1))
    f1[:, 0::2] = np.cos(ang1)
    f1[:, 1::2] = -np.sin(ang1)
    ang2 = tau * ((np.arange(LC_R2)[:, None] * np.arange(LC_R2)[None, :]) % LC_R2) / LC_R2
    c, s = np.cos(ang2), np.sin(ang2)
    f2 = np.empty((2 * LC_R2, 2 * LC_R2))
    f2[:LC_R2, 0::2], f2[:LC_R2, 1::2] = c, s
    f2[LC_R2:, 0::2], f2[LC_R2:, 1::2] = -s, c
    f3 = np.empty((2 * LC_R2, 2 * LC_R2))
    f3[0::2, :LC_R2], f3[0::2, LC_R2:] = c, -s
    f3[1::2, :LC_R2], f3[1::2, LC_R2:] = s, c
    m = LC_R2 * np.arange(r1 // 2)[None, :] + np.arange(LC_R2)[:, None]
    ang4 = tau * ((m[:, :, None] * k1[None, None, :]) % nn) / nn
    f4 = np.empty((LC_R2, r1 // 2, 2 * r1))
    f4[:, :, 0::2] = np.cos(ang4) / nn
    f4[:, :, 1::2] = -np.sin(ang4) / nn
    return tuple(jnp.asarray(a, BF16) for a in (f1, f2, f3, f4))


def _hy1_kernel(x_ref, f_ref, o_ref, *, tl_n, nh):
    c = pl.program_id(2)
    for tl in range(tl_n):
        xs = jnp.concatenate([x_ref[h, pl.ds(c * tl_n + tl, nh, stride=LC_R2), :] for h in range(HC_PAIR)],
                             axis=1).astype(BF16)
        a = jnp.dot(f_ref[tl], xs, preferred_element_type=F32)
        packed = pltpu.bitcast(a.astype(BF16), jnp.uint32)
        for h in range(HC_PAIR):
            for j in range(packed.shape[0] // 8):
                o_ref[h, j, tl * 8:(tl + 1) * 8, :] = packed[j * 8:(j + 1) * 8, h * HC:(h + 1) * HC]


def hy_stage1(x, f1, *, blk_off, n_blk, batch, rows_hi):
    r2 = f1.shape[1]
    rows = rows_hi * LC_R2
    tl_n = LC_TL
    return pl.pallas_call(
        functools.partial(_hy1_kernel, tl_n=tl_n, nh=rows_hi),
        grid=(n_blk // HC_PAIR, batch, LC_R2 // tl_n),
        in_specs=[pl.BlockSpec((HC_PAIR, rows, HC), lambda p, b, c: (blk_off // HC_PAIR + p, b, 0)),
                  pl.BlockSpec((tl_n, r2, rows_hi), lambda p, b, c: (c, 0, 0))],
        out_specs=pl.BlockSpec((HC_PAIR, None, r2 // 16, tl_n * 8, HC), lambda p, b, c: (p, b, 0, c, 0)),
        out_shape=jax.ShapeDtypeStruct((n_blk, batch, r2 // 16, LC_R2 * 8, HC), jnp.uint32),
        compiler_params=_cparams(3),
        name="hy_stage1",
    )(x, f1[:, :, :rows_hi])


def _hy2_kernel(a_ref, f2_ref, *rest, spectrum_only):
    if spectrum_only:
        (o_ref,) = rest
    else:
        h_ref, f3_ref, o_ref = rest
    for kk in range(8):
        w = jnp.concatenate([a_ref[h, pl.ds(kk, LC_R2, stride=8), :] for h in range(HC_PAIR)], axis=1)
        a_in = pltpu.bitcast(w, BF16)
        x = jnp.dot(f2_ref[...], a_in, preferred_element_type=F32)
        if spectrum_only:
            o_ref[kk] = x.astype(o_ref.dtype)
            continue
        hs = h_ref[kk].astype(F32)
        xr, xi = x[:LC_R2], x[LC_R2:]
        hr, hi = hs[:LC_R2], hs[LC_R2:]
        prod = jnp.concatenate([xr * hr - xi * hi, xr * hi + xi * hr], axis=0).astype(BF16)
        g = jnp.dot(f3_ref[...], prod, preferred_element_type=F32)
        packed = pltpu.bitcast(g.astype(BF16), jnp.uint32)
        for h in range(HC_PAIR):
            o_ref[h, pl.ds(kk, LC_R2, stride=8), :] = packed[:, h * HC:(h + 1) * HC]


def hy_spectrum(a, f2):
    cb, _, nj, rows, _ = a.shape
    mat = pl.BlockSpec((2 * LC_R2, 2 * LC_R2), lambda j, p: (0, 0))
    return pl.pallas_call(
        functools.partial(_hy2_kernel, spectrum_only=True),
        grid=(nj, cb // HC_PAIR),
        in_specs=[pl.BlockSpec((HC_PAIR, None, None, rows, HC), lambda j, p: (p, 0, j, 0, 0)), mat],
        out_specs=pl.BlockSpec((None, 8, 2 * LC_R2, HC_PAIR * HC), lambda j, p: (p, j, 0, 0)),
        out_shape=jax.ShapeDtypeStruct((cb // HC_PAIR, nj * 8, 2 * LC_R2, HC_PAIR * HC), BF16),
        compiler_params=_cparams(2),
        name="hy_spectrum",
    )(a, f2)


def hy_stage23(a, spec, f2, f3):
    cb, bsz, nj, rows, _ = a.shape
    mat = pl.BlockSpec((2 * LC_R2, 2 * LC_R2), lambda j, p, b: (0, 0))
    blk = pl.BlockSpec((HC_PAIR, None, None, rows, HC), lambda j, p, b: (p, b, j, 0, 0))
    return pl.pallas_call(
        functools.partial(_hy2_kernel, spectrum_only=False),
        grid=(nj, cb // HC_PAIR, bsz),
        in_specs=[blk, mat,
                  pl.BlockSpec((None, 8, 2 * LC_R2, HC_PAIR * HC), lambda j, p, b: (p, j, 0, 0)), mat],
        out_specs=blk,
        out_shape=jax.ShapeDtypeStruct(a.shape, jnp.uint32),
        compiler_params=_cparams(3),
        name="hy_stage23",
    )(a, f2, spec, f3)


def _hy4_kernel(g_ref, f_ref, o_ref, *, tl_n, nh):
    c = pl.program_id(2)
    for tl in range(tl_n):
        w = jnp.concatenate([g_ref[h, :, tl * 8:(tl + 1) * 8, :].reshape(-1, HC) for h in range(HC_PAIR)], axis=1)
        g_in = pltpu.bitcast(w, BF16)
        y = jnp.dot(f_ref[tl], g_in, preferred_element_type=F32)
        for h in range(HC_PAIR):
            o_ref[h, pl.ds(c * tl_n + tl, nh, stride=LC_R2), :] = y[:, h * HC:(h + 1) * HC]


def hy_stage4(g, f4, *, seq_len):
    cb, bsz, nj, _, _ = g.shape
    nh = seq_len // LC_R2
    tl_n = LC_TL
    return pl.pallas_call(
        functools.partial(_hy4_kernel, tl_n=tl_n, nh=nh),
        grid=(cb // HC_PAIR, bsz, LC_R2 // tl_n),
        in_specs=[pl.BlockSpec((HC_PAIR, None, nj, tl_n * 8, HC), lambda p, b, c: (p, b, 0, c, 0)),
                  pl.BlockSpec((tl_n, nh, 16 * nj), lambda p, b, c: (c, 0, 0))],
        out_specs=pl.BlockSpec((HC_PAIR, seq_len, HC), lambda p, b, c: (p, b, 0)),
        out_shape=jax.ShapeDtypeStruct((cb, bsz * seq_len, HC), F32),
        compiler_params=_cparams(3),
        name="hy_stage4",
    )(g, f4)


def _hy_gate_kernel(y_ref, u_ref, gate_ref, skip_ref, o_ref):
    o_ref[...] = (gate_ref[...] * (y_ref[...] + skip_ref[...] * u_ref[...])).astype(o_ref.dtype)


def hy_gate(y, u_arr, u_off, gate_arr, gate_off, skip, *, natural, tr=4096):
    cb, rows, _ = y.shape
    tr = min(tr, rows)
    if natural:
        out_spec = pl.BlockSpec((tr, HC), lambda i, r: (r, i))
        out_shape = jax.ShapeDtypeStruct((rows, cb * HC), BF16)
    else:
        out_spec = pl.BlockSpec((None, tr, HC), lambda i, r: (i, r, 0))
        out_shape = jax.ShapeDtypeStruct((cb, rows, HC), F32)
    return pl.pallas_call(
        _hy_gate_kernel,
        grid=(cb, rows // tr),
        in_specs=[pl.BlockSpec((None, tr, HC), lambda i, r: (i, r, 0)),
                  pl.BlockSpec((None, tr, HC), lambda i, r: (u_off + i, r, 0)),
                  pl.BlockSpec((None, tr, HC), lambda i, r: (gate_off + i, r, 0)),
                  pl.BlockSpec((None, 1, HC), lambda i, r: (i, 0, 0))],
        out_specs=out_spec,
        out_shape=out_shape,
        compiler_params=_cparams(2),
        name="hy_gate",
    )(y, u_arr, gate_arr, skip.reshape(cb, 1, HC))


def _hy_taps_kernel(a_ref, t_ref, w_ref, dl_ref, o_ref):
    a = a_ref[...]
    a_hi = a.astype(BF16)
    a_lo = (a - a_hi.astype(F32)).astype(BF16)
    w = w_ref[...]
    w_hi = w.astype(BF16)
    w_lo = (w - w_hi.astype(F32)).astype(BF16)
    h = (jnp.dot(a_hi, w_hi, preferred_element_type=F32) + jnp.dot(a_hi, w_lo, preferred_element_type=F32)
         + jnp.dot(a_lo, w_hi, preferred_element_type=F32))
    o_ref[...] = h * jnp.exp(-t_ref[...] * dl_ref[...])


def hy_taps(a_all, t_all, w4, deltas, *, seq_len, order, tr=2048):
    nrow, width = a_all.shape
    d = deltas.shape[0]
    tr = min(tr, seq_len)
    per_dir = seq_len // tr
    return pl.pallas_call(
        _hy_taps_kernel,
        grid=(d // HC, nrow // tr),
        in_specs=[pl.BlockSpec((tr, width), lambda i, r: (r, 0)),
                  pl.BlockSpec((tr, 1), lambda i, r: (r, 0)),
                  pl.BlockSpec((None, None, width, HC), lambda i, r: (order, r // per_dir, 0, i)),
                  pl.BlockSpec((1, HC), lambda i, r: (0, i))],
        out_specs=pl.BlockSpec((None, tr, HC), lambda i, r: (i, r, 0)),
        out_shape=jax.ShapeDtypeStruct((d // HC, nrow, HC), F32),
        compiler_params=_cparams(2),
        name="hy_taps",
    )(a_all, t_all, w4.reshape(width, HY_ORDER, 2, d).transpose(1, 2, 0, 3), deltas.reshape(1, d))


def _hyena_filter_hidden(n, f_w1, f_b1, f_w2, f_b2, f_w3, f_b3, f_freq):
    hp = lax.Precision.HIGHEST
    lin = jnp.linspace(0.0, 1.0, n, dtype=F32)
    idx = jnp.arange(n, dtype=F32)
    bands = jnp.linspace(1e-4, HY_BANDS - 1, HY_BANDS, dtype=F32)[None, :]
    t = jnp.concatenate([lin, jnp.zeros((1,), F32), lin[:0:-1]])[:, None]
    pos = jnp.concatenate([idx, jnp.zeros((1,), F32), idx[:0:-1]])[:, None]
    ang = (2.0 * math.pi / n) * pos
    emb = jnp.concatenate([t, jnp.cos(bands * ang), -jnp.sin(bands * ang)], axis=-1)
    a = jnp.sin(f_freq * (jnp.dot(emb, f_w1, precision=hp) + f_b1))
    a = jnp.sin(f_freq * (jnp.dot(a, f_w2, precision=hp) + f_b2))
    a = jnp.sin(f_freq * (jnp.dot(a, f_w3, precision=hp) + f_b3))
    keep = (jnp.arange(2 * n) != n)[:, None]
    return jnp.where(keep, a, 0.0), t


def _hyena_deltas(d):
    return jnp.abs(jnp.linspace(math.log(HY_DECAY_TARGET) / HY_FAST_DECAY,
                                math.log(HY_DECAY_TARGET) / HY_SLOW_DECAY, d, dtype=F32))


def hyena_long_convs2(zc, a_all, t_all, w4, skip, *, batch, seq_len, d):
    cb = d // HC
    f1, f2, f3, f4 = _hy_mats(seq_len)
    nh = seq_len // LC_R2
    deltas = _hyena_deltas(d)
    u, u_off = zc, 2 * cb
    for o in range(HY_ORDER):
        taps = hy_taps(a_all, t_all, w4, deltas, seq_len=seq_len, order=o)
        spec = hy_spectrum(hy_stage1(taps, f1, blk_off=0, n_blk=cb, batch=1, rows_hi=2 * nh), f2)
        a = hy_stage1(u, f1, blk_off=u_off, n_blk=cb, batch=batch, rows_hi=nh)
        y = hy_stage4(hy_stage23(a, spec, f2, f3), f4, seq_len=seq_len)
        u = hy_gate(y, u, u_off, zc, o * cb, skip[o], natural=(o == HY_ORDER - 1))
        u_off = 0
    return u


def _final_norm_kernel(x_ref, g_ref, o_ref):
    x = x_ref[...]
    o_ref[...] = x * lax.rsqrt(jnp.mean(x * x, axis=-1, keepdims=True) + NORM_EPS) * g_ref[...]


def final_norm(x, g, *, rows, tm=ROW_TILE):
    d = x.shape[1]
    return pl.pallas_call(
        _final_norm_kernel,
        grid=(rows // tm,),
        in_specs=[pl.BlockSpec((tm, d), lambda i: (i, 0)), pl.BlockSpec((1, d), lambda i: (0, 0))],
        out_specs=pl.BlockSpec((tm, d), lambda i: (i, 0)),
        out_shape=jax.ShapeDtypeStruct((rows, d), F32),
        compiler_params=_cparams(1),
        name="final_norm",
    )(x, g)


def _rope_tables(n_lat, seq_len, n_rows):
    t = jnp.arange(n_rows, dtype=jnp.int32)
    row = ((t % seq_len) // GRID_W).astype(F32)
    col = (t % GRID_W).astype(F32)
    lane = jnp.arange(LANES)
    dd = lane % HEAD_DIM
    quarter = HEAD_DIM // 4
    inv_freq = ROPE_BASE ** (-(dd % quarter).astype(F32) / quarter)
    pos = jnp.where(dd[None, :] < HEAD_DIM // 2, row[:, None], col[:, None])
    ang = pos * inv_freq[None, :]
    sign = jnp.where((dd % (HEAD_DIM // 2)) < quarter, -1.0, 1.0).astype(F32)
    lat = (t < n_lat)[:, None]
    cos = jnp.where(lat, jnp.cos(ang), 1.0)
    sin = jnp.where(lat, jnp.sin(ang) * sign[None, :], 0.0)
    return cos, sin


def _wa_head_perm(n_heads):
    order = []
    for p in range(n_heads // (2 * WA_GROUP)):
        for g in range(WA_GROUP):
            order += [2 * WA_GROUP * p + g, 2 * WA_GROUP * p + WA_GROUP + g]
    cols = jnp.asarray(order, jnp.int32)[:, None] * HEAD_DIM + jnp.arange(HEAD_DIM, dtype=jnp.int32)[None, :]
    return cols.reshape(-1)


def kernel(x, c, ctx, c_ctx, w_ada, b_ada, g_mix, g_ffn, hy_w_in, hy_b_in, hy_w_short, hy_b_short, hy_f_w1, hy_f_b1, hy_f_w2, hy_f_b2, hy_f_w3, hy_f_b3, hy_f_freq, hy_f_w4, hy_skip, hy_w_out, hy_b_out, cf_w_pw1, cf_b_pw1, cf_w_dw, cf_b_dw, cf_ln_g, cf_ln_b, cf_w_pw2, cf_b_pw2, wa_w_qkv, wa_w_o, wa_sinks, na_w_qkv, na_w_o, na_rpb, moe_w_router, moe_b_router, moe_w_gu, moe_b_gu, moe_w_down, moe_b_down, g_final):
    bsz, n, d = x.shape
    n_ctx = ctx.shape[1]
    depth = w_ada.shape[0]
    n_lat = bsz * n
    m_all = n_lat + bsz * n_ctx

    cond = jnp.zeros((8, d), F32).at[:bsz].set(jax.nn.silu(c)).at[bsz].set(jax.nn.silu(c_ctx))
    mods = adaln(cond, w_ada, b_ada)
    xu = jnp.concatenate([x.reshape(n_lat, d), ctx.reshape(bsz * n_ctx, d)], axis=0)
    zeros_d = jnp.zeros((1, d), F32)

    for i in range(depth):
        kind, j = i % N_MIXERS, i // N_MIXERS
        last = i == depth - 1
        mod = [mods[i, :bsz + 1, k * d:(k + 1) * d].reshape(bsz + 1, 1, d) for k in range(6)]
        m_out = n_lat if last else m_all
        gm = g_mix[i].reshape(1, d)
        need_ctx_in = (not last) or kind >= 2
        m_in = m_all if need_ctx_in else n_lat
        x_in = xu[:m_in]

        if kind == 0:
            z = nm_matmul(x_in, gm, mod[0], mod[1], hy_w_in[j].astype(BF16), hy_b_in[j].reshape(1, -1),
                          seg_rows=n)
            filt_w = (hy_f_w1[j], hy_f_b1[j], hy_f_w2[j], hy_f_b2[j], hy_f_w3[j], hy_f_b3[j], hy_f_freq[j],
                      hy_f_w4[j])
            w_s, b_s = hy_w_short[j], hy_b_short[j].reshape(1, -1)
            w_o, b_o = hy_w_out[j].astype(BF16), hy_b_out[j].reshape(1, d)
            zc = dwconv(z, w_s, b_s, start=0, total=n_lat, seq_len=n, tc=HC, blocked=True, rows=2048,
                        out_dtype=F32)
            a_all, t_all = _hyena_filter_hidden(n, *filt_w[:-1])
            y = hyena_long_convs2(zc, a_all, t_all, hy_f_w4[j], hy_skip[j], batch=bsz, seq_len=n, d=d)
            xu = mm_res(y, w_o, b_o, xu, mod[2], seg_rows=n)
            if m_in > n_lat:
                zcc = dwconv(z, w_s, b_s, start=n_lat, total=m_in - n_lat, seq_len=n_ctx, tc=512)
                taps_c = _hyena_taps(n_ctx, d, *filt_w)
                fwd, inv = _dft_small_mats(n_ctx)
                yc = short_long_conv(zcc, 2 * d, zcc, 0, plain_mm(fwd, taps_c[0]), hy_skip[j][0].reshape(1, d),
                                     fwd, inv, batch=bsz, seq_len=n_ctx)
                yc = short_long_conv(yc, 0, zcc, d, plain_mm(fwd, taps_c[1]), hy_skip[j][1].reshape(1, d),
                                     fwd, inv, batch=bsz, seq_len=n_ctx)
                if not last:
                    xu = mm_res(yc, w_o, b_o, xu, mod[2], seg_rows=n, row_off=n_lat)
            y = None
        elif kind == 1:
            w1 = cf_w_pw1[j].astype(BF16).reshape(d, 2, d).transpose(1, 0, 2)
            a = nm_matmul(x_in, gm, mod[0], mod[1], w1, cf_b_pw1[j].reshape(2, 1, d), seg_rows=n, mode="glu")
            conv = functools.partial(dwconv, a, cf_w_dw[j], cf_b_dw[j].reshape(1, d), post="ln_silu",
                                     ln=(cf_ln_g[j].reshape(1, d), cf_ln_b[j].reshape(1, d)), rows=64)
            y = conv(start=0, total=n_lat, seq_len=n, same_rows=True)
            if m_in > n_lat:
                y = conv(start=n_lat, total=m_in - n_lat, seq_len=n_ctx, same_rows=True, into=y)
            w_o, b_o = cf_w_pw2[j].astype(BF16), cf_b_pw2[j].reshape(1, d)
        elif kind == 2:
            perm = _wa_head_perm(d // HEAD_DIM)
            scale = HEAD_DIM ** -0.5
            w_qkv = jnp.concatenate([wa_w_qkv[j][:, :d][:, perm] * scale, wa_w_qkv[j][:, d:]], axis=1).astype(BF16)
            n_out = w_qkv.shape[1]
            qkv = nm_matmul(x_in, gm, mod[0], mod[1], w_qkv, jnp.zeros((1, n_out), F32), seg_rows=n,
                            mode="rope", rope=_rope_tables(n_lat, n, m_in), n_rope_cols=d + (n_out - d) // 2)
            y = window_attention(qkv, wa_sinks[j], batch=bsz, seq_len=n, ctx_len=n_ctx, d=d)
            w_o, b_o = wa_w_o[j][perm].astype(BF16), zeros_d
        else:
            scale = HEAD_DIM ** -0.5
            w_qkv = jnp.concatenate([na_w_qkv[j][:, :d] * scale, na_w_qkv[j][:, d:]], axis=1).astype(BF16)
            qkv = nm_matmul(x_in, gm, mod[0], mod[1], w_qkv, jnp.zeros((1, 3 * d), F32), seg_rows=n)
            y = neighbourhood_attention(qkv, na_rpb[j], batch=bsz, seq_len=n, ctx_len=n_ctx, d=d,
                                        with_ctx_out=not last)
            w_o, b_o = na_w_o[j].astype(BF16), zeros_d
        if y is not None:
            xu = mm_res(y, w_o, b_o, xu, mod[2], seg_rows=n, rows=m_out)

        xu = moe_layer(xu, g_ffn[i].reshape(1, d), mod[3], mod[4], mod[5], moe_w_router[i], moe_b_router[i],
                       moe_w_gu, moe_b_gu[i], moe_w_down, moe_b_down[i], layer=i, seg_rows=n, rows=m_out)
    return final_norm(xu, g_final.reshape(1, d), rows=n_lat).reshape(bsz, n, d)
```

```python
import functools
import math

import jax
import jax.numpy as jnp
import numpy as np
from jax import lax
from jax.experimental import pallas as pl
from jax.experimental.pallas import tpu as pltpu

F32 = jnp.float32
BF16 = jnp.bfloat16

GRID_W = 64
N_MIXERS = 4
NORM_EPS = 1e-6
NEG_INF = -1e30
HEAD_DIM = 64
ROPE_BASE = 10000.0

HY_ORDER = 2
HY_BANDS = 16
HY_DECAY_TARGET = 1e-2
HY_FAST_DECAY = 0.3
HY_SLOW_DECAY = 1.5

WA_GROUP = 4
WA_WINDOW = 128
NA_WIN_ROWS = 8
NA_WIN_COLS = 16

TOP_K = 4
SWIGLU_LIMIT = 7.0
SWIGLU_ALPHA = 1.702

LANES = 128
ROW_TILE = 512
MOE_ROWS = 512
VMEM_LIMIT = 56 * 1024 * 1024
EXPERT_VMEM_LIMIT = 58 * 1024 * 1024


def _cparams(n_axes):
    return pltpu.CompilerParams(dimension_semantics=("arbitrary",) * n_axes,
                                vmem_limit_bytes=VMEM_LIMIT)


def _col_tile(n, pref=1024):
    t = min(pref, n)
    while n % t:
        t //= 2
    return t


def _adaln_kernel(c_ref, w_ref, b_ref, o_ref):
    w = w_ref[0]
    w_hi = w.astype(BF16)
    w_lo = (w - w_hi.astype(F32)).astype(BF16)
    c = c_ref[...]
    c_hi = c.astype(BF16)
    c_lo = (c - c_hi.astype(F32)).astype(BF16)
    acc = jnp.dot(c_hi, w_hi, preferred_element_type=F32)
    acc += jnp.dot(c_hi, w_lo, preferred_element_type=F32)
    acc += jnp.dot(c_lo, w_hi, preferred_element_type=F32)
    o_ref[0] = acc + b_ref[0]


def adaln(cond, w_ada, b_ada):
    depth, d, n6 = w_ada.shape
    tn = _col_tile(n6, 1024)
    return pl.pallas_call(
        _adaln_kernel,
        grid=(depth, n6 // tn),
        in_specs=[pl.BlockSpec((8, d), lambda l, j: (0, 0)),
                  pl.BlockSpec((1, d, tn), lambda l, j: (l, 0, j)),
                  pl.BlockSpec((1, 1, tn), lambda l, j: (l, 0, j))],
        out_specs=pl.BlockSpec((1, 8, tn), lambda l, j: (l, 0, j)),
        out_shape=jax.ShapeDtypeStruct((depth, 8, n6), F32),
        compiler_params=_cparams(2),
        name="adaln",
    )(cond, w_ada, b_ada.reshape(depth, 1, n6))


def _norm_mod(x, g, shift, scale):
    y = x * lax.rsqrt(jnp.mean(x * x, axis=-1, keepdims=True) + NORM_EPS)
    return (y * g) * (1 + scale) + shift


def _rope_tile(x, cos, sin):
    lane = lax.broadcasted_iota(jnp.int32, x.shape, 1)
    nxt = pltpu.roll(x, LANES - 16, axis=1)
    prv = pltpu.roll(x, 16, axis=1)
    partner = jnp.where((lane // 16) % 2 == 0, nxt, prv)
    return x * cos + partner * sin


def _nm_mm_kernel(x_ref, g_ref, sh_ref, sc_ref, w_ref, b_ref, *rest, mode, n_rope):
    if mode == "rope":
        cos_ref, sin_ref, o_ref, h_ref = rest
    else:
        o_ref, h_ref = rest
    j = pl.program_id(1)

    @pl.when(j == 0)
    def _():
        h_ref[...] = _norm_mod(x_ref[...], g_ref[...], sh_ref[0], sc_ref[0]).astype(BF16)

    h = h_ref[...]
    if mode == "glu":
        a = jnp.dot(h, w_ref[0], preferred_element_type=F32) + b_ref[0]
        gate = jnp.dot(h, w_ref[1], preferred_element_type=F32) + b_ref[1]
        o_ref[...] = (a * jax.nn.sigmoid(gate)).astype(o_ref.dtype)
        return
    acc = jnp.dot(h, w_ref[...], preferred_element_type=F32) + b_ref[...]
    if mode == "rope":
        @pl.when(j < n_rope)
        def _():
            cos = cos_ref[...]
            sin = sin_ref[...]
            for g in range(acc.shape[1] // LANES):
                sl = slice(g * LANES, (g + 1) * LANES)
                o_ref[:, sl] = _rope_tile(acc[:, sl], cos, sin).astype(o_ref.dtype)

        @pl.when(j >= n_rope)
        def _():
            o_ref[...] = acc.astype(o_ref.dtype)
    else:
        o_ref[...] = acc.astype(o_ref.dtype)


def nm_matmul(x, g, shift, scale, w, b, *, seg_rows, mode="plain", rope=None, n_rope_cols=0,
              tm=ROW_TILE, tn=None):
    m, d = x.shape
    n_seg = shift.shape[0]
    n = w.shape[-1]
    tn = tn or (_col_tile(math.gcd(n, n_rope_cols), 512) if mode == "rope" else _col_tile(n, 1024))
    seg = lambda i, j: (jnp.minimum(i * tm // seg_rows, n_seg - 1), 0, 0)
    in_specs = [pl.BlockSpec((tm, d), lambda i, j: (i, 0)),
                pl.BlockSpec((1, d), lambda i, j: (0, 0)),
                pl.BlockSpec((1, 1, d), seg),
                pl.BlockSpec((1, 1, d), seg)]
    if mode == "glu":
        in_specs += [pl.BlockSpec((2, d, tn), lambda i, j: (0, 0, j)),
                     pl.BlockSpec((2, 1, tn), lambda i, j: (0, 0, j))]
    else:
        in_specs += [pl.BlockSpec((d, tn), lambda i, j: (0, j)),
                     pl.BlockSpec((1, tn), lambda i, j: (0, j))]
    args = [x, g, shift, scale, w, b]
    if mode == "rope":
        in_specs += [pl.BlockSpec((tm, LANES), lambda i, j: (i, 0))] * 2
        args += list(rope)
    return pl.pallas_call(
        functools.partial(_nm_mm_kernel, mode=mode, n_rope=n_rope_cols // tn),
        grid=(m // tm, n // tn),
        in_specs=in_specs,
        out_specs=pl.BlockSpec((tm, tn), lambda i, j: (i, j)),
        out_shape=jax.ShapeDtypeStruct((m, n), BF16),
        scratch_shapes=[pltpu.VMEM((tm, d), BF16)],
        compiler_params=_cparams(2),
        name="nm_matmul_" + mode,
    )(*args)


def _mm_res_kernel(a_ref, w_ref, b_ref, res_ref, gate_ref, o_ref):
    if len(a_ref.shape) == 3:
        kc = a_ref.shape[2]
        acc = b_ref[...]
        for cb in range(a_ref.shape[0]):
            acc = acc + jnp.dot(a_ref[cb], w_ref[cb * kc:(cb + 1) * kc, :], preferred_element_type=F32)
    else:
        acc = jnp.dot(a_ref[...], w_ref[...], preferred_element_type=F32) + b_ref[...]
    o_ref[...] = res_ref[...] + gate_ref[0] * acc


def mm_res(a, w, b, res, gate, *, seg_rows, row_off=0, rows=None, tm=ROW_TILE, tn=None):
    if a.ndim == 3:
        m = rows or a.shape[1]
        k = a.shape[0] * a.shape[2]
        a_spec = pl.BlockSpec((a.shape[0], tm, a.shape[2]), lambda i, j: (0, i, 0))
    else:
        m, k = rows or a.shape[0], a.shape[1]
        a_spec = pl.BlockSpec((tm, k), lambda i, j: (i, 0))
    n = w.shape[1]
    n_seg = gate.shape[0]
    tn = tn or _col_tile(n, 1024)
    blk_off = row_off // tm
    seg = lambda i, j: (jnp.minimum((blk_off + i) * tm // seg_rows, n_seg - 1), 0, j)
    return pl.pallas_call(
        _mm_res_kernel,
        grid=(m // tm, n // tn),
        in_specs=[a_spec,
                  pl.BlockSpec((k, tn), lambda i, j: (0, j)),
                  pl.BlockSpec((1, tn), lambda i, j: (0, j)),
                  pl.BlockSpec((tm, tn), lambda i, j: (blk_off + i, j)),
                  pl.BlockSpec((1, 1, tn), seg)],
        out_specs=pl.BlockSpec((tm, tn), lambda i, j: (blk_off + i, j)),
        out_shape=jax.ShapeDtypeStruct(res.shape, F32),
        input_output_aliases={3: 0},
        compiler_params=_cparams(2),
        name="mm_res",
    )(a, w, b, res, gate)


CONV_HALO = 16


def _dwconv_kernel(prev_ref, x_ref, next_ref, w_ref, b_ref, *rest, taps, blocks_per_seq, post, sub, n_alias):
    if n_alias:
        n_tail = 3 if post == "ln_silu" else 2
        rest = rest[:-n_tail - 1] + rest[-n_tail:]
    if post == "ln_silu":
        g_ref, beta_ref, o_ref, win_ref, conv_ref = rest
    else:
        o_ref, win_ref = rest
    i = pl.program_id(0)
    pos = i % blocks_per_seq
    rows, c = x_ref.shape
    half = taps // 2
    zero = jnp.zeros((CONV_HALO, c), F32)
    win_ref[CONV_HALO:CONV_HALO + rows, :] = x_ref[...].astype(F32)

    @pl.when(pos == 0)
    def _():
        win_ref[0:CONV_HALO, :] = zero

    @pl.when(pos > 0)
    def _():
        win_ref[0:CONV_HALO, :] = prev_ref[...].astype(F32)

    @pl.when(pos == blocks_per_seq - 1)
    def _():
        win_ref[CONV_HALO + rows:, :] = zero

    @pl.when(pos < blocks_per_seq - 1)
    def _():
        win_ref[CONV_HALO + rows:, :] = next_ref[...].astype(F32)

    bias = b_ref[...]
    if post == "ln_silu":
        ext = rows + sub
        groups = {}
        for t in range(taps):
            off = CONV_HALO - half + t
            groups.setdefault(off % sub, []).append((t, off // sub))
        lw = min(2 * LANES, c)
        for lc in range(0, c, lw):
            lanes = slice(lc, lc + lw)
            y = jnp.broadcast_to(bias[:, lanes], (rows, lw))
            for s, members in sorted(groups.items()):
                acc = None
                for t, a in members:
                    term = pltpu.repeat(w_ref[t, :, lanes], ext // sub, axis=0) * win_ref[sub * a:sub * a + ext, lanes]
                    acc = term if acc is None else acc + term
                y = y + acc[s:s + rows]
            conv_ref[:, lanes] = y
        for s in range(rows // sub):
            acc = conv_ref[s * sub:(s + 1) * sub, :]
            mu = jnp.mean(acc, axis=-1, keepdims=True)
            xc = acc - mu
            var = jnp.mean(xc * xc, axis=-1, keepdims=True)
            y = xc * lax.rsqrt(var + NORM_EPS) * g_ref[...] + beta_ref[...]
            o_ref[s * sub:(s + 1) * sub, :] = (y * jax.nn.sigmoid(y)).astype(o_ref.dtype)
        return
    for s in range(rows // sub):
        base = CONV_HALO + s * sub - half
        acc = jnp.broadcast_to(bias, (sub, c))
        for t in range(taps):
            acc = acc + w_ref[t] * win_ref[base + t:base + t + sub, :]
        o_ref[s * sub:(s + 1) * sub, :] = acc.astype(o_ref.dtype)


def dwconv(x, w, b, *, start, total, seq_len, post=None, ln=None, rows=256, tc=None, out_dtype=None,
           blocked=False, same_rows=False, into=None):
    m, c = x.shape
    taps = w.shape[0]
    tc = tc or c
    out_dtype = out_dtype or BF16
    sub = 8
    w = jnp.broadcast_to(w[:, None, :], (taps, sub, c))
    r = min(rows, seq_len)
    bps = seq_len // r
    hb = r // CONV_HALO
    off = start // r
    offh = start // CONV_HALO
    nh = m // CONV_HALO
    in_specs = [pl.BlockSpec((CONV_HALO, tc), lambda i, j: (jnp.maximum(offh + i * hb - 1, 0), j)),
                pl.BlockSpec((r, tc), lambda i, j: (off + i, j)),
                pl.BlockSpec((CONV_HALO, tc), lambda i, j: (jnp.minimum(offh + (i + 1) * hb, nh - 1), j)),
                pl.BlockSpec((taps, sub, tc), lambda i, j: (0, 0, j)),
                pl.BlockSpec((1, tc), lambda i, j: (0, j))]
    args = [x, x, x, w, b]
    if post == "ln_silu":
        in_specs += [pl.BlockSpec((1, tc), lambda i, j: (0, j))] * 2
        args += list(ln)
    aliases = {}
    if blocked:
        out_spec = pl.BlockSpec((None, r, tc), lambda i, j: (j, i, 0))
        out_shape = jax.ShapeDtypeStruct((c // tc, total, tc), out_dtype)
    elif same_rows:
        out_spec = pl.BlockSpec((r, tc), lambda i, j: (off + i, j))
        out_shape = jax.ShapeDtypeStruct((m, c), out_dtype)
        if into is not None:
            in_specs.append(pl.BlockSpec(memory_space=pl.ANY))
            args.append(into)
            aliases = {len(args) - 1: 0}
    else:
        out_spec = pl.BlockSpec((r, tc), lambda i, j: (i, j))
        out_shape = jax.ShapeDtypeStruct((total, c), out_dtype)
    return pl.pallas_call(
        functools.partial(_dwconv_kernel, taps=taps, blocks_per_seq=bps, post=post, sub=sub,
                          n_alias=len(aliases)),
        grid=(total // r, c // tc),
        in_specs=in_specs,
        out_specs=out_spec,
        out_shape=out_shape,
        input_output_aliases=aliases,
        scratch_shapes=[pltpu.VMEM((r + 2 * CONV_HALO, tc), F32)]
        + ([pltpu.VMEM((r, tc), F32)] if post == "ln_silu" else []),
        compiler_params=_cparams(2),
        name="dwconv%d" % taps,
    )(*args)


def _masked_halves(q, lane_lo):
    zero = jnp.zeros_like(q)
    return jnp.where(lane_lo, q, zero), jnp.where(lane_lo, zero, q)


def _softmax_pv(s_parts, v_parts, sink):
    m = s_parts[0].max(axis=-1, keepdims=True)
    for s in s_parts[1:]:
        m = jnp.maximum(m, s.max(axis=-1, keepdims=True))
    if sink is not None:
        m = jnp.maximum(m, sink)
    denom = jnp.exp(sink - m) if sink is not None else 0.0
    o = None
    for s, v in zip(s_parts, v_parts):
        p = jnp.exp(s - m)
        denom = denom + p.sum(axis=-1, keepdims=True)
        pv = jnp.dot(p.astype(BF16), v, preferred_element_type=F32)
        o = pv if o is None else o + pv
    return o / denom


def _nt_dot(a, b):
    return lax.dot_general(a, b, (((1,), (1,)), ((), ())), preferred_element_type=F32)


def _wattn_kernel(sink_ref, q_ref, *rest, local, seq_len, blk):
    if local:
        k0, k1, k2, v0, v1, v2, kc_ref, vc_ref, o_ref = rest
    else:
        kc_ref, vc_ref, _, o_ref = rest
    p = pl.program_id(2)
    i = pl.program_id(1)
    lane_lo = lax.broadcasted_iota(jnp.int32, (blk, LANES), 1) < HEAD_DIM
    kc = kc_ref[...]
    vc = vc_ref[...]
    if local:
        kl = jnp.concatenate([k0[...], k1[...], k2[...]], axis=0)
        vl = jnp.concatenate([v0[...], v1[...], v2[...]], axis=0)
        qpos = i * blk + lax.broadcasted_iota(jnp.int32, (blk, 3 * blk), 0)
        kpos = (i - 1) * blk + lax.broadcasted_iota(jnp.int32, (blk, 3 * blk), 1)
        valid = (jnp.abs(kpos - qpos) <= WA_WINDOW) & (kpos >= 0) & (kpos < seq_len)
        valid = jnp.concatenate([valid] * WA_GROUP, axis=0)
    qa, qb = [], []
    for g in range(WA_GROUP):
        a, b = _masked_halves(q_ref[:, g * LANES:(g + 1) * LANES], lane_lo)
        qa.append(a)
        qb.append(b)
    outs = []
    for half, qs in enumerate((qa, qb)):
        qs = jnp.concatenate(qs, axis=0)
        sink = jnp.concatenate(
            [jnp.full((blk, 1), sink_ref[8 * p + 4 * half + g], F32) for g in range(WA_GROUP)], axis=0)
        s_parts, v_parts = [], []
        if local:
            s_parts.append(jnp.where(valid, _nt_dot(qs, kl), NEG_INF))
            v_parts.append(vl)
        s_parts.append(_nt_dot(qs, kc))
        v_parts.append(vc)
        outs.append(_softmax_pv(s_parts, v_parts, sink))
    for g in range(WA_GROUP):
        rows = slice(g * blk, (g + 1) * blk)
        o_ref[:, g * LANES:(g + 1) * LANES] = jnp.where(lane_lo, outs[0][rows], outs[1][rows]).astype(o_ref.dtype)


def window_attention(qkv, sinks, *, batch, seq_len, ctx_len, d, blk=128):
    n_pairs = d // (2 * WA_GROUP * HEAD_DIM)
    kcol = d // LANES
    vcol = kcol + n_pairs
    nblk = seq_len // blk
    cb0 = batch * seq_len // ctx_len
    qw = WA_GROUP * LANES
    kern = functools.partial(_wattn_kernel, seq_len=seq_len)
    smem = pl.BlockSpec(memory_space=pltpu.SMEM)

    def kspec(col0, shift):
        return pl.BlockSpec((blk, LANES),
                            lambda b, i, p: (b * nblk + jnp.clip(i + shift, 0, nblk - 1), col0 + p))

    ctx_k = pl.BlockSpec((ctx_len, LANES), lambda b, i, p: (cb0 + b, kcol + p))
    ctx_v = pl.BlockSpec((ctx_len, LANES), lambda b, i, p: (cb0 + b, vcol + p))
    lat = pl.pallas_call(
        functools.partial(kern, local=True, blk=blk),
        grid=(batch, nblk, n_pairs),
        in_specs=[smem, pl.BlockSpec((blk, qw), lambda b, i, p: (b * nblk + i, p))]
        + [kspec(kcol, s) for s in (-1, 0, 1)] + [kspec(vcol, s) for s in (-1, 0, 1)] + [ctx_k, ctx_v],
        out_specs=pl.BlockSpec((blk, qw), lambda b, i, p: (b * nblk + i, p)),
        out_shape=jax.ShapeDtypeStruct((qkv.shape[0], d), BF16),
        compiler_params=_cparams(3),
        name="window_attn",
    )(sinks, qkv, *([qkv] * 8))
    return pl.pallas_call(
        functools.partial(kern, local=False, blk=ctx_len),
        grid=(batch, 1, n_pairs),
        in_specs=[smem, pl.BlockSpec((ctx_len, qw), lambda b, i, p: (cb0 + b, p)), ctx_k, ctx_v,
                  pl.BlockSpec(memory_space=pl.ANY)],
        out_specs=pl.BlockSpec((ctx_len, qw), lambda b, i, p: (cb0 + b, p)),
        out_shape=jax.ShapeDtypeStruct((qkv.shape[0], d), BF16),
        input_output_aliases={4: 0},
        compiler_params=_cparams(3),
        name="ctx_attn",
    )(sinks, qkv, qkv, qkv, lat)


def _nattn_kernel(q_ref, k0, k1, k2, v0, v1, v2, kc_ref, vc_ref, bias_ref, o_ref, ks_ref, vs_ref,
                  *, grid_rows, rows_per_blk):
    j = pl.program_id(2)
    blk = k0.shape[0]
    for t, (kr, vr) in enumerate(((k0, v0), (k1, v1), (k2, v2))):
        ks_ref[t * blk:(t + 1) * blk, :] = kr[...]
        vs_ref[t * blk:(t + 1) * blk, :] = vr[...]
    kc = kc_ref[...]
    vc = vc_ref[...]
    lane_lo = lax.broadcasted_iota(jnp.int32, (GRID_W, LANES), 1) < HEAD_DIM
    strip = NA_WIN_ROWS * GRID_W
    for r in range(rows_per_blk):
        row = j * rows_per_blk + r
        r0 = jnp.clip(row - NA_WIN_ROWS // 2, 0, grid_rows - NA_WIN_ROWS)
        start = pl.multiple_of((r0 - (j - 1) * rows_per_blk) * GRID_W, GRID_W)
        cls = row - r0
        qa, qb = _masked_halves(q_ref[r * GRID_W:(r + 1) * GRID_W, :], lane_lo)
        qs = jnp.concatenate([qa, qb], axis=0)
        kn = ks_ref[pl.ds(start, strip), :]
        vn = vs_ref[pl.ds(start, strip), :]
        s_nb = _nt_dot(qs, kn) + bias_ref[cls]
        s_cx = _nt_dot(qs, kc)
        o = _softmax_pv([s_nb, s_cx], [vn, vc], None)
        o_ref[r * GRID_W:(r + 1) * GRID_W, :] = jnp.where(lane_lo, o[:GRID_W], o[GRID_W:]).astype(o_ref.dtype)


def _na_bias_table(rpb):
    h = rpb.shape[0]
    n_dcol = 2 * NA_WIN_COLS - 1
    cols = jnp.arange(GRID_W)
    col_start = jnp.clip(cols - NA_WIN_COLS // 2, 0, GRID_W - NA_WIN_COLS)
    inwin = (cols[None, :] >= col_start[:, None]) & (cols[None, :] < col_start[:, None] + NA_WIN_COLS)
    dcol = cols[None, :] - cols[:, None] + NA_WIN_COLS - 1
    pick = (dcol[None] == jnp.arange(n_dcol)[:, None, None]).astype(F32).reshape(n_dcol, -1)
    spread = jnp.dot(rpb.astype(F32).reshape(-1, n_dcol), pick, precision=lax.Precision.HIGHEST)
    spread = spread.reshape(h, 2 * NA_WIN_ROWS - 1, GRID_W, GRID_W)
    spread = jnp.where(inwin[None, None], spread, NEG_INF)
    tab = jnp.stack([spread[:, NA_WIN_ROWS - 1 - cls:2 * NA_WIN_ROWS - 1 - cls]
                     for cls in range(NA_WIN_ROWS)], axis=1)
    tab = tab.transpose(0, 1, 3, 2, 4).reshape(h, NA_WIN_ROWS, GRID_W, NA_WIN_ROWS * GRID_W)
    tab = tab.reshape(h // 2, 2, NA_WIN_ROWS, GRID_W, NA_WIN_ROWS * GRID_W).transpose(0, 2, 1, 3, 4)
    return tab.reshape(h // 2, NA_WIN_ROWS, 2 * GRID_W, NA_WIN_ROWS * GRID_W)


def neighbourhood_attention(qkv, rpb, *, batch, seq_len, ctx_len, d, with_ctx_out):
    n_pairs = d // LANES
    rows_per_blk = NA_WIN_ROWS
    blk = rows_per_blk * GRID_W
    grid_rows = seq_len // GRID_W
    nblk = seq_len // blk
    cb0 = batch * seq_len // ctx_len
    bias = _na_bias_table(rpb)

    def kspec(col0, shift):
        return pl.BlockSpec((blk, LANES),
                            lambda b, p, j: (b * nblk + jnp.clip(j + shift, 0, nblk - 1), col0 + p))

    ctx_k = pl.BlockSpec((ctx_len, LANES), lambda b, p, j: (cb0 + b, n_pairs + p))
    ctx_v = pl.BlockSpec((ctx_len, LANES), lambda b, p, j: (cb0 + b, 2 * n_pairs + p))
    lat = pl.pallas_call(
        functools.partial(_nattn_kernel, grid_rows=grid_rows, rows_per_blk=rows_per_blk),
        grid=(batch, n_pairs, nblk),
        in_specs=[pl.BlockSpec((blk, LANES), lambda b, p, j: (b * nblk + j, p))]
        + [kspec(n_pairs, s) for s in (-1, 0, 1)] + [kspec(2 * n_pairs, s) for s in (-1, 0, 1)]
        + [ctx_k, ctx_v,
           pl.BlockSpec((None, NA_WIN_ROWS, 2 * GRID_W, NA_WIN_ROWS * GRID_W), lambda b, p, j: (p, 0, 0, 0))],
        out_specs=pl.BlockSpec((blk, LANES), lambda b, p, j: (b * nblk + j, p)),
        out_shape=jax.ShapeDtypeStruct((batch * seq_len, d), BF16),
        scratch_shapes=[pltpu.VMEM((3 * blk, LANES), BF16), pltpu.VMEM((3 * blk, LANES), BF16)],
        compiler_params=_cparams(3),
        name="neighbourhood_attn",
    )(qkv, *([qkv] * 8), bias)
    if not with_ctx_out:
        return lat
    ctx = pl.pallas_call(
        _cattn_kernel,
        grid=(batch, n_pairs),
        in_specs=[pl.BlockSpec((ctx_len, LANES), lambda b, p: (cb0 + b, p)),
                  pl.BlockSpec((ctx_len, LANES), lambda b, p: (cb0 + b, n_pairs + p)),
                  pl.BlockSpec((ctx_len, LANES), lambda b, p: (cb0 + b, 2 * n_pairs + p))],
        out_specs=pl.BlockSpec((ctx_len, LANES), lambda b, p: (b, p)),
        out_shape=jax.ShapeDtypeStruct((batch * ctx_len, d), BF16),
        compiler_params=_cparams(2),
        name="ctx_mha",
    )(qkv, qkv, qkv)
    return jnp.concatenate([lat, ctx], axis=0)


def _cattn_kernel(q_ref, k_ref, v_ref, o_ref):
    rows = q_ref.shape[0]
    lane_lo = lax.broadcasted_iota(jnp.int32, (rows, LANES), 1) < HEAD_DIM
    qa, qb = _masked_halves(q_ref[...], lane_lo)
    qs = jnp.concatenate([qa, qb], axis=0)
    o = _softmax_pv([_nt_dot(qs, k_ref[...])], [v_ref[...]], None)
    o_ref[...] = jnp.where(lane_lo, o[:rows], o[rows:]).astype(o_ref.dtype)


def _router_kernel(x_ref, g_ref, sh_ref, sc_ref, wh_ref, wl_ref, b_ref, h_ref, idx_ref, gate_ref, cnt_ref,
                   run_ref):
    @pl.when(pl.program_id(0) == 0)
    def _():
        run_ref[...] = jnp.zeros_like(run_ref)

    h = _norm_mod(x_ref[...], g_ref[...], sh_ref[0], sc_ref[0])
    h_hi = h.astype(BF16)
    h_ref[...] = h_hi
    h_lo = (h - h_hi.astype(F32)).astype(BF16)
    logits = (jnp.dot(h_hi, wh_ref[...], preferred_element_type=F32)
              + jnp.dot(h_hi, wl_ref[...], preferred_element_type=F32)
              + jnp.dot(h_lo, wh_ref[...], preferred_element_type=F32)) + b_ref[...]
    lane = lax.broadcasted_iota(jnp.int32, logits.shape, 1)
    tm = logits.shape[0]
    idx_out = jnp.zeros(logits.shape, jnp.int32)
    val_out = jnp.zeros(logits.shape, F32)
    tri = jnp.where(lax.broadcasted_iota(jnp.int32, (tm, tm), 0) > lax.broadcasted_iota(jnp.int32, (tm, tm), 1),
                    1.0, 0.0).astype(BF16)
    run = run_ref[...]
    top = None
    denom = 0.0
    for k in range(TOP_K):
        m = logits.max(axis=-1, keepdims=True)
        sel = jnp.min(jnp.where(logits == m, lane, LANES), axis=-1, keepdims=True)
        if top is None:
            top = m
        e = jnp.exp(m - top)
        denom = denom + e
        onehot = lane == sel
        before = jnp.dot(tri, jnp.where(onehot, 1.0, 0.0).astype(BF16), preferred_element_type=F32)
        rank = jnp.sum(jnp.where(onehot, before + run, 0.0), axis=-1, keepdims=True).astype(jnp.int32)
        run = run + jnp.sum(jnp.where(onehot, 1.0, 0.0), axis=0, keepdims=True)
        idx_out = jnp.where(lane == k, sel, idx_out)
        idx_out = jnp.where(lane == TOP_K + k, rank, idx_out)
        val_out = jnp.where(lane == k, e, val_out)
        logits = jnp.where(onehot, -jnp.inf, logits)
    run_ref[...] = run
    cnt_ref[...] = run
    idx_ref[...] = idx_out
    gate_ref[...] = val_out / denom


def router(x, g, shift, scale, w_router, b_router, *, seg_rows, rows, row_off=0, tm=ROW_TILE):
    m, d = rows, x.shape[1]
    n_seg = shift.shape[0]
    n_e = w_router.shape[1]
    w_pad = jnp.zeros((d, LANES), F32).at[:, :n_e].set(w_router)
    w_hi = w_pad.astype(BF16)
    w_lo = (w_pad - w_hi.astype(F32)).astype(BF16)
    b_pad = jnp.full((1, LANES), -jnp.inf, F32).at[0, :n_e].set(b_router)
    blk_off = row_off // tm
    seg = lambda i: (jnp.minimum((blk_off + i) * tm // seg_rows, n_seg - 1), 0, 0)
    return pl.pallas_call(
        _router_kernel,
        grid=(m // tm,),
        in_specs=[pl.BlockSpec((tm, d), lambda i: (blk_off + i, 0)),
                  pl.BlockSpec((1, d), lambda i: (0, 0)),
                  pl.BlockSpec((1, 1, d), seg),
                  pl.BlockSpec((1, 1, d), seg),
                  pl.BlockSpec((d, LANES), lambda i: (0, 0)),
                  pl.BlockSpec((d, LANES), lambda i: (0, 0)),
                  pl.BlockSpec((1, LANES), lambda i: (0, 0))],
        out_specs=[pl.BlockSpec((tm, d), lambda i: (i, 0)),
                   pl.BlockSpec((tm, LANES), lambda i: (i, 0)),
                   pl.BlockSpec((tm, LANES), lambda i: (i, 0)),
                   pl.BlockSpec((1, LANES), lambda i: (0, 0))],
        out_shape=[jax.ShapeDtypeStruct((m, d), BF16),
                   jax.ShapeDtypeStruct((m, LANES), jnp.int32),
                   jax.ShapeDtypeStruct((m, LANES), F32),
                   jax.ShapeDtypeStruct((1, LANES), F32)],
        scratch_shapes=[pltpu.VMEM((1, LANES), F32)],
        compiler_params=_cparams(1),
        name="router",
    )(x, g, shift, scale, w_hi, w_lo, b_pad)


def _expert_kernel(blk_e_ref, n_used_ref, x_ref, wgu_ref, bgu_ref, wd_ref, bd_ref, o_ref, wgu_s, wd_s, *, f):
    i = pl.program_id(0)
    used = i < n_used_ref[0]

    @pl.when(used & ((i == 0) | (blk_e_ref[i] != blk_e_ref[jnp.maximum(i - 1, 0)])))
    def _():
        wgu_s[...] = wgu_ref[0].astype(BF16)
        wd_s[...] = wd_ref[0].astype(BF16)

    @pl.when(used)
    def _():
        gu = jnp.dot(x_ref[...], wgu_s[...], preferred_element_type=F32) + bgu_ref[0]
        g = jnp.minimum(gu[:, :f], SWIGLU_LIMIT)
        u = jnp.clip(gu[:, f:], -SWIGLU_LIMIT, SWIGLU_LIMIT)
        act = (u + 1) * (g * jax.nn.sigmoid(SWIGLU_ALPHA * g))
        y = jnp.dot(act.astype(BF16), wd_s[...], preferred_element_type=F32) + bd_ref[0]
        o_ref[...] = y.astype(o_ref.dtype)

    @pl.when(i >= n_used_ref[0])
    def _():
        o_ref[...] = jnp.zeros_like(o_ref)


def expert_mlp(xs, blk_e, n_used, w_gu, b_gu, w_down, b_down, *, layer, bm=MOE_ROWS):
    n_slot, d = xs.shape
    _, n_e, _, f2 = w_gu.shape
    f = f2 // 2
    return pl.pallas_call(
        functools.partial(_expert_kernel, f=f),
        grid_spec=pltpu.PrefetchScalarGridSpec(
            num_scalar_prefetch=2,
            grid=(n_slot // bm,),
            in_specs=[pl.BlockSpec((bm, d), lambda i, be, nu: (i, 0)),
                      pl.BlockSpec((None, 1, d, f2), lambda i, be, nu: (layer, be[i], 0, 0)),
                      pl.BlockSpec((1, 1, f2), lambda i, be, nu: (be[i], 0, 0)),
                      pl.BlockSpec((None, 1, f, d), lambda i, be, nu: (layer, be[i], 0, 0)),
                      pl.BlockSpec((1, 1, d), lambda i, be, nu: (be[i], 0, 0))],
            out_specs=pl.BlockSpec((bm, d), lambda i, be, nu: (i, 0)),
            scratch_shapes=[pltpu.VMEM((d, f2), BF16), pltpu.VMEM((f, d), BF16)]),
        out_shape=jax.ShapeDtypeStruct((n_slot, d), BF16),
        compiler_params=pltpu.CompilerParams(dimension_semantics=("arbitrary",),
                                             vmem_limit_bytes=EXPERT_VMEM_LIMIT),
        name="expert_mlp",
    )(blk_e, n_used, xs, w_gu, b_gu.reshape(n_e, 1, f2), w_down, b_down.reshape(n_e, 1, d))


def _combine_kernel(res_ref, y_ref, w_ref, gate_ref, o_ref):
    w = w_ref[...]
    y = w[:, 0:1] * y_ref[0].astype(F32)
    for k in range(1, TOP_K):
        y = y + w[:, k:k + 1] * y_ref[k].astype(F32)
    o_ref[...] = res_ref[...] + gate_ref[0] * y


def moe_dispatch(x, g, shift, scale, w_router, b_router, *, seg_rows, rows, row_off, bm=MOE_ROWS):
    m = rows
    n_e = w_router.shape[1]
    h, route, gates, counts = router(x, g, shift, scale, w_router, b_router, seg_rows=seg_rows, rows=rows,
                                     row_off=row_off)
    idx = route[:, :TOP_K]
    rank = route[:, TOP_K:2 * TOP_K]
    n_asg = m * TOP_K
    counts = counts[0, :n_e].astype(jnp.int32)
    padded = (counts + bm - 1) // bm * bm
    pad_end = jnp.cumsum(padded)
    pad_start = pad_end - padded
    start_of = jnp.sum(jnp.where(idx[:, :, None] == jnp.arange(n_e, dtype=jnp.int32), pad_start, 0), axis=-1)
    dest = (start_of + rank).reshape(-1)
    n_blk = -(-(n_asg + n_e * (bm - 1)) // bm)
    n_slot = n_blk * bm
    tok = (jnp.arange(n_asg, dtype=jnp.int32) // TOP_K)
    slot_tok = (jnp.arange(n_slot, dtype=jnp.int32) % m).at[dest].set(tok, unique_indices=True)
    blk_start = jnp.arange(n_blk, dtype=jnp.int32) * bm
    blk_e = jnp.minimum(jnp.sum((pad_end[None, :] <= blk_start[:, None]).astype(jnp.int32), axis=1), n_e - 1)
    n_used = (pad_end[-1] // bm).astype(jnp.int32).reshape(1)
    xs = h.at[slot_tok].get(mode="promise_in_bounds")
    dest_km = dest.reshape(m, TOP_K).T.reshape(-1)
    return xs, blk_e, n_used, dest_km, gates


def moe_combine(x, y_slot, dest_km, gates, gate, *, seg_rows, rows, row_off):
    m, d = rows, x.shape[1]
    y_tok = y_slot.at[dest_km].get(mode="promise_in_bounds").reshape(TOP_K, m, d)
    tm = 256
    n_seg = gate.shape[0]
    blk_off = row_off // tm
    seg = lambda i: (jnp.minimum((blk_off + i) * tm // seg_rows, n_seg - 1), 0, 0)
    return pl.pallas_call(
        _combine_kernel,
        grid=(m // tm,),
        in_specs=[pl.BlockSpec((tm, d), lambda i: (blk_off + i, 0)),
                  pl.BlockSpec((TOP_K, tm, d), lambda i: (0, i, 0)),
                  pl.BlockSpec((tm, LANES), lambda i: (i, 0)),
                  pl.BlockSpec((1, 1, d), seg)],
        out_specs=pl.BlockSpec((tm, d), lambda i: (blk_off + i, 0)),
        out_shape=jax.ShapeDtypeStruct(x.shape, F32),
        input_output_aliases={0: 0},
        compiler_params=_cparams(1),
        name="moe_combine",
    )(x, y_tok, gates, gate)


LC_R2 = 128


@functools.lru_cache(maxsize=None)
def _dft_small_mats(n):
    nn = 2 * n
    ang = 2.0 * np.pi * ((np.arange(nn)[:, None] * np.arange(nn)[None, :]) % nn) / nn
    fwd = np.concatenate([np.cos(ang), -np.sin(ang)], 0)
    inv = np.concatenate([np.cos(ang[:n]), -np.sin(ang[:n])], 1) / nn
    return jnp.asarray(fwd, BF16), jnp.asarray(inv, BF16)


def _plain_mm_kernel(a_ref, b_ref, o_ref):
    o_ref[...] = jnp.dot(a_ref[...], b_ref[...].astype(BF16), preferred_element_type=F32)


def plain_mm(a, b, tn=512):
    m, k = a.shape
    n = b.shape[1]
    tn = _col_tile(n, tn)
    return pl.pallas_call(
        _plain_mm_kernel,
        grid=(n // tn,),
        in_specs=[pl.BlockSpec((m, k), lambda j: (0, 0)), pl.BlockSpec((k, tn), lambda j: (0, j))],
        out_specs=pl.BlockSpec((m, tn), lambda j: (0, j)),
        out_shape=jax.ShapeDtypeStruct((m, n), F32),
        compiler_params=_cparams(1),
        name="plain_mm",
    )(a, b)


def _short_conv_kernel(u_ref, gate_ref, h_ref, skip_ref, fwd_ref, inv_ref, o_ref):
    u = u_ref[...]
    x = jnp.dot(fwd_ref[...], u.astype(BF16), preferred_element_type=F32)
    h = h_ref[...]
    nn = x.shape[0] // 2
    xr, xi, hr, hi = x[:nn], x[nn:], h[:nn], h[nn:]
    prod = jnp.concatenate([xr * hr - xi * hi, xr * hi + xi * hr], axis=0).astype(BF16)
    y = jnp.dot(inv_ref[...], prod, preferred_element_type=F32)
    o_ref[...] = (gate_ref[...].astype(F32) * (y + skip_ref[...] * u.astype(F32))).astype(o_ref.dtype)


def short_long_conv(u_arr, u_col, gate_arr, gate_col, spec, skip, fwd, inv, *, batch, seq_len, tl=512):
    d = spec.shape[1]
    tl = _col_tile(d, tl)
    uc, gc = u_col // tl, gate_col // tl
    nn2 = spec.shape[0]
    return pl.pallas_call(
        _short_conv_kernel,
        grid=(batch, d // tl),
        in_specs=[pl.BlockSpec((seq_len, tl), lambda b, j: (b, uc + j)),
                  pl.BlockSpec((seq_len, tl), lambda b, j: (b, gc + j)),
                  pl.BlockSpec((nn2, tl), lambda b, j: (0, j)),
                  pl.BlockSpec((1, tl), lambda b, j: (0, j)),
                  pl.BlockSpec((nn2, seq_len), lambda b, j: (0, 0)),
                  pl.BlockSpec((seq_len, nn2), lambda b, j: (0, 0))],
        out_specs=pl.BlockSpec((seq_len, tl), lambda b, j: (b, j)),
        out_shape=jax.ShapeDtypeStruct((batch * seq_len, d), BF16),
        compiler_params=_cparams(2),
        name="short_long_conv",
    )(u_arr, gate_arr, spec, skip, fwd[:, :seq_len], inv)


def _hyena_taps(n, d, f_w1, f_b1, f_w2, f_b2, f_w3, f_b3, f_freq, f_w4):
    hp = lax.Precision.HIGHEST
    lin = jnp.linspace(0.0, 1.0, n, dtype=F32)
    idx = jnp.arange(n, dtype=F32)
    bands = jnp.linspace(1e-4, HY_BANDS - 1, HY_BANDS, dtype=F32)[None, :]
    deltas = jnp.abs(jnp.linspace(math.log(HY_DECAY_TARGET) / HY_FAST_DECAY,
                                  math.log(HY_DECAY_TARGET) / HY_SLOW_DECAY, d, dtype=F32))

    def branch(t, pos, direction):
        t = t[:, None]
        ang = (2.0 * math.pi / n) * pos[:, None]
        emb = jnp.concatenate([t, jnp.cos(bands * ang), -jnp.sin(bands * ang)], axis=-1)
        a = jnp.sin(f_freq * (jnp.dot(emb, f_w1, precision=hp) + f_b1))
        a = jnp.sin(f_freq * (jnp.dot(a, f_w2, precision=hp) + f_b2))
        a = jnp.sin(f_freq * (jnp.dot(a, f_w3, precision=hp) + f_b3))
        decay = jnp.exp(-t * deltas[None, :])
        w4 = f_w4.reshape(f_w4.shape[0], HY_ORDER, 2, d)[:, :, direction]
        return [jnp.dot(a, w4[:, o], precision=hp) * decay for o in range(HY_ORDER)]

    fwd = branch(lin, idx, 0)
    bwd = branch(lin[:0:-1], idx[:0:-1], 1)
    zero = jnp.zeros((1, d), F32)
    return [jnp.concatenate([fwd[o], zero, bwd[o]], axis=0) for o in range(HY_ORDER)]


HC = 128
HC_PAIR = 2
LC_TL = 16


@functools.lru_cache(maxsize=None)
def _hy_mats(n):
    nn = 2 * n
    r1 = nn // LC_R2
    tau = 2.0 * np.pi
    k1 = np.arange(r1)
    t = LC_R2 * np.arange(r1)[None, :] + np.arange(LC_R2)[:, None]
    ang1 = tau * ((k1[None, :, None] * t[:, None, :]) % nn) / nn
    f1 = np.empty((LC_R2, 2 * r1, r1))
    f1[:, 0::2] = np.cos(ang1)
    f1[:, 1::2] = -np.sin(ang1)
    ang2 = tau * ((np.arange(LC_R2)[:, None] * np.arange(LC_R2)[None, :]) % LC_R2) / LC_R2
    c, s = np.cos(ang2), np.sin(ang2)
    f2 = np.empty((2 * LC_R2, 2 * LC_R2))
    f2[:LC_R2, 0::2], f2[:LC_R2, 1::2] = c, s
    f2[LC_R2:, 0::2], f2[LC_R2:, 1::2] = -s, c
    f3 = np.empty((2 * LC_R2, 2 * LC_R2))
    f3[0::2, :LC_R2], f3[0::2, LC_R2:] = c, -s
    f3[1::2, :LC_R2], f3[1::2, LC_R2:] = s, c
    m = LC_R2 * np.arange(r1 // 2)[None, :] + np.arange(LC_R2)[:, None]
    ang4 = tau * ((m[:, :, None] * k1[None, None, :]) % nn) / nn
    f4 = np.empty((LC_R2, r1 // 2, 2 * r1))
    f4[:, :, 0::2] = np.cos(ang4) / nn
    f4[:, :, 1::2] = -np.sin(ang4) / nn
    return tuple(jnp.asarray(a, BF16) for a in (f1, f2, f3, f4))


def _hy1_kernel(x_ref, f_ref, o_ref, *, tl_n, nh):
    c = pl.program_id(2)
    for tl in range(tl_n):
        xs = jnp.concatenate([x_ref[h, pl.ds(c * tl_n + tl, nh, stride=LC_R2), :] for h in range(HC_PAIR)],
                             axis=1).astype(BF16)
        a = jnp.dot(f_ref[tl], xs, preferred_element_type=F32)
        packed = pltpu.bitcast(a.astype(BF16), jnp.uint32)
        for h in range(HC_PAIR):
            for j in range(packed.shape[0] // 8):
                o_ref[h, j, tl * 8:(tl + 1) * 8, :] = packed[j * 8:(j + 1) * 8, h * HC:(h + 1) * HC]


def hy_stage1(x, f1, *, blk_off, n_blk, batch, rows_hi):
    r2 = f1.shape[1]
    rows = rows_hi * LC_R2
    tl_n = LC_TL
    return pl.pallas_call(
        functools.partial(_hy1_kernel, tl_n=tl_n, nh=rows_hi),
        grid=(n_blk // HC_PAIR, batch, LC_R2 // tl_n),
        in_specs=[pl.BlockSpec((HC_PAIR, rows, HC), lambda p, b, c: (blk_off // HC_PAIR + p, b, 0)),
                  pl.BlockSpec((tl_n, r2, rows_hi), lambda p, b, c: (c, 0, 0))],
        out_specs=pl.BlockSpec((HC_PAIR, None, r2 // 16, tl_n * 8, HC), lambda p, b, c: (p, b, 0, c, 0)),
        out_shape=jax.ShapeDtypeStruct((n_blk, batch, r2 // 16, LC_R2 * 8, HC), jnp.uint32),
        compiler_params=_cparams(3),
        name="hy_stage1",
    )(x, f1[:, :, :rows_hi])


def _hy2_kernel(a_ref, f2_ref, *rest, spectrum_only):
    if spectrum_only:
        (o_ref,) = rest
    else:
        h_ref, f3_ref, o_ref = rest
    for kk in range(8):
        w = jnp.concatenate([a_ref[h, pl.ds(kk, LC_R2, stride=8), :] for h in range(HC_PAIR)], axis=1)
        a_in = pltpu.bitcast(w, BF16)
        x = jnp.dot(f2_ref[...], a_in, preferred_element_type=F32)
        if spectrum_only:
            o_ref[kk] = x.astype(o_ref.dtype)
            continue
        hs = h_ref[kk].astype(F32)
        xr, xi = x[:LC_R2], x[LC_R2:]
        hr, hi = hs[:LC_R2], hs[LC_R2:]
        prod = jnp.concatenate([xr * hr - xi * hi, xr * hi + xi * hr], axis=0).astype(BF16)
        g = jnp.dot(f3_ref[...], prod, preferred_element_type=F32)
        packed = pltpu.bitcast(g.astype(BF16), jnp.uint32)
        for h in range(HC_PAIR):
            o_ref[h, pl.ds(kk, LC_R2, stride=8), :] = packed[:, h * HC:(h + 1) * HC]


def hy_spectrum(a, f2):
    cb, _, nj, rows, _ = a.shape
    mat = pl.BlockSpec((2 * LC_R2, 2 * LC_R2), lambda j, p: (0, 0))
    return pl.pallas_call(
        functools.partial(_hy2_kernel, spectrum_only=True),
        grid=(nj, cb // HC_PAIR),
        in_specs=[pl.BlockSpec((HC_PAIR, None, None, rows, HC), lambda j, p: (p, 0, j, 0, 0)), mat],
        out_specs=pl.BlockSpec((None, 8, 2 * LC_R2, HC_PAIR * HC), lambda j, p: (p, j, 0, 0)),
        out_shape=jax.ShapeDtypeStruct((cb // HC_PAIR, nj * 8, 2 * LC_R2, HC_PAIR * HC), BF16),
        compiler_params=_cparams(2),
        name="hy_spectrum",
    )(a, f2)


def hy_stage23(a, spec, f2, f3):
    cb, bsz, nj, rows, _ = a.shape
    mat = pl.BlockSpec((2 * LC_R2, 2 * LC_R2), lambda j, p, b: (0, 0))
    blk = pl.BlockSpec((HC_PAIR, None, None, rows, HC), lambda j, p, b: (p, b, j, 0, 0))
    return pl.pallas_call(
        functools.partial(_hy2_kernel, spectrum_only=False),
        grid=(nj, cb // HC_PAIR, bsz),
        in_specs=[blk, mat,
                  pl.BlockSpec((None, 8, 2 * LC_R2, HC_PAIR * HC), lambda j, p, b: (p, j, 0, 0)), mat],
        out_specs=blk,
        out_shape=jax.ShapeDtypeStruct(a.shape, jnp.uint32),
        compiler_params=_cparams(3),
        name="hy_stage23",
    )(a, f2, spec, f3)


def _hy4_kernel(g_ref, f_ref, o_ref, *, tl_n, nh):
    c = pl.program_id(2)
    for tl in range(tl_n):
        w = jnp.concatenate([g_ref[h, :, tl * 8:(tl + 1) * 8, :].reshape(-1, HC) for h in range(HC_PAIR)], axis=1)
        g_in = pltpu.bitcast(w, BF16)
        y = jnp.dot(f_ref[tl], g_in, preferred_element_type=F32)
        for h in range(HC_PAIR):
            o_ref[h, pl.ds(c * tl_n + tl, nh, stride=LC_R2), :] = y[:, h * HC:(h + 1) * HC]


def hy_stage4(g, f4, *, seq_len):
    cb, bsz, nj, _, _ = g.shape
    nh = seq_len // LC_R2
    tl_n = LC_TL
    return pl.pallas_call(
        functools.partial(_hy4_kernel, tl_n=tl_n, nh=nh),
        grid=(cb // HC_PAIR, bsz, LC_R2 // tl_n),
        in_specs=[pl.BlockSpec((HC_PAIR, None, nj, tl_n * 8, HC), lambda p, b, c: (p, b, 0, c, 0)),
                  pl.BlockSpec((tl_n, nh, 16 * nj), lambda p, b, c: (c, 0, 0))],
        out_specs=pl.BlockSpec((HC_PAIR, seq_len, HC), lambda p, b, c: (p, b, 0)),
        out_shape=jax.ShapeDtypeStruct((cb, bsz * seq_len, HC), F32),
        compiler_params=_cparams(3),
        name="hy_stage4",
    )(g, f4)


def _hy_gate_kernel(y_ref, u_ref, gate_ref, skip_ref, o_ref):
    o_ref[...] = (gate_ref[...] * (y_ref[...] + skip_ref[...] * u_ref[...])).astype(o_ref.dtype)


def hy_gate(y, u_arr, u_off, gate_arr, gate_off, skip, *, natural, tr=4096):
    cb, rows, _ = y.shape
    tr = min(tr, rows)
    if natural:
        out_spec = pl.BlockSpec((tr, HC), lambda i, r: (r, i))
        out_shape = jax.ShapeDtypeStruct((rows, cb * HC), BF16)
    else:
        out_spec = pl.BlockSpec((None, tr, HC), lambda i, r: (i, r, 0))
        out_shape = jax.ShapeDtypeStruct((cb, rows, HC), F32)
    return pl.pallas_call(
        _hy_gate_kernel,
        grid=(cb, rows // tr),
        in_specs=[pl.BlockSpec((None, tr, HC), lambda i, r: (i, r, 0)),
                  pl.BlockSpec((None, tr, HC), lambda i, r: (u_off + i, r, 0)),
                  pl.BlockSpec((None, tr, HC), lambda i, r: (gate_off + i, r, 0)),
                  pl.BlockSpec((None, 1, HC), lambda i, r: (i, 0, 0))],
        out_specs=out_spec,
        out_shape=out_shape,
        compiler_params=_cparams(2),
        name="hy_gate",
    )(y, u_arr, gate_arr, skip.reshape(cb, 1, HC))


def _hy_taps_kernel(a_ref, t_ref, w_ref, dl_ref, o_ref):
    a = a_ref[...]
    a_hi = a.astype(BF16)
    a_lo = (a - a_hi.astype(F32)).astype(BF16)
    w = w_ref[...]
    w_hi = w.astype(BF16)
    w_lo = (w - w_hi.astype(F32)).astype(BF16)
    h = (jnp.dot(a_hi, w_hi, preferred_element_type=F32) + jnp.dot(a_hi, w_lo, preferred_element_type=F32)
         + jnp.dot(a_lo, w_hi, preferred_element_type=F32))
    o_ref[...] = h * jnp.exp(-t_ref[...] * dl_ref[...])


def hy_taps(a_all, t_all, w4, deltas, *, seq_len, order, tr=2048):
    nrow, width = a_all.shape
    d = deltas.shape[0]
    tr = min(tr, seq_len)
    per_dir = seq_len // tr
    return pl.pallas_call(
        _hy_taps_kernel,
        grid=(d // HC, nrow // tr),
        in_specs=[pl.BlockSpec((tr, width), lambda i, r: (r, 0)),
                  pl.BlockSpec((tr, 1), lambda i, r: (r, 0)),
                  pl.BlockSpec((None, None, width, HC), lambda i, r: (order, r // per_dir, 0, i)),
                  pl.BlockSpec((1, HC), lambda i, r: (0, i))],
        out_specs=pl.BlockSpec((None, tr, HC), lambda i, r: (i, r, 0)),
        out_shape=jax.ShapeDtypeStruct((d // HC, nrow, HC), F32),
        compiler_params=_cparams(2),
        name="hy_taps",
    )(a_all, t_all, w4.reshape(width, HY_ORDER, 2, d).transpose(1, 2, 0, 3), deltas.reshape(1, d))


def _hyena_filter_hidden(n, f_w1, f_b1, f_w2, f_b2, f_w3, f_b3, f_freq):
    hp = lax.Precision.HIGHEST
    lin = jnp.linspace(0.0, 1.0, n, dtype=F32)
    idx = jnp.arange(n, dtype=F32)
    bands = jnp.linspace(1e-4, HY_BANDS - 1, HY_BANDS, dtype=F32)[None, :]
    t = jnp.concatenate([lin, jnp.zeros((1,), F32), lin[:0:-1]])[:, None]
    pos = jnp.concatenate([idx, jnp.zeros((1,), F32), idx[:0:-1]])[:, None]
    ang = (2.0 * math.pi / n) * pos
    emb = jnp.concatenate([t, jnp.cos(bands * ang), -jnp.sin(bands * ang)], axis=-1)
    a = jnp.sin(f_freq * (jnp.dot(emb, f_w1, precision=hp) + f_b1))
    a = jnp.sin(f_freq * (jnp.dot(a, f_w2, precision=hp) + f_b2))
    a = jnp.sin(f_freq * (jnp.dot(a, f_w3, precision=hp) + f_b3))
    keep = (jnp.arange(2 * n) != n)[:, None]
    return jnp.where(keep, a, 0.0), t


def _hyena_deltas(d):
    return jnp.abs(jnp.linspace(math.log(HY_DECAY_TARGET) / HY_FAST_DECAY,
                                math.log(HY_DECAY_TARGET) / HY_SLOW_DECAY, d, dtype=F32))


def hyena_long_convs2(zc, a_all, t_all, w4, skip, *, batch, seq_len, d):
    cb = d // HC
    f1, f2, f3, f4 = _hy_mats(seq_len)
    nh = seq_len // LC_R2
    deltas = _hyena_deltas(d)
    u, u_off = zc, 2 * cb
    for o in range(HY_ORDER):
        taps = hy_taps(a_all, t_all, w4, deltas, seq_len=seq_len, order=o)
        spec = hy_spectrum(hy_stage1(taps, f1, blk_off=0, n_blk=cb, batch=1, rows_hi=2 * nh), f2)
        a = hy_stage1(u, f1, blk_off=u_off, n_blk=cb, batch=batch, rows_hi=nh)
        y = hy_stage4(hy_stage23(a, spec, f2, f3), f4, seq_len=seq_len)
        u = hy_gate(y, u, u_off, zc, o * cb, skip[o], natural=(o == HY_ORDER - 1))
        u_off = 0
    return u


def _final_norm_kernel(x_ref, g_ref, o_ref):
    x = x_ref[...]
    o_ref[...] = x * lax.rsqrt(jnp.mean(x * x, axis=-1, keepdims=True) + NORM_EPS) * g_ref[...]


def final_norm(x, g, *, rows, tm=ROW_TILE):
    d = x.shape[1]
    return pl.pallas_call(
        _final_norm_kernel,
        grid=(rows // tm,),
        in_specs=[pl.BlockSpec((tm, d), lambda i: (i, 0)), pl.BlockSpec((1, d), lambda i: (0, 0))],
        out_specs=pl.BlockSpec((tm, d), lambda i: (i, 0)),
        out_shape=jax.ShapeDtypeStruct((rows, d), F32),
        compiler_params=_cparams(1),
        name="final_norm",
    )(x, g)


def _rope_tables(n_lat, seq_len, n_rows):
    t = jnp.arange(n_rows, dtype=jnp.int32)
    row = ((t % seq_len) // GRID_W).astype(F32)
    col = (t % GRID_W).astype(F32)
    lane = jnp.arange(LANES)
    dd = lane % HEAD_DIM
    quarter = HEAD_DIM // 4
    inv_freq = ROPE_BASE ** (-(dd % quarter).astype(F32) / quarter)
    pos = jnp.where(dd[None, :] < HEAD_DIM // 2, row[:, None], col[:, None])
    ang = pos * inv_freq[None, :]
    sign = jnp.where((dd % (HEAD_DIM // 2)) < quarter, -1.0, 1.0).astype(F32)
    lat = (t < n_lat)[:, None]
    cos = jnp.where(lat, jnp.cos(ang), 1.0)
    sin = jnp.where(lat, jnp.sin(ang) * sign[None, :], 0.0)
    return cos, sin


def _wa_head_perm(n_heads):
    order = []
    for p in range(n_heads // (2 * WA_GROUP)):
        for g in range(WA_GROUP):
            order += [2 * WA_GROUP * p + g, 2 * WA_GROUP * p + WA_GROUP + g]
    cols = jnp.asarray(order, jnp.int32)[:, None] * HEAD_DIM + jnp.arange(HEAD_DIM, dtype=jnp.int32)[None, :]
    return cols.reshape(-1)


def kernel(x, c, ctx, c_ctx, w_ada, b_ada, g_mix, g_ffn, hy_w_in, hy_b_in, hy_w_short, hy_b_short, hy_f_w1, hy_f_b1, hy_f_w2, hy_f_b2, hy_f_w3, hy_f_b3, hy_f_freq, hy_f_w4, hy_skip, hy_w_out, hy_b_out, cf_w_pw1, cf_b_pw1, cf_w_dw, cf_b_dw, cf_ln_g, cf_ln_b, cf_w_pw2, cf_b_pw2, wa_w_qkv, wa_w_o, wa_sinks, na_w_qkv, na_w_o, na_rpb, moe_w_router, moe_b_router, moe_w_gu, moe_b_gu, moe_w_down, moe_b_down, g_final):
    bsz, n, d = x.shape
    n_ctx = ctx.shape[1]
    depth = w_ada.shape[0]
    n_lat = bsz * n
    m_all = n_lat + bsz * n_ctx

    cond = jnp.zeros((8, d), F32).at[:bsz].set(jax.nn.silu(c)).at[bsz].set(jax.nn.silu(c_ctx))
    mods = adaln(cond, w_ada, b_ada)
    xu = jnp.concatenate([x.reshape(n_lat, d), ctx.reshape(bsz * n_ctx, d)], axis=0)
    zeros_d = jnp.zeros((1, d), F32)

    for i in range(depth):
        kind, j = i % N_MIXERS, i // N_MIXERS
        last = i == depth - 1
        mod = [mods[i, :bsz + 1, k * d:(k + 1) * d].reshape(bsz + 1, 1, d) for k in range(6)]
        m_out = n_lat if last else m_all
        gm = g_mix[i].reshape(1, d)
        need_ctx_in = (not last) or kind >= 2
        m_in = m_all if need_ctx_in else n_lat
        x_in = xu[:m_in]

        if kind == 0:
            z = nm_matmul(x_in, gm, mod[0], mod[1], hy_w_in[j].astype(BF16), hy_b_in[j].reshape(1, -1),
                          seg_rows=n)
            filt_w = (hy_f_w1[j], hy_f_b1[j], hy_f_w2[j], hy_f_b2[j], hy_f_w3[j], hy_f_b3[j], hy_f_freq[j],
                      hy_f_w4[j])
            w_s, b_s = hy_w_short[j], hy_b_short[j].reshape(1, -1)
            w_o, b_o = hy_w_out[j].astype(BF16), hy_b_out[j].reshape(1, d)
            zc = dwconv(z, w_s, b_s, start=0, total=n_lat, seq_len=n, tc=HC, blocked=True, rows=2048,
                        out_dtype=F32)
            a_all, t_all = _hyena_filter_hidden(n, *filt_w[:-1])
            y = hyena_long_convs2(zc, a_all, t_all, hy_f_w4[j], hy_skip[j], batch=bsz, seq_len=n, d=d)
            xu = mm_res(y, w_o, b_o, xu, mod[2], seg_rows=n)
            if m_in > n_lat:
                zcc = dwconv(z, w_s, b_s, start=n_lat, total=m_in - n_lat, seq_len=n_ctx, tc=512)
                taps_c = _hyena_taps(n_ctx, d, *filt_w)
                fwd, inv = _dft_small_mats(n_ctx)
                yc = short_long_conv(zcc, 2 * d, zcc, 0, plain_mm(fwd, taps_c[0]), hy_skip[j][0].reshape(1, d),
                                     fwd, inv, batch=bsz, seq_len=n_ctx)
                yc = short_long_conv(yc, 0, zcc, d, plain_mm(fwd, taps_c[1]), hy_skip[j][1].reshape(1, d),
                                     fwd, inv, batch=bsz, seq_len=n_ctx)
                if not last:
                    xu = mm_res(yc, w_o, b_o, xu, mod[2], seg_rows=n, row_off=n_lat)
            y = None
        elif kind == 1:
            w1 = cf_w_pw1[j].astype(BF16).reshape(d, 2, d).transpose(1, 0, 2)
            a = nm_matmul(x_in, gm, mod[0], mod[1], w1, cf_b_pw1[j].reshape(2, 1, d), seg_rows=n, mode="glu")
            conv = functools.partial(dwconv, a, cf_w_dw[j], cf_b_dw[j].reshape(1, d), post="ln_silu",
                                     ln=(cf_ln_g[j].reshape(1, d), cf_ln_b[j].reshape(1, d)), rows=64)
            y = conv(start=0, total=n_lat, seq_len=n, same_rows=True)
            if m_in > n_lat:
                y = conv(start=n_lat, total=m_in - n_lat, seq_len=n_ctx, same_rows=True, into=y)
            w_o, b_o = cf_w_pw2[j].astype(BF16), cf_b_pw2[j].reshape(1, d)
        elif kind == 2:
            perm = _wa_head_perm(d // HEAD_DIM)
            scale = HEAD_DIM ** -0.5
            w_qkv = jnp.concatenate([wa_w_qkv[j][:, :d][:, perm] * scale, wa_w_qkv[j][:, d:]], axis=1).astype(BF16)
            n_out = w_qkv.shape[1]
            qkv = nm_matmul(x_in, gm, mod[0], mod[1], w_qkv, jnp.zeros((1, n_out), F32), seg_rows=n,
                            mode="rope", rope=_rope_tables(n_lat, n, m_in), n_rope_cols=d + (n_out - d) // 2)
            y = window_attention(qkv, wa_sinks[j], batch=bsz, seq_len=n, ctx_len=n_ctx, d=d)
            w_o, b_o = wa_w_o[j][perm].astype(BF16), zeros_d
        else:
            scale = HEAD_DIM ** -0.5
            w_qkv = jnp.concatenate([na_w_qkv[j][:, :d] * scale, na_w_qkv[j][:, d:]], axis=1).astype(BF16)
            qkv = nm_matmul(x_in, gm, mod[0], mod[1], w_qkv, jnp.zeros((1, 3 * d), F32), seg_rows=n)
            y = neighbourhood_attention(qkv, na_rpb[j], batch=bsz, seq_len=n, ctx_len=n_ctx, d=d,
                                        with_ctx_out=not last)
            w_o, b_o = na_w_o[j].astype(BF16), zeros_d
        if y is not None:
            xu = mm_res(y, w_o, b_o, xu, mod[2], seg_rows=n, rows=m_out)

        ranges = ((0, n), (n, m_out - n))
        routed = [moe_dispatch(xu, g_ffn[i].reshape(1, d), mod[3], mod[4], moe_w_router[i], moe_b_router[i],
                               seg_rows=n, rows=cnt, row_off=off) for off, cnt in ranges]
        y_slots = [expert_mlp(xs, blk_e, n_used, moe_w_gu, moe_b_gu[i], moe_w_down, moe_b_down[i], layer=i)
                   for xs, blk_e, n_used, _, _ in routed]
        for (off, cnt), (_, _, _, dest_km, gates), y_slot in zip(ranges, routed, y_slots):
            xu = moe_combine(xu, y_slot, dest_km, gates, mod[5], seg_rows=n, rows=cnt, row_off=off)
    return final_norm(xu, g_final.reshape(1, d), rows=n_lat).reshape(bsz, n, d)
```

```python
import functools
import math

import jax
import jax.numpy as jnp
import numpy as np
from jax import lax
from jax.experimental import pallas as pl
from jax.experimental.pallas import tpu as pltpu

F32 = jnp.float32
BF16 = jnp.bfloat16

GRID_W = 64
N_MIXERS = 4
NORM_EPS = 1e-6
NEG_INF = -1e30
HEAD_DIM = 64
ROPE_BASE = 10000.0

HY_ORDER = 2
HY_BANDS = 16
HY_DECAY_TARGET = 1e-2
HY_FAST_DECAY = 0.3
HY_SLOW_DECAY = 1.5

WA_GROUP = 4
WA_WINDOW = 128
NA_WIN_ROWS = 8
NA_WIN_COLS = 16

TOP_K = 4
SWIGLU_LIMIT = 7.0
SWIGLU_ALPHA = 1.702

LANES = 128
ROW_TILE = 512
MOE_ROWS = 512
VMEM_LIMIT = 56 * 1024 * 1024
EXPERT_VMEM_LIMIT = 58 * 1024 * 1024


def _cparams(n_axes):
    return pltpu.CompilerParams(dimension_semantics=("arbitrary",) * n_axes,
                                vmem_limit_bytes=VMEM_LIMIT)


def _col_tile(n, pref=1024):
    t = min(pref, n)
    while n % t:
        t //= 2
    return t


def _adaln_kernel(c_ref, w_ref, b_ref, o_ref):
    w = w_ref[0]
    w_hi = w.astype(BF16)
    w_lo = (w - w_hi.astype(F32)).astype(BF16)
    c = c_ref[...]
    c_hi = c.astype(BF16)
    c_lo = (c - c_hi.astype(F32)).astype(BF16)
    acc = jnp.dot(c_hi, w_hi, preferred_element_type=F32)
    acc += jnp.dot(c_hi, w_lo, preferred_element_type=F32)
    acc += jnp.dot(c_lo, w_hi, preferred_element_type=F32)
    o_ref[0] = acc + b_ref[0]


def adaln(cond, w_ada, b_ada):
    depth, d, n6 = w_ada.shape
    tn = _col_tile(n6, 1024)
    return pl.pallas_call(
        _adaln_kernel,
        grid=(depth, n6 // tn),
        in_specs=[pl.BlockSpec((8, d), lambda l, j: (0, 0)),
                  pl.BlockSpec((1, d, tn), lambda l, j: (l, 0, j)),
                  pl.BlockSpec((1, 1, tn), lambda l, j: (l, 0, j))],
        out_specs=pl.BlockSpec((1, 8, tn), lambda l, j: (l, 0, j)),
        out_shape=jax.ShapeDtypeStruct((depth, 8, n6), F32),
        compiler_params=_cparams(2),
        name="adaln",
    )(cond, w_ada, b_ada.reshape(depth, 1, n6))


def _norm_mod(x, g, shift, scale):
    y = x * lax.rsqrt(jnp.mean(x * x, axis=-1, keepdims=True) + NORM_EPS)
    return (y * g) * (1 + scale) + shift


def _rope_tile(x, cos, sin):
    lane = lax.broadcasted_iota(jnp.int32, x.shape, 1)
    nxt = pltpu.roll(x, LANES - 16, axis=1)
    prv = pltpu.roll(x, 16, axis=1)
    partner = jnp.where((lane // 16) % 2 == 0, nxt, prv)
    return x * cos + partner * sin


def _nm_mm_kernel(x_ref, g_ref, sh_ref, sc_ref, w_ref, b_ref, *rest, mode, n_rope):
    if mode == "rope":
        cos_ref, sin_ref, o_ref, h_ref = rest
    else:
        o_ref, h_ref = rest
    j = pl.program_id(1)

    @pl.when(j == 0)
    def _():
        h_ref[...] = _norm_mod(x_ref[...], g_ref[...], sh_ref[0], sc_ref[0]).astype(BF16)

    h = h_ref[...]
    if mode == "glu":
        a = jnp.dot(h, w_ref[0], preferred_element_type=F32) + b_ref[0]
        gate = jnp.dot(h, w_ref[1], preferred_element_type=F32) + b_ref[1]
        o_ref[...] = (a * jax.nn.sigmoid(gate)).astype(o_ref.dtype)
        return
    acc = jnp.dot(h, w_ref[...], preferred_element_type=F32) + b_ref[...]
    if mode == "rope":
        @pl.when(j < n_rope)
        def _():
            cos = cos_ref[...]
            sin = sin_ref[...]
            for g in range(acc.shape[1] // LANES):
                sl = slice(g * LANES, (g + 1) * LANES)
                o_ref[:, sl] = _rope_tile(acc[:, sl], cos, sin).astype(o_ref.dtype)

        @pl.when(j >= n_rope)
        def _():
            o_ref[...] = acc.astype(o_ref.dtype)
    else:
        o_ref[...] = acc.astype(o_ref.dtype)


def nm_matmul(x, g, shift, scale, w, b, *, seg_rows, mode="plain", rope=None, n_rope_cols=0,
              tm=ROW_TILE, tn=None):
    m, d = x.shape
    n_seg = shift.shape[0]
    n = w.shape[-1]
    tn = tn or (_col_tile(math.gcd(n, n_rope_cols), 512) if mode == "rope" else _col_tile(n, 1024))
    seg = lambda i, j: (jnp.minimum(i * tm // seg_rows, n_seg - 1), 0, 0)
    in_specs = [pl.BlockSpec((tm, d), lambda i, j: (i, 0)),
                pl.BlockSpec((1, d), lambda i, j: (0, 0)),
                pl.BlockSpec((1, 1, d), seg),
                pl.BlockSpec((1, 1, d), seg)]
    if mode == "glu":
        in_specs += [pl.BlockSpec((2, d, tn), lambda i, j: (0, 0, j)),
                     pl.BlockSpec((2, 1, tn), lambda i, j: (0, 0, j))]
    else:
        in_specs += [pl.BlockSpec((d, tn), lambda i, j: (0, j)),
                     pl.BlockSpec((1, tn), lambda i, j: (0, j))]
    args = [x, g, shift, scale, w, b]
    if mode == "rope":
        in_specs += [pl.BlockSpec((tm, LANES), lambda i, j: (i, 0))] * 2
        args += list(rope)
    return pl.pallas_call(
        functools.partial(_nm_mm_kernel, mode=mode, n_rope=n_rope_cols // tn),
        grid=(m // tm, n // tn),
        in_specs=in_specs,
        out_specs=pl.BlockSpec((tm, tn), lambda i, j: (i, j)),
        out_shape=jax.ShapeDtypeStruct((m, n), BF16),
        scratch_shapes=[pltpu.VMEM((tm, d), BF16)],
        compiler_params=_cparams(2),
        name="nm_matmul_" + mode,
    )(*args)


def _mm_res_kernel(a_ref, w_ref, b_ref, res_ref, gate_ref, o_ref):
    if len(a_ref.shape) == 3:
        kc = a_ref.shape[2]
        acc = b_ref[...]
        for cb in range(a_ref.shape[0]):
            acc = acc + jnp.dot(a_ref[cb], w_ref[cb * kc:(cb + 1) * kc, :], preferred_element_type=F32)
    else:
        acc = jnp.dot(a_ref[...], w_ref[...], preferred_element_type=F32) + b_ref[...]
    o_ref[...] = res_ref[...] + gate_ref[0] * acc


def mm_res(a, w, b, res, gate, *, seg_rows, row_off=0, rows=None, tm=ROW_TILE, tn=None):
    if a.ndim == 3:
        m = rows or a.shape[1]
        k = a.shape[0] * a.shape[2]
        a_spec = pl.BlockSpec((a.shape[0], tm, a.shape[2]), lambda i, j: (0, i, 0))
    else:
        m, k = rows or a.shape[0], a.shape[1]
        a_spec = pl.BlockSpec((tm, k), lambda i, j: (i, 0))
    n = w.shape[1]
    n_seg = gate.shape[0]
    tn = tn or _col_tile(n, 1024)
    blk_off = row_off // tm
    seg = lambda i, j: (jnp.minimum((blk_off + i) * tm // seg_rows, n_seg - 1), 0, j)
    return pl.pallas_call(
        _mm_res_kernel,
        grid=(m // tm, n // tn),
        in_specs=[a_spec,
                  pl.BlockSpec((k, tn), lambda i, j: (0, j)),
                  pl.BlockSpec((1, tn), lambda i, j: (0, j)),
                  pl.BlockSpec((tm, tn), lambda i, j: (blk_off + i, j)),
                  pl.BlockSpec((1, 1, tn), seg)],
        out_specs=pl.BlockSpec((tm, tn), lambda i, j: (blk_off + i, j)),
        out_shape=jax.ShapeDtypeStruct(res.shape, F32),
        input_output_aliases={3: 0},
        compiler_params=_cparams(2),
        name="mm_res",
    )(a, w, b, res, gate)


CONV_HALO = 16


def _dwconv_kernel(prev_ref, x_ref, next_ref, w_ref, b_ref, *rest, taps, blocks_per_seq, post, sub, n_alias):
    if n_alias:
        n_tail = 3 if post == "ln_silu" else 2
        rest = rest[:-n_tail - 1] + rest[-n_tail:]
    if post == "ln_silu":
        g_ref, beta_ref, o_ref, win_ref, conv_ref = rest
    else:
        o_ref, win_ref = rest
    i = pl.program_id(0)
    pos = i % blocks_per_seq
    rows, c = x_ref.shape
    half = taps // 2
    zero = jnp.zeros((CONV_HALO, c), F32)
    win_ref[CONV_HALO:CONV_HALO + rows, :] = x_ref[...].astype(F32)

    @pl.when(pos == 0)
    def _():
        win_ref[0:CONV_HALO, :] = zero

    @pl.when(pos > 0)
    def _():
        win_ref[0:CONV_HALO, :] = prev_ref[...].astype(F32)

    @pl.when(pos == blocks_per_seq - 1)
    def _():
        win_ref[CONV_HALO + rows:, :] = zero

    @pl.when(pos < blocks_per_seq - 1)
    def _():
        win_ref[CONV_HALO + rows:, :] = next_ref[...].astype(F32)

    bias = b_ref[...]
    if post == "ln_silu":
        ext = rows + sub
        groups = {}
        for t in range(taps):
            off = CONV_HALO - half + t
            groups.setdefault(off % sub, []).append((t, off // sub))
        lw = min(2 * LANES, c)
        for lc in range(0, c, lw):
            lanes = slice(lc, lc + lw)
            y = jnp.broadcast_to(bias[:, lanes], (rows, lw))
            for s, members in sorted(groups.items()):
                acc = None
                for t, a in members:
                    term = pltpu.repeat(w_ref[t, :, lanes], ext // sub, axis=0) * win_ref[sub * a:sub * a + ext, lanes]
                    acc = term if acc is None else acc + term
                y = y + acc[s:s + rows]
            conv_ref[:, lanes] = y
        for s in range(rows // sub):
            acc = conv_ref[s * sub:(s + 1) * sub, :]
            mu = jnp.mean(acc, axis=-1, keepdims=True)
            xc = acc - mu
            var = jnp.mean(xc * xc, axis=-1, keepdims=True)
            y = xc * lax.rsqrt(var + NORM_EPS) * g_ref[...] + beta_ref[...]
            o_ref[s * sub:(s + 1) * sub, :] = (y * jax.nn.sigmoid(y)).astype(o_ref.dtype)
        return
    for s in range(rows // sub):
        base = CONV_HALO + s * sub - half
        acc = jnp.broadcast_to(bias, (sub, c))
        for t in range(taps):
            acc = acc + w_ref[t] * win_ref[base + t:base + t + sub, :]
        o_ref[s * sub:(s + 1) * sub, :] = acc.astype(o_ref.dtype)


def dwconv(x, w, b, *, start, total, seq_len, post=None, ln=None, rows=256, tc=None, out_dtype=None,
           blocked=False, same_rows=False, into=None):
    m, c = x.shape
    taps = w.shape[0]
    tc = tc or c
    out_dtype = out_dtype or BF16
    sub = 8
    w = jnp.broadcast_to(w[:, None, :], (taps, sub, c))
    r = min(rows, seq_len)
    bps = seq_len // r
    hb = r // CONV_HALO
    off = start // r
    offh = start // CONV_HALO
    nh = m // CONV_HALO
    in_specs = [pl.BlockSpec((CONV_HALO, tc), lambda i, j: (jnp.maximum(offh + i * hb - 1, 0), j)),
                pl.BlockSpec((r, tc), lambda i, j: (off + i, j)),
                pl.BlockSpec((CONV_HALO, tc), lambda i, j: (jnp.minimum(offh + (i + 1) * hb, nh - 1), j)),
                pl.BlockSpec((taps, sub, tc), lambda i, j: (0, 0, j)),
                pl.BlockSpec((1, tc), lambda i, j: (0, j))]
    args = [x, x, x, w, b]
    if post == "ln_silu":
        in_specs += [pl.BlockSpec((1, tc), lambda i, j: (0, j))] * 2
        args += list(ln)
    aliases = {}
    if blocked:
        out_spec = pl.BlockSpec((None, r, tc), lambda i, j: (j, i, 0))
        out_shape = jax.ShapeDtypeStruct((c // tc, total, tc), out_dtype)
    elif same_rows:
        out_spec = pl.BlockSpec((r, tc), lambda i, j: (off + i, j))
        out_shape = jax.ShapeDtypeStruct((m, c), out_dtype)
        if into is not None:
            in_specs.append(pl.BlockSpec(memory_space=pl.ANY))
            args.append(into)
            aliases = {len(args) - 1: 0}
    else:
        out_spec = pl.BlockSpec((r, tc), lambda i, j: (i, j))
        out_shape = jax.ShapeDtypeStruct((total, c), out_dtype)
    return pl.pallas_call(
        functools.partial(_dwconv_kernel, taps=taps, blocks_per_seq=bps, post=post, sub=sub,
                          n_alias=len(aliases)),
        grid=(total // r, c // tc),
        in_specs=in_specs,
        out_specs=out_spec,
        out_shape=out_shape,
        input_output_aliases=aliases,
        scratch_shapes=[pltpu.VMEM((r + 2 * CONV_HALO, tc), F32)]
        + ([pltpu.VMEM((r, tc), F32)] if post == "ln_silu" else []),
        compiler_params=_cparams(2),
        name="dwconv%d" % taps,
    )(*args)


def _masked_halves(q, lane_lo):
    zero = jnp.zeros_like(q)
    return jnp.where(lane_lo, q, zero), jnp.where(lane_lo, zero, q)


def _softmax_pv(s_parts, v_parts, sink):
    m = s_parts[0].max(axis=-1, keepdims=True)
    for s in s_parts[1:]:
        m = jnp.maximum(m, s.max(axis=-1, keepdims=True))
    if sink is not None:
        m = jnp.maximum(m, sink)
    denom = jnp.exp(sink - m) if sink is not None else 0.0
    o = None
    for s, v in zip(s_parts, v_parts):
        p = jnp.exp(s - m)
        denom = denom + p.sum(axis=-1, keepdims=True)
        pv = jnp.dot(p.astype(BF16), v, preferred_element_type=F32)
        o = pv if o is None else o + pv
    return o / denom


def _nt_dot(a, b):
    return lax.dot_general(a, b, (((1,), (1,)), ((), ())), preferred_element_type=F32)


def _wattn_kernel(sink_ref, q_ref, *rest, local, seq_len, blk):
    if local:
        n_kb = (len(rest) - 3) // 2
        k_refs, v_refs = rest[:n_kb], rest[n_kb:2 * n_kb]
        kc_ref, vc_ref, o_ref = rest[2 * n_kb:]
    else:
        kc_ref, vc_ref, _, o_ref = rest
    p = pl.program_id(2)
    i = pl.program_id(1)
    lane_lo = lax.broadcasted_iota(jnp.int32, (blk, LANES), 1) < HEAD_DIM
    kc = kc_ref[...]
    vc = vc_ref[...]
    if local:
        kl = jnp.concatenate([r[...] for r in k_refs], axis=0)
        vl = jnp.concatenate([r[...] for r in v_refs], axis=0)
        n_keys = n_kb * WA_WINDOW
        qpos = i * blk + lax.broadcasted_iota(jnp.int32, (blk, n_keys), 0)
        kpos = i * blk - WA_WINDOW + lax.broadcasted_iota(jnp.int32, (blk, n_keys), 1)
        valid = (jnp.abs(kpos - qpos) <= WA_WINDOW) & (kpos >= 0) & (kpos < seq_len)
        valid = jnp.concatenate([valid] * WA_GROUP, axis=0)
    qa, qb = [], []
    for g in range(WA_GROUP):
        a, b = _masked_halves(q_ref[:, g * LANES:(g + 1) * LANES], lane_lo)
        qa.append(a)
        qb.append(b)
    outs = []
    for half, qs in enumerate((qa, qb)):
        qs = jnp.concatenate(qs, axis=0)
        sink = jnp.concatenate(
            [jnp.full((blk, 1), sink_ref[8 * p + 4 * half + g], F32) for g in range(WA_GROUP)], axis=0)
        s_parts, v_parts = [], []
        if local:
            s_parts.append(jnp.where(valid, _nt_dot(qs, kl), NEG_INF))
            v_parts.append(vl)
        s_parts.append(_nt_dot(qs, kc))
        v_parts.append(vc)
        outs.append(_softmax_pv(s_parts, v_parts, sink))
    for g in range(WA_GROUP):
        rows = slice(g * blk, (g + 1) * blk)
        o_ref[:, g * LANES:(g + 1) * LANES] = jnp.where(lane_lo, outs[0][rows], outs[1][rows]).astype(o_ref.dtype)


def window_attention(qkv, sinks, *, batch, seq_len, ctx_len, d, blk=256):
    n_pairs = d // (2 * WA_GROUP * HEAD_DIM)
    kcol = d // LANES
    vcol = kcol + n_pairs
    nblk = seq_len // blk
    kb_per_q = blk // WA_WINDOW
    n_kblk = seq_len // WA_WINDOW
    shifts = range(-1, kb_per_q + 1)
    cb0 = batch * seq_len // ctx_len
    qw = WA_GROUP * LANES
    kern = functools.partial(_wattn_kernel, seq_len=seq_len)
    smem = pl.BlockSpec(memory_space=pltpu.SMEM)

    def kspec(col0, shift):
        return pl.BlockSpec((WA_WINDOW, LANES),
                            lambda b, i, p: (b * n_kblk + jnp.clip(i * kb_per_q + shift, 0, n_kblk - 1), col0 + p))

    ctx_k = pl.BlockSpec((ctx_len, LANES), lambda b, i, p: (cb0 + b, kcol + p))
    ctx_v = pl.BlockSpec((ctx_len, LANES), lambda b, i, p: (cb0 + b, vcol + p))
    lat = pl.pallas_call(
        functools.partial(kern, local=True, blk=blk),
        grid=(batch, nblk, n_pairs),
        in_specs=[smem, pl.BlockSpec((blk, qw), lambda b, i, p: (b * nblk + i, p))]
        + [kspec(kcol, s) for s in shifts] + [kspec(vcol, s) for s in shifts] + [ctx_k, ctx_v],
        out_specs=pl.BlockSpec((blk, qw), lambda b, i, p: (b * nblk + i, p)),
        out_shape=jax.ShapeDtypeStruct((qkv.shape[0], d), BF16),
        compiler_params=_cparams(3),
        name="window_attn",
    )(sinks, qkv, *([qkv] * (2 * len(shifts) + 2)))
    return pl.pallas_call(
        functools.partial(kern, local=False, blk=ctx_len),
        grid=(batch, 1, n_pairs),
        in_specs=[smem, pl.BlockSpec((ctx_len, qw), lambda b, i, p: (cb0 + b, p)), ctx_k, ctx_v,
                  pl.BlockSpec(memory_space=pl.ANY)],
        out_specs=pl.BlockSpec((ctx_len, qw), lambda b, i, p: (cb0 + b, p)),
        out_shape=jax.ShapeDtypeStruct((qkv.shape[0], d), BF16),
        input_output_aliases={4: 0},
        compiler_params=_cparams(3),
        name="ctx_attn",
    )(sinks, qkv, qkv, qkv, lat)


def _nattn_kernel(q_ref, k0, k1, k2, v0, v1, v2, kc_ref, vc_ref, bias_ref, o_ref, ks_ref, vs_ref,
                  *, grid_rows, rows_per_blk):
    j = pl.program_id(2)
    blk = k0.shape[0]
    for t, (kr, vr) in enumerate(((k0, v0), (k1, v1), (k2, v2))):
        ks_ref[t * blk:(t + 1) * blk, :] = kr[...]
        vs_ref[t * blk:(t + 1) * blk, :] = vr[...]
    kc = kc_ref[...]
    vc = vc_ref[...]
    lane_lo = lax.broadcasted_iota(jnp.int32, (GRID_W, LANES), 1) < HEAD_DIM
    strip = NA_WIN_ROWS * GRID_W
    for r in range(rows_per_blk):
        row = j * rows_per_blk + r
        r0 = jnp.clip(row - NA_WIN_ROWS // 2, 0, grid_rows - NA_WIN_ROWS)
        start = pl.multiple_of((r0 - (j - 1) * rows_per_blk) * GRID_W, GRID_W)
        cls = row - r0
        qa, qb = _masked_halves(q_ref[r * GRID_W:(r + 1) * GRID_W, :], lane_lo)
        qs = jnp.concatenate([qa, qb], axis=0)
        kn = ks_ref[pl.ds(start, strip), :]
        vn = vs_ref[pl.ds(start, strip), :]
        s_nb = _nt_dot(qs, kn) + bias_ref[cls]
        s_cx = _nt_dot(qs, kc)
        o = _softmax_pv([s_nb, s_cx], [vn, vc], None)
        o_ref[r * GRID_W:(r + 1) * GRID_W, :] = jnp.where(lane_lo, o[:GRID_W], o[GRID_W:]).astype(o_ref.dtype)


def _na_bias_table(rpb):
    h = rpb.shape[0]
    n_dcol = 2 * NA_WIN_COLS - 1
    cols = jnp.arange(GRID_W)
    col_start = jnp.clip(cols - NA_WIN_COLS // 2, 0, GRID_W - NA_WIN_COLS)
    inwin = (cols[None, :] >= col_start[:, None]) & (cols[None, :] < col_start[:, None] + NA_WIN_COLS)
    dcol = cols[None, :] - cols[:, None] + NA_WIN_COLS - 1
    pick = (dcol[None] == jnp.arange(n_dcol)[:, None, None]).astype(F32).reshape(n_dcol, -1)
    spread = jnp.dot(rpb.astype(F32).reshape(-1, n_dcol), pick, precision=lax.Precision.HIGHEST)
    spread = spread.reshape(h, 2 * NA_WIN_ROWS - 1, GRID_W, GRID_W)
    spread = jnp.where(inwin[None, None], spread, NEG_INF)
    tab = jnp.stack([spread[:, NA_WIN_ROWS - 1 - cls:2 * NA_WIN_ROWS - 1 - cls]
                     for cls in range(NA_WIN_ROWS)], axis=1)
    tab = tab.transpose(0, 1, 3, 2, 4).reshape(h, NA_WIN_ROWS, GRID_W, NA_WIN_ROWS * GRID_W)
    tab = tab.reshape(h // 2, 2, NA_WIN_ROWS, GRID_W, NA_WIN_ROWS * GRID_W).transpose(0, 2, 1, 3, 4)
    return tab.reshape(h // 2, NA_WIN_ROWS, 2 * GRID_W, NA_WIN_ROWS * GRID_W)


def neighbourhood_attention(qkv, rpb, *, batch, seq_len, ctx_len, d, with_ctx_out):
    n_pairs = d // LANES
    rows_per_blk = NA_WIN_ROWS
    blk = rows_per_blk * GRID_W
    grid_rows = seq_len // GRID_W
    nblk = seq_len // blk
    cb0 = batch * seq_len // ctx_len
    bias = _na_bias_table(rpb)

    def kspec(col0, shift):
        return pl.BlockSpec((blk, LANES),
                            lambda b, p, j: (b * nblk + jnp.clip(j + shift, 0, nblk - 1), col0 + p))

    ctx_k = pl.BlockSpec((ctx_len, LANES), lambda b, p, j: (cb0 + b, n_pairs + p))
    ctx_v = pl.BlockSpec((ctx_len, LANES), lambda b, p, j: (cb0 + b, 2 * n_pairs + p))
    lat = pl.pallas_call(
        functools.partial(_nattn_kernel, grid_rows=grid_rows, rows_per_blk=rows_per_blk),
        grid=(batch, n_pairs, nblk),
        in_specs=[pl.BlockSpec((blk, LANES), lambda b, p, j: (b * nblk + j, p))]
        + [kspec(n_pairs, s) for s in (-1, 0, 1)] + [kspec(2 * n_pairs, s) for s in (-1, 0, 1)]
        + [ctx_k, ctx_v,
           pl.BlockSpec((None, NA_WIN_ROWS, 2 * GRID_W, NA_WIN_ROWS * GRID_W), lambda b, p, j: (p, 0, 0, 0))],
        out_specs=pl.BlockSpec((blk, LANES), lambda b, p, j: (b * nblk + j, p)),
        out_shape=jax.ShapeDtypeStruct((batch * seq_len, d), BF16),
        scratch_shapes=[pltpu.VMEM((3 * blk, LANES), BF16), pltpu.VMEM((3 * blk, LANES), BF16)],
        compiler_params=_cparams(3),
        name="neighbourhood_attn",
    )(qkv, *([qkv] * 8), bias)
    if not with_ctx_out:
        return lat
    ctx = pl.pallas_call(
        _cattn_kernel,
        grid=(batch, n_pairs),
        in_specs=[pl.BlockSpec((ctx_len, LANES), lambda b, p: (cb0 + b, p)),
                  pl.BlockSpec((ctx_len, LANES), lambda b, p: (cb0 + b, n_pairs + p)),
                  pl.BlockSpec((ctx_len, LANES), lambda b, p: (cb0 + b, 2 * n_pairs + p))],
        out_specs=pl.BlockSpec((ctx_len, LANES), lambda b, p: (b, p)),
        out_shape=jax.ShapeDtypeStruct((batch * ctx_len, d), BF16),
        compiler_params=_cparams(2),
        name="ctx_mha",
    )(qkv, qkv, qkv)
    return jnp.concatenate([lat, ctx], axis=0)


def _cattn_kernel(q_ref, k_ref, v_ref, o_ref):
    rows = q_ref.shape[0]
    lane_lo = lax.broadcasted_iota(jnp.int32, (rows, LANES), 1) < HEAD_DIM
    qa, qb = _masked_halves(q_ref[...], lane_lo)
    qs = jnp.concatenate([qa, qb], axis=0)
    o = _softmax_pv([_nt_dot(qs, k_ref[...])], [v_ref[...]], None)
    o_ref[...] = jnp.where(lane_lo, o[:rows], o[rows:]).astype(o_ref.dtype)


def _router_kernel(x_ref, g_ref, sh_ref, sc_ref, wh_ref, wl_ref, b_ref, h_ref, idx_ref, gate_ref, cnt_ref,
                   run_ref):
    @pl.when(pl.program_id(0) == 0)
    def _():
        run_ref[...] = jnp.zeros_like(run_ref)

    h = _norm_mod(x_ref[...], g_ref[...], sh_ref[0], sc_ref[0])
    h_hi = h.astype(BF16)
    h_ref[...] = h_hi
    h_lo = (h - h_hi.astype(F32)).astype(BF16)
    logits = (jnp.dot(h_hi, wh_ref[...], preferred_element_type=F32)
              + jnp.dot(h_hi, wl_ref[...], preferred_element_type=F32)
              + jnp.dot(h_lo, wh_ref[...], preferred_element_type=F32)) + b_ref[...]
    lane = lax.broadcasted_iota(jnp.int32, logits.shape, 1)
    tm = logits.shape[0]
    idx_out = jnp.zeros(logits.shape, jnp.int32)
    val_out = jnp.zeros(logits.shape, F32)
    tri = jnp.where(lax.broadcasted_iota(jnp.int32, (tm, tm), 0) > lax.broadcasted_iota(jnp.int32, (tm, tm), 1),
                    1.0, 0.0).astype(BF16)
    run = run_ref[...]
    top = None
    denom = 0.0
    for k in range(TOP_K):
        m = logits.max(axis=-1, keepdims=True)
        sel = jnp.min(jnp.where(logits == m, lane, LANES), axis=-1, keepdims=True)
        if top is None:
            top = m
        e = jnp.exp(m - top)
        denom = denom + e
        onehot = lane == sel
        before = jnp.dot(tri, jnp.where(onehot, 1.0, 0.0).astype(BF16), preferred_element_type=F32)
        rank = jnp.sum(jnp.where(onehot, before + run, 0.0), axis=-1, keepdims=True).astype(jnp.int32)
        run = run + jnp.sum(jnp.where(onehot, 1.0, 0.0), axis=0, keepdims=True)
        idx_out = jnp.where(lane == k, sel, idx_out)
        idx_out = jnp.where(lane == TOP_K + k, rank, idx_out)
        val_out = jnp.where(lane == k, e, val_out)
        logits = jnp.where(onehot, -jnp.inf, logits)
    run_ref[...] = run
    cnt_ref[...] = run
    idx_ref[...] = idx_out
    gate_ref[...] = val_out / denom


def router(x, g, shift, scale, w_router, b_router, *, seg_rows, rows, row_off=0, tm=ROW_TILE):
    m, d = rows, x.shape[1]
    n_seg = shift.shape[0]
    n_e = w_router.shape[1]
    w_pad = jnp.zeros((d, LANES), F32).at[:, :n_e].set(w_router)
    w_hi = w_pad.astype(BF16)
    w_lo = (w_pad - w_hi.astype(F32)).astype(BF16)
    b_pad = jnp.full((1, LANES), -jnp.inf, F32).at[0, :n_e].set(b_router)
    blk_off = row_off // tm
    seg = lambda i: (jnp.minimum((blk_off + i) * tm // seg_rows, n_seg - 1), 0, 0)
    return pl.pallas_call(
        _router_kernel,
        grid=(m // tm,),
        in_specs=[pl.BlockSpec((tm, d), lambda i: (blk_off + i, 0)),
                  pl.BlockSpec((1, d), lambda i: (0, 0)),
                  pl.BlockSpec((1, 1, d), seg),
                  pl.BlockSpec((1, 1, d), seg),
                  pl.BlockSpec((d, LANES), lambda i: (0, 0)),
                  pl.BlockSpec((d, LANES), lambda i: (0, 0)),
                  pl.BlockSpec((1, LANES), lambda i: (0, 0))],
        out_specs=[pl.BlockSpec((tm, d), lambda i: (i, 0)),
                   pl.BlockSpec((tm, LANES), lambda i: (i, 0)),
                   pl.BlockSpec((tm, LANES), lambda i: (i, 0)),
                   pl.BlockSpec((1, LANES), lambda i: (0, 0))],
        out_shape=[jax.ShapeDtypeStruct((m, d), BF16),
                   jax.ShapeDtypeStruct((m, LANES), jnp.int32),
                   jax.ShapeDtypeStruct((m, LANES), F32),
                   jax.ShapeDtypeStruct((1, LANES), F32)],
        scratch_shapes=[pltpu.VMEM((1, LANES), F32)],
        compiler_params=_cparams(1),
        name="router",
    )(x, g, shift, scale, w_hi, w_lo, b_pad)


def _expert_kernel(blk_e_ref, n_used_ref, x_ref, wgu_ref, bgu_ref, wd_ref, bd_ref, o_ref, wgu_s, wd_s, *, f):
    i = pl.program_id(0)
    used = i < n_used_ref[0]

    @pl.when(used & ((i == 0) | (blk_e_ref[i] != blk_e_ref[jnp.maximum(i - 1, 0)])))
    def _():
        wgu_s[...] = wgu_ref[0].astype(BF16)
        wd_s[...] = wd_ref[0].astype(BF16)

    @pl.when(used)
    def _():
        gu = jnp.dot(x_ref[...], wgu_s[...], preferred_element_type=F32) + bgu_ref[0]
        g = jnp.minimum(gu[:, :f], SWIGLU_LIMIT)
        u = jnp.clip(gu[:, f:], -SWIGLU_LIMIT, SWIGLU_LIMIT)
        act = (u + 1) * (g * jax.nn.sigmoid(SWIGLU_ALPHA * g))
        y = jnp.dot(act.astype(BF16), wd_s[...], preferred_element_type=F32) + bd_ref[0]
        o_ref[...] = y.astype(o_ref.dtype)

    @pl.when(i >= n_used_ref[0])
    def _():
        o_ref[...] = jnp.zeros_like(o_ref)


def expert_mlp(xs, blk_e, n_used, w_gu, b_gu, w_down, b_down, *, layer, bm=MOE_ROWS):
    n_slot, d = xs.shape
    _, n_e, _, f2 = w_gu.shape
    f = f2 // 2
    return pl.pallas_call(
        functools.partial(_expert_kernel, f=f),
        grid_spec=pltpu.PrefetchScalarGridSpec(
            num_scalar_prefetch=2,
            grid=(n_slot // bm,),
            in_specs=[pl.BlockSpec((bm, d), lambda i, be, nu: (i, 0)),
                      pl.BlockSpec((None, 1, d, f2), lambda i, be, nu: (layer, be[i], 0, 0)),
                      pl.BlockSpec((1, 1, f2), lambda i, be, nu: (be[i], 0, 0)),
                      pl.BlockSpec((None, 1, f, d), lambda i, be, nu: (layer, be[i], 0, 0)),
                      pl.BlockSpec((1, 1, d), lambda i, be, nu: (be[i], 0, 0))],
            out_specs=pl.BlockSpec((bm, d), lambda i, be, nu: (i, 0)),
            scratch_shapes=[pltpu.VMEM((d, f2), BF16), pltpu.VMEM((f, d), BF16)]),
        out_shape=jax.ShapeDtypeStruct((n_slot, d), BF16),
        compiler_params=pltpu.CompilerParams(dimension_semantics=("arbitrary",),
                                             vmem_limit_bytes=EXPERT_VMEM_LIMIT),
        name="expert_mlp",
    )(blk_e, n_used, xs, w_gu, b_gu.reshape(n_e, 1, f2), w_down, b_down.reshape(n_e, 1, d))


def _combine_kernel(res_ref, y_ref, w_ref, gate_ref, o_ref):
    w = w_ref[...]
    y = w[:, 0:1] * y_ref[0].astype(F32)
    for k in range(1, TOP_K):
        y = y + w[:, k:k + 1] * y_ref[k].astype(F32)
    o_ref[...] = res_ref[...] + gate_ref[0] * y


def moe_dispatch(x, g, shift, scale, w_router, b_router, *, seg_rows, rows, row_off, bm=MOE_ROWS):
    m = rows
    n_e = w_router.shape[1]
    h, route, gates, counts = router(x, g, shift, scale, w_router, b_router, seg_rows=seg_rows, rows=rows,
                                     row_off=row_off)
    idx = route[:, :TOP_K]
    rank = route[:, TOP_K:2 * TOP_K]
    n_asg = m * TOP_K
    counts = counts[0, :n_e].astype(jnp.int32)
    padded = (counts + bm - 1) // bm * bm
    pad_end = jnp.cumsum(padded)
    pad_start = pad_end - padded
    start_of = jnp.sum(jnp.where(idx[:, :, None] == jnp.arange(n_e, dtype=jnp.int32), pad_start, 0), axis=-1)
    dest = (start_of + rank).reshape(-1)
    n_blk = -(-(n_asg + n_e * (bm - 1)) // bm)
    n_slot = n_blk * bm
    tok = (jnp.arange(n_asg, dtype=jnp.int32) // TOP_K)
    slot_tok = (jnp.arange(n_slot, dtype=jnp.int32) % m).at[dest].set(tok, unique_indices=True)
    blk_start = jnp.arange(n_blk, dtype=jnp.int32) * bm
    blk_e = jnp.minimum(jnp.sum((pad_end[None, :] <= blk_start[:, None]).astype(jnp.int32), axis=1), n_e - 1)
    n_used = (pad_end[-1] // bm).astype(jnp.int32).reshape(1)
    xs = h.at[slot_tok].get(mode="promise_in_bounds")
    dest_km = dest.reshape(m, TOP_K).T.reshape(-1)
    return xs, blk_e, n_used, dest_km, gates


def moe_combine(x, y_slot, dest_km, gates, gate, *, seg_rows, rows, row_off):
    m, d = rows, x.shape[1]
    y_tok = y_slot.at[dest_km].get(mode="promise_in_bounds").reshape(TOP_K, m, d)
    tm = 256
    n_seg = gate.shape[0]
    blk_off = row_off // tm
    seg = lambda i: (jnp.minimum((blk_off + i) * tm // seg_rows, n_seg - 1), 0, 0)
    return pl.pallas_call(
        _combine_kernel,
        grid=(m // tm,),
        in_specs=[pl.BlockSpec((tm, d), lambda i: (blk_off + i, 0)),
                  pl.BlockSpec((TOP_K, tm, d), lambda i: (0, i, 0)),
                  pl.BlockSpec((tm, LANES), lambda i: (i, 0)),
                  pl.BlockSpec((1, 1, d), seg)],
        out_specs=pl.BlockSpec((tm, d), lambda i: (blk_off + i, 0)),
        out_shape=jax.ShapeDtypeStruct(x.shape, F32),
        input_output_aliases={0: 0},
        compiler_params=_cparams(1),
        name="moe_combine",
    )(x, y_tok, gates, gate)


LC_R2 = 128


@functools.lru_cache(maxsize=None)
def _dft_small_mats(n):
    nn = 2 * n
    ang = 2.0 * np.pi * ((np.arange(nn)[:, None] * np.arange(nn)[None, :]) % nn) / nn
    fwd = np.concatenate([np.cos(ang), -np.sin(ang)], 0)
    inv = np.concatenate([np.cos(ang[:n]), -np.sin(ang[:n])], 1) / nn
    return jnp.asarray(fwd, BF16), jnp.asarray(inv, BF16)


def _plain_mm_kernel(a_ref, b_ref, o_ref):
    o_ref[...] = jnp.dot(a_ref[...], b_ref[...].astype(BF16), preferred_element_type=F32)


def plain_mm(a, b, tn=512):
    m, k = a.shape
    n = b.shape[1]
    tn = _col_tile(n, tn)
    return pl.pallas_call(
        _plain_mm_kernel,
        grid=(n // tn,),
        in_specs=[pl.BlockSpec((m, k), lambda j: (0, 0)), pl.BlockSpec((k, tn), lambda j: (0, j))],
        out_specs=pl.BlockSpec((m, tn), lambda j: (0, j)),
        out_shape=jax.ShapeDtypeStruct((m, n), F32),
        compiler_params=_cparams(1),
        name="plain_mm",
    )(a, b)


def _short_conv_kernel(u_ref, gate_ref, h_ref, skip_ref, fwd_ref, inv_ref, o_ref):
    u = u_ref[...]
    x = jnp.dot(fwd_ref[...], u.astype(BF16), preferred_element_type=F32)
    h = h_ref[...]
    nn = x.shape[0] // 2
    xr, xi, hr, hi = x[:nn], x[nn:], h[:nn], h[nn:]
    prod = jnp.concatenate([xr * hr - xi * hi, xr * hi + xi * hr], axis=0).astype(BF16)
    y = jnp.dot(inv_ref[...], prod, preferred_element_type=F32)
    o_ref[...] = (gate_ref[...].astype(F32) * (y + skip_ref[...] * u.astype(F32))).astype(o_ref.dtype)


def short_long_conv(u_arr, u_col, gate_arr, gate_col, spec, skip, fwd, inv, *, batch, seq_len, tl=512):
    d = spec.shape[1]
    tl = _col_tile(d, tl)
    uc, gc = u_col // tl, gate_col // tl
    nn2 = spec.shape[0]
    return pl.pallas_call(
        _short_conv_kernel,
        grid=(batch, d // tl),
        in_specs=[pl.BlockSpec((seq_len, tl), lambda b, j: (b, uc + j)),
                  pl.BlockSpec((seq_len, tl), lambda b, j: (b, gc + j)),
                  pl.BlockSpec((nn2, tl), lambda b, j: (0, j)),
                  pl.BlockSpec((1, tl), lambda b, j: (0, j)),
                  pl.BlockSpec((nn2, seq_len), lambda b, j: (0, 0)),
                  pl.BlockSpec((seq_len, nn2), lambda b, j: (0, 0))],
        out_specs=pl.BlockSpec((seq_len, tl), lambda b, j: (b, j)),
        out_shape=jax.ShapeDtypeStruct((batch * seq_len, d), BF16),
        compiler_params=_cparams(2),
        name="short_long_conv",
    )(u_arr, gate_arr, spec, skip, fwd[:, :seq_len], inv)


def _hyena_taps(n, d, f_w1, f_b1, f_w2, f_b2, f_w3, f_b3, f_freq, f_w4):
    hp = lax.Precision.HIGHEST
    lin = jnp.linspace(0.0, 1.0, n, dtype=F32)
    idx = jnp.arange(n, dtype=F32)
    bands = jnp.linspace(1e-4, HY_BANDS - 1, HY_BANDS, dtype=F32)[None, :]
    deltas = jnp.abs(jnp.linspace(math.log(HY_DECAY_TARGET) / HY_FAST_DECAY,
                                  math.log(HY_DECAY_TARGET) / HY_SLOW_DECAY, d, dtype=F32))

    def branch(t, pos, direction):
        t = t[:, None]
        ang = (2.0 * math.pi / n) * pos[:, None]
        emb = jnp.concatenate([t, jnp.cos(bands * ang), -jnp.sin(bands * ang)], axis=-1)
        a = jnp.sin(f_freq * (jnp.dot(emb, f_w1, precision=hp) + f_b1))
        a = jnp.sin(f_freq * (jnp.dot(a, f_w2, precision=hp) + f_b2))
        a = jnp.sin(f_freq * (jnp.dot(a, f_w3, precision=hp) + f_b3))
        decay = jnp.exp(-t * deltas[None, :])
        w4 = f_w4.reshape(f_w4.shape[0], HY_ORDER, 2, d)[:, :, direction]
        return [jnp.dot(a, w4[:, o], precision=hp) * decay for o in range(HY_ORDER)]

    fwd = branch(lin, idx, 0)
    bwd = branch(lin[:0:-1], idx[:0:-1], 1)
    zero = jnp.zeros((1, d), F32)
    return [jnp.concatenate([fwd[o], zero, bwd[o]], axis=0) for o in range(HY_ORDER)]


HC = 128
HC_PAIR = 2
LC_TL = 16
HY_JN = 2


@functools.lru_cache(maxsize=None)
def _hy_mats(n):
    nn = 2 * n
    r1 = nn // LC_R2
    tau = 2.0 * np.pi
    k1 = np.arange(r1)
    t = LC_R2 * np.arange(r1)[None, :] + np.arange(LC_R2)[:, None]
    ang1 = tau * ((k1[None, :, None] * t[:, None, :]) % nn) / nn
    f1 = np.empty((LC_R2, 2 * r1, r1))
    f1[:, 0::2] = np.cos(ang1)
    f1[:, 1::2] = -np.sin(ang1)
    ang2 = tau * ((np.arange(LC_R2)[:, None] * np.arange(LC_R2)[None, :]) % LC_R2) / LC_R2
    c, s = np.cos(ang2), np.sin(ang2)
    f2 = np.empty((2 * LC_R2, 2 * LC_R2))
    f2[:LC_R2, 0::2], f2[:LC_R2, 1::2] = c, s
    f2[LC_R2:, 0::2], f2[LC_R2:, 1::2] = -s, c
    f3 = np.empty((2 * LC_R2, 2 * LC_R2))
    f3[0::2, :LC_R2], f3[0::2, LC_R2:] = c, -s
    f3[1::2, :LC_R2], f3[1::2, LC_R2:] = s, c
    m = LC_R2 * np.arange(r1 // 2)[None, :] + np.arange(LC_R2)[:, None]
    ang4 = tau * ((m[:, :, None] * k1[None, None, :]) % nn) / nn
    f4 = np.empty((LC_R2, r1 // 2, 2 * r1))
    f4[:, :, 0::2] = np.cos(ang4) / nn
    f4[:, :, 1::2] = -np.sin(ang4) / nn
    return tuple(jnp.asarray(a, BF16) for a in (f1, f2, f3, f4))


def _hy1_kernel(x_ref, f_ref, o_ref, *, tl_n, nh):
    c = pl.program_id(2)
    for tl in range(tl_n):
        xs = jnp.concatenate([x_ref[h, pl.ds(c * tl_n + tl, nh, stride=LC_R2), :] for h in range(HC_PAIR)],
                             axis=1).astype(BF16)
        a = jnp.dot(f_ref[tl], xs, preferred_element_type=F32)
        packed = pltpu.bitcast(a.astype(BF16), jnp.uint32)
        for h in range(HC_PAIR):
            for j in range(packed.shape[0] // 8):
                o_ref[h, j, tl * 8:(tl + 1) * 8, :] = packed[j * 8:(j + 1) * 8, h * HC:(h + 1) * HC]


def hy_stage1(x, f1, *, blk_off, n_blk, batch, rows_hi):
    r2 = f1.shape[1]
    rows = rows_hi * LC_R2
    tl_n = LC_TL
    return pl.pallas_call(
        functools.partial(_hy1_kernel, tl_n=tl_n, nh=rows_hi),
        grid=(n_blk // HC_PAIR, batch, LC_R2 // tl_n),
        in_specs=[pl.BlockSpec((HC_PAIR, rows, HC), lambda p, b, c: (blk_off // HC_PAIR + p, b, 0)),
                  pl.BlockSpec((tl_n, r2, rows_hi), lambda p, b, c: (c, 0, 0))],
        out_specs=pl.BlockSpec((HC_PAIR, None, r2 // 16, tl_n * 8, HC), lambda p, b, c: (p, b, 0, c, 0)),
        out_shape=jax.ShapeDtypeStruct((n_blk, batch, r2 // 16, LC_R2 * 8, HC), jnp.uint32),
        compiler_params=_cparams(3),
        name="hy_stage1",
    )(x, f1[:, :, :rows_hi])


def _hy2_kernel(a_ref, f2_ref, *rest, spectrum_only):
    if spectrum_only:
        (o_ref,) = rest
    else:
        h_ref, f3_ref, o_ref = rest
    for jk in range(8 * a_ref.shape[1]):
        jj, kk = divmod(jk, 8)
        w = jnp.concatenate([a_ref[h, jj, pl.ds(kk, LC_R2, stride=8), :] for h in range(HC_PAIR)], axis=1)
        a_in = pltpu.bitcast(w, BF16)
        x = jnp.dot(f2_ref[...], a_in, preferred_element_type=F32)
        if spectrum_only:
            o_ref[jk] = x.astype(o_ref.dtype)
            continue
        hs = h_ref[jk].astype(F32)
        xr, xi = x[:LC_R2], x[LC_R2:]
        hr, hi = hs[:LC_R2], hs[LC_R2:]
        prod = jnp.concatenate([xr * hr - xi * hi, xr * hi + xi * hr], axis=0).astype(BF16)
        g = jnp.dot(f3_ref[...], prod, preferred_element_type=F32)
        packed = pltpu.bitcast(g.astype(BF16), jnp.uint32)
        for h in range(HC_PAIR):
            o_ref[h, jj, pl.ds(kk, LC_R2, stride=8), :] = packed[:, h * HC:(h + 1) * HC]


def hy_spectrum(a, f2):
    cb, _, nj, rows, _ = a.shape
    jn = HY_JN if nj % HY_JN == 0 else 1
    mat = pl.BlockSpec((2 * LC_R2, 2 * LC_R2), lambda j, p: (0, 0))
    return pl.pallas_call(
        functools.partial(_hy2_kernel, spectrum_only=True),
        grid=(nj // jn, cb // HC_PAIR),
        in_specs=[pl.BlockSpec((HC_PAIR, None, jn, rows, HC), lambda j, p: (p, 0, j, 0, 0)), mat],
        out_specs=pl.BlockSpec((None, 8 * jn, 2 * LC_R2, HC_PAIR * HC), lambda j, p: (p, j, 0, 0)),
        out_shape=jax.ShapeDtypeStruct((cb // HC_PAIR, nj * 8, 2 * LC_R2, HC_PAIR * HC), BF16),
        compiler_params=_cparams(2),
        name="hy_spectrum",
    )(a, f2)


def hy_stage23(a, spec, f2, f3):
    cb, bsz, nj, rows, _ = a.shape
    jn = HY_JN if nj % HY_JN == 0 else 1
    mat = pl.BlockSpec((2 * LC_R2, 2 * LC_R2), lambda j, p, b: (0, 0))
    blk = pl.BlockSpec((HC_PAIR, None, jn, rows, HC), lambda j, p, b: (p, b, j, 0, 0))
    return pl.pallas_call(
        functools.partial(_hy2_kernel, spectrum_only=False),
        grid=(nj // jn, cb // HC_PAIR, bsz),
        in_specs=[blk, mat,
                  pl.BlockSpec((None, 8 * jn, 2 * LC_R2, HC_PAIR * HC), lambda j, p, b: (p, j, 0, 0)), mat],
        out_specs=blk,
        out_shape=jax.ShapeDtypeStruct(a.shape, jnp.uint32),
        compiler_params=_cparams(3),
        name="hy_stage23",
    )(a, f2, spec, f3)


def _hy4_kernel(g_ref, f_ref, o_ref, *, tl_n, nh):
    c = pl.program_id(2)
    for tl in range(tl_n):
        w = jnp.concatenate([g_ref[h, :, tl * 8:(tl + 1) * 8, :].reshape(-1, HC) for h in range(HC_PAIR)], axis=1)
        g_in = pltpu.bitcast(w, BF16)
        y = jnp.dot(f_ref[tl], g_in, preferred_element_type=F32)
        for h in range(HC_PAIR):
            o_ref[h, pl.ds(c * tl_n + tl, nh, stride=LC_R2), :] = y[:, h * HC:(h + 1) * HC]


def hy_stage4(g, f4, *, seq_len):
    cb, bsz, nj, _, _ = g.shape
    nh = seq_len // LC_R2
    tl_n = LC_TL
    return pl.pallas_call(
        functools.partial(_hy4_kernel, tl_n=tl_n, nh=nh),
        grid=(cb // HC_PAIR, bsz, LC_R2 // tl_n),
        in_specs=[pl.BlockSpec((HC_PAIR, None, nj, tl_n * 8, HC), lambda p, b, c: (p, b, 0, c, 0)),
                  pl.BlockSpec((tl_n, nh, 16 * nj), lambda p, b, c: (c, 0, 0))],
        out_specs=pl.BlockSpec((HC_PAIR, seq_len, HC), lambda p, b, c: (p, b, 0)),
        out_shape=jax.ShapeDtypeStruct((cb, bsz * seq_len, HC), F32),
        compiler_params=_cparams(3),
        name="hy_stage4",
    )(g, f4)


def _hy_gate_kernel(y_ref, u_ref, gate_ref, skip_ref, o_ref):
    o_ref[...] = (gate_ref[...] * (y_ref[...] + skip_ref[...] * u_ref[...])).astype(o_ref.dtype)


def hy_gate(y, u_arr, u_off, gate_arr, gate_off, skip, *, natural, tr=4096):
    cb, rows, _ = y.shape
    tr = min(tr, rows)
    if natural:
        out_spec = pl.BlockSpec((tr, HC), lambda i, r: (r, i))
        out_shape = jax.ShapeDtypeStruct((rows, cb * HC), BF16)
    else:
        out_spec = pl.BlockSpec((None, tr, HC), lambda i, r: (i, r, 0))
        out_shape = jax.ShapeDtypeStruct((cb, rows, HC), F32)
    return pl.pallas_call(
        _hy_gate_kernel,
        grid=(cb, rows // tr),
        in_specs=[pl.BlockSpec((None, tr, HC), lambda i, r: (i, r, 0)),
                  pl.BlockSpec((None, tr, HC), lambda i, r: (u_off + i, r, 0)),
                  pl.BlockSpec((None, tr, HC), lambda i, r: (gate_off + i, r, 0)),
                  pl.BlockSpec((None, 1, HC), lambda i, r: (i, 0, 0))],
        out_specs=out_spec,
        out_shape=out_shape,
        compiler_params=_cparams(2),
        name="hy_gate",
    )(y, u_arr, gate_arr, skip.reshape(cb, 1, HC))


def _hy_taps_kernel(a_ref, t_ref, w_ref, dl_ref, o_ref):
    a = a_ref[...]
    a_hi = a.astype(BF16)
    a_lo = (a - a_hi.astype(F32)).astype(BF16)
    w = w_ref[...]
    w_hi = w.astype(BF16)
    w_lo = (w - w_hi.astype(F32)).astype(BF16)
    h = (jnp.dot(a_hi, w_hi, preferred_element_type=F32) + jnp.dot(a_hi, w_lo, preferred_element_type=F32)
         + jnp.dot(a_lo, w_hi, preferred_element_type=F32))
    o_ref[...] = h * jnp.exp(-t_ref[...] * dl_ref[...])


def hy_taps(a_all, t_all, w4, deltas, *, seq_len, order, tr=2048):
    nrow, width = a_all.shape
    d = deltas.shape[0]
    tr = min(tr, seq_len)
    per_dir = seq_len // tr
    return pl.pallas_call(
        _hy_taps_kernel,
        grid=(d // HC, nrow // tr),
        in_specs=[pl.BlockSpec((tr, width), lambda i, r: (r, 0)),
                  pl.BlockSpec((tr, 1), lambda i, r: (r, 0)),
                  pl.BlockSpec((None, None, width, HC), lambda i, r: (order, r // per_dir, 0, i)),
                  pl.BlockSpec((1, HC), lambda i, r: (0, i))],
        out_specs=pl.BlockSpec((None, tr, HC), lambda i, r: (i, r, 0)),
        out_shape=jax.ShapeDtypeStruct((d // HC, nrow, HC), F32),
        compiler_params=_cparams(2),
        name="hy_taps",
    )(a_all, t_all, w4.reshape(width, HY_ORDER, 2, d).transpose(1, 2, 0, 3), deltas.reshape(1, d))


def _hyena_filter_hidden(n, f_w1, f_b1, f_w2, f_b2, f_w3, f_b3, f_freq):
    hp = lax.Precision.HIGHEST
    lin = jnp.linspace(0.0, 1.0, n, dtype=F32)
    idx = jnp.arange(n, dtype=F32)
    bands = jnp.linspace(1e-4, HY_BANDS - 1, HY_BANDS, dtype=F32)[None, :]
    t = jnp.concatenate([lin, jnp.zeros((1,), F32), lin[:0:-1]])[:, None]
    pos = jnp.concatenate([idx, jnp.zeros((1,), F32), idx[:0:-1]])[:, None]
    ang = (2.0 * math.pi / n) * pos
    emb = jnp.concatenate([t, jnp.cos(bands * ang), -jnp.sin(bands * ang)], axis=-1)
    a = jnp.sin(f_freq * (jnp.dot(emb, f_w1, precision=hp) + f_b1))
    a = jnp.sin(f_freq * (jnp.dot(a, f_w2, precision=hp) + f_b2))
    a = jnp.sin(f_freq * (jnp.dot(a, f_w3, precision=hp) + f_b3))
    keep = (jnp.arange(2 * n) != n)[:, None]
    return jnp.where(keep, a, 0.0), t


def _hyena_deltas(d):
    return jnp.abs(jnp.linspace(math.log(HY_DECAY_TARGET) / HY_FAST_DECAY,
                                math.log(HY_DECAY_TARGET) / HY_SLOW_DECAY, d, dtype=F32))


def hyena_long_convs2(zc, a_all, t_all, w4, skip, *, batch, seq_len, d):
    cb = d // HC
    f1, f2, f3, f4 = _hy_mats(seq_len)
    nh = seq_len // LC_R2
    deltas = _hyena_deltas(d)
    u, u_off = zc, 2 * cb
    for o in range(HY_ORDER):
        taps = hy_taps(a_all, t_all, w4, deltas, seq_len=seq_len, order=o)
        spec = hy_spectrum(hy_stage1(taps, f1, blk_off=0, n_blk=cb, batch=1, rows_hi=2 * nh), f2)
        a = hy_stage1(u, f1, blk_off=u_off, n_blk=cb, batch=batch, rows_hi=nh)
        y = hy_stage4(hy_stage23(a, spec, f2, f3), f4, seq_len=seq_len)
        u = hy_gate(y, u, u_off, zc, o * cb, skip[o], natural=(o == HY_ORDER - 1))
        u_off = 0
    return u


def _final_norm_kernel(x_ref, g_ref, o_ref):
    x = x_ref[...]
    o_ref[...] = x * lax.rsqrt(jnp.mean(x * x, axis=-1, keepdims=True) + NORM_EPS) * g_ref[...]


def final_norm(x, g, *, rows, tm=ROW_TILE):
    d = x.shape[1]
    return pl.pallas_call(
        _final_norm_kernel,
        grid=(rows // tm,),
        in_specs=[pl.BlockSpec((tm, d), lambda i: (i, 0)), pl.BlockSpec((1, d), lambda i: (0, 0))],
        out_specs=pl.BlockSpec((tm, d), lambda i: (i, 0)),
        out_shape=jax.ShapeDtypeStruct((rows, d), F32),
        compiler_params=_cparams(1),
        name="final_norm",
    )(x, g)


def _rope_tables(n_lat, seq_len, n_rows):
    t = jnp.arange(n_rows, dtype=jnp.int32)
    row = ((t % seq_len) // GRID_W).astype(F32)
    col = (t % GRID_W).astype(F32)
    lane = jnp.arange(LANES)
    dd = lane % HEAD_DIM
    quarter = HEAD_DIM // 4
    inv_freq = ROPE_BASE ** (-(dd % quarter).astype(F32) / quarter)
    pos = jnp.where(dd[None, :] < HEAD_DIM // 2, row[:, None], col[:, None])
    ang = pos * inv_freq[None, :]
    sign = jnp.where((dd % (HEAD_DIM // 2)) < quarter, -1.0, 1.0).astype(F32)
    lat = (t < n_lat)[:, None]
    cos = jnp.where(lat, jnp.cos(ang), 1.0)
    sin = jnp.where(lat, jnp.sin(ang) * sign[None, :], 0.0)
    return cos, sin


def _wa_head_perm(n_heads):
    order = []
    for p in range(n_heads // (2 * WA_GROUP)):
        for g in range(WA_GROUP):
            order += [2 * WA_GROUP * p + g, 2 * WA_GROUP * p + WA_GROUP + g]
    cols = jnp.asarray(order, jnp.int32)[:, None] * HEAD_DIM + jnp.arange(HEAD_DIM, dtype=jnp.int32)[None, :]
    return cols.reshape(-1)


def kernel(x, c, ctx, c_ctx, w_ada, b_ada, g_mix, g_ffn, hy_w_in, hy_b_in, hy_w_short, hy_b_short, hy_f_w1, hy_f_b1, hy_f_w2, hy_f_b2, hy_f_w3, hy_f_b3, hy_f_freq, hy_f_w4, hy_skip, hy_w_out, hy_b_out, cf_w_pw1, cf_b_pw1, cf_w_dw, cf_b_dw, cf_ln_g, cf_ln_b, cf_w_pw2, cf_b_pw2, wa_w_qkv, wa_w_o, wa_sinks, na_w_qkv, na_w_o, na_rpb, moe_w_router, moe_b_router, moe_w_gu, moe_b_gu, moe_w_down, moe_b_down, g_final):
    bsz, n, d = x.shape
    n_ctx = ctx.shape[1]
    depth = w_ada.shape[0]
    n_lat = bsz * n
    m_all = n_lat + bsz * n_ctx

    cond = jnp.zeros((8, d), F32).at[:bsz].set(jax.nn.silu(c)).at[bsz].set(jax.nn.silu(c_ctx))
    mods = adaln(cond, w_ada, b_ada)
    xu = jnp.concatenate([x.reshape(n_lat, d), ctx.reshape(bsz * n_ctx, d)], axis=0)
    zeros_d = jnp.zeros((1, d), F32)

    for i in range(depth):
        kind, j = i % N_MIXERS, i // N_MIXERS
        last = i == depth - 1
        mod = [mods[i, :bsz + 1, k * d:(k + 1) * d].reshape(bsz + 1, 1, d) for k in range(6)]
        m_out = n_lat if last else m_all
        gm = g_mix[i].reshape(1, d)
        need_ctx_in = (not last) or kind >= 2
        m_in = m_all if need_ctx_in else n_lat
        x_in = xu[:m_in]

        if kind == 0:
            z = nm_matmul(x_in, gm, mod[0], mod[1], hy_w_in[j].astype(BF16), hy_b_in[j].reshape(1, -1),
                          seg_rows=n)
            filt_w = (hy_f_w1[j], hy_f_b1[j], hy_f_w2[j], hy_f_b2[j], hy_f_w3[j], hy_f_b3[j], hy_f_freq[j],
                      hy_f_w4[j])
            w_s, b_s = hy_w_short[j], hy_b_short[j].reshape(1, -1)
            w_o, b_o = hy_w_out[j].astype(BF16), hy_b_out[j].reshape(1, d)
            zc = dwconv(z, w_s, b_s, start=0, total=n_lat, seq_len=n, tc=HC, blocked=True, rows=2048,
                        out_dtype=F32)
            a_all, t_all = _hyena_filter_hidden(n, *filt_w[:-1])
            y = hyena_long_convs2(zc, a_all, t_all, hy_f_w4[j], hy_skip[j], batch=bsz, seq_len=n, d=d)
            xu = mm_res(y, w_o, b_o, xu, mod[2], seg_rows=n)
            if m_in > n_lat:
                zcc = dwconv(z, w_s, b_s, start=n_lat, total=m_in - n_lat, seq_len=n_ctx, tc=512)
                taps_c = _hyena_taps(n_ctx, d, *filt_w)
                fwd, inv = _dft_small_mats(n_ctx)
                yc = short_long_conv(zcc, 2 * d, zcc, 0, plain_mm(fwd, taps_c[0]), hy_skip[j][0].reshape(1, d),
                                     fwd, inv, batch=bsz, seq_len=n_ctx)
                yc = short_long_conv(yc, 0, zcc, d, plain_mm(fwd, taps_c[1]), hy_skip[j][1].reshape(1, d),
                                     fwd, inv, batch=bsz, seq_len=n_ctx)
                if not last:
                    xu = mm_res(yc, w_o, b_o, xu, mod[2], seg_rows=n, row_off=n_lat)
            y = None
        elif kind == 1:
            w1 = cf_w_pw1[j].astype(BF16).reshape(d, 2, d).transpose(1, 0, 2)
            a = nm_matmul(x_in, gm, mod[0], mod[1], w1, cf_b_pw1[j].reshape(2, 1, d), seg_rows=n, mode="glu")
            conv = functools.partial(dwconv, a, cf_w_dw[j], cf_b_dw[j].reshape(1, d), post="ln_silu",
                                     ln=(cf_ln_g[j].reshape(1, d), cf_ln_b[j].reshape(1, d)), rows=64)
            y = conv(start=0, total=n_lat, seq_len=n, same_rows=True)
            if m_in > n_lat:
                y = conv(start=n_lat, total=m_in - n_lat, seq_len=n_ctx, same_rows=True, into=y)
            w_o, b_o = cf_w_pw2[j].astype(BF16), cf_b_pw2[j].reshape(1, d)
        elif kind == 2:
            perm = _wa_head_perm(d // HEAD_DIM)
            scale = HEAD_DIM ** -0.5
            w_qkv = jnp.concatenate([wa_w_qkv[j][:, :d][:, perm] * scale, wa_w_qkv[j][:, d:]], axis=1).astype(BF16)
            n_out = w_qkv.shape[1]
            qkv = nm_matmul(x_in, gm, mod[0], mod[1], w_qkv, jnp.zeros((1, n_out), F32), seg_rows=n,
                            mode="rope", rope=_rope_tables(n_lat, n, m_in), n_rope_cols=d + (n_out - d) // 2)
            y = window_attention(qkv, wa_sinks[j], batch=bsz, seq_len=n, ctx_len=n_ctx, d=d)
            w_o, b_o = wa_w_o[j][perm].astype(BF16), zeros_d
        else:
            scale = HEAD_DIM ** -0.5
            w_qkv = jnp.concatenate([na_w_qkv[j][:, :d] * scale, na_w_qkv[j][:, d:]], axis=1).astype(BF16)
            qkv = nm_matmul(x_in, gm, mod[0], mod[1], w_qkv, jnp.zeros((1, 3 * d), F32), seg_rows=n)
            y = neighbourhood_attention(qkv, na_rpb[j], batch=bsz, seq_len=n, ctx_len=n_ctx, d=d,
                                        with_ctx_out=not last)
            w_o, b_o = na_w_o[j].astype(BF16), zeros_d
        if y is not None:
            xu = mm_res(y, w_o, b_o, xu, mod[2], seg_rows=n, rows=m_out)

        ranges = ((0, m_out),)
        routed = [moe_dispatch(xu, g_ffn[i].reshape(1, d), mod[3], mod[4], moe_w_router[i], moe_b_router[i],
                               seg_rows=n, rows=cnt, row_off=off) for off, cnt in ranges]
        y_slots = [expert_mlp(xs, blk_e, n_used, moe_w_gu, moe_b_gu[i], moe_w_down, moe_b_down[i], layer=i)
                   for xs, blk_e, n_used, _, _ in routed]
        for (off, cnt), (_, _, _, dest_km, gates), y_slot in zip(ranges, routed, y_slots):
            xu = moe_combine(xu, y_slot, dest_km, gates, mod[5], seg_rows=n, rows=cnt, row_off=off)
    return final_norm(xu, g_final.reshape(1, d), rows=n_lat).reshape(bsz, n, d)
```

```python
import functools
import math

import jax
import jax.numpy as jnp
import numpy as np
from jax import lax
from jax.experimental import pallas as pl
from jax.experimental.pallas import tpu as pltpu

F32 = jnp.float32
BF16 = jnp.bfloat16

GRID_W = 64
N_MIXERS = 4
NORM_EPS = 1e-6
NEG_INF = -1e30
HEAD_DIM = 64
ROPE_BASE = 10000.0

HY_ORDER = 2
HY_BANDS = 16
HY_DECAY_TARGET = 1e-2
HY_FAST_DECAY = 0.3
HY_SLOW_DECAY = 1.5

WA_GROUP = 4
WA_WINDOW = 128
NA_WIN_ROWS = 8
NA_WIN_COLS = 16

TOP_K = 4
SWIGLU_LIMIT = 7.0
SWIGLU_ALPHA = 1.702

LANES = 128
ROW_TILE = 512
MOE_ROWS = 512
VMEM_LIMIT = 56 * 1024 * 1024
EXPERT_VMEM_LIMIT = 58 * 1024 * 1024


def _cparams(n_axes):
    return pltpu.CompilerParams(dimension_semantics=("arbitrary",) * n_axes,
                                vmem_limit_bytes=VMEM_LIMIT)


def _col_tile(n, pref=1024):
    t = min(pref, n)
    while n % t:
        t //= 2
    return t


def _adaln_kernel(c_ref, w_ref, b_ref, o_ref):
    w = w_ref[0]
    w_hi = w.astype(BF16)
    w_lo = (w - w_hi.astype(F32)).astype(BF16)
    c = c_ref[...]
    c_hi = c.astype(BF16)
    c_lo = (c - c_hi.astype(F32)).astype(BF16)
    acc = jnp.dot(c_hi, w_hi, preferred_element_type=F32)
    acc += jnp.dot(c_hi, w_lo, preferred_element_type=F32)
    acc += jnp.dot(c_lo, w_hi, preferred_element_type=F32)
    o_ref[0] = acc + b_ref[0]


def adaln(cond, w_ada, b_ada):
    depth, d, n6 = w_ada.shape
    tn = _col_tile(n6, 1024)
    return pl.pallas_call(
        _adaln_kernel,
        grid=(depth, n6 // tn),
        in_specs=[pl.BlockSpec((8, d), lambda l, j: (0, 0)),
                  pl.BlockSpec((1, d, tn), lambda l, j: (l, 0, j)),
                  pl.BlockSpec((1, 1, tn), lambda l, j: (l, 0, j))],
        out_specs=pl.BlockSpec((1, 8, tn), lambda l, j: (l, 0, j)),
        out_shape=jax.ShapeDtypeStruct((depth, 8, n6), F32),
        compiler_params=_cparams(2),
        name="adaln",
    )(cond, w_ada, b_ada.reshape(depth, 1, n6))


def _norm_mod(x, g, shift, scale):
    y = x * lax.rsqrt(jnp.mean(x * x, axis=-1, keepdims=True) + NORM_EPS)
    return (y * g) * (1 + scale) + shift


def _rope_tile(x, cos, sin):
    lane = lax.broadcasted_iota(jnp.int32, x.shape, 1)
    nxt = pltpu.roll(x, LANES - 16, axis=1)
    prv = pltpu.roll(x, 16, axis=1)
    partner = jnp.where((lane // 16) % 2 == 0, nxt, prv)
    return x * cos + partner * sin


def _nm_mm_kernel(x_ref, g_ref, sh_ref, sc_ref, w_ref, b_ref, *rest, mode, n_rope):
    if mode == "rope":
        cos_ref, sin_ref, o_ref, h_ref = rest
    else:
        o_ref, h_ref = rest
    j = pl.program_id(1)

    @pl.when(j == 0)
    def _():
        h_ref[...] = _norm_mod(x_ref[...], g_ref[...], sh_ref[0], sc_ref[0]).astype(BF16)

    h = h_ref[...]
    if mode == "glu":
        a = jnp.dot(h, w_ref[0], preferred_element_type=F32) + b_ref[0]
        gate = jnp.dot(h, w_ref[1], preferred_element_type=F32) + b_ref[1]
        o_ref[...] = (a * jax.nn.sigmoid(gate)).astype(o_ref.dtype)
        return
    acc = jnp.dot(h, w_ref[...], preferred_element_type=F32) + b_ref[...]
    if mode == "rope":
        @pl.when(j < n_rope)
        def _():
            cos = cos_ref[...]
            sin = sin_ref[...]
            for g in range(acc.shape[1] // LANES):
                sl = slice(g * LANES, (g + 1) * LANES)
                o_ref[:, sl] = _rope_tile(acc[:, sl], cos, sin).astype(o_ref.dtype)

        @pl.when(j >= n_rope)
        def _():
            o_ref[...] = acc.astype(o_ref.dtype)
    else:
        o_ref[...] = acc.astype(o_ref.dtype)


def nm_matmul(x, g, shift, scale, w, b, *, seg_rows, mode="plain", rope=None, n_rope_cols=0,
              tm=ROW_TILE, tn=None):
    m, d = x.shape
    n_seg = shift.shape[0]
    n = w.shape[-1]
    tn = tn or (_col_tile(math.gcd(n, n_rope_cols), 512) if mode == "rope" else _col_tile(n, 1024))
    seg = lambda i, j: (jnp.minimum(i * tm // seg_rows, n_seg - 1), 0, 0)
    in_specs = [pl.BlockSpec((tm, d), lambda i, j: (i, 0)),
                pl.BlockSpec((1, d), lambda i, j: (0, 0)),
                pl.BlockSpec((1, 1, d), seg),
                pl.BlockSpec((1, 1, d), seg)]
    if mode == "glu":
        in_specs += [pl.BlockSpec((2, d, tn), lambda i, j: (0, 0, j)),
                     pl.BlockSpec((2, 1, tn), lambda i, j: (0, 0, j))]
    else:
        in_specs += [pl.BlockSpec((d, tn), lambda i, j: (0, j)),
                     pl.BlockSpec((1, tn), lambda i, j: (0, j))]
    args = [x, g, shift, scale, w, b]
    if mode == "rope":
        in_specs += [pl.BlockSpec((tm, LANES), lambda i, j: (i, 0))] * 2
        args += list(rope)
    return pl.pallas_call(
        functools.partial(_nm_mm_kernel, mode=mode, n_rope=n_rope_cols // tn),
        grid=(m // tm, n // tn),
        in_specs=in_specs,
        out_specs=pl.BlockSpec((tm, tn), lambda i, j: (i, j)),
        out_shape=jax.ShapeDtypeStruct((m, n), BF16),
        scratch_shapes=[pltpu.VMEM((tm, d), BF16)],
        compiler_params=_cparams(2),
        name="nm_matmul_" + mode,
    )(*args)


def _mm_res_kernel(a_ref, w_ref, b_ref, res_ref, gate_ref, o_ref):
    if len(a_ref.shape) == 3:
        kc = a_ref.shape[2]
        acc = b_ref[...]
        for cb in range(a_ref.shape[0]):
            acc = acc + jnp.dot(a_ref[cb], w_ref[cb * kc:(cb + 1) * kc, :], preferred_element_type=F32)
    else:
        acc = jnp.dot(a_ref[...], w_ref[...], preferred_element_type=F32) + b_ref[...]
    o_ref[...] = res_ref[...] + gate_ref[0] * acc


def mm_res(a, w, b, res, gate, *, seg_rows, row_off=0, rows=None, tm=ROW_TILE, tn=None):
    if a.ndim == 3:
        m = rows or a.shape[1]
        k = a.shape[0] * a.shape[2]
        a_spec = pl.BlockSpec((a.shape[0], tm, a.shape[2]), lambda i, j: (0, i, 0))
    else:
        m, k = rows or a.shape[0], a.shape[1]
        a_spec = pl.BlockSpec((tm, k), lambda i, j: (i, 0))
    n = w.shape[1]
    n_seg = gate.shape[0]
    tn = tn or _col_tile(n, 1024)
    blk_off = row_off // tm
    seg = lambda i, j: (jnp.minimum((blk_off + i) * tm // seg_rows, n_seg - 1), 0, j)
    return pl.pallas_call(
        _mm_res_kernel,
        grid=(m // tm, n // tn),
        in_specs=[a_spec,
                  pl.BlockSpec((k, tn), lambda i, j: (0, j)),
                  pl.BlockSpec((1, tn), lambda i, j: (0, j)),
                  pl.BlockSpec((tm, tn), lambda i, j: (blk_off + i, j)),
                  pl.BlockSpec((1, 1, tn), seg)],
        out_specs=pl.BlockSpec((tm, tn), lambda i, j: (blk_off + i, j)),
        out_shape=jax.ShapeDtypeStruct(res.shape, F32),
        input_output_aliases={3: 0},
        compiler_params=_cparams(2),
        name="mm_res",
    )(a, w, b, res, gate)


CONV_HALO = 16


def _dwconv_kernel(prev_ref, x_ref, next_ref, w_ref, b_ref, *rest, taps, blocks_per_seq, post, sub, n_alias):
    if n_alias:
        n_tail = 3 if post == "ln_silu" else 2
        rest = rest[:-n_tail - 1] + rest[-n_tail:]
    if post == "ln_silu":
        g_ref, beta_ref, o_ref, win_ref, conv_ref = rest
    else:
        o_ref, win_ref = rest
    i = pl.program_id(0)
    pos = i % blocks_per_seq
    rows, c = x_ref.shape
    half = taps // 2
    zero = jnp.zeros((CONV_HALO, c), F32)
    win_ref[CONV_HALO:CONV_HALO + rows, :] = x_ref[...].astype(F32)

    @pl.when(pos == 0)
    def _():
        win_ref[0:CONV_HALO, :] = zero

    @pl.when(pos > 0)
    def _():
        win_ref[0:CONV_HALO, :] = prev_ref[...].astype(F32)

    @pl.when(pos == blocks_per_seq - 1)
    def _():
        win_ref[CONV_HALO + rows:, :] = zero

    @pl.when(pos < blocks_per_seq - 1)
    def _():
        win_ref[CONV_HALO + rows:, :] = next_ref[...].astype(F32)

    bias = b_ref[...]
    if post == "ln_silu":
        ext = rows + sub
        groups = {}
        for t in range(taps):
            off = CONV_HALO - half + t
            groups.setdefault(off % sub, []).append((t, off // sub))
        lw = min(2 * LANES, c)
        for lc in range(0, c, lw):
            lanes = slice(lc, lc + lw)
            y = jnp.broadcast_to(bias[:, lanes], (rows, lw))
            for s, members in sorted(groups.items()):
                acc = None
                for t, a in members:
                    term = pltpu.repeat(w_ref[t, :, lanes], ext // sub, axis=0) * win_ref[sub * a:sub * a + ext, lanes]
                    acc = term if acc is None else acc + term
                y = y + acc[s:s + rows]
            conv_ref[:, lanes] = y
        for s in range(rows // sub):
            acc = conv_ref[s * sub:(s + 1) * sub, :]
            mu = jnp.mean(acc, axis=-1, keepdims=True)
            xc = acc - mu
            var = jnp.mean(xc * xc, axis=-1, keepdims=True)
            y = xc * lax.rsqrt(var + NORM_EPS) * g_ref[...] + beta_ref[...]
            o_ref[s * sub:(s + 1) * sub, :] = (y * jax.nn.sigmoid(y)).astype(o_ref.dtype)
        return
    for s in range(rows // sub):
        base = CONV_HALO + s * sub - half
        acc = jnp.broadcast_to(bias, (sub, c))
        for t in range(taps):
            acc = acc + w_ref[t] * win_ref[base + t:base + t + sub, :]
        o_ref[s * sub:(s + 1) * sub, :] = acc.astype(o_ref.dtype)


def dwconv(x, w, b, *, start, total, seq_len, post=None, ln=None, rows=256, tc=None, out_dtype=None,
           blocked=False, same_rows=False, into=None):
    m, c = x.shape
    taps = w.shape[0]
    tc = tc or c
    out_dtype = out_dtype or BF16
    sub = 8
    w = jnp.broadcast_to(w[:, None, :], (taps, sub, c))
    r = min(rows, seq_len)
    bps = seq_len // r
    hb = r // CONV_HALO
    off = start // r
    offh = start // CONV_HALO
    nh = m // CONV_HALO
    in_specs = [pl.BlockSpec((CONV_HALO, tc), lambda i, j: (jnp.maximum(offh + i * hb - 1, 0), j)),
                pl.BlockSpec((r, tc), lambda i, j: (off + i, j)),
                pl.BlockSpec((CONV_HALO, tc), lambda i, j: (jnp.minimum(offh + (i + 1) * hb, nh - 1), j)),
                pl.BlockSpec((taps, sub, tc), lambda i, j: (0, 0, j)),
                pl.BlockSpec((1, tc), lambda i, j: (0, j))]
    args = [x, x, x, w, b]
    if post == "ln_silu":
        in_specs += [pl.BlockSpec((1, tc), lambda i, j: (0, j))] * 2
        args += list(ln)
    aliases = {}
    if blocked:
        out_spec = pl.BlockSpec((None, r, tc), lambda i, j: (j, i, 0))
        out_shape = jax.ShapeDtypeStruct((c // tc, total, tc), out_dtype)
    elif same_rows:
        out_spec = pl.BlockSpec((r, tc), lambda i, j: (off + i, j))
        out_shape = jax.ShapeDtypeStruct((m, c), out_dtype)
        if into is not None:
            in_specs.append(pl.BlockSpec(memory_space=pl.ANY))
            args.append(into)
            aliases = {len(args) - 1: 0}
    else:
        out_spec = pl.BlockSpec((r, tc), lambda i, j: (i, j))
        out_shape = jax.ShapeDtypeStruct((total, c), out_dtype)
    return pl.pallas_call(
        functools.partial(_dwconv_kernel, taps=taps, blocks_per_seq=bps, post=post, sub=sub,
                          n_alias=len(aliases)),
        grid=(total // r, c // tc),
        in_specs=in_specs,
        out_specs=out_spec,
        out_shape=out_shape,
        input_output_aliases=aliases,
        scratch_shapes=[pltpu.VMEM((r + 2 * CONV_HALO, tc), F32)]
        + ([pltpu.VMEM((r, tc), F32)] if post == "ln_silu" else []),
        compiler_params=_cparams(2),
        name="dwconv%d" % taps,
    )(*args)


def _masked_halves(q, lane_lo):
    zero = jnp.zeros_like(q)
    return jnp.where(lane_lo, q, zero), jnp.where(lane_lo, zero, q)


def _softmax_pv(s_parts, v_parts, sink):
    m = s_parts[0].max(axis=-1, keepdims=True)
    for s in s_parts[1:]:
        m = jnp.maximum(m, s.max(axis=-1, keepdims=True))
    if sink is not None:
        m = jnp.maximum(m, sink)
    denom = jnp.exp(sink - m) if sink is not None else 0.0
    o = None
    for s, v in zip(s_parts, v_parts):
        p = jnp.exp(s - m)
        denom = denom + p.sum(axis=-1, keepdims=True)
        pv = jnp.dot(p.astype(BF16), v, preferred_element_type=F32)
        o = pv if o is None else o + pv
    return o / denom


def _nt_dot(a, b):
    return lax.dot_general(a, b, (((1,), (1,)), ((), ())), preferred_element_type=F32)


def _wattn_kernel(sink_ref, q_ref, *rest, local, seq_len, blk):
    if local:
        n_kb = (len(rest) - 3) // 2
        k_refs, v_refs = rest[:n_kb], rest[n_kb:2 * n_kb]
        kc_ref, vc_ref, o_ref = rest[2 * n_kb:]
    else:
        kc_ref, vc_ref, _, o_ref = rest
    p = pl.program_id(2)
    i = pl.program_id(1)
    lane_lo = lax.broadcasted_iota(jnp.int32, (blk, LANES), 1) < HEAD_DIM
    kc = kc_ref[...]
    vc = vc_ref[...]
    if local:
        kl = jnp.concatenate([r[...] for r in k_refs], axis=0)
        vl = jnp.concatenate([r[...] for r in v_refs], axis=0)
        n_keys = n_kb * WA_WINDOW
        qpos = i * blk + lax.broadcasted_iota(jnp.int32, (blk, n_keys), 0)
        kpos = i * blk - WA_WINDOW + lax.broadcasted_iota(jnp.int32, (blk, n_keys), 1)
        valid = (jnp.abs(kpos - qpos) <= WA_WINDOW) & (kpos >= 0) & (kpos < seq_len)
        valid = jnp.concatenate([valid] * WA_GROUP, axis=0)
    qa, qb = [], []
    for g in range(WA_GROUP):
        a, b = _masked_halves(q_ref[:, g * LANES:(g + 1) * LANES], lane_lo)
        qa.append(a)
        qb.append(b)
    outs = []
    for half, qs in enumerate((qa, qb)):
        qs = jnp.concatenate(qs, axis=0)
        sink = jnp.concatenate(
            [jnp.full((blk, 1), sink_ref[8 * p + 4 * half + g], F32) for g in range(WA_GROUP)], axis=0)
        s_parts, v_parts = [], []
        if local:
            s_parts.append(jnp.where(valid, _nt_dot(qs, kl), NEG_INF))
            v_parts.append(vl)
        s_parts.append(_nt_dot(qs, kc))
        v_parts.append(vc)
        outs.append(_softmax_pv(s_parts, v_parts, sink))
    for g in range(WA_GROUP):
        rows = slice(g * blk, (g + 1) * blk)
        o_ref[:, g * LANES:(g + 1) * LANES] = jnp.where(lane_lo, outs[0][rows], outs[1][rows]).astype(o_ref.dtype)


def window_attention(qkv, sinks, *, batch, seq_len, ctx_len, d, blk=512):
    n_pairs = d // (2 * WA_GROUP * HEAD_DIM)
    kcol = d // LANES
    vcol = kcol + n_pairs
    nblk = seq_len // blk
    kb_per_q = blk // WA_WINDOW
    n_kblk = seq_len // WA_WINDOW
    shifts = range(-1, kb_per_q + 1)
    cb0 = batch * seq_len // ctx_len
    qw = WA_GROUP * LANES
    kern = functools.partial(_wattn_kernel, seq_len=seq_len)
    smem = pl.BlockSpec(memory_space=pltpu.SMEM)

    def kspec(col0, shift):
        return pl.BlockSpec((WA_WINDOW, LANES),
                            lambda b, i, p: (b * n_kblk + jnp.clip(i * kb_per_q + shift, 0, n_kblk - 1), col0 + p))

    ctx_k = pl.BlockSpec((ctx_len, LANES), lambda b, i, p: (cb0 + b, kcol + p))
    ctx_v = pl.BlockSpec((ctx_len, LANES), lambda b, i, p: (cb0 + b, vcol + p))
    lat = pl.pallas_call(
        functools.partial(kern, local=True, blk=blk),
        grid=(batch, nblk, n_pairs),
        in_specs=[smem, pl.BlockSpec((blk, qw), lambda b, i, p: (b * nblk + i, p))]
        + [kspec(kcol, s) for s in shifts] + [kspec(vcol, s) for s in shifts] + [ctx_k, ctx_v],
        out_specs=pl.BlockSpec((blk, qw), lambda b, i, p: (b * nblk + i, p)),
        out_shape=jax.ShapeDtypeStruct((qkv.shape[0], d), BF16),
        compiler_params=_cparams(3),
        name="window_attn",
    )(sinks, qkv, *([qkv] * (2 * len(shifts) + 2)))
    return pl.pallas_call(
        functools.partial(kern, local=False, blk=ctx_len),
        grid=(batch, 1, n_pairs),
        in_specs=[smem, pl.BlockSpec((ctx_len, qw), lambda b, i, p: (cb0 + b, p)), ctx_k, ctx_v,
                  pl.BlockSpec(memory_space=pl.ANY)],
        out_specs=pl.BlockSpec((ctx_len, qw), lambda b, i, p: (cb0 + b, p)),
        out_shape=jax.ShapeDtypeStruct((qkv.shape[0], d), BF16),
        input_output_aliases={4: 0},
        compiler_params=_cparams(3),
        name="ctx_attn",
    )(sinks, qkv, qkv, qkv, lat)


def _nattn_kernel(q_ref, k0, k1, k2, v0, v1, v2, kc_ref, vc_ref, bias_ref, o_ref, ks_ref, vs_ref,
                  *, grid_rows, rows_per_blk):
    j = pl.program_id(2)
    blk = k0.shape[0]
    for t, (kr, vr) in enumerate(((k0, v0), (k1, v1), (k2, v2))):
        ks_ref[t * blk:(t + 1) * blk, :] = kr[...]
        vs_ref[t * blk:(t + 1) * blk, :] = vr[...]
    kc = kc_ref[...]
    vc = vc_ref[...]
    lane_lo = lax.broadcasted_iota(jnp.int32, (GRID_W, LANES), 1) < HEAD_DIM
    strip = NA_WIN_ROWS * GRID_W
    for r in range(rows_per_blk):
        row = j * rows_per_blk + r
        r0 = jnp.clip(row - NA_WIN_ROWS // 2, 0, grid_rows - NA_WIN_ROWS)
        start = pl.multiple_of((r0 - (j - 1) * rows_per_blk) * GRID_W, GRID_W)
        cls = row - r0
        qa, qb = _masked_halves(q_ref[r * GRID_W:(r + 1) * GRID_W, :], lane_lo)
        qs = jnp.concatenate([qa, qb], axis=0)
        kn = ks_ref[pl.ds(start, strip), :]
        vn = vs_ref[pl.ds(start, strip), :]
        s_nb = _nt_dot(qs, kn) + bias_ref[cls]
        s_cx = _nt_dot(qs, kc)
        o = _softmax_pv([s_nb, s_cx], [vn, vc], None)
        o_ref[r * GRID_W:(r + 1) * GRID_W, :] = jnp.where(lane_lo, o[:GRID_W], o[GRID_W:]).astype(o_ref.dtype)


def _na_bias_table(rpb):
    h = rpb.shape[0]
    n_dcol = 2 * NA_WIN_COLS - 1
    cols = jnp.arange(GRID_W)
    col_start = jnp.clip(cols - NA_WIN_COLS // 2, 0, GRID_W - NA_WIN_COLS)
    inwin = (cols[None, :] >= col_start[:, None]) & (cols[None, :] < col_start[:, None] + NA_WIN_COLS)
    dcol = cols[None, :] - cols[:, None] + NA_WIN_COLS - 1
    pick = (dcol[None] == jnp.arange(n_dcol)[:, None, None]).astype(F32).reshape(n_dcol, -1)
    spread = jnp.dot(rpb.astype(F32).reshape(-1, n_dcol), pick, precision=lax.Precision.HIGHEST)
    spread = spread.reshape(h, 2 * NA_WIN_ROWS - 1, GRID_W, GRID_W)
    spread = jnp.where(inwin[None, None], spread, NEG_INF)
    tab = jnp.stack([spread[:, NA_WIN_ROWS - 1 - cls:2 * NA_WIN_ROWS - 1 - cls]
                     for cls in range(NA_WIN_ROWS)], axis=1)
    tab = tab.transpose(0, 1, 3, 2, 4).reshape(h, NA_WIN_ROWS, GRID_W, NA_WIN_ROWS * GRID_W)
    tab = tab.reshape(h // 2, 2, NA_WIN_ROWS, GRID_W, NA_WIN_ROWS * GRID_W).transpose(0, 2, 1, 3, 4)
    return tab.reshape(h // 2, NA_WIN_ROWS, 2 * GRID_W, NA_WIN_ROWS * GRID_W)


def neighbourhood_attention(qkv, rpb, *, batch, seq_len, ctx_len, d, with_ctx_out):
    n_pairs = d // LANES
    rows_per_blk = 2 * NA_WIN_ROWS
    blk = rows_per_blk * GRID_W
    grid_rows = seq_len // GRID_W
    nblk = seq_len // blk
    cb0 = batch * seq_len // ctx_len
    bias = _na_bias_table(rpb)

    def kspec(col0, shift):
        return pl.BlockSpec((blk, LANES),
                            lambda b, p, j: (b * nblk + jnp.clip(j + shift, 0, nblk - 1), col0 + p))

    ctx_k = pl.BlockSpec((ctx_len, LANES), lambda b, p, j: (cb0 + b, n_pairs + p))
    ctx_v = pl.BlockSpec((ctx_len, LANES), lambda b, p, j: (cb0 + b, 2 * n_pairs + p))
    lat = pl.pallas_call(
        functools.partial(_nattn_kernel, grid_rows=grid_rows, rows_per_blk=rows_per_blk),
        grid=(batch, n_pairs, nblk),
        in_specs=[pl.BlockSpec((blk, LANES), lambda b, p, j: (b * nblk + j, p))]
        + [kspec(n_pairs, s) for s in (-1, 0, 1)] + [kspec(2 * n_pairs, s) for s in (-1, 0, 1)]
        + [ctx_k, ctx_v,
           pl.BlockSpec((None, NA_WIN_ROWS, 2 * GRID_W, NA_WIN_ROWS * GRID_W), lambda b, p, j: (p, 0, 0, 0))],
        out_specs=pl.BlockSpec((blk, LANES), lambda b, p, j: (b * nblk + j, p)),
        out_shape=jax.ShapeDtypeStruct((batch * seq_len, d), BF16),
        scratch_shapes=[pltpu.VMEM((3 * blk, LANES), BF16), pltpu.VMEM((3 * blk, LANES), BF16)],
        compiler_params=_cparams(3),
        name="neighbourhood_attn",
    )(qkv, *([qkv] * 8), bias)
    if not with_ctx_out:
        return lat
    ctx = pl.pallas_call(
        _cattn_kernel,
        grid=(batch, n_pairs),
        in_specs=[pl.BlockSpec((ctx_len, LANES), lambda b, p: (cb0 + b, p)),
                  pl.BlockSpec((ctx_len, LANES), lambda b, p: (cb0 + b, n_pairs + p)),
                  pl.BlockSpec((ctx_len, LANES), lambda b, p: (cb0 + b, 2 * n_pairs + p))],
        out_specs=pl.BlockSpec((ctx_len, LANES), lambda b, p: (b, p)),
        out_shape=jax.ShapeDtypeStruct((batch * ctx_len, d), BF16),
        compiler_params=_cparams(2),
        name="ctx_mha",
    )(qkv, qkv, qkv)
    return jnp.concatenate([lat, ctx], axis=0)


def _cattn_kernel(q_ref, k_ref, v_ref, o_ref):
    rows = q_ref.shape[0]
    lane_lo = lax.broadcasted_iota(jnp.int32, (rows, LANES), 1) < HEAD_DIM
    qa, qb = _masked_halves(q_ref[...], lane_lo)
    qs = jnp.concatenate([qa, qb], axis=0)
    o = _softmax_pv([_nt_dot(qs, k_ref[...])], [v_ref[...]], None)
    o_ref[...] = jnp.where(lane_lo, o[:rows], o[rows:]).astype(o_ref.dtype)


def _router_kernel(x_ref, g_ref, sh_ref, sc_ref, wh_ref, wl_ref, b_ref, h_ref, idx_ref, gate_ref, cnt_ref,
                   run_ref):
    @pl.when(pl.program_id(0) == 0)
    def _():
        run_ref[...] = jnp.zeros_like(run_ref)

    h = _norm_mod(x_ref[...], g_ref[...], sh_ref[0], sc_ref[0])
    h_hi = h.astype(BF16)
    h_ref[...] = h_hi
    h_lo = (h - h_hi.astype(F32)).astype(BF16)
    logits = (jnp.dot(h_hi, wh_ref[...], preferred_element_type=F32)
              + jnp.dot(h_hi, wl_ref[...], preferred_element_type=F32)
              + jnp.dot(h_lo, wh_ref[...], preferred_element_type=F32)) + b_ref[...]
    lane = lax.broadcasted_iota(jnp.int32, logits.shape, 1)
    tm = logits.shape[0]
    idx_out = jnp.zeros(logits.shape, jnp.int32)
    val_out = jnp.zeros(logits.shape, F32)
    tri = jnp.where(lax.broadcasted_iota(jnp.int32, (tm, tm), 0) > lax.broadcasted_iota(jnp.int32, (tm, tm), 1),
                    1.0, 0.0).astype(BF16)
    run = run_ref[...]
    top = None
    denom = 0.0
    for k in range(TOP_K):
        m = logits.max(axis=-1, keepdims=True)
        sel = jnp.min(jnp.where(logits == m, lane, LANES), axis=-1, keepdims=True)
        if top is None:
            top = m
        e = jnp.exp(m - top)
        denom = denom + e
        onehot = lane == sel
        before = jnp.dot(tri, jnp.where(onehot, 1.0, 0.0).astype(BF16), preferred_element_type=F32)
        rank = jnp.sum(jnp.where(onehot, before + run, 0.0), axis=-1, keepdims=True).astype(jnp.int32)
        run = run + jnp.sum(jnp.where(onehot, 1.0, 0.0), axis=0, keepdims=True)
        idx_out = jnp.where(lane == k, sel, idx_out)
        idx_out = jnp.where(lane == TOP_K + k, rank, idx_out)
        val_out = jnp.where(lane == k, e, val_out)
        logits = jnp.where(onehot, -jnp.inf, logits)
    run_ref[...] = run
    cnt_ref[...] = run
    idx_ref[...] = idx_out
    gate_ref[...] = val_out / denom


def router(x, g, shift, scale, w_router, b_router, *, seg_rows, rows, row_off=0, tm=ROW_TILE):
    m, d = rows, x.shape[1]
    n_seg = shift.shape[0]
    n_e = w_router.shape[1]
    w_pad = jnp.zeros((d, LANES), F32).at[:, :n_e].set(w_router)
    w_hi = w_pad.astype(BF16)
    w_lo = (w_pad - w_hi.astype(F32)).astype(BF16)
    b_pad = jnp.full((1, LANES), -jnp.inf, F32).at[0, :n_e].set(b_router)
    blk_off = row_off // tm
    seg = lambda i: (jnp.minimum((blk_off + i) * tm // seg_rows, n_seg - 1), 0, 0)
    return pl.pallas_call(
        _router_kernel,
        grid=(m // tm,),
        in_specs=[pl.BlockSpec((tm, d), lambda i: (blk_off + i, 0)),
                  pl.BlockSpec((1, d), lambda i: (0, 0)),
                  pl.BlockSpec((1, 1, d), seg),
                  pl.BlockSpec((1, 1, d), seg),
                  pl.BlockSpec((d, LANES), lambda i: (0, 0)),
                  pl.BlockSpec((d, LANES), lambda i: (0, 0)),
                  pl.BlockSpec((1, LANES), lambda i: (0, 0))],
        out_specs=[pl.BlockSpec((tm, d), lambda i: (i, 0)),
                   pl.BlockSpec((tm, LANES), lambda i: (i, 0)),
                   pl.BlockSpec((tm, LANES), lambda i: (i, 0)),
                   pl.BlockSpec((1, LANES), lambda i: (0, 0))],
        out_shape=[jax.ShapeDtypeStruct((m, d), BF16),
                   jax.ShapeDtypeStruct((m, LANES), jnp.int32),
                   jax.ShapeDtypeStruct((m, LANES), F32),
                   jax.ShapeDtypeStruct((1, LANES), F32)],
        scratch_shapes=[pltpu.VMEM((1, LANES), F32)],
        compiler_params=_cparams(1),
        name="router",
    )(x, g, shift, scale, w_hi, w_lo, b_pad)


def _expert_kernel(blk_e_ref, n_used_ref, x_ref, wgu_ref, bgu_ref, wd_ref, bd_ref, o_ref, wgu_s, wd_s, *, f):
    i = pl.program_id(0)
    used = i < n_used_ref[0]

    @pl.when(used & ((i == 0) | (blk_e_ref[i] != blk_e_ref[jnp.maximum(i - 1, 0)])))
    def _():
        wgu_s[...] = wgu_ref[0].astype(BF16)
        wd_s[...] = wd_ref[0].astype(BF16)

    @pl.when(used)
    def _():
        gu = jnp.dot(x_ref[...], wgu_s[...], preferred_element_type=F32) + bgu_ref[0]
        g = jnp.minimum(gu[:, :f], SWIGLU_LIMIT)
        u = jnp.clip(gu[:, f:], -SWIGLU_LIMIT, SWIGLU_LIMIT)
        act = (u + 1) * (g * jax.nn.sigmoid(SWIGLU_ALPHA * g))
        y = jnp.dot(act.astype(BF16), wd_s[...], preferred_element_type=F32) + bd_ref[0]
        o_ref[...] = y.astype(o_ref.dtype)

    @pl.when(i >= n_used_ref[0])
    def _():
        o_ref[...] = jnp.zeros_like(o_ref)


def expert_mlp(xs, blk_e, n_used, w_gu, b_gu, w_down, b_down, *, layer, bm=MOE_ROWS):
    n_slot, d = xs.shape
    _, n_e, _, f2 = w_gu.shape
    f = f2 // 2
    return pl.pallas_call(
        functools.partial(_expert_kernel, f=f),
        grid_spec=pltpu.PrefetchScalarGridSpec(
            num_scalar_prefetch=2,
            grid=(n_slot // bm,),
            in_specs=[pl.BlockSpec((bm, d), lambda i, be, nu: (i, 0)),
                      pl.BlockSpec((None, 1, d, f2), lambda i, be, nu: (layer, be[i], 0, 0)),
                      pl.BlockSpec((1, 1, f2), lambda i, be, nu: (be[i], 0, 0)),
                      pl.BlockSpec((None, 1, f, d), lambda i, be, nu: (layer, be[i], 0, 0)),
                      pl.BlockSpec((1, 1, d), lambda i, be, nu: (be[i], 0, 0))],
            out_specs=pl.BlockSpec((bm, d), lambda i, be, nu: (i, 0)),
            scratch_shapes=[pltpu.VMEM((d, f2), BF16), pltpu.VMEM((f, d), BF16)]),
        out_shape=jax.ShapeDtypeStruct((n_slot, d), BF16),
        compiler_params=pltpu.CompilerParams(dimension_semantics=("arbitrary",),
                                             vmem_limit_bytes=EXPERT_VMEM_LIMIT),
        name="expert_mlp",
    )(blk_e, n_used, xs, w_gu, b_gu.reshape(n_e, 1, f2), w_down, b_down.reshape(n_e, 1, d))


def _combine_kernel(res_ref, y_ref, w_ref, gate_ref, o_ref):
    w = w_ref[...]
    y = w[:, 0:1] * y_ref[0].astype(F32)
    for k in range(1, TOP_K):
        y = y + w[:, k:k + 1] * y_ref[k].astype(F32)
    o_ref[...] = res_ref[...] + gate_ref[0] * y


def moe_dispatch(x, g, shift, scale, w_router, b_router, *, seg_rows, rows, row_off, bm=MOE_ROWS):
    m = rows
    n_e = w_router.shape[1]
    h, route, gates, counts = router(x, g, shift, scale, w_router, b_router, seg_rows=seg_rows, rows=rows,
                                     row_off=row_off)
    idx = route[:, :TOP_K]
    rank = route[:, TOP_K:2 * TOP_K]
    n_asg = m * TOP_K
    counts = counts[0, :n_e].astype(jnp.int32)
    padded = (counts + bm - 1) // bm * bm
    pad_end = jnp.cumsum(padded)
    pad_start = pad_end - padded
    start_of = jnp.sum(jnp.where(idx[:, :, None] == jnp.arange(n_e, dtype=jnp.int32), pad_start, 0), axis=-1)
    dest = (start_of + rank).reshape(-1)
    n_blk = -(-(n_asg + n_e * (bm - 1)) // bm)
    n_slot = n_blk * bm
    tok = (jnp.arange(n_asg, dtype=jnp.int32) // TOP_K)
    slot_tok = (jnp.arange(n_slot, dtype=jnp.int32) % m).at[dest].set(tok, unique_indices=True)
    blk_start = jnp.arange(n_blk, dtype=jnp.int32) * bm
    blk_e = jnp.minimum(jnp.sum((pad_end[None, :] <= blk_start[:, None]).astype(jnp.int32), axis=1), n_e - 1)
    n_used = (pad_end[-1] // bm).astype(jnp.int32).reshape(1)
    xs = h.at[slot_tok].get(mode="promise_in_bounds")
    dest_km = dest.reshape(m, TOP_K).T.reshape(-1)
    return xs, blk_e, n_used, dest_km, gates


def moe_combine(x, y_slot, dest_km, gates, gate, *, seg_rows, rows, row_off):
    m, d = rows, x.shape[1]
    y_tok = y_slot.at[dest_km].get(mode="promise_in_bounds").reshape(TOP_K, m, d)
    tm = 256
    n_seg = gate.shape[0]
    blk_off = row_off // tm
    seg = lambda i: (jnp.minimum((blk_off + i) * tm // seg_rows, n_seg - 1), 0, 0)
    return pl.pallas_call(
        _combine_kernel,
        grid=(m // tm,),
        in_specs=[pl.BlockSpec((tm, d), lambda i: (blk_off + i, 0)),
                  pl.BlockSpec((TOP_K, tm, d), lambda i: (0, i, 0)),
                  pl.BlockSpec((tm, LANES), lambda i: (i, 0)),
                  pl.BlockSpec((1, 1, d), seg)],
        out_specs=pl.BlockSpec((tm, d), lambda i: (blk_off + i, 0)),
        out_shape=jax.ShapeDtypeStruct(x.shape, F32),
        input_output_aliases={0: 0},
        compiler_params=_cparams(1),
        name="moe_combine",
    )(x, y_tok, gates, gate)


LC_R2 = 128


@functools.lru_cache(maxsize=None)
def _dft_small_mats(n):
    nn = 2 * n
    ang = 2.0 * np.pi * ((np.arange(nn)[:, None] * np.arange(nn)[None, :]) % nn) / nn
    fwd = np.concatenate([np.cos(ang), -np.sin(ang)], 0)
    inv = np.concatenate([np.cos(ang[:n]), -np.sin(ang[:n])], 1) / nn
    return jnp.asarray(fwd, BF16), jnp.asarray(inv, BF16)


def _plain_mm_kernel(a_ref, b_ref, o_ref):
    o_ref[...] = jnp.dot(a_ref[...], b_ref[...].astype(BF16), preferred_element_type=F32)


def plain_mm(a, b, tn=512):
    m, k = a.shape
    n = b.shape[1]
    tn = _col_tile(n, tn)
    return pl.pallas_call(
        _plain_mm_kernel,
        grid=(n // tn,),
        in_specs=[pl.BlockSpec((m, k), lambda j: (0, 0)), pl.BlockSpec((k, tn), lambda j: (0, j))],
        out_specs=pl.BlockSpec((m, tn), lambda j: (0, j)),
        out_shape=jax.ShapeDtypeStruct((m, n), F32),
        compiler_params=_cparams(1),
        name="plain_mm",
    )(a, b)


def _short_conv_kernel(u_ref, gate_ref, h_ref, skip_ref, fwd_ref, inv_ref, o_ref):
    u = u_ref[...]
    x = jnp.dot(fwd_ref[...], u.astype(BF16), preferred_element_type=F32)
    h = h_ref[...]
    nn = x.shape[0] // 2
    xr, xi, hr, hi = x[:nn], x[nn:], h[:nn], h[nn:]
    prod = jnp.concatenate([xr * hr - xi * hi, xr * hi + xi * hr], axis=0).astype(BF16)
    y = jnp.dot(inv_ref[...], prod, preferred_element_type=F32)
    o_ref[...] = (gate_ref[...].astype(F32) * (y + skip_ref[...] * u.astype(F32))).astype(o_ref.dtype)


def short_long_conv(u_arr, u_col, gate_arr, gate_col, spec, skip, fwd, inv, *, batch, seq_len, tl=512):
    d = spec.shape[1]
    tl = _col_tile(d, tl)
    uc, gc = u_col // tl, gate_col // tl
    nn2 = spec.shape[0]
    return pl.pallas_call(
        _short_conv_kernel,
        grid=(batch, d // tl),
        in_specs=[pl.BlockSpec((seq_len, tl), lambda b, j: (b, uc + j)),
                  pl.BlockSpec((seq_len, tl), lambda b, j: (b, gc + j)),
                  pl.BlockSpec((nn2, tl), lambda b, j: (0, j)),
                  pl.BlockSpec((1, tl), lambda b, j: (0, j)),
                  pl.BlockSpec((nn2, seq_len), lambda b, j: (0, 0)),
                  pl.BlockSpec((seq_len, nn2), lambda b, j: (0, 0))],
        out_specs=pl.BlockSpec((seq_len, tl), lambda b, j: (b, j)),
        out_shape=jax.ShapeDtypeStruct((batch * seq_len, d), BF16),
        compiler_params=_cparams(2),
        name="short_long_conv",
    )(u_arr, gate_arr, spec, skip, fwd[:, :seq_len], inv)


def _hyena_taps(n, d, f_w1, f_b1, f_w2, f_b2, f_w3, f_b3, f_freq, f_w4):
    hp = lax.Precision.HIGHEST
    lin = jnp.linspace(0.0, 1.0, n, dtype=F32)
    idx = jnp.arange(n, dtype=F32)
    bands = jnp.linspace(1e-4, HY_BANDS - 1, HY_BANDS, dtype=F32)[None, :]
    deltas = jnp.abs(jnp.linspace(math.log(HY_DECAY_TARGET) / HY_FAST_DECAY,
                                  math.log(HY_DECAY_TARGET) / HY_SLOW_DECAY, d, dtype=F32))

    def branch(t, pos, direction):
        t = t[:, None]
        ang = (2.0 * math.pi / n) * pos[:, None]
        emb = jnp.concatenate([t, jnp.cos(bands * ang), -jnp.sin(bands * ang)], axis=-1)
        a = jnp.sin(f_freq * (jnp.dot(emb, f_w1, precision=hp) + f_b1))
        a = jnp.sin(f_freq * (jnp.dot(a, f_w2, precision=hp) + f_b2))
        a = jnp.sin(f_freq * (jnp.dot(a, f_w3, precision=hp) + f_b3))
        decay = jnp.exp(-t * deltas[None, :])
        w4 = f_w4.reshape(f_w4.shape[0], HY_ORDER, 2, d)[:, :, direction]
        return [jnp.dot(a, w4[:, o], precision=hp) * decay for o in range(HY_ORDER)]

    fwd = branch(lin, idx, 0)
    bwd = branch(lin[:0:-1], idx[:0:-1], 1)
    zero = jnp.zeros((1, d), F32)
    return [jnp.concatenate([fwd[o], zero, bwd[o]], axis=0) for o in range(HY_ORDER)]


HC = 128
HC_PAIR = 2
LC_TL = 16
HY_JN = 2


@functools.lru_cache(maxsize=None)
def _hy_mats(n):
    nn = 2 * n
    r1 = nn // LC_R2
    tau = 2.0 * np.pi
    k1 = np.arange(r1)
    t = LC_R2 * np.arange(r1)[None, :] + np.arange(LC_R2)[:, None]
    ang1 = tau * ((k1[None, :, None] * t[:, None, :]) % nn) / nn
    f1 = np.empty((LC_R2, 2 * r1, r1))
    f1[:, 0::2] = np.cos(ang1)
    f1[:, 1::2] = -np.sin(ang1)
    ang2 = tau * ((np.arange(LC_R2)[:, None] * np.arange(LC_R2)[None, :]) % LC_R2) / LC_R2
    c, s = np.cos(ang2), np.sin(ang2)
    f2 = np.empty((2 * LC_R2, 2 * LC_R2))
    f2[:LC_R2, 0::2], f2[:LC_R2, 1::2] = c, s
    f2[LC_R2:, 0::2], f2[LC_R2:, 1::2] = -s, c
    f3 = np.empty((2 * LC_R2, 2 * LC_R2))
    f3[0::2, :LC_R2], f3[0::2, LC_R2:] = c, -s
    f3[1::2, :LC_R2], f3[1::2, LC_R2:] = s, c
    m = LC_R2 * np.arange(r1 // 2)[None, :] + np.arange(LC_R2)[:, None]
    ang4 = tau * ((m[:, :, None] * k1[None, None, :]) % nn) / nn
    f4 = np.empty((LC_R2, r1 // 2, 2 * r1))
    f4[:, :, 0::2] = np.cos(ang4) / nn
    f4[:, :, 1::2] = -np.sin(ang4) / nn
    return tuple(jnp.asarray(a, BF16) for a in (f1, f2, f3, f4))


def _hy1_kernel(x_ref, f_ref, o_ref, *, tl_n, nh):
    c = pl.program_id(2)
    for tl in range(tl_n):
        xs = jnp.concatenate([x_ref[h, pl.ds(c * tl_n + tl, nh, stride=LC_R2), :] for h in range(HC_PAIR)],
                             axis=1).astype(BF16)
        a = jnp.dot(f_ref[tl], xs, preferred_element_type=F32)
        packed = pltpu.bitcast(a.astype(BF16), jnp.uint32)
        for h in range(HC_PAIR):
            for j in range(packed.shape[0] // 8):
                o_ref[h, j, tl * 8:(tl + 1) * 8, :] = packed[j * 8:(j + 1) * 8, h * HC:(h + 1) * HC]


def hy_stage1(x, f1, *, blk_off, n_blk, batch, rows_hi):
    r2 = f1.shape[1]
    rows = rows_hi * LC_R2
    tl_n = LC_TL
    return pl.pallas_call(
        functools.partial(_hy1_kernel, tl_n=tl_n, nh=rows_hi),
        grid=(n_blk // HC_PAIR, batch, LC_R2 // tl_n),
        in_specs=[pl.BlockSpec((HC_PAIR, rows, HC), lambda p, b, c: (blk_off // HC_PAIR + p, b, 0)),
                  pl.BlockSpec((tl_n, r2, rows_hi), lambda p, b, c: (c, 0, 0))],
        out_specs=pl.BlockSpec((HC_PAIR, None, r2 // 16, tl_n * 8, HC), lambda p, b, c: (p, b, 0, c, 0)),
        out_shape=jax.ShapeDtypeStruct((n_blk, batch, r2 // 16, LC_R2 * 8, HC), jnp.uint32),
        compiler_params=_cparams(3),
        name="hy_stage1",
    )(x, f1[:, :, :rows_hi])


def _hy2_kernel(a_ref, f2_ref, *rest, spectrum_only):
    if spectrum_only:
        (o_ref,) = rest
    else:
        h_ref, f3_ref, o_ref = rest
    for jk in range(8 * a_ref.shape[1]):
        jj, kk = divmod(jk, 8)
        w = jnp.concatenate([a_ref[h, jj, pl.ds(kk, LC_R2, stride=8), :] for h in range(HC_PAIR)], axis=1)
        a_in = pltpu.bitcast(w, BF16)
        x = jnp.dot(f2_ref[...], a_in, preferred_element_type=F32)
        if spectrum_only:
            o_ref[jk] = x.astype(o_ref.dtype)
            continue
        hs = h_ref[jk].astype(F32)
        xr, xi = x[:LC_R2], x[LC_R2:]
        hr, hi = hs[:LC_R2], hs[LC_R2:]
        prod = jnp.concatenate([xr * hr - xi * hi, xr * hi + xi * hr], axis=0).astype(BF16)
        g = jnp.dot(f3_ref[...], prod, preferred_element_type=F32)
        packed = pltpu.bitcast(g.astype(BF16), jnp.uint32)
        for h in range(HC_PAIR):
            o_ref[h, jj, pl.ds(kk, LC_R2, stride=8), :] = packed[:, h * HC:(h + 1) * HC]


def hy_spectrum(a, f2):
    cb, _, nj, rows, _ = a.shape
    jn = HY_JN if nj % HY_JN == 0 else 1
    mat = pl.BlockSpec((2 * LC_R2, 2 * LC_R2), lambda j, p: (0, 0))
    return pl.pallas_call(
        functools.partial(_hy2_kernel, spectrum_only=True),
        grid=(nj // jn, cb // HC_PAIR),
        in_specs=[pl.BlockSpec((HC_PAIR, None, jn, rows, HC), lambda j, p: (p, 0, j, 0, 0)), mat],
        out_specs=pl.BlockSpec((None, 8 * jn, 2 * LC_R2, HC_PAIR * HC), lambda j, p: (p, j, 0, 0)),
        out_shape=jax.ShapeDtypeStruct((cb // HC_PAIR, nj * 8, 2 * LC_R2, HC_PAIR * HC), BF16),
        compiler_params=_cparams(2),
        name="hy_spectrum",
    )(a, f2)


def hy_stage23(a, spec, f2, f3):
    cb, bsz, nj, rows, _ = a.shape
    jn = HY_JN if nj % HY_JN == 0 else 1
    mat = pl.BlockSpec((2 * LC_R2, 2 * LC_R2), lambda j, p, b: (0, 0))
    blk = pl.BlockSpec((HC_PAIR, None, jn, rows, HC), lambda j, p, b: (p, b, j, 0, 0))
    return pl.pallas_call(
        functools.partial(_hy2_kernel, spectrum_only=False),
        grid=(nj // jn, cb // HC_PAIR, bsz),
        in_specs=[blk, mat,
                  pl.BlockSpec((None, 8 * jn, 2 * LC_R2, HC_PAIR * HC), lambda j, p, b: (p, j, 0, 0)), mat],
        out_specs=blk,
        out_shape=jax.ShapeDtypeStruct(a.shape, jnp.uint32),
        compiler_params=_cparams(3),
        name="hy_stage23",
    )(a, f2, spec, f3)


def _hy4_kernel(g_ref, f_ref, o_ref, *, tl_n, nh):
    c = pl.program_id(2)
    for tl in range(tl_n):
        w = jnp.concatenate([g_ref[h, :, tl * 8:(tl + 1) * 8, :].reshape(-1, HC) for h in range(HC_PAIR)], axis=1)
        g_in = pltpu.bitcast(w, BF16)
        y = jnp.dot(f_ref[tl], g_in, preferred_element_type=F32)
        for h in range(HC_PAIR):
            o_ref[h, pl.ds(c * tl_n + tl, nh, stride=LC_R2), :] = y[:, h * HC:(h + 1) * HC]


def hy_stage4(g, f4, *, seq_len):
    cb, bsz, nj, _, _ = g.shape
    nh = seq_len // LC_R2
    tl_n = LC_TL
    return pl.pallas_call(
        functools.partial(_hy4_kernel, tl_n=tl_n, nh=nh),
        grid=(cb // HC_PAIR, bsz, LC_R2 // tl_n),
        in_specs=[pl.BlockSpec((HC_PAIR, None, nj, tl_n * 8, HC), lambda p, b, c: (p, b, 0, c, 0)),
                  pl.BlockSpec((tl_n, nh, 16 * nj), lambda p, b, c: (c, 0, 0))],
        out_specs=pl.BlockSpec((HC_PAIR, seq_len, HC), lambda p, b, c: (p, b, 0)),
        out_shape=jax.ShapeDtypeStruct((cb, bsz * seq_len, HC), F32),
        compiler_params=_cparams(3),
        name="hy_stage4",
    )(g, f4)


def _hy_gate_kernel(y_ref, u_ref, gate_ref, skip_ref, o_ref):
    o_ref[...] = (gate_ref[...] * (y_ref[...] + skip_ref[...] * u_ref[...])).astype(o_ref.dtype)


def hy_gate(y, u_arr, u_off, gate_arr, gate_off, skip, *, natural, tr=4096):
    cb, rows, _ = y.shape
    tr = min(tr, rows)
    if natural:
        out_spec = pl.BlockSpec((tr, HC), lambda i, r: (r, i))
        out_shape = jax.ShapeDtypeStruct((rows, cb * HC), BF16)
    else:
        out_spec = pl.BlockSpec((None, tr, HC), lambda i, r: (i, r, 0))
        out_shape = jax.ShapeDtypeStruct((cb, rows, HC), F32)
    return pl.pallas_call(
        _hy_gate_kernel,
        grid=(cb, rows // tr),
        in_specs=[pl.BlockSpec((None, tr, HC), lambda i, r: (i, r, 0)),
                  pl.BlockSpec((None, tr, HC), lambda i, r: (u_off + i, r, 0)),
                  pl.BlockSpec((None, tr, HC), lambda i, r: (gate_off + i, r, 0)),
                  pl.BlockSpec((None, 1, HC), lambda i, r: (i, 0, 0))],
        out_specs=out_spec,
        out_shape=out_shape,
        compiler_params=_cparams(2),
        name="hy_gate",
    )(y, u_arr, gate_arr, skip.reshape(cb, 1, HC))


def _hy_taps_kernel(a_ref, t_ref, w_ref, dl_ref, o_ref):
    a = a_ref[...]
    a_hi = a.astype(BF16)
    a_lo = (a - a_hi.astype(F32)).astype(BF16)
    w = w_ref[...]
    w_hi = w.astype(BF16)
    w_lo = (w - w_hi.astype(F32)).astype(BF16)
    h = (jnp.dot(a_hi, w_hi, preferred_element_type=F32) + jnp.dot(a_hi, w_lo, preferred_element_type=F32)
         + jnp.dot(a_lo, w_hi, preferred_element_type=F32))
    o_ref[...] = h * jnp.exp(-t_ref[...] * dl_ref[...])


def hy_taps(a_all, t_all, w4, deltas, *, seq_len, order, tr=2048):
    nrow, width = a_all.shape
    d = deltas.shape[0]
    tr = min(tr, seq_len)
    per_dir = seq_len // tr
    return pl.pallas_call(
        _hy_taps_kernel,
        grid=(d // HC, nrow // tr),
        in_specs=[pl.BlockSpec((tr, width), lambda i, r: (r, 0)),
                  pl.BlockSpec((tr, 1), lambda i, r: (r, 0)),
                  pl.BlockSpec((None, None, width, HC), lambda i, r: (order, r // per_dir, 0, i)),
                  pl.BlockSpec((1, HC), lambda i, r: (0, i))],
        out_specs=pl.BlockSpec((None, tr, HC), lambda i, r: (i, r, 0)),
        out_shape=jax.ShapeDtypeStruct((d // HC, nrow, HC), F32),
        compiler_params=_cparams(2),
        name="hy_taps",
    )(a_all, t_all, w4.reshape(width, HY_ORDER, 2, d).transpose(1, 2, 0, 3), deltas.reshape(1, d))


def _hyena_filter_hidden(n, f_w1, f_b1, f_w2, f_b2, f_w3, f_b3, f_freq):
    hp = lax.Precision.HIGHEST
    lin = jnp.linspace(0.0, 1.0, n, dtype=F32)
    idx = jnp.arange(n, dtype=F32)
    bands = jnp.linspace(1e-4, HY_BANDS - 1, HY_BANDS, dtype=F32)[None, :]
    t = jnp.concatenate([lin, jnp.zeros((1,), F32), lin[:0:-1]])[:, None]
    pos = jnp.concatenate([idx, jnp.zeros((1,), F32), idx[:0:-1]])[:, None]
    ang = (2.0 * math.pi / n) * pos
    emb = jnp.concatenate([t, jnp.cos(bands * ang), -jnp.sin(bands * ang)], axis=-1)
    a = jnp.sin(f_freq * (jnp.dot(emb, f_w1, precision=hp) + f_b1))
    a = jnp.sin(f_freq * (jnp.dot(a, f_w2, precision=hp) + f_b2))
    a = jnp.sin(f_freq * (jnp.dot(a, f_w3, precision=hp) + f_b3))
    keep = (jnp.arange(2 * n) != n)[:, None]
    return jnp.where(keep, a, 0.0), t


def _hyena_deltas(d):
    return jnp.abs(jnp.linspace(math.log(HY_DECAY_TARGET) / HY_FAST_DECAY,
                                math.log(HY_DECAY_TARGET) / HY_SLOW_DECAY, d, dtype=F32))


def hyena_long_convs2(zc, a_all, t_all, w4, skip, *, batch, seq_len, d):
    cb = d // HC
    f1, f2, f3, f4 = _hy_mats(seq_len)
    nh = seq_len // LC_R2
    deltas = _hyena_deltas(d)
    u, u_off = zc, 2 * cb
    for o in range(HY_ORDER):
        taps = hy_taps(a_all, t_all, w4, deltas, seq_len=seq_len, order=o)
        spec = hy_spectrum(hy_stage1(taps, f1, blk_off=0, n_blk=cb, batch=1, rows_hi=2 * nh), f2)
        a = hy_stage1(u, f1, blk_off=u_off, n_blk=cb, batch=batch, rows_hi=nh)
        y = hy_stage4(hy_stage23(a, spec, f2, f3), f4, seq_len=seq_len)
        u = hy_gate(y, u, u_off, zc, o * cb, skip[o], natural=(o == HY_ORDER - 1))
        u_off = 0
    return u


def _final_norm_kernel(x_ref, g_ref, o_ref):
    x = x_ref[...]
    o_ref[...] = x * lax.rsqrt(jnp.mean(x * x, axis=-1, keepdims=True) + NORM_EPS) * g_ref[...]


def final_norm(x, g, *, rows, tm=ROW_TILE):
    d = x.shape[1]
    return pl.pallas_call(
        _final_norm_kernel,
        grid=(rows // tm,),
        in_specs=[pl.BlockSpec((tm, d), lambda i: (i, 0)), pl.BlockSpec((1, d), lambda i: (0, 0))],
        out_specs=pl.BlockSpec((tm, d), lambda i: (i, 0)),
        out_shape=jax.ShapeDtypeStruct((rows, d), F32),
        compiler_params=_cparams(1),
        name="final_norm",
    )(x, g)


def _rope_tables(n_lat, seq_len, n_rows):
    t = jnp.arange(n_rows, dtype=jnp.int32)
    row = ((t % seq_len) // GRID_W).astype(F32)
    col = (t % GRID_W).astype(F32)
    lane = jnp.arange(LANES)
    dd = lane % HEAD_DIM
    quarter = HEAD_DIM // 4
    inv_freq = ROPE_BASE ** (-(dd % quarter).astype(F32) / quarter)
    pos = jnp.where(dd[None, :] < HEAD_DIM // 2, row[:, None], col[:, None])
    ang = pos * inv_freq[None, :]
    sign = jnp.where((dd % (HEAD_DIM // 2)) < quarter, -1.0, 1.0).astype(F32)
    lat = (t < n_lat)[:, None]
    cos = jnp.where(lat, jnp.cos(ang), 1.0)
    sin = jnp.where(lat, jnp.sin(ang) * sign[None, :], 0.0)
    return cos, sin


def _wa_head_perm(n_heads):
    order = []
    for p in range(n_heads // (2 * WA_GROUP)):
        for g in range(WA_GROUP):
            order += [2 * WA_GROUP * p + g, 2 * WA_GROUP * p + WA_GROUP + g]
    cols = jnp.asarray(order, jnp.int32)[:, None] * HEAD_DIM + jnp.arange(HEAD_DIM, dtype=jnp.int32)[None, :]
    return cols.reshape(-1)


def kernel(x, c, ctx, c_ctx, w_ada, b_ada, g_mix, g_ffn, hy_w_in, hy_b_in, hy_w_short, hy_b_short, hy_f_w1, hy_f_b1, hy_f_w2, hy_f_b2, hy_f_w3, hy_f_b3, hy_f_freq, hy_f_w4, hy_skip, hy_w_out, hy_b_out, cf_w_pw1, cf_b_pw1, cf_w_dw, cf_b_dw, cf_ln_g, cf_ln_b, cf_w_pw2, cf_b_pw2, wa_w_qkv, wa_w_o, wa_sinks, na_w_qkv, na_w_o, na_rpb, moe_w_router, moe_b_router, moe_w_gu, moe_b_gu, moe_w_down, moe_b_down, g_final):
    bsz, n, d = x.shape
    n_ctx = ctx.shape[1]
    depth = w_ada.shape[0]
    n_lat = bsz * n
    m_all = n_lat + bsz * n_ctx

    cond = jnp.zeros((8, d), F32).at[:bsz].set(jax.nn.silu(c)).at[bsz].set(jax.nn.silu(c_ctx))
    mods = adaln(cond, w_ada, b_ada)
    xu = jnp.concatenate([x.reshape(n_lat, d), ctx.reshape(bsz * n_ctx, d)], axis=0)
    zeros_d = jnp.zeros((1, d), F32)

    for i in range(depth):
        kind, j = i % N_MIXERS, i // N_MIXERS
        last = i == depth - 1
        mod = [mods[i, :bsz + 1, k * d:(k + 1) * d].reshape(bsz + 1, 1, d) for k in range(6)]
        m_out = n_lat if last else m_all
        gm = g_mix[i].reshape(1, d)
        need_ctx_in = (not last) or kind >= 2
        m_in = m_all if need_ctx_in else n_lat
        x_in = xu[:m_in]

        if kind == 0:
            z = nm_matmul(x_in, gm, mod[0], mod[1], hy_w_in[j].astype(BF16), hy_b_in[j].reshape(1, -1),
                          seg_rows=n)
            filt_w = (hy_f_w1[j], hy_f_b1[j], hy_f_w2[j], hy_f_b2[j], hy_f_w3[j], hy_f_b3[j], hy_f_freq[j],
                      hy_f_w4[j])
            w_s, b_s = hy_w_short[j], hy_b_short[j].reshape(1, -1)
            w_o, b_o = hy_w_out[j].astype(BF16), hy_b_out[j].reshape(1, d)
            zc = dwconv(z, w_s, b_s, start=0, total=n_lat, seq_len=n, tc=HC, blocked=True, rows=2048,
                        out_dtype=F32)
            a_all, t_all = _hyena_filter_hidden(n, *filt_w[:-1])
            y = hyena_long_convs2(zc, a_all, t_all, hy_f_w4[j], hy_skip[j], batch=bsz, seq_len=n, d=d)
            xu = mm_res(y, w_o, b_o, xu, mod[2], seg_rows=n)
            if m_in > n_lat:
                zcc = dwconv(z, w_s, b_s, start=n_lat, total=m_in - n_lat, seq_len=n_ctx, tc=512)
                taps_c = _hyena_taps(n_ctx, d, *filt_w)
                fwd, inv = _dft_small_mats(n_ctx)
                yc = short_long_conv(zcc, 2 * d, zcc, 0, plain_mm(fwd, taps_c[0]), hy_skip[j][0].reshape(1, d),
                                     fwd, inv, batch=bsz, seq_len=n_ctx)
                yc = short_long_conv(yc, 0, zcc, d, plain_mm(fwd, taps_c[1]), hy_skip[j][1].reshape(1, d),
                                     fwd, inv, batch=bsz, seq_len=n_ctx)
                if not last:
                    xu = mm_res(yc, w_o, b_o, xu, mod[2], seg_rows=n, row_off=n_lat)
            y = None
        elif kind == 1:
            w1 = cf_w_pw1[j].astype(BF16).reshape(d, 2, d).transpose(1, 0, 2)
            a = nm_matmul(x_in, gm, mod[0], mod[1], w1, cf_b_pw1[j].reshape(2, 1, d), seg_rows=n, mode="glu")
            conv = functools.partial(dwconv, a, cf_w_dw[j], cf_b_dw[j].reshape(1, d), post="ln_silu",
                                     ln=(cf_ln_g[j].reshape(1, d), cf_ln_b[j].reshape(1, d)), rows=64)
            y = conv(start=0, total=n_lat, seq_len=n, same_rows=True)
            if m_in > n_lat:
                y = conv(start=n_lat, total=m_in - n_lat, seq_len=n_ctx, same_rows=True, into=y)
            w_o, b_o = cf_w_pw2[j].astype(BF16), cf_b_pw2[j].reshape(1, d)
        elif kind == 2:
            perm = _wa_head_perm(d // HEAD_DIM)
            scale = HEAD_DIM ** -0.5
            w_qkv = jnp.concatenate([wa_w_qkv[j][:, :d][:, perm] * scale, wa_w_qkv[j][:, d:]], axis=1).astype(BF16)
            n_out = w_qkv.shape[1]
            qkv = nm_matmul(x_in, gm, mod[0], mod[1], w_qkv, jnp.zeros((1, n_out), F32), seg_rows=n,
                            mode="rope", rope=_rope_tables(n_lat, n, m_in), n_rope_cols=d + (n_out - d) // 2)
            y = window_attention(qkv, wa_sinks[j], batch=bsz, seq_len=n, ctx_len=n_ctx, d=d)
            w_o, b_o = wa_w_o[j][perm].astype(BF16), zeros_d
        else:
            scale = HEAD_DIM ** -0.5
            w_qkv = jnp.concatenate([na_w_qkv[j][:, :d] * scale, na_w_qkv[j][:, d:]], axis=1).astype(BF16)
            qkv = nm_matmul(x_in, gm, mod[0], mod[1], w_qkv, jnp.zeros((1, 3 * d), F32), seg_rows=n)
            y = neighbourhood_attention(qkv, na_rpb[j], batch=bsz, seq_len=n, ctx_len=n_ctx, d=d,
                                        with_ctx_out=not last)
            w_o, b_o = na_w_o[j].astype(BF16), zeros_d
        if y is not None:
            xu = mm_res(y, w_o, b_o, xu, mod[2], seg_rows=n, rows=m_out)

        ranges = ((0, m_out),)
        routed = [moe_dispatch(xu, g_ffn[i].reshape(1, d), mod[3], mod[4], moe_w_router[i], moe_b_router[i],
                               seg_rows=n, rows=cnt, row_off=off) for off, cnt in ranges]
        y_slots = [expert_mlp(xs, blk_e, n_used, moe_w_gu, moe_b_gu[i], moe_w_down, moe_b_down[i], layer=i)
                   for xs, blk_e, n_used, _, _ in routed]
        for (off, cnt), (_, _, _, dest_km, gates), y_slot in zip(ranges, routed, y_slots):
            xu = moe_combine(xu, y_slot, dest_km, gates, mod[5], seg_rows=n, rows=cnt, row_off=off)
    return final_norm(xu, g_final.reshape(1, d), rows=n_lat).reshape(bsz, n, d)
```

```python
import functools
import math

import jax
import jax.numpy as jnp
import numpy as np
from jax import lax
from jax.experimental import pallas as pl
from jax.experimental.pallas import tpu as pltpu

F32 = jnp.float32
BF16 = jnp.bfloat16

GRID_W = 64
N_MIXERS = 4
NORM_EPS = 1e-6
NEG_INF = -1e30
HEAD_DIM = 64
ROPE_BASE = 10000.0

HY_ORDER = 2
HY_BANDS = 16
HY_DECAY_TARGET = 1e-2
HY_FAST_DECAY = 0.3
HY_SLOW_DECAY = 1.5

WA_GROUP = 4
WA_WINDOW = 128
NA_WIN_ROWS = 8
NA_WIN_COLS = 16

TOP_K = 4
SWIGLU_LIMIT = 7.0
SWIGLU_ALPHA = 1.702

LANES = 128
ROW_TILE = 512
BIG_ROW_TILE = 1024
MOE_ROWS = 512
VMEM_LIMIT = 56 * 1024 * 1024
EXPERT_VMEM_LIMIT = 58 * 1024 * 1024


def _cparams(n_axes):
    return pltpu.CompilerParams(dimension_semantics=("arbitrary",) * n_axes,
                                vmem_limit_bytes=VMEM_LIMIT)


def _col_tile(n, pref=1024):
    t = min(pref, n)
    while n % t:
        t //= 2
    return t


def _adaln_kernel(c_ref, w_ref, b_ref, o_ref):
    w = w_ref[0]
    w_hi = w.astype(BF16)
    w_lo = (w - w_hi.astype(F32)).astype(BF16)
    c = c_ref[...]
    c_hi = c.astype(BF16)
    c_lo = (c - c_hi.astype(F32)).astype(BF16)
    acc = jnp.dot(c_hi, w_hi, preferred_element_type=F32)
    acc += jnp.dot(c_hi, w_lo, preferred_element_type=F32)
    acc += jnp.dot(c_lo, w_hi, preferred_element_type=F32)
    o_ref[0] = acc + b_ref[0]


def adaln(cond, w_ada, b_ada):
    depth, d, n6 = w_ada.shape
    tn = _col_tile(n6, 1024)
    return pl.pallas_call(
        _adaln_kernel,
        grid=(depth, n6 // tn),
        in_specs=[pl.BlockSpec((8, d), lambda l, j: (0, 0)),
                  pl.BlockSpec((1, d, tn), lambda l, j: (l, 0, j)),
                  pl.BlockSpec((1, 1, tn), lambda l, j: (l, 0, j))],
        out_specs=pl.BlockSpec((1, 8, tn), lambda l, j: (l, 0, j)),
        out_shape=jax.ShapeDtypeStruct((depth, 8, n6), F32),
        compiler_params=_cparams(2),
        name="adaln",
    )(cond, w_ada, b_ada.reshape(depth, 1, n6))


def _norm_mod(x, g, shift, scale):
    y = x * lax.rsqrt(jnp.mean(x * x, axis=-1, keepdims=True) + NORM_EPS)
    return (y * g) * (1 + scale) + shift


def _rope_tile(x, cos, sin):
    lane = lax.broadcasted_iota(jnp.int32, x.shape, 1)
    nxt = pltpu.roll(x, LANES - 16, axis=1)
    prv = pltpu.roll(x, 16, axis=1)
    partner = jnp.where((lane // 16) % 2 == 0, nxt, prv)
    return x * cos + partner * sin


def _nm_mm_kernel(x_ref, g_ref, sh_ref, sc_ref, w_ref, b_ref, *rest, mode, n_rope, n_alias):
    if n_alias:
        rest = rest[:-3] + rest[-2:]
    if mode == "rope":
        cos_ref, sin_ref, o_ref, h_ref = rest
    else:
        o_ref, h_ref = rest
    j = pl.program_id(1)

    @pl.when(j == 0)
    def _():
        h_ref[...] = _norm_mod(x_ref[...], g_ref[...], sh_ref[0], sc_ref[0]).astype(BF16)

    h = h_ref[...]
    if mode == "glu":
        a = jnp.dot(h, w_ref[0], preferred_element_type=F32) + b_ref[0]
        gate = jnp.dot(h, w_ref[1], preferred_element_type=F32) + b_ref[1]
        o_ref[...] = (a * jax.nn.sigmoid(gate)).astype(o_ref.dtype)
        return
    acc = jnp.dot(h, w_ref[...], preferred_element_type=F32) + b_ref[...]
    if mode == "rope":
        @pl.when(j < n_rope)
        def _():
            cos = cos_ref[...]
            sin = sin_ref[...]
            for g in range(acc.shape[1] // LANES):
                sl = slice(g * LANES, (g + 1) * LANES)
                o_ref[:, sl] = _rope_tile(acc[:, sl], cos, sin).astype(o_ref.dtype)

        @pl.when(j >= n_rope)
        def _():
            o_ref[...] = acc.astype(o_ref.dtype)
    else:
        o_ref[...] = acc.astype(o_ref.dtype)


def nm_matmul(x, g, shift, scale, w, b, *, seg_rows, mode="plain", rope=None, n_rope_cols=0,
              tm=ROW_TILE, tn=None):
    m, d = x.shape
    n_seg = shift.shape[0]
    n = w.shape[-1]
    tn = tn or (_col_tile(math.gcd(n, n_rope_cols), 512) if mode == "rope" else _col_tile(n, 1024))
    big = BIG_ROW_TILE if seg_rows % BIG_ROW_TILE == 0 else tm
    lat = min(m, (m // seg_rows) * seg_rows) if m >= seg_rows else 0
    ranges = [(0, lat, big)] if lat else []
    if m > lat:
        ranges.append((lat, m - lat, tm))
    out = None
    for row_off, rows, rt in ranges:
        blk_off = row_off // rt
        seg = lambda i, j, rt=rt, blk_off=blk_off: (jnp.minimum((blk_off + i) * rt // seg_rows, n_seg - 1), 0, 0)
        row_blk = lambda i, j, blk_off=blk_off: (blk_off + i, 0)
        in_specs = [pl.BlockSpec((rt, d), row_blk),
                    pl.BlockSpec((1, d), lambda i, j: (0, 0)),
                    pl.BlockSpec((1, 1, d), seg),
                    pl.BlockSpec((1, 1, d), seg)]
        if mode == "glu":
            in_specs += [pl.BlockSpec((2, d, tn), lambda i, j: (0, 0, j)),
                         pl.BlockSpec((2, 1, tn), lambda i, j: (0, 0, j))]
        else:
            in_specs += [pl.BlockSpec((d, tn), lambda i, j: (0, j)),
                         pl.BlockSpec((1, tn), lambda i, j: (0, j))]
        args = [x, g, shift, scale, w, b]
        if mode == "rope":
            in_specs += [pl.BlockSpec((rt, LANES), row_blk)] * 2
            args += list(rope)
        aliases = {}
        if out is not None:
            in_specs.append(pl.BlockSpec(memory_space=pl.ANY))
            args.append(out)
            aliases = {len(args) - 1: 0}
        out = pl.pallas_call(
            functools.partial(_nm_mm_kernel, mode=mode, n_rope=n_rope_cols // tn, n_alias=len(aliases)),
            grid=(rows // rt, n // tn),
            in_specs=in_specs,
            out_specs=pl.BlockSpec((rt, tn), lambda i, j, blk_off=blk_off: (blk_off + i, j)),
            out_shape=jax.ShapeDtypeStruct((m, n), BF16),
            input_output_aliases=aliases,
            scratch_shapes=[pltpu.VMEM((rt, d), BF16)],
            compiler_params=_cparams(2),
            name="nm_matmul_" + mode,
        )(*args)
    return out


def _mm_res_kernel(a_ref, w_ref, b_ref, res_ref, gate_ref, o_ref):
    if len(a_ref.shape) == 3:
        kc = a_ref.shape[2]
        acc = b_ref[...]
        for cb in range(a_ref.shape[0]):
            acc = acc + jnp.dot(a_ref[cb], w_ref[cb * kc:(cb + 1) * kc, :], preferred_element_type=F32)
    else:
        acc = jnp.dot(a_ref[...], w_ref[...], preferred_element_type=F32) + b_ref[...]
    o_ref[...] = res_ref[...] + gate_ref[0] * acc


def mm_res(a, w, b, res, gate, *, seg_rows, row_off=0, rows=None, a_off=0, tm=ROW_TILE, tn=None):
    a_blk = a_off // tm
    if a.ndim == 3:
        m = rows or a.shape[1]
        k = a.shape[0] * a.shape[2]
        a_spec = pl.BlockSpec((a.shape[0], tm, a.shape[2]), lambda i, j: (0, a_blk + i, 0))
    else:
        m, k = rows or a.shape[0], a.shape[1]
        a_spec = pl.BlockSpec((tm, k), lambda i, j: (a_blk + i, 0))
    n = w.shape[1]
    n_seg = gate.shape[0]
    tn = tn or _col_tile(n, 1024)
    blk_off = row_off // tm
    seg = lambda i, j: (jnp.minimum((blk_off + i) * tm // seg_rows, n_seg - 1), 0, j)
    return pl.pallas_call(
        _mm_res_kernel,
        grid=(m // tm, n // tn),
        in_specs=[a_spec,
                  pl.BlockSpec((k, tn), lambda i, j: (0, j)),
                  pl.BlockSpec((1, tn), lambda i, j: (0, j)),
                  pl.BlockSpec((tm, tn), lambda i, j: (blk_off + i, j)),
                  pl.BlockSpec((1, 1, tn), seg)],
        out_specs=pl.BlockSpec((tm, tn), lambda i, j: (blk_off + i, j)),
        out_shape=jax.ShapeDtypeStruct(res.shape, F32),
        input_output_aliases={3: 0},
        compiler_params=_cparams(2),
        name="mm_res",
    )(a, w, b, res, gate)


CONV_HALO = 16


def _dwconv_kernel(prev_ref, x_ref, next_ref, w_ref, b_ref, *rest, taps, blocks_per_seq, post, sub, n_alias):
    if n_alias:
        n_tail = 3 if post == "ln_silu" else 2
        rest = rest[:-n_tail - 1] + rest[-n_tail:]
    if post == "ln_silu":
        g_ref, beta_ref, o_ref, win_ref, conv_ref = rest
    else:
        o_ref, win_ref = rest
    i = pl.program_id(0)
    pos = i % blocks_per_seq
    rows, c = x_ref.shape
    half = taps // 2
    zero = jnp.zeros((CONV_HALO, c), F32)
    win_ref[CONV_HALO:CONV_HALO + rows, :] = x_ref[...].astype(F32)

    @pl.when(pos == 0)
    def _():
        win_ref[0:CONV_HALO, :] = zero

    @pl.when(pos > 0)
    def _():
        win_ref[0:CONV_HALO, :] = prev_ref[...].astype(F32)

    @pl.when(pos == blocks_per_seq - 1)
    def _():
        win_ref[CONV_HALO + rows:, :] = zero

    @pl.when(pos < blocks_per_seq - 1)
    def _():
        win_ref[CONV_HALO + rows:, :] = next_ref[...].astype(F32)

    bias = b_ref[...]
    if post == "ln_silu":
        ext = rows + sub
        groups = {}
        for t in range(taps):
            off = CONV_HALO - half + t
            groups.setdefault(off % sub, []).append((t, off // sub))
        lw = min(2 * LANES, c)
        for lc in range(0, c, lw):
            lanes = slice(lc, lc + lw)
            y = jnp.broadcast_to(bias[:, lanes], (rows, lw))
            for s, members in sorted(groups.items()):
                acc = None
                for t, a in members:
                    term = pltpu.repeat(w_ref[t, :, lanes], ext // sub, axis=0) * win_ref[sub * a:sub * a + ext, lanes]
                    acc = term if acc is None else acc + term
                y = y + acc[s:s + rows]
            conv_ref[:, lanes] = y
        for s in range(rows // sub):
            acc = conv_ref[s * sub:(s + 1) * sub, :]
            mu = jnp.mean(acc, axis=-1, keepdims=True)
            xc = acc - mu
            var = jnp.mean(xc * xc, axis=-1, keepdims=True)
            y = xc * lax.rsqrt(var + NORM_EPS) * g_ref[...] + beta_ref[...]
            o_ref[s * sub:(s + 1) * sub, :] = (y * jax.nn.sigmoid(y)).astype(o_ref.dtype)
        return
    for s in range(rows // sub):
        base = CONV_HALO + s * sub - half
        acc = jnp.broadcast_to(bias, (sub, c))
        for t in range(taps):
            acc = acc + w_ref[t] * win_ref[base + t:base + t + sub, :]
        o_ref[s * sub:(s + 1) * sub, :] = acc.astype(o_ref.dtype)


def dwconv(x, w, b, *, start, total, seq_len, post=None, ln=None, rows=256, tc=None, out_dtype=None,
           blocked=False, same_rows=False, into=None):
    m, c = x.shape
    taps = w.shape[0]
    tc = tc or c
    out_dtype = out_dtype or BF16
    sub = 8
    w = jnp.broadcast_to(w[:, None, :], (taps, sub, c))
    r = min(rows, seq_len)
    bps = seq_len // r
    hb = r // CONV_HALO
    off = start // r
    offh = start // CONV_HALO
    nh = m // CONV_HALO
    in_specs = [pl.BlockSpec((CONV_HALO, tc), lambda i, j: (jnp.maximum(offh + i * hb - 1, 0), j)),
                pl.BlockSpec((r, tc), lambda i, j: (off + i, j)),
                pl.BlockSpec((CONV_HALO, tc), lambda i, j: (jnp.minimum(offh + (i + 1) * hb, nh - 1), j)),
                pl.BlockSpec((taps, sub, tc), lambda i, j: (0, 0, j)),
                pl.BlockSpec((1, tc), lambda i, j: (0, j))]
    args = [x, x, x, w, b]
    if post == "ln_silu":
        in_specs += [pl.BlockSpec((1, tc), lambda i, j: (0, j))] * 2
        args += list(ln)
    aliases = {}
    if blocked:
        out_spec = pl.BlockSpec((None, r, tc), lambda i, j: (j, i, 0))
        out_shape = jax.ShapeDtypeStruct((c // tc, total, tc), out_dtype)
    elif same_rows:
        out_spec = pl.BlockSpec((r, tc), lambda i, j: (off + i, j))
        out_shape = jax.ShapeDtypeStruct((m, c), out_dtype)
        if into is not None:
            in_specs.append(pl.BlockSpec(memory_space=pl.ANY))
            args.append(into)
            aliases = {len(args) - 1: 0}
    else:
        out_spec = pl.BlockSpec((r, tc), lambda i, j: (i, j))
        out_shape = jax.ShapeDtypeStruct((total, c), out_dtype)
    return pl.pallas_call(
        functools.partial(_dwconv_kernel, taps=taps, blocks_per_seq=bps, post=post, sub=sub,
                          n_alias=len(aliases)),
        grid=(total // r, c // tc),
        in_specs=in_specs,
        out_specs=out_spec,
        out_shape=out_shape,
        input_output_aliases=aliases,
        scratch_shapes=[pltpu.VMEM((r + 2 * CONV_HALO, tc), F32)]
        + ([pltpu.VMEM((r, tc), F32)] if post == "ln_silu" else []),
        compiler_params=_cparams(2),
        name="dwconv%d" % taps,
    )(*args)


def _masked_halves(q, lane_lo):
    zero = jnp.zeros_like(q)
    return jnp.where(lane_lo, q, zero), jnp.where(lane_lo, zero, q)


def _softmax_pv(s_parts, v_parts, sink):
    m = s_parts[0].max(axis=-1, keepdims=True)
    for s in s_parts[1:]:
        m = jnp.maximum(m, s.max(axis=-1, keepdims=True))
    if sink is not None:
        m = jnp.maximum(m, sink)
    denom = jnp.exp(sink - m) if sink is not None else 0.0
    o = None
    for s, v in zip(s_parts, v_parts):
        p = jnp.exp(s - m)
        denom = denom + p.sum(axis=-1, keepdims=True)
        pv = jnp.dot(p.astype(BF16), v, preferred_element_type=F32)
        o = pv if o is None else o + pv
    return o / denom


def _nt_dot(a, b):
    return lax.dot_general(a, b, (((1,), (1,)), ((), ())), preferred_element_type=F32)


def _wattn_kernel(sink_ref, q_ref, *rest, local, seq_len, blk):
    if local:
        n_kb = (len(rest) - 3) // 2
        k_refs, v_refs = rest[:n_kb], rest[n_kb:2 * n_kb]
        kc_ref, vc_ref, o_ref = rest[2 * n_kb:]
    else:
        kc_ref, vc_ref, _, o_ref = rest
    p = pl.program_id(2)
    i = pl.program_id(1)
    lane_lo = lax.broadcasted_iota(jnp.int32, (blk, LANES), 1) < HEAD_DIM
    kc = kc_ref[...]
    vc = vc_ref[...]
    if local:
        kl = jnp.concatenate([r[...] for r in k_refs], axis=0)
        vl = jnp.concatenate([r[...] for r in v_refs], axis=0)
        n_keys = n_kb * WA_WINDOW
        qpos = i * blk + lax.broadcasted_iota(jnp.int32, (blk, n_keys), 0)
        kpos = i * blk - WA_WINDOW + lax.broadcasted_iota(jnp.int32, (blk, n_keys), 1)
        valid = (jnp.abs(kpos - qpos) <= WA_WINDOW) & (kpos >= 0) & (kpos < seq_len)
        valid = jnp.concatenate([valid] * WA_GROUP, axis=0)
    qa, qb = [], []
    for g in range(WA_GROUP):
        a, b = _masked_halves(q_ref[:, g * LANES:(g + 1) * LANES], lane_lo)
        qa.append(a)
        qb.append(b)
    outs = []
    for half, qs in enumerate((qa, qb)):
        qs = jnp.concatenate(qs, axis=0)
        sink = jnp.concatenate(
            [jnp.full((blk, 1), sink_ref[8 * p + 4 * half + g], F32) for g in range(WA_GROUP)], axis=0)
        s_parts, v_parts = [], []
        if local:
            s_parts.append(jnp.where(valid, _nt_dot(qs, kl), NEG_INF))
            v_parts.append(vl)
        s_parts.append(_nt_dot(qs, kc))
        v_parts.append(vc)
        outs.append(_softmax_pv(s_parts, v_parts, sink))
    for g in range(WA_GROUP):
        rows = slice(g * blk, (g + 1) * blk)
        o_ref[:, g * LANES:(g + 1) * LANES] = jnp.where(lane_lo, outs[0][rows], outs[1][rows]).astype(o_ref.dtype)


def window_attention(qkv, sinks, *, batch, seq_len, ctx_len, d, blk=512):
    n_pairs = d // (2 * WA_GROUP * HEAD_DIM)
    kcol = d // LANES
    vcol = kcol + n_pairs
    nblk = seq_len // blk
    kb_per_q = blk // WA_WINDOW
    n_kblk = seq_len // WA_WINDOW
    shifts = range(-1, kb_per_q + 1)
    cb0 = batch * seq_len // ctx_len
    qw = WA_GROUP * LANES
    kern = functools.partial(_wattn_kernel, seq_len=seq_len)
    smem = pl.BlockSpec(memory_space=pltpu.SMEM)

    def kspec(col0, shift):
        return pl.BlockSpec((WA_WINDOW, LANES),
                            lambda b, i, p: (b * n_kblk + jnp.clip(i * kb_per_q + shift, 0, n_kblk - 1), col0 + p))

    ctx_k = pl.BlockSpec((ctx_len, LANES), lambda b, i, p: (cb0 + b, kcol + p))
    ctx_v = pl.BlockSpec((ctx_len, LANES), lambda b, i, p: (cb0 + b, vcol + p))
    lat = pl.pallas_call(
        functools.partial(kern, local=True, blk=blk),
        grid=(batch, nblk, n_pairs),
        in_specs=[smem, pl.BlockSpec((blk, qw), lambda b, i, p: (b * nblk + i, p))]
        + [kspec(kcol, s) for s in shifts] + [kspec(vcol, s) for s in shifts] + [ctx_k, ctx_v],
        out_specs=pl.BlockSpec((blk, qw), lambda b, i, p: (b * nblk + i, p)),
        out_shape=jax.ShapeDtypeStruct((qkv.shape[0], d), BF16),
        compiler_params=_cparams(3),
        name="window_attn",
    )(sinks, qkv, *([qkv] * (2 * len(shifts) + 2)))
    return pl.pallas_call(
        functools.partial(kern, local=False, blk=ctx_len),
        grid=(batch, 1, n_pairs),
        in_specs=[smem, pl.BlockSpec((ctx_len, qw), lambda b, i, p: (cb0 + b, p)), ctx_k, ctx_v,
                  pl.BlockSpec(memory_space=pl.ANY)],
        out_specs=pl.BlockSpec((ctx_len, qw), lambda b, i, p: (cb0 + b, p)),
        out_shape=jax.ShapeDtypeStruct((qkv.shape[0], d), BF16),
        input_output_aliases={4: 0},
        compiler_params=_cparams(3),
        name="ctx_attn",
    )(sinks, qkv, qkv, qkv, lat)


def _nattn_kernel(q_ref, k0, k1, k2, v0, v1, v2, kc_ref, vc_ref, bias_ref, o_ref, ks_ref, vs_ref,
                  *, grid_rows, rows_per_blk):
    j = pl.program_id(2)
    blk = k0.shape[0]
    for t, (kr, vr) in enumerate(((k0, v0), (k1, v1), (k2, v2))):
        ks_ref[t * blk:(t + 1) * blk, :] = kr[...]
        vs_ref[t * blk:(t + 1) * blk, :] = vr[...]
    kc = kc_ref[...]
    vc = vc_ref[...]
    lane_lo = lax.broadcasted_iota(jnp.int32, (GRID_W, LANES), 1) < HEAD_DIM
    strip = NA_WIN_ROWS * GRID_W
    for r in range(rows_per_blk):
        row = j * rows_per_blk + r
        r0 = jnp.clip(row - NA_WIN_ROWS // 2, 0, grid_rows - NA_WIN_ROWS)
        start = pl.multiple_of((r0 - (j - 1) * rows_per_blk) * GRID_W, GRID_W)
        cls = row - r0
        qa, qb = _masked_halves(q_ref[r * GRID_W:(r + 1) * GRID_W, :], lane_lo)
        qs = jnp.concatenate([qa, qb], axis=0)
        kn = ks_ref[pl.ds(start, strip), :]
        vn = vs_ref[pl.ds(start, strip), :]
        s_nb = _nt_dot(qs, kn) + bias_ref[cls]
        s_cx = _nt_dot(qs, kc)
        o = _softmax_pv([s_nb, s_cx], [vn, vc], None)
        o_ref[r * GRID_W:(r + 1) * GRID_W, :] = jnp.where(lane_lo, o[:GRID_W], o[GRID_W:]).astype(o_ref.dtype)


def _na_bias_table(rpb):
    h = rpb.shape[0]
    n_dcol = 2 * NA_WIN_COLS - 1
    cols = jnp.arange(GRID_W)
    col_start = jnp.clip(cols - NA_WIN_COLS // 2, 0, GRID_W - NA_WIN_COLS)
    inwin = (cols[None, :] >= col_start[:, None]) & (cols[None, :] < col_start[:, None] + NA_WIN_COLS)
    dcol = cols[None, :] - cols[:, None] + NA_WIN_COLS - 1
    pick = (dcol[None] == jnp.arange(n_dcol)[:, None, None]).astype(F32).reshape(n_dcol, -1)
    spread = jnp.dot(rpb.astype(F32).reshape(-1, n_dcol), pick, precision=lax.Precision.HIGHEST)
    spread = spread.reshape(h, 2 * NA_WIN_ROWS - 1, GRID_W, GRID_W)
    spread = jnp.where(inwin[None, None], spread, NEG_INF)
    tab = jnp.stack([spread[:, NA_WIN_ROWS - 1 - cls:2 * NA_WIN_ROWS - 1 - cls]
                     for cls in range(NA_WIN_ROWS)], axis=1)
    tab = tab.transpose(0, 1, 3, 2, 4).reshape(h, NA_WIN_ROWS, GRID_W, NA_WIN_ROWS * GRID_W)
    tab = tab.reshape(h // 2, 2, NA_WIN_ROWS, GRID_W, NA_WIN_ROWS * GRID_W).transpose(0, 2, 1, 3, 4)
    return tab.reshape(h // 2, NA_WIN_ROWS, 2 * GRID_W, NA_WIN_ROWS * GRID_W)


def neighbourhood_attention(qkv, rpb, *, batch, seq_len, ctx_len, d, with_ctx_out):
    n_pairs = d // LANES
    rows_per_blk = 2 * NA_WIN_ROWS
    blk = rows_per_blk * GRID_W
    grid_rows = seq_len // GRID_W
    nblk = seq_len // blk
    cb0 = batch * seq_len // ctx_len
    bias = _na_bias_table(rpb)

    def kspec(col0, shift):
        return pl.BlockSpec((blk, LANES),
                            lambda b, p, j: (b * nblk + jnp.clip(j + shift, 0, nblk - 1), col0 + p))

    ctx_k = pl.BlockSpec((ctx_len, LANES), lambda b, p, j: (cb0 + b, n_pairs + p))
    ctx_v = pl.BlockSpec((ctx_len, LANES), lambda b, p, j: (cb0 + b, 2 * n_pairs + p))
    lat = pl.pallas_call(
        functools.partial(_nattn_kernel, grid_rows=grid_rows, rows_per_blk=rows_per_blk),
        grid=(batch, n_pairs, nblk),
        in_specs=[pl.BlockSpec((blk, LANES), lambda b, p, j: (b * nblk + j, p))]
        + [kspec(n_pairs, s) for s in (-1, 0, 1)] + [kspec(2 * n_pairs, s) for s in (-1, 0, 1)]
        + [ctx_k, ctx_v,
           pl.BlockSpec((None, NA_WIN_ROWS, 2 * GRID_W, NA_WIN_ROWS * GRID_W), lambda b, p, j: (p, 0, 0, 0))],
        out_specs=pl.BlockSpec((blk, LANES), lambda b, p, j: (b * nblk + j, p)),
        out_shape=jax.ShapeDtypeStruct((batch * seq_len, d), BF16),
        scratch_shapes=[pltpu.VMEM((3 * blk, LANES), BF16), pltpu.VMEM((3 * blk, LANES), BF16)],
        compiler_params=_cparams(3),
        name="neighbourhood_attn",
    )(qkv, *([qkv] * 8), bias)
    if not with_ctx_out:
        return lat
    ctx = pl.pallas_call(
        _cattn_kernel,
        grid=(batch, n_pairs),
        in_specs=[pl.BlockSpec((ctx_len, LANES), lambda b, p: (cb0 + b, p)),
                  pl.BlockSpec((ctx_len, LANES), lambda b, p: (cb0 + b, n_pairs + p)),
                  pl.BlockSpec((ctx_len, LANES), lambda b, p: (cb0 + b, 2 * n_pairs + p))],
        out_specs=pl.BlockSpec((ctx_len, LANES), lambda b, p: (b, p)),
        out_shape=jax.ShapeDtypeStruct((batch * ctx_len, d), BF16),
        compiler_params=_cparams(2),
        name="ctx_mha",
    )(qkv, qkv, qkv)
    return jnp.concatenate([lat, ctx], axis=0)


def _cattn_kernel(q_ref, k_ref, v_ref, o_ref):
    rows = q_ref.shape[0]
    lane_lo = lax.broadcasted_iota(jnp.int32, (rows, LANES), 1) < HEAD_DIM
    qa, qb = _masked_halves(q_ref[...], lane_lo)
    qs = jnp.concatenate([qa, qb], axis=0)
    o = _softmax_pv([_nt_dot(qs, k_ref[...])], [v_ref[...]], None)
    o_ref[...] = jnp.where(lane_lo, o[:rows], o[rows:]).astype(o_ref.dtype)


def _router_kernel(x_ref, g_ref, sh_ref, sc_ref, wh_ref, wl_ref, b_ref, h_ref, idx_ref, gate_ref, cnt_ref,
                   run_ref):
    @pl.when(pl.program_id(0) == 0)
    def _():
        run_ref[...] = jnp.zeros_like(run_ref)

    h = _norm_mod(x_ref[...], g_ref[...], sh_ref[0], sc_ref[0])
    h_hi = h.astype(BF16)
    h_ref[...] = h_hi
    h_lo = (h - h_hi.astype(F32)).astype(BF16)
    logits = (jnp.dot(h_hi, wh_ref[...], preferred_element_type=F32)
              + jnp.dot(h_hi, wl_ref[...], preferred_element_type=F32)
              + jnp.dot(h_lo, wh_ref[...], preferred_element_type=F32)) + b_ref[...]
    lane = lax.broadcasted_iota(jnp.int32, logits.shape, 1)
    tm = logits.shape[0]
    idx_out = jnp.zeros(logits.shape, jnp.int32)
    val_out = jnp.zeros(logits.shape, F32)
    tri = jnp.where(lax.broadcasted_iota(jnp.int32, (tm, tm), 0) > lax.broadcasted_iota(jnp.int32, (tm, tm), 1),
                    1.0, 0.0).astype(BF16)
    run = run_ref[...]
    top = None
    denom = 0.0
    for k in range(TOP_K):
        m = logits.max(axis=-1, keepdims=True)
        sel = jnp.min(jnp.where(logits == m, lane, LANES), axis=-1, keepdims=True)
        if top is None:
            top = m
        e = jnp.exp(m - top)
        denom = denom + e
        onehot = lane == sel
        before = jnp.dot(tri, jnp.where(onehot, 1.0, 0.0).astype(BF16), preferred_element_type=F32)
        rank = jnp.sum(jnp.where(onehot, before + run, 0.0), axis=-1, keepdims=True).astype(jnp.int32)
        run = run + jnp.sum(jnp.where(onehot, 1.0, 0.0), axis=0, keepdims=True)
        idx_out = jnp.where(lane == k, sel, idx_out)
        idx_out = jnp.where(lane == TOP_K + k, rank, idx_out)
        val_out = jnp.where(lane == k, e, val_out)
        logits = jnp.where(onehot, -jnp.inf, logits)
    run_ref[...] = run
    cnt_ref[...] = run
    idx_ref[...] = idx_out
    gate_ref[...] = val_out / denom


def router(x, g, shift, scale, w_router, b_router, *, seg_rows, rows, row_off=0, tm=ROW_TILE):
    m, d = rows, x.shape[1]
    n_seg = shift.shape[0]
    n_e = w_router.shape[1]
    w_pad = jnp.zeros((d, LANES), F32).at[:, :n_e].set(w_router)
    w_hi = w_pad.astype(BF16)
    w_lo = (w_pad - w_hi.astype(F32)).astype(BF16)
    b_pad = jnp.full((1, LANES), -jnp.inf, F32).at[0, :n_e].set(b_router)
    blk_off = row_off // tm
    seg = lambda i: (jnp.minimum((blk_off + i) * tm // seg_rows, n_seg - 1), 0, 0)
    return pl.pallas_call(
        _router_kernel,
        grid=(m // tm,),
        in_specs=[pl.BlockSpec((tm, d), lambda i: (blk_off + i, 0)),
                  pl.BlockSpec((1, d), lambda i: (0, 0)),
                  pl.BlockSpec((1, 1, d), seg),
                  pl.BlockSpec((1, 1, d), seg),
                  pl.BlockSpec((d, LANES), lambda i: (0, 0)),
                  pl.BlockSpec((d, LANES), lambda i: (0, 0)),
                  pl.BlockSpec((1, LANES), lambda i: (0, 0))],
        out_specs=[pl.BlockSpec((tm, d), lambda i: (i, 0)),
                   pl.BlockSpec((tm, LANES), lambda i: (i, 0)),
                   pl.BlockSpec((tm, LANES), lambda i: (i, 0)),
                   pl.BlockSpec((1, LANES), lambda i: (0, 0))],
        out_shape=[jax.ShapeDtypeStruct((m, d), BF16),
                   jax.ShapeDtypeStruct((m, LANES), jnp.int32),
                   jax.ShapeDtypeStruct((m, LANES), F32),
                   jax.ShapeDtypeStruct((1, LANES), F32)],
        scratch_shapes=[pltpu.VMEM((1, LANES), F32)],
        compiler_params=_cparams(1),
        name="router",
    )(x, g, shift, scale, w_hi, w_lo, b_pad)


def _expert_kernel(blk_e_ref, n_used_ref, x_ref, wgu_ref, bgu_ref, wd_ref, bd_ref, o_ref, wgu_s, wd_s, *, f):
    i = pl.program_id(0)
    used = i < n_used_ref[0]

    @pl.when(used & ((i == 0) | (blk_e_ref[i] != blk_e_ref[jnp.maximum(i - 1, 0)])))
    def _():
        wgu_s[...] = wgu_ref[0].astype(BF16)
        wd_s[...] = wd_ref[0].astype(BF16)

    @pl.when(used)
    def _():
        gu = jnp.dot(x_ref[...], wgu_s[...], preferred_element_type=F32) + bgu_ref[0]
        g = jnp.minimum(gu[:, :f], SWIGLU_LIMIT)
        u = jnp.clip(gu[:, f:], -SWIGLU_LIMIT, SWIGLU_LIMIT)
        act = (u + 1) * (g * jax.nn.sigmoid(SWIGLU_ALPHA * g))
        y = jnp.dot(act.astype(BF16), wd_s[...], preferred_element_type=F32) + bd_ref[0]
        o_ref[...] = y.astype(o_ref.dtype)

    @pl.when(i >= n_used_ref[0])
    def _():
        o_ref[...] = jnp.zeros_like(o_ref)


def expert_mlp(xs, blk_e, n_used, w_gu, b_gu, w_down, b_down, *, layer, bm=MOE_ROWS):
    n_slot, d = xs.shape
    _, n_e, _, f2 = w_gu.shape
    f = f2 // 2
    return pl.pallas_call(
        functools.partial(_expert_kernel, f=f),
        grid_spec=pltpu.PrefetchScalarGridSpec(
            num_scalar_prefetch=2,
            grid=(n_slot // bm,),
            in_specs=[pl.BlockSpec((bm, d), lambda i, be, nu: (i, 0)),
                      pl.BlockSpec((None, 1, d, f2), lambda i, be, nu: (layer, be[i], 0, 0)),
                      pl.BlockSpec((1, 1, f2), lambda i, be, nu: (be[i], 0, 0)),
                      pl.BlockSpec((None, 1, f, d), lambda i, be, nu: (layer, be[i], 0, 0)),
                      pl.BlockSpec((1, 1, d), lambda i, be, nu: (be[i], 0, 0))],
            out_specs=pl.BlockSpec((bm, d), lambda i, be, nu: (i, 0)),
            scratch_shapes=[pltpu.VMEM((d, f2), BF16), pltpu.VMEM((f, d), BF16)]),
        out_shape=jax.ShapeDtypeStruct((n_slot, d), BF16),
        compiler_params=pltpu.CompilerParams(dimension_semantics=("arbitrary",),
                                             vmem_limit_bytes=EXPERT_VMEM_LIMIT),
        name="expert_mlp",
    )(blk_e, n_used, xs, w_gu, b_gu.reshape(n_e, 1, f2), w_down, b_down.reshape(n_e, 1, d))


def _combine_kernel(res_ref, y_ref, w_ref, gate_ref, o_ref):
    w = w_ref[...]
    y = w[:, 0:1] * y_ref[0].astype(F32)
    for k in range(1, TOP_K):
        y = y + w[:, k:k + 1] * y_ref[k].astype(F32)
    o_ref[...] = res_ref[...] + gate_ref[0] * y


def moe_dispatch(x, g, shift, scale, w_router, b_router, *, seg_rows, rows, row_off, bm=MOE_ROWS):
    m = rows
    n_e = w_router.shape[1]
    h, route, gates, counts = router(x, g, shift, scale, w_router, b_router, seg_rows=seg_rows, rows=rows,
                                     row_off=row_off)
    idx = route[:, :TOP_K]
    rank = route[:, TOP_K:2 * TOP_K]
    n_asg = m * TOP_K
    counts = counts[0, :n_e].astype(jnp.int32)
    padded = (counts + bm - 1) // bm * bm
    pad_end = jnp.cumsum(padded)
    pad_start = pad_end - padded
    start_of = jnp.sum(jnp.where(idx[:, :, None] == jnp.arange(n_e, dtype=jnp.int32), pad_start, 0), axis=-1)
    dest = (start_of + rank).reshape(-1)
    n_blk = -(-(n_asg + n_e * (bm - 1)) // bm)
    n_slot = n_blk * bm
    tok = (jnp.arange(n_asg, dtype=jnp.int32) // TOP_K)
    slot_tok = (jnp.arange(n_slot, dtype=jnp.int32) % m).at[dest].set(tok, unique_indices=True)
    blk_start = jnp.arange(n_blk, dtype=jnp.int32) * bm
    blk_e = jnp.minimum(jnp.sum((pad_end[None, :] <= blk_start[:, None]).astype(jnp.int32), axis=1), n_e - 1)
    n_used = (pad_end[-1] // bm).astype(jnp.int32).reshape(1)
    xs = h.at[slot_tok].get(mode="promise_in_bounds")
    dest_km = dest.reshape(m, TOP_K).T.reshape(-1)
    return xs, blk_e, n_used, dest_km, gates


def moe_combine(x, y_slot, dest_km, gates, gate, *, seg_rows, rows, row_off):
    m, d = rows, x.shape[1]
    y_tok = y_slot.at[dest_km].get(mode="promise_in_bounds").reshape(TOP_K, m, d)
    tm = 256
    n_seg = gate.shape[0]
    blk_off = row_off // tm
    seg = lambda i: (jnp.minimum((blk_off + i) * tm // seg_rows, n_seg - 1), 0, 0)
    return pl.pallas_call(
        _combine_kernel,
        grid=(m // tm,),
        in_specs=[pl.BlockSpec((tm, d), lambda i: (blk_off + i, 0)),
                  pl.BlockSpec((TOP_K, tm, d), lambda i: (0, i, 0)),
                  pl.BlockSpec((tm, LANES), lambda i: (i, 0)),
                  pl.BlockSpec((1, 1, d), seg)],
        out_specs=pl.BlockSpec((tm, d), lambda i: (blk_off + i, 0)),
        out_shape=jax.ShapeDtypeStruct(x.shape, F32),
        input_output_aliases={0: 0},
        compiler_params=_cparams(1),
        name="moe_combine",
    )(x, y_tok, gates, gate)


LC_R2 = 128


@functools.lru_cache(maxsize=None)
def _dft_small_mats(n):
    nn = 2 * n
    ang = 2.0 * np.pi * ((np.arange(nn)[:, None] * np.arange(nn)[None, :]) % nn) / nn
    fwd = np.concatenate([np.cos(ang), -np.sin(ang)], 0)
    inv = np.concatenate([np.cos(ang[:n]), -np.sin(ang[:n])], 1) / nn
    return jnp.asarray(fwd, BF16), jnp.asarray(inv, BF16)


def _plain_mm_kernel(a_ref, b_ref, o_ref):
    o_ref[...] = jnp.dot(a_ref[...], b_ref[...].astype(BF16), preferred_element_type=F32)


def plain_mm(a, b, tn=512):
    m, k = a.shape
    n = b.shape[1]
    tn = _col_tile(n, tn)
    return pl.pallas_call(
        _plain_mm_kernel,
        grid=(n // tn,),
        in_specs=[pl.BlockSpec((m, k), lambda j: (0, 0)), pl.BlockSpec((k, tn), lambda j: (0, j))],
        out_specs=pl.BlockSpec((m, tn), lambda j: (0, j)),
        out_shape=jax.ShapeDtypeStruct((m, n), F32),
        compiler_params=_cparams(1),
        name="plain_mm",
    )(a, b)


def _short_conv_kernel(u_ref, gate_ref, h_ref, skip_ref, fwd_ref, inv_ref, o_ref):
    u = u_ref[...]
    x = jnp.dot(fwd_ref[...], u.astype(BF16), preferred_element_type=F32)
    h = h_ref[...]
    nn = x.shape[0] // 2
    xr, xi, hr, hi = x[:nn], x[nn:], h[:nn], h[nn:]
    prod = jnp.concatenate([xr * hr - xi * hi, xr * hi + xi * hr], axis=0).astype(BF16)
    y = jnp.dot(inv_ref[...], prod, preferred_element_type=F32)
    o_ref[...] = (gate_ref[...].astype(F32) * (y + skip_ref[...] * u.astype(F32))).astype(o_ref.dtype)


def short_long_conv(u_arr, u_col, gate_arr, gate_col, spec, skip, fwd, inv, *, batch, seq_len, tl=512):
    d = spec.shape[1]
    tl = _col_tile(d, tl)
    uc, gc = u_col // tl, gate_col // tl
    nn2 = spec.shape[0]
    return pl.pallas_call(
        _short_conv_kernel,
        grid=(batch, d // tl),
        in_specs=[pl.BlockSpec((seq_len, tl), lambda b, j: (b, uc + j)),
                  pl.BlockSpec((seq_len, tl), lambda b, j: (b, gc + j)),
                  pl.BlockSpec((nn2, tl), lambda b, j: (0, j)),
                  pl.BlockSpec((1, tl), lambda b, j: (0, j)),
                  pl.BlockSpec((nn2, seq_len), lambda b, j: (0, 0)),
                  pl.BlockSpec((seq_len, nn2), lambda b, j: (0, 0))],
        out_specs=pl.BlockSpec((seq_len, tl), lambda b, j: (b, j)),
        out_shape=jax.ShapeDtypeStruct((batch * seq_len, d), BF16),
        compiler_params=_cparams(2),
        name="short_long_conv",
    )(u_arr, gate_arr, spec, skip, fwd[:, :seq_len], inv)


def _hyena_taps(n, d, f_w1, f_b1, f_w2, f_b2, f_w3, f_b3, f_freq, f_w4):
    hp = lax.Precision.HIGHEST
    lin = jnp.linspace(0.0, 1.0, n, dtype=F32)
    idx = jnp.arange(n, dtype=F32)
    bands = jnp.linspace(1e-4, HY_BANDS - 1, HY_BANDS, dtype=F32)[None, :]
    deltas = jnp.abs(jnp.linspace(math.log(HY_DECAY_TARGET) / HY_FAST_DECAY,
                                  math.log(HY_DECAY_TARGET) / HY_SLOW_DECAY, d, dtype=F32))

    def branch(t, pos, direction):
        t = t[:, None]
        ang = (2.0 * math.pi / n) * pos[:, None]
        emb = jnp.concatenate([t, jnp.cos(bands * ang), -jnp.sin(bands * ang)], axis=-1)
        a = jnp.sin(f_freq * (jnp.dot(emb, f_w1, precision=hp) + f_b1))
        a = jnp.sin(f_freq * (jnp.dot(a, f_w2, precision=hp) + f_b2))
        a = jnp.sin(f_freq * (jnp.dot(a, f_w3, precision=hp) + f_b3))
        decay = jnp.exp(-t * deltas[None, :])
        w4 = f_w4.reshape(f_w4.shape[0], HY_ORDER, 2, d)[:, :, direction]
        return [jnp.dot(a, w4[:, o], precision=hp) * decay for o in range(HY_ORDER)]

    fwd = branch(lin, idx, 0)
    bwd = branch(lin[:0:-1], idx[:0:-1], 1)
    zero = jnp.zeros((1, d), F32)
    return [jnp.concatenate([fwd[o], zero, bwd[o]], axis=0) for o in range(HY_ORDER)]


HC = 128
HC_PAIR = 2
LC_TL = 16
HY_JN = 2


@functools.lru_cache(maxsize=None)
def _hy_mats(n):
    nn = 2 * n
    r1 = nn // LC_R2
    tau = 2.0 * np.pi
    k1 = np.arange(r1)
    t = LC_R2 * np.arange(r1)[None, :] + np.arange(LC_R2)[:, None]
    ang1 = tau * ((k1[None, :, None] * t[:, None, :]) % nn) / nn
    f1 = np.empty((LC_R2, 2 * r1, r1))
    f1[:, 0::2] = np.cos(ang1)
    f1[:, 1::2] = -np.sin(ang1)
    ang2 = tau * ((np.arange(LC_R2)[:, None] * np.arange(LC_R2)[None, :]) % LC_R2) / LC_R2
    c, s = np.cos(ang2), np.sin(ang2)
    f2 = np.empty((2 * LC_R2, 2 * LC_R2))
    f2[:LC_R2, 0::2], f2[:LC_R2, 1::2] = c, s
    f2[LC_R2:, 0::2], f2[LC_R2:, 1::2] = -s, c
    f3 = np.empty((2 * LC_R2, 2 * LC_R2))
    f3[0::2, :LC_R2], f3[0::2, LC_R2:] = c, -s
    f3[1::2, :LC_R2], f3[1::2, LC_R2:] = s, c
    m = LC_R2 * np.arange(r1 // 2)[None, :] + np.arange(LC_R2)[:, None]
    ang4 = tau * ((m[:, :, None] * k1[None, None, :]) % nn) / nn
    f4 = np.empty((LC_R2, r1 // 2, 2 * r1))
    f4[:, :, 0::2] = np.cos(ang4) / nn
    f4[:, :, 1::2] = -np.sin(ang4) / nn
    return tuple(jnp.asarray(a, BF16) for a in (f1, f2, f3, f4))


def _hy1_kernel(x_ref, f_ref, o_ref, *, tl_n, nh):
    c = pl.program_id(2)
    for tl in range(tl_n):
        xs = jnp.concatenate([x_ref[h, pl.ds(c * tl_n + tl, nh, stride=LC_R2), :] for h in range(HC_PAIR)],
                             axis=1).astype(BF16)
        a = jnp.dot(f_ref[tl], xs, preferred_element_type=F32)
        packed = pltpu.bitcast(a.astype(BF16), jnp.uint32)
        for h in range(HC_PAIR):
            for j in range(packed.shape[0] // 8):
                o_ref[h, j, tl * 8:(tl + 1) * 8, :] = packed[j * 8:(j + 1) * 8, h * HC:(h + 1) * HC]


def hy_stage1(x, f1, *, blk_off, n_blk, batch, rows_hi):
    r2 = f1.shape[1]
    rows = rows_hi * LC_R2
    tl_n = LC_TL
    return pl.pallas_call(
        functools.partial(_hy1_kernel, tl_n=tl_n, nh=rows_hi),
        grid=(n_blk // HC_PAIR, batch, LC_R2 // tl_n),
        in_specs=[pl.BlockSpec((HC_PAIR, rows, HC), lambda p, b, c: (blk_off // HC_PAIR + p, b, 0)),
                  pl.BlockSpec((tl_n, r2, rows_hi), lambda p, b, c: (c, 0, 0))],
        out_specs=pl.BlockSpec((HC_PAIR, None, r2 // 16, tl_n * 8, HC), lambda p, b, c: (p, b, 0, c, 0)),
        out_shape=jax.ShapeDtypeStruct((n_blk, batch, r2 // 16, LC_R2 * 8, HC), jnp.uint32),
        compiler_params=_cparams(3),
        name="hy_stage1",
    )(x, f1[:, :, :rows_hi])


def _hy2_kernel(a_ref, f2_ref, *rest, spectrum_only):
    if spectrum_only:
        (o_ref,) = rest
    else:
        h_ref, f3_ref, o_ref = rest
    for jk in range(8 * a_ref.shape[1]):
        jj, kk = divmod(jk, 8)
        w = jnp.concatenate([a_ref[h, jj, pl.ds(kk, LC_R2, stride=8), :] for h in range(HC_PAIR)], axis=1)
        a_in = pltpu.bitcast(w, BF16)
        x = jnp.dot(f2_ref[...], a_in, preferred_element_type=F32)
        if spectrum_only:
            o_ref[jk] = x.astype(o_ref.dtype)
            continue
        hs = h_ref[jk].astype(F32)
        xr, xi = x[:LC_R2], x[LC_R2:]
        hr, hi = hs[:LC_R2], hs[LC_R2:]
        prod = jnp.concatenate([xr * hr - xi * hi, xr * hi + xi * hr], axis=0).astype(BF16)
        g = jnp.dot(f3_ref[...], prod, preferred_element_type=F32)
        packed = pltpu.bitcast(g.astype(BF16), jnp.uint32)
        for h in range(HC_PAIR):
            o_ref[h, jj, pl.ds(kk, LC_R2, stride=8), :] = packed[:, h * HC:(h + 1) * HC]


def hy_spectrum(a, f2):
    cb, _, nj, rows, _ = a.shape
    jn = HY_JN if nj % HY_JN == 0 else 1
    mat = pl.BlockSpec((2 * LC_R2, 2 * LC_R2), lambda j, p: (0, 0))
    return pl.pallas_call(
        functools.partial(_hy2_kernel, spectrum_only=True),
        grid=(nj // jn, cb // HC_PAIR),
        in_specs=[pl.BlockSpec((HC_PAIR, None, jn, rows, HC), lambda j, p: (p, 0, j, 0, 0)), mat],
        out_specs=pl.BlockSpec((None, 8 * jn, 2 * LC_R2, HC_PAIR * HC), lambda j, p: (p, j, 0, 0)),
        out_shape=jax.ShapeDtypeStruct((cb // HC_PAIR, nj * 8, 2 * LC_R2, HC_PAIR * HC), BF16),
        compiler_params=_cparams(2),
        name="hy_spectrum",
    )(a, f2)


def hy_stage23(a, spec, f2, f3):
    cb, bsz, nj, rows, _ = a.shape
    jn = HY_JN if nj % HY_JN == 0 else 1
    mat = pl.BlockSpec((2 * LC_R2, 2 * LC_R2), lambda j, p, b: (0, 0))
    blk = pl.BlockSpec((HC_PAIR, None, jn, rows, HC), lambda j, p, b: (p, b, j, 0, 0))
    return pl.pallas_call(
        functools.partial(_hy2_kernel, spectrum_only=False),
        grid=(nj // jn, cb // HC_PAIR, bsz),
        in_specs=[blk, mat,
                  pl.BlockSpec((None, 8 * jn, 2 * LC_R2, HC_PAIR * HC), lambda j, p, b: (p, j, 0, 0)), mat],
        out_specs=blk,
        out_shape=jax.ShapeDtypeStruct(a.shape, jnp.uint32),
        compiler_params=_cparams(3),
        name="hy_stage23",
    )(a, f2, spec, f3)


def _hy4_kernel(g_ref, f_ref, o_ref, *, tl_n, nh):
    c = pl.program_id(2)
    for tl in range(tl_n):
        w = jnp.concatenate([g_ref[h, :, tl * 8:(tl + 1) * 8, :].reshape(-1, HC) for h in range(HC_PAIR)], axis=1)
        g_in = pltpu.bitcast(w, BF16)
        y = jnp.dot(f_ref[tl], g_in, preferred_element_type=F32)
        for h in range(HC_PAIR):
            o_ref[h, pl.ds(c * tl_n + tl, nh, stride=LC_R2), :] = y[:, h * HC:(h + 1) * HC]


def hy_stage4(g, f4, *, seq_len):
    cb, bsz, nj, _, _ = g.shape
    nh = seq_len // LC_R2
    tl_n = LC_TL
    return pl.pallas_call(
        functools.partial(_hy4_kernel, tl_n=tl_n, nh=nh),
        grid=(cb // HC_PAIR, bsz, LC_R2 // tl_n),
        in_specs=[pl.BlockSpec((HC_PAIR, None, nj, tl_n * 8, HC), lambda p, b, c: (p, b, 0, c, 0)),
                  pl.BlockSpec((tl_n, nh, 16 * nj), lambda p, b, c: (c, 0, 0))],
        out_specs=pl.BlockSpec((HC_PAIR, seq_len, HC), lambda p, b, c: (p, b, 0)),
        out_shape=jax.ShapeDtypeStruct((cb, bsz * seq_len, HC), F32),
        compiler_params=_cparams(3),
        name="hy_stage4",
    )(g, f4)


def _hy_gate_kernel(y_ref, u_ref, gate_ref, skip_ref, o_ref):
    o_ref[...] = (gate_ref[...] * (y_ref[...] + skip_ref[...] * u_ref[...])).astype(o_ref.dtype)


def hy_gate(y, u_arr, u_off, gate_arr, gate_off, skip, *, natural, tr=4096):
    cb, rows, _ = y.shape
    tr = min(tr, rows)
    if natural:
        out_spec = pl.BlockSpec((tr, HC), lambda i, r: (r, i))
        out_shape = jax.ShapeDtypeStruct((rows, cb * HC), BF16)
    else:
        out_spec = pl.BlockSpec((None, tr, HC), lambda i, r: (i, r, 0))
        out_shape = jax.ShapeDtypeStruct((cb, rows, HC), F32)
    return pl.pallas_call(
        _hy_gate_kernel,
        grid=(cb, rows // tr),
        in_specs=[pl.BlockSpec((None, tr, HC), lambda i, r: (i, r, 0)),
                  pl.BlockSpec((None, tr, HC), lambda i, r: (u_off + i, r, 0)),
                  pl.BlockSpec((None, tr, HC), lambda i, r: (gate_off + i, r, 0)),
                  pl.BlockSpec((None, 1, HC), lambda i, r: (i, 0, 0))],
        out_specs=out_spec,
        out_shape=out_shape,
        compiler_params=_cparams(2),
        name="hy_gate",
    )(y, u_arr, gate_arr, skip.reshape(cb, 1, HC))


def _hy_taps_kernel(a_ref, t_ref, w_ref, dl_ref, o_ref):
    a = a_ref[...]
    a_hi = a.astype(BF16)
    a_lo = (a - a_hi.astype(F32)).astype(BF16)
    w = w_ref[...]
    w_hi = w.astype(BF16)
    w_lo = (w - w_hi.astype(F32)).astype(BF16)
    h = (jnp.dot(a_hi, w_hi, preferred_element_type=F32) + jnp.dot(a_hi, w_lo, preferred_element_type=F32)
         + jnp.dot(a_lo, w_hi, preferred_element_type=F32))
    o_ref[...] = h * jnp.exp(-t_ref[...] * dl_ref[...])


def hy_taps(a_all, t_all, w4, deltas, *, seq_len, order, tr=2048):
    nrow, width = a_all.shape
    d = deltas.shape[0]
    tr = min(tr, seq_len)
    per_dir = seq_len // tr
    return pl.pallas_call(
        _hy_taps_kernel,
        grid=(d // HC, nrow // tr),
        in_specs=[pl.BlockSpec((tr, width), lambda i, r: (r, 0)),
                  pl.BlockSpec((tr, 1), lambda i, r: (r, 0)),
                  pl.BlockSpec((None, None, width, HC), lambda i, r: (order, r // per_dir, 0, i)),
                  pl.BlockSpec((1, HC), lambda i, r: (0, i))],
        out_specs=pl.BlockSpec((None, tr, HC), lambda i, r: (i, r, 0)),
        out_shape=jax.ShapeDtypeStruct((d // HC, nrow, HC), F32),
        compiler_params=_cparams(2),
        name="hy_taps",
    )(a_all, t_all, w4.reshape(width, HY_ORDER, 2, d).transpose(1, 2, 0, 3), deltas.reshape(1, d))


def _hyena_filter_hidden(n, f_w1, f_b1, f_w2, f_b2, f_w3, f_b3, f_freq):
    hp = lax.Precision.HIGHEST
    lin = jnp.linspace(0.0, 1.0, n, dtype=F32)
    idx = jnp.arange(n, dtype=F32)
    bands = jnp.linspace(1e-4, HY_BANDS - 1, HY_BANDS, dtype=F32)[None, :]
    t = jnp.concatenate([lin, jnp.zeros((1,), F32), lin[:0:-1]])[:, None]
    pos = jnp.concatenate([idx, jnp.zeros((1,), F32), idx[:0:-1]])[:, None]
    ang = (2.0 * math.pi / n) * pos
    emb = jnp.concatenate([t, jnp.cos(bands * ang), -jnp.sin(bands * ang)], axis=-1)
    a = jnp.sin(f_freq * (jnp.dot(emb, f_w1, precision=hp) + f_b1))
    a = jnp.sin(f_freq * (jnp.dot(a, f_w2, precision=hp) + f_b2))
    a = jnp.sin(f_freq * (jnp.dot(a, f_w3, precision=hp) + f_b3))
    keep = (jnp.arange(2 * n) != n)[:, None]
    return jnp.where(keep, a, 0.0), t


def _hyena_deltas(d):
    return jnp.abs(jnp.linspace(math.log(HY_DECAY_TARGET) / HY_FAST_DECAY,
                                math.log(HY_DECAY_TARGET) / HY_SLOW_DECAY, d, dtype=F32))


def hyena_long_convs2(zc, a_all, t_all, w4, skip, *, batch, seq_len, d):
    cb = d // HC
    f1, f2, f3, f4 = _hy_mats(seq_len)
    nh = seq_len // LC_R2
    deltas = _hyena_deltas(d)
    u, u_off = zc, 2 * cb
    for o in range(HY_ORDER):
        taps = hy_taps(a_all, t_all, w4, deltas, seq_len=seq_len, order=o)
        spec = hy_spectrum(hy_stage1(taps, f1, blk_off=0, n_blk=cb, batch=1, rows_hi=2 * nh), f2)
        a = hy_stage1(u, f1, blk_off=u_off, n_blk=cb, batch=batch, rows_hi=nh)
        y = hy_stage4(hy_stage23(a, spec, f2, f3), f4, seq_len=seq_len)
        u = hy_gate(y, u, u_off, zc, o * cb, skip[o], natural=(o == HY_ORDER - 1))
        u_off = 0
    return u


def _final_norm_kernel(x_ref, g_ref, o_ref):
    x = x_ref[...]
    o_ref[...] = x * lax.rsqrt(jnp.mean(x * x, axis=-1, keepdims=True) + NORM_EPS) * g_ref[...]


def final_norm(x, g, *, rows, tm=ROW_TILE):
    d = x.shape[1]
    return pl.pallas_call(
        _final_norm_kernel,
        grid=(rows // tm,),
        in_specs=[pl.BlockSpec((tm, d), lambda i: (i, 0)), pl.BlockSpec((1, d), lambda i: (0, 0))],
        out_specs=pl.BlockSpec((tm, d), lambda i: (i, 0)),
        out_shape=jax.ShapeDtypeStruct((rows, d), F32),
        compiler_params=_cparams(1),
        name="final_norm",
    )(x, g)


def _rope_tables(n_lat, seq_len, n_rows):
    t = jnp.arange(n_rows, dtype=jnp.int32)
    row = ((t % seq_len) // GRID_W).astype(F32)
    col = (t % GRID_W).astype(F32)
    lane = jnp.arange(LANES)
    dd = lane % HEAD_DIM
    quarter = HEAD_DIM // 4
    inv_freq = ROPE_BASE ** (-(dd % quarter).astype(F32) / quarter)
    pos = jnp.where(dd[None, :] < HEAD_DIM // 2, row[:, None], col[:, None])
    ang = pos * inv_freq[None, :]
    sign = jnp.where((dd % (HEAD_DIM // 2)) < quarter, -1.0, 1.0).astype(F32)
    lat = (t < n_lat)[:, None]
    cos = jnp.where(lat, jnp.cos(ang), 1.0)
    sin = jnp.where(lat, jnp.sin(ang) * sign[None, :], 0.0)
    return cos, sin


def _wa_head_perm(n_heads):
    order = []
    for p in range(n_heads // (2 * WA_GROUP)):
        for g in range(WA_GROUP):
            order += [2 * WA_GROUP * p + g, 2 * WA_GROUP * p + WA_GROUP + g]
    cols = jnp.asarray(order, jnp.int32)[:, None] * HEAD_DIM + jnp.arange(HEAD_DIM, dtype=jnp.int32)[None, :]
    return cols.reshape(-1)


def kernel(x, c, ctx, c_ctx, w_ada, b_ada, g_mix, g_ffn, hy_w_in, hy_b_in, hy_w_short, hy_b_short, hy_f_w1, hy_f_b1, hy_f_w2, hy_f_b2, hy_f_w3, hy_f_b3, hy_f_freq, hy_f_w4, hy_skip, hy_w_out, hy_b_out, cf_w_pw1, cf_b_pw1, cf_w_dw, cf_b_dw, cf_ln_g, cf_ln_b, cf_w_pw2, cf_b_pw2, wa_w_qkv, wa_w_o, wa_sinks, na_w_qkv, na_w_o, na_rpb, moe_w_router, moe_b_router, moe_w_gu, moe_b_gu, moe_w_down, moe_b_down, g_final):
    bsz, n, d = x.shape
    n_ctx = ctx.shape[1]
    depth = w_ada.shape[0]
    n_lat = bsz * n
    m_all = n_lat + bsz * n_ctx

    cond = jnp.zeros((8, d), F32).at[:bsz].set(jax.nn.silu(c)).at[bsz].set(jax.nn.silu(c_ctx))
    mods = adaln(cond, w_ada, b_ada)
    xu = jnp.concatenate([x.reshape(n_lat, d), ctx.reshape(bsz * n_ctx, d)], axis=0)
    zeros_d = jnp.zeros((1, d), F32)

    for i in range(depth):
        kind, j = i % N_MIXERS, i // N_MIXERS
        last = i == depth - 1
        mod = [mods[i, :bsz + 1, k * d:(k + 1) * d].reshape(bsz + 1, 1, d) for k in range(6)]
        m_out = n_lat if last else m_all
        gm = g_mix[i].reshape(1, d)
        need_ctx_in = (not last) or kind >= 2
        m_in = m_all if need_ctx_in else n_lat
        x_in = xu[:m_in]

        if kind == 0:
            z = nm_matmul(x_in, gm, mod[0], mod[1], hy_w_in[j].astype(BF16), hy_b_in[j].reshape(1, -1),
                          seg_rows=n)
            filt_w = (hy_f_w1[j], hy_f_b1[j], hy_f_w2[j], hy_f_b2[j], hy_f_w3[j], hy_f_b3[j], hy_f_freq[j],
                      hy_f_w4[j])
            w_s, b_s = hy_w_short[j], hy_b_short[j].reshape(1, -1)
            w_o, b_o = hy_w_out[j].astype(BF16), hy_b_out[j].reshape(1, d)
            zc = dwconv(z, w_s, b_s, start=0, total=n_lat, seq_len=n, tc=HC, blocked=True, rows=2048,
                        out_dtype=F32)
            a_all, t_all = _hyena_filter_hidden(n, *filt_w[:-1])
            y = hyena_long_convs2(zc, a_all, t_all, hy_f_w4[j], hy_skip[j], batch=bsz, seq_len=n, d=d)
            xu = mm_res(y, w_o, b_o, xu, mod[2], seg_rows=n, tm=BIG_ROW_TILE)
            if m_in > n_lat:
                zcc = dwconv(z, w_s, b_s, start=n_lat, total=m_in - n_lat, seq_len=n_ctx, tc=512)
                taps_c = _hyena_taps(n_ctx, d, *filt_w)
                fwd, inv = _dft_small_mats(n_ctx)
                yc = short_long_conv(zcc, 2 * d, zcc, 0, plain_mm(fwd, taps_c[0]), hy_skip[j][0].reshape(1, d),
                                     fwd, inv, batch=bsz, seq_len=n_ctx)
                yc = short_long_conv(yc, 0, zcc, d, plain_mm(fwd, taps_c[1]), hy_skip[j][1].reshape(1, d),
                                     fwd, inv, batch=bsz, seq_len=n_ctx)
                if not last:
                    xu = mm_res(yc, w_o, b_o, xu, mod[2], seg_rows=n, row_off=n_lat)
            y = None
        elif kind == 1:
            w1 = cf_w_pw1[j].astype(BF16).reshape(d, 2, d).transpose(1, 0, 2)
            a = nm_matmul(x_in, gm, mod[0], mod[1], w1, cf_b_pw1[j].reshape(2, 1, d), seg_rows=n, mode="glu")
            conv = functools.partial(dwconv, a, cf_w_dw[j], cf_b_dw[j].reshape(1, d), post="ln_silu",
                                     ln=(cf_ln_g[j].reshape(1, d), cf_ln_b[j].reshape(1, d)), rows=64)
            y = conv(start=0, total=n_lat, seq_len=n, same_rows=True)
            if m_in > n_lat:
                y = conv(start=n_lat, total=m_in - n_lat, seq_len=n_ctx, same_rows=True, into=y)
            w_o, b_o = cf_w_pw2[j].astype(BF16), cf_b_pw2[j].reshape(1, d)
        elif kind == 2:
            perm = _wa_head_perm(d // HEAD_DIM)
            scale = HEAD_DIM ** -0.5
            w_qkv = jnp.concatenate([wa_w_qkv[j][:, :d][:, perm] * scale, wa_w_qkv[j][:, d:]], axis=1).astype(BF16)
            n_out = w_qkv.shape[1]
            qkv = nm_matmul(x_in, gm, mod[0], mod[1], w_qkv, jnp.zeros((1, n_out), F32), seg_rows=n,
                            mode="rope", rope=_rope_tables(n_lat, n, m_in), n_rope_cols=d + (n_out - d) // 2)
            y = window_attention(qkv, wa_sinks[j], batch=bsz, seq_len=n, ctx_len=n_ctx, d=d)
            w_o, b_o = wa_w_o[j][perm].astype(BF16), zeros_d
        else:
            scale = HEAD_DIM ** -0.5
            w_qkv = jnp.concatenate([na_w_qkv[j][:, :d] * scale, na_w_qkv[j][:, d:]], axis=1).astype(BF16)
            qkv = nm_matmul(x_in, gm, mod[0], mod[1], w_qkv, jnp.zeros((1, 3 * d), F32), seg_rows=n)
            y = neighbourhood_attention(qkv, na_rpb[j], batch=bsz, seq_len=n, ctx_len=n_ctx, d=d,
                                        with_ctx_out=not last)
            w_o, b_o = na_w_o[j].astype(BF16), zeros_d
        if y is not None:
            xu = mm_res(y, w_o, b_o, xu, mod[2], seg_rows=n, rows=n_lat, tm=BIG_ROW_TILE)
            if m_out > n_lat:
                xu = mm_res(y, w_o, b_o, xu, mod[2], seg_rows=n, rows=m_out - n_lat, row_off=n_lat, a_off=n_lat)

        ranges = ((0, m_out),)
        routed = [moe_dispatch(xu, g_ffn[i].reshape(1, d), mod[3], mod[4], moe_w_router[i], moe_b_router[i],
                               seg_rows=n, rows=cnt, row_off=off) for off, cnt in ranges]
        y_slots = [expert_mlp(xs, blk_e, n_used, moe_w_gu, moe_b_gu[i], moe_w_down, moe_b_down[i], layer=i)
                   for xs, blk_e, n_used, _, _ in routed]
        for (off, cnt), (_, _, _, dest_km, gates), y_slot in zip(ranges, routed, y_slots):
            xu = moe_combine(xu, y_slot, dest_km, gates, mod[5], seg_rows=n, rows=cnt, row_off=off)
    return final_norm(xu, g_final.reshape(1, d), rows=n_lat).reshape(bsz, n, d)
```

```python
import functools
import math

import jax
import jax.numpy as jnp
import numpy as np
from jax import lax
from jax.experimental import pallas as pl
from jax.experimental.pallas import tpu as pltpu

F32 = jnp.float32
BF16 = jnp.bfloat16

GRID_W = 64
N_MIXERS = 4
NORM_EPS = 1e-6
NEG_INF = -1e30
HEAD_DIM = 64
ROPE_BASE = 10000.0

HY_ORDER = 2
HY_BANDS = 16
HY_DECAY_TARGET = 1e-2
HY_FAST_DECAY = 0.3
HY_SLOW_DECAY = 1.5

WA_GROUP = 4
WA_WINDOW = 128
NA_WIN_ROWS = 8
NA_WIN_COLS = 16

TOP_K = 4
SWIGLU_LIMIT = 7.0
SWIGLU_ALPHA = 1.702

LANES = 128
ROW_TILE = 512
BIG_ROW_TILE = 1024
MOE_ROWS = 512
VMEM_LIMIT = 56 * 1024 * 1024
EXPERT_VMEM_LIMIT = 58 * 1024 * 1024


def _cparams(n_axes):
    return pltpu.CompilerParams(dimension_semantics=("arbitrary",) * n_axes,
                                vmem_limit_bytes=VMEM_LIMIT)


def _col_tile(n, pref=1024):
    t = min(pref, n)
    while n % t:
        t //= 2
    return t


def _adaln_kernel(c_ref, w_ref, b_ref, o_ref):
    w = w_ref[0]
    w_hi = w.astype(BF16)
    w_lo = (w - w_hi.astype(F32)).astype(BF16)
    c = c_ref[...]
    c_hi = c.astype(BF16)
    c_lo = (c - c_hi.astype(F32)).astype(BF16)
    acc = jnp.dot(c_hi, w_hi, preferred_element_type=F32)
    acc += jnp.dot(c_hi, w_lo, preferred_element_type=F32)
    acc += jnp.dot(c_lo, w_hi, preferred_element_type=F32)
    o_ref[0] = acc + b_ref[0]


def adaln(cond, w_ada, b_ada):
    depth, d, n6 = w_ada.shape
    tn = _col_tile(n6, 1024)
    return pl.pallas_call(
        _adaln_kernel,
        grid=(depth, n6 // tn),
        in_specs=[pl.BlockSpec((8, d), lambda l, j: (0, 0)),
                  pl.BlockSpec((1, d, tn), lambda l, j: (l, 0, j)),
                  pl.BlockSpec((1, 1, tn), lambda l, j: (l, 0, j))],
        out_specs=pl.BlockSpec((1, 8, tn), lambda l, j: (l, 0, j)),
        out_shape=jax.ShapeDtypeStruct((depth, 8, n6), F32),
        compiler_params=_cparams(2),
        name="adaln",
    )(cond, w_ada, b_ada.reshape(depth, 1, n6))


def _norm_mod(x, g, shift, scale):
    y = x * lax.rsqrt(jnp.mean(x * x, axis=-1, keepdims=True) + NORM_EPS)
    return (y * g) * (1 + scale) + shift


def _rope_tile(x, cos, sin):
    lane = lax.broadcasted_iota(jnp.int32, x.shape, 1)
    nxt = pltpu.roll(x, LANES - 16, axis=1)
    prv = pltpu.roll(x, 16, axis=1)
    partner = jnp.where((lane // 16) % 2 == 0, nxt, prv)
    return x * cos + partner * sin


def _nm_mm_kernel(x_ref, g_ref, sh_ref, sc_ref, w_ref, b_ref, *rest, mode, n_rope, n_alias):
    if n_alias:
        rest = rest[:-3] + rest[-2:]
    if mode == "rope":
        cos_ref, sin_ref, o_ref, h_ref = rest
    else:
        o_ref, h_ref = rest
    j = pl.program_id(1)

    @pl.when(j == 0)
    def _():
        h_ref[...] = _norm_mod(x_ref[...], g_ref[...], sh_ref[0], sc_ref[0]).astype(BF16)

    h = h_ref[...]
    if mode == "glu":
        a = jnp.dot(h, w_ref[0], preferred_element_type=F32) + b_ref[0]
        gate = jnp.dot(h, w_ref[1], preferred_element_type=F32) + b_ref[1]
        o_ref[...] = (a * jax.nn.sigmoid(gate)).astype(o_ref.dtype)
        return
    acc = jnp.dot(h, w_ref[...], preferred_element_type=F32) + b_ref[...]
    if mode == "rope":
        @pl.when(j < n_rope)
        def _():
            cos = cos_ref[...]
            sin = sin_ref[...]
            for g in range(acc.shape[1] // LANES):
                sl = slice(g * LANES, (g + 1) * LANES)
                o_ref[:, sl] = _rope_tile(acc[:, sl], cos, sin).astype(o_ref.dtype)

        @pl.when(j >= n_rope)
        def _():
            o_ref[...] = acc.astype(o_ref.dtype)
    else:
        o_ref[...] = acc.astype(o_ref.dtype)


def nm_matmul(x, g, shift, scale, w, b, *, seg_rows, mode="plain", rope=None, n_rope_cols=0,
              tm=ROW_TILE, tn=None):
    m, d = x.shape
    n_seg = shift.shape[0]
    n = w.shape[-1]
    tn = tn or (_col_tile(math.gcd(n, n_rope_cols), 512) if mode == "rope" else _col_tile(n, 1024))
    big = BIG_ROW_TILE if seg_rows % BIG_ROW_TILE == 0 else tm
    lat = min(m, (m // seg_rows) * seg_rows) if m >= seg_rows else 0
    ranges = [(0, lat, big)] if lat else []
    if m > lat:
        ranges.append((lat, m - lat, tm))
    out = None
    for row_off, rows, rt in ranges:
        blk_off = row_off // rt
        seg = lambda i, j, rt=rt, blk_off=blk_off: (jnp.minimum((blk_off + i) * rt // seg_rows, n_seg - 1), 0, 0)
        row_blk = lambda i, j, blk_off=blk_off: (blk_off + i, 0)
        in_specs = [pl.BlockSpec((rt, d), row_blk),
                    pl.BlockSpec((1, d), lambda i, j: (0, 0)),
                    pl.BlockSpec((1, 1, d), seg),
                    pl.BlockSpec((1, 1, d), seg)]
        if mode == "glu":
            in_specs += [pl.BlockSpec((2, d, tn), lambda i, j: (0, 0, j)),
                         pl.BlockSpec((2, 1, tn), lambda i, j: (0, 0, j))]
        else:
            in_specs += [pl.BlockSpec((d, tn), lambda i, j: (0, j)),
                         pl.BlockSpec((1, tn), lambda i, j: (0, j))]
        args = [x, g, shift, scale, w, b]
        if mode == "rope":
            in_specs += [pl.BlockSpec((rt, LANES), row_blk)] * 2
            args += list(rope)
        aliases = {}
        if out is not None:
            in_specs.append(pl.BlockSpec(memory_space=pl.ANY))
            args.append(out)
            aliases = {len(args) - 1: 0}
        out = pl.pallas_call(
            functools.partial(_nm_mm_kernel, mode=mode, n_rope=n_rope_cols // tn, n_alias=len(aliases)),
            grid=(rows // rt, n // tn),
            in_specs=in_specs,
            out_specs=pl.BlockSpec((rt, tn), lambda i, j, blk_off=blk_off: (blk_off + i, j)),
            out_shape=jax.ShapeDtypeStruct((m, n), BF16),
            input_output_aliases=aliases,
            scratch_shapes=[pltpu.VMEM((rt, d), BF16)],
            compiler_params=_cparams(2),
            name="nm_matmul_" + mode,
        )(*args)
    return out


def _mm_res_kernel(a_ref, w_ref, b_ref, res_ref, gate_ref, o_ref):
    if len(a_ref.shape) == 3:
        kc = a_ref.shape[2]
        acc = b_ref[...]
        for cb in range(a_ref.shape[0]):
            acc = acc + jnp.dot(a_ref[cb], w_ref[cb * kc:(cb + 1) * kc, :], preferred_element_type=F32)
    else:
        acc = jnp.dot(a_ref[...], w_ref[...], preferred_element_type=F32) + b_ref[...]
    o_ref[...] = res_ref[...] + gate_ref[0] * acc


def mm_res(a, w, b, res, gate, *, seg_rows, row_off=0, rows=None, a_off=0, tm=ROW_TILE, tn=None):
    a_blk = a_off // tm
    if a.ndim == 3:
        m = rows or a.shape[1]
        k = a.shape[0] * a.shape[2]
        a_spec = pl.BlockSpec((a.shape[0], tm, a.shape[2]), lambda i, j: (0, a_blk + i, 0))
    else:
        m, k = rows or a.shape[0], a.shape[1]
        a_spec = pl.BlockSpec((tm, k), lambda i, j: (a_blk + i, 0))
    n = w.shape[1]
    n_seg = gate.shape[0]
    tn = tn or _col_tile(n, 1024)
    blk_off = row_off // tm
    seg = lambda i, j: (jnp.minimum((blk_off + i) * tm // seg_rows, n_seg - 1), 0, j)
    return pl.pallas_call(
        _mm_res_kernel,
        grid=(m // tm, n // tn),
        in_specs=[a_spec,
                  pl.BlockSpec((k, tn), lambda i, j: (0, j)),
                  pl.BlockSpec((1, tn), lambda i, j: (0, j)),
                  pl.BlockSpec((tm, tn), lambda i, j: (blk_off + i, j)),
                  pl.BlockSpec((1, 1, tn), seg)],
        out_specs=pl.BlockSpec((tm, tn), lambda i, j: (blk_off + i, j)),
        out_shape=jax.ShapeDtypeStruct(res.shape, F32),
        input_output_aliases={3: 0},
        compiler_params=_cparams(2),
        name="mm_res",
    )(a, w, b, res, gate)


CONV_HALO = 16


def _dwconv_kernel(prev_ref, x_ref, next_ref, w_ref, b_ref, *rest, taps, blocks_per_seq, post, sub, n_alias):
    if n_alias:
        n_tail = 3 if post == "ln_silu" else 2
        rest = rest[:-n_tail - 1] + rest[-n_tail:]
    if post == "ln_silu":
        g_ref, beta_ref, o_ref, win_ref, conv_ref = rest
    else:
        o_ref, win_ref = rest
    i = pl.program_id(0)
    pos = i % blocks_per_seq
    rows, c = x_ref.shape
    half = taps // 2
    zero = jnp.zeros((CONV_HALO, c), F32)
    win_ref[CONV_HALO:CONV_HALO + rows, :] = x_ref[...].astype(F32)

    @pl.when(pos == 0)
    def _():
        win_ref[0:CONV_HALO, :] = zero

    @pl.when(pos > 0)
    def _():
        win_ref[0:CONV_HALO, :] = prev_ref[...].astype(F32)

    @pl.when(pos == blocks_per_seq - 1)
    def _():
        win_ref[CONV_HALO + rows:, :] = zero

    @pl.when(pos < blocks_per_seq - 1)
    def _():
        win_ref[CONV_HALO + rows:, :] = next_ref[...].astype(F32)

    bias = b_ref[...]
    if post == "ln_silu":
        ext = rows + sub
        groups = {}
        for t in range(taps):
            off = CONV_HALO - half + t
            groups.setdefault(off % sub, []).append((t, off // sub))
        lw = min(2 * LANES, c)
        for lc in range(0, c, lw):
            lanes = slice(lc, lc + lw)
            y = jnp.broadcast_to(bias[:, lanes], (rows, lw))
            for s, members in sorted(groups.items()):
                acc = None
                for t, a in members:
                    term = pltpu.repeat(w_ref[t, :, lanes], ext // sub, axis=0) * win_ref[sub * a:sub * a + ext, lanes]
                    acc = term if acc is None else acc + term
                y = y + acc[s:s + rows]
            conv_ref[:, lanes] = y
        for s in range(rows // sub):
            acc = conv_ref[s * sub:(s + 1) * sub, :]
            mu = jnp.mean(acc, axis=-1, keepdims=True)
            xc = acc - mu
            var = jnp.mean(xc * xc, axis=-1, keepdims=True)
            y = xc * lax.rsqrt(var + NORM_EPS) * g_ref[...] + beta_ref[...]
            o_ref[s * sub:(s + 1) * sub, :] = (y * jax.nn.sigmoid(y)).astype(o_ref.dtype)
        return
    for s in range(rows // sub):
        base = CONV_HALO + s * sub - half
        acc = jnp.broadcast_to(bias, (sub, c))
        for t in range(taps):
            acc = acc + w_ref[t] * win_ref[base + t:base + t + sub, :]
        o_ref[s * sub:(s + 1) * sub, :] = acc.astype(o_ref.dtype)


def dwconv(x, w, b, *, start, total, seq_len, post=None, ln=None, rows=256, tc=None, out_dtype=None,
           blocked=False, same_rows=False, into=None):
    m, c = x.shape
    taps = w.shape[0]
    tc = tc or c
    out_dtype = out_dtype or BF16
    sub = 8
    w = jnp.broadcast_to(w[:, None, :], (taps, sub, c))
    r = min(rows, seq_len)
    bps = seq_len // r
    hb = r // CONV_HALO
    off = start // r
    offh = start // CONV_HALO
    nh = m // CONV_HALO
    in_specs = [pl.BlockSpec((CONV_HALO, tc), lambda i, j: (jnp.maximum(offh + i * hb - 1, 0), j)),
                pl.BlockSpec((r, tc), lambda i, j: (off + i, j)),
                pl.BlockSpec((CONV_HALO, tc), lambda i, j: (jnp.minimum(offh + (i + 1) * hb, nh - 1), j)),
                pl.BlockSpec((taps, sub, tc), lambda i, j: (0, 0, j)),
                pl.BlockSpec((1, tc), lambda i, j: (0, j))]
    args = [x, x, x, w, b]
    if post == "ln_silu":
        in_specs += [pl.BlockSpec((1, tc), lambda i, j: (0, j))] * 2
        args += list(ln)
    aliases = {}
    if blocked:
        out_spec = pl.BlockSpec((None, r, tc), lambda i, j: (j, i, 0))
        out_shape = jax.ShapeDtypeStruct((c // tc, total, tc), out_dtype)
    elif same_rows:
        out_spec = pl.BlockSpec((r, tc), lambda i, j: (off + i, j))
        out_shape = jax.ShapeDtypeStruct((m, c), out_dtype)
        if into is not None:
            in_specs.append(pl.BlockSpec(memory_space=pl.ANY))
            args.append(into)
            aliases = {len(args) - 1: 0}
    else:
        out_spec = pl.BlockSpec((r, tc), lambda i, j: (i, j))
        out_shape = jax.ShapeDtypeStruct((total, c), out_dtype)
    return pl.pallas_call(
        functools.partial(_dwconv_kernel, taps=taps, blocks_per_seq=bps, post=post, sub=sub,
                          n_alias=len(aliases)),
        grid=(total // r, c // tc),
        in_specs=in_specs,
        out_specs=out_spec,
        out_shape=out_shape,
        input_output_aliases=aliases,
        scratch_shapes=[pltpu.VMEM((r + 2 * CONV_HALO, tc), F32)]
        + ([pltpu.VMEM((r, tc), F32)] if post == "ln_silu" else []),
        compiler_params=_cparams(2),
        name="dwconv%d" % taps,
    )(*args)


def _masked_halves(q, lane_lo):
    zero = jnp.zeros_like(q)
    return jnp.where(lane_lo, q, zero), jnp.where(lane_lo, zero, q)


def _softmax_pv(s_parts, v_parts, sink):
    m = s_parts[0].max(axis=-1, keepdims=True)
    for s in s_parts[1:]:
        m = jnp.maximum(m, s.max(axis=-1, keepdims=True))
    if sink is not None:
        m = jnp.maximum(m, sink)
    denom = jnp.exp(sink - m) if sink is not None else 0.0
    o = None
    for s, v in zip(s_parts, v_parts):
        p = jnp.exp(s - m)
        denom = denom + p.sum(axis=-1, keepdims=True)
        pv = jnp.dot(p.astype(BF16), v, preferred_element_type=F32)
        o = pv if o is None else o + pv
    return o / denom


def _nt_dot(a, b):
    return lax.dot_general(a, b, (((1,), (1,)), ((), ())), preferred_element_type=F32)


def _wattn_kernel(sink_ref, q_ref, *rest, local, seq_len, blk):
    if local:
        n_kb = (len(rest) - 3) // 2
        k_refs, v_refs = rest[:n_kb], rest[n_kb:2 * n_kb]
        kc_ref, vc_ref, o_ref = rest[2 * n_kb:]
    else:
        kc_ref, vc_ref, _, o_ref = rest
    p = pl.program_id(2)
    i = pl.program_id(1)
    lane_lo = lax.broadcasted_iota(jnp.int32, (blk, LANES), 1) < HEAD_DIM
    kc = kc_ref[...]
    vc = vc_ref[...]
    if local:
        kl = jnp.concatenate([r[...] for r in k_refs], axis=0)
        vl = jnp.concatenate([r[...] for r in v_refs], axis=0)
        n_keys = n_kb * WA_WINDOW
        qpos = i * blk + lax.broadcasted_iota(jnp.int32, (blk, n_keys), 0)
        kpos = i * blk - WA_WINDOW + lax.broadcasted_iota(jnp.int32, (blk, n_keys), 1)
        valid = (jnp.abs(kpos - qpos) <= WA_WINDOW) & (kpos >= 0) & (kpos < seq_len)
        valid = jnp.concatenate([valid] * WA_GROUP, axis=0)
    qa, qb = [], []
    for g in range(WA_GROUP):
        a, b = _masked_halves(q_ref[:, g * LANES:(g + 1) * LANES], lane_lo)
        qa.append(a)
        qb.append(b)
    outs = []
    for half, qs in enumerate((qa, qb)):
        qs = jnp.concatenate(qs, axis=0)
        sink = jnp.concatenate(
            [jnp.full((blk, 1), sink_ref[8 * p + 4 * half + g], F32) for g in range(WA_GROUP)], axis=0)
        s_parts, v_parts = [], []
        if local:
            s_parts.append(jnp.where(valid, _nt_dot(qs, kl), NEG_INF))
            v_parts.append(vl)
        s_parts.append(_nt_dot(qs, kc))
        v_parts.append(vc)
        outs.append(_softmax_pv(s_parts, v_parts, sink))
    for g in range(WA_GROUP):
        rows = slice(g * blk, (g + 1) * blk)
        o_ref[:, g * LANES:(g + 1) * LANES] = jnp.where(lane_lo, outs[0][rows], outs[1][rows]).astype(o_ref.dtype)


def window_attention(qkv, sinks, *, batch, seq_len, ctx_len, d, blk=512):
    n_pairs = d // (2 * WA_GROUP * HEAD_DIM)
    kcol = d // LANES
    vcol = kcol + n_pairs
    nblk = seq_len // blk
    kb_per_q = blk // WA_WINDOW
    n_kblk = seq_len // WA_WINDOW
    shifts = range(-1, kb_per_q + 1)
    cb0 = batch * seq_len // ctx_len
    qw = WA_GROUP * LANES
    kern = functools.partial(_wattn_kernel, seq_len=seq_len)
    smem = pl.BlockSpec(memory_space=pltpu.SMEM)

    def kspec(col0, shift):
        return pl.BlockSpec((WA_WINDOW, LANES),
                            lambda b, i, p: (b * n_kblk + jnp.clip(i * kb_per_q + shift, 0, n_kblk - 1), col0 + p))

    ctx_k = pl.BlockSpec((ctx_len, LANES), lambda b, i, p: (cb0 + b, kcol + p))
    ctx_v = pl.BlockSpec((ctx_len, LANES), lambda b, i, p: (cb0 + b, vcol + p))
    lat = pl.pallas_call(
        functools.partial(kern, local=True, blk=blk),
        grid=(batch, nblk, n_pairs),
        in_specs=[smem, pl.BlockSpec((blk, qw), lambda b, i, p: (b * nblk + i, p))]
        + [kspec(kcol, s) for s in shifts] + [kspec(vcol, s) for s in shifts] + [ctx_k, ctx_v],
        out_specs=pl.BlockSpec((blk, qw), lambda b, i, p: (b * nblk + i, p)),
        out_shape=jax.ShapeDtypeStruct((qkv.shape[0], d), BF16),
        compiler_params=_cparams(3),
        name="window_attn",
    )(sinks, qkv, *([qkv] * (2 * len(shifts) + 2)))
    return pl.pallas_call(
        functools.partial(kern, local=False, blk=ctx_len),
        grid=(batch, 1, n_pairs),
        in_specs=[smem, pl.BlockSpec((ctx_len, qw), lambda b, i, p: (cb0 + b, p)), ctx_k, ctx_v,
                  pl.BlockSpec(memory_space=pl.ANY)],
        out_specs=pl.BlockSpec((ctx_len, qw), lambda b, i, p: (cb0 + b, p)),
        out_shape=jax.ShapeDtypeStruct((qkv.shape[0], d), BF16),
        input_output_aliases={4: 0},
        compiler_params=_cparams(3),
        name="ctx_attn",
    )(sinks, qkv, qkv, qkv, lat)


def _nattn_kernel(q_ref, k0, k1, k2, v0, v1, v2, kc_ref, vc_ref, bias_ref, o_ref, ks_ref, vs_ref,
                  *, grid_rows, rows_per_blk):
    j = pl.program_id(2)
    blk = k0.shape[0]
    for t, (kr, vr) in enumerate(((k0, v0), (k1, v1), (k2, v2))):
        ks_ref[t * blk:(t + 1) * blk, :] = kr[...]
        vs_ref[t * blk:(t + 1) * blk, :] = vr[...]
    kc = kc_ref[...]
    vc = vc_ref[...]
    lane_lo = lax.broadcasted_iota(jnp.int32, (GRID_W, LANES), 1) < HEAD_DIM
    strip = NA_WIN_ROWS * GRID_W
    for r in range(rows_per_blk):
        row = j * rows_per_blk + r
        r0 = jnp.clip(row - NA_WIN_ROWS // 2, 0, grid_rows - NA_WIN_ROWS)
        start = pl.multiple_of((r0 - (j - 1) * rows_per_blk) * GRID_W, GRID_W)
        cls = row - r0
        qa, qb = _masked_halves(q_ref[r * GRID_W:(r + 1) * GRID_W, :], lane_lo)
        qs = jnp.concatenate([qa, qb], axis=0)
        kn = ks_ref[pl.ds(start, strip), :]
        vn = vs_ref[pl.ds(start, strip), :]
        s_nb = _nt_dot(qs, kn) + bias_ref[cls]
        s_cx = _nt_dot(qs, kc)
        o = _softmax_pv([s_nb, s_cx], [vn, vc], None)
        o_ref[r * GRID_W:(r + 1) * GRID_W, :] = jnp.where(lane_lo, o[:GRID_W], o[GRID_W:]).astype(o_ref.dtype)


def _na_bias_table(rpb):
    h = rpb.shape[0]
    n_dcol = 2 * NA_WIN_COLS - 1
    cols = jnp.arange(GRID_W)
    col_start = jnp.clip(cols - NA_WIN_COLS // 2, 0, GRID_W - NA_WIN_COLS)
    inwin = (cols[None, :] >= col_start[:, None]) & (cols[None, :] < col_start[:, None] + NA_WIN_COLS)
    dcol = cols[None, :] - cols[:, None] + NA_WIN_COLS - 1
    pick = (dcol[None] == jnp.arange(n_dcol)[:, None, None]).astype(F32).reshape(n_dcol, -1)
    spread = jnp.dot(rpb.astype(F32).reshape(-1, n_dcol), pick, precision=lax.Precision.HIGHEST)
    spread = spread.reshape(h, 2 * NA_WIN_ROWS - 1, GRID_W, GRID_W)
    spread = jnp.where(inwin[None, None], spread, NEG_INF)
    tab = jnp.stack([spread[:, NA_WIN_ROWS - 1 - cls:2 * NA_WIN_ROWS - 1 - cls]
                     for cls in range(NA_WIN_ROWS)], axis=1)
    tab = tab.transpose(0, 1, 3, 2, 4).reshape(h, NA_WIN_ROWS, GRID_W, NA_WIN_ROWS * GRID_W)
    tab = tab.reshape(h // 2, 2, NA_WIN_ROWS, GRID_W, NA_WIN_ROWS * GRID_W).transpose(0, 2, 1, 3, 4)
    return tab.reshape(h // 2, NA_WIN_ROWS, 2 * GRID_W, NA_WIN_ROWS * GRID_W)


def neighbourhood_attention(qkv, rpb, *, batch, seq_len, ctx_len, d, with_ctx_out):
    n_pairs = d // LANES
    rows_per_blk = min(4 * NA_WIN_ROWS, seq_len // GRID_W)
    blk = rows_per_blk * GRID_W
    grid_rows = seq_len // GRID_W
    nblk = seq_len // blk
    cb0 = batch * seq_len // ctx_len
    bias = _na_bias_table(rpb)

    def kspec(col0, shift):
        return pl.BlockSpec((blk, LANES),
                            lambda b, p, j: (b * nblk + jnp.clip(j + shift, 0, nblk - 1), col0 + p))

    ctx_k = pl.BlockSpec((ctx_len, LANES), lambda b, p, j: (cb0 + b, n_pairs + p))
    ctx_v = pl.BlockSpec((ctx_len, LANES), lambda b, p, j: (cb0 + b, 2 * n_pairs + p))
    lat = pl.pallas_call(
        functools.partial(_nattn_kernel, grid_rows=grid_rows, rows_per_blk=rows_per_blk),
        grid=(batch, n_pairs, nblk),
        in_specs=[pl.BlockSpec((blk, LANES), lambda b, p, j: (b * nblk + j, p))]
        + [kspec(n_pairs, s) for s in (-1, 0, 1)] + [kspec(2 * n_pairs, s) for s in (-1, 0, 1)]
        + [ctx_k, ctx_v,
           pl.BlockSpec((None, NA_WIN_ROWS, 2 * GRID_W, NA_WIN_ROWS * GRID_W), lambda b, p, j: (p, 0, 0, 0))],
        out_specs=pl.BlockSpec((blk, LANES), lambda b, p, j: (b * nblk + j, p)),
        out_shape=jax.ShapeDtypeStruct((batch * seq_len, d), BF16),
        scratch_shapes=[pltpu.VMEM((3 * blk, LANES), BF16), pltpu.VMEM((3 * blk, LANES), BF16)],
        compiler_params=_cparams(3),
        name="neighbourhood_attn",
    )(qkv, *([qkv] * 8), bias)
    if not with_ctx_out:
        return lat
    ctx = pl.pallas_call(
        _cattn_kernel,
        grid=(batch, n_pairs),
        in_specs=[pl.BlockSpec((ctx_len, LANES), lambda b, p: (cb0 + b, p)),
                  pl.BlockSpec((ctx_len, LANES), lambda b, p: (cb0 + b, n_pairs + p)),
                  pl.BlockSpec((ctx_len, LANES), lambda b, p: (cb0 + b, 2 * n_pairs + p))],
        out_specs=pl.BlockSpec((ctx_len, LANES), lambda b, p: (b, p)),
        out_shape=jax.ShapeDtypeStruct((batch * ctx_len, d), BF16),
        compiler_params=_cparams(2),
        name="ctx_mha",
    )(qkv, qkv, qkv)
    return jnp.concatenate([lat, ctx], axis=0)


def _cattn_kernel(q_ref, k_ref, v_ref, o_ref):
    rows = q_ref.shape[0]
    lane_lo = lax.broadcasted_iota(jnp.int32, (rows, LANES), 1) < HEAD_DIM
    qa, qb = _masked_halves(q_ref[...], lane_lo)
    qs = jnp.concatenate([qa, qb], axis=0)
    o = _softmax_pv([_nt_dot(qs, k_ref[...])], [v_ref[...]], None)
    o_ref[...] = jnp.where(lane_lo, o[:rows], o[rows:]).astype(o_ref.dtype)


def _router_kernel(x_ref, g_ref, sh_ref, sc_ref, wh_ref, wl_ref, b_ref, h_ref, idx_ref, gate_ref, cnt_ref,
                   run_ref):
    @pl.when(pl.program_id(0) == 0)
    def _():
        run_ref[...] = jnp.zeros_like(run_ref)

    h = _norm_mod(x_ref[...], g_ref[...], sh_ref[0], sc_ref[0])
    h_hi = h.astype(BF16)
    h_ref[...] = h_hi
    h_lo = (h - h_hi.astype(F32)).astype(BF16)
    logits = (jnp.dot(h_hi, wh_ref[...], preferred_element_type=F32)
              + jnp.dot(h_hi, wl_ref[...], preferred_element_type=F32)
              + jnp.dot(h_lo, wh_ref[...], preferred_element_type=F32)) + b_ref[...]
    lane = lax.broadcasted_iota(jnp.int32, logits.shape, 1)
    tm = logits.shape[0]
    idx_out = jnp.zeros(logits.shape, jnp.int32)
    val_out = jnp.zeros(logits.shape, F32)
    tri = jnp.where(lax.broadcasted_iota(jnp.int32, (tm, tm), 0) > lax.broadcasted_iota(jnp.int32, (tm, tm), 1),
                    1.0, 0.0).astype(BF16)
    run = run_ref[...]
    top = None
    denom = 0.0
    for k in range(TOP_K):
        m = logits.max(axis=-1, keepdims=True)
        sel = jnp.min(jnp.where(logits == m, lane, LANES), axis=-1, keepdims=True)
        if top is None:
            top = m
        e = jnp.exp(m - top)
        denom = denom + e
        onehot = lane == sel
        before = jnp.dot(tri, jnp.where(onehot, 1.0, 0.0).astype(BF16), preferred_element_type=F32)
        rank = jnp.sum(jnp.where(onehot, before + run, 0.0), axis=-1, keepdims=True).astype(jnp.int32)
        run = run + jnp.sum(jnp.where(onehot, 1.0, 0.0), axis=0, keepdims=True)
        idx_out = jnp.where(lane == k, sel, idx_out)
        idx_out = jnp.where(lane == TOP_K + k, rank, idx_out)
        val_out = jnp.where(lane == k, e, val_out)
        logits = jnp.where(onehot, -jnp.inf, logits)
    run_ref[...] = run
    cnt_ref[...] = run
    idx_ref[...] = idx_out
    gate_ref[...] = val_out / denom


def router(x, g, shift, scale, w_router, b_router, *, seg_rows, rows, row_off=0, tm=ROW_TILE):
    m, d = rows, x.shape[1]
    n_seg = shift.shape[0]
    n_e = w_router.shape[1]
    w_pad = jnp.zeros((d, LANES), F32).at[:, :n_e].set(w_router)
    w_hi = w_pad.astype(BF16)
    w_lo = (w_pad - w_hi.astype(F32)).astype(BF16)
    b_pad = jnp.full((1, LANES), -jnp.inf, F32).at[0, :n_e].set(b_router)
    blk_off = row_off // tm
    seg = lambda i: (jnp.minimum((blk_off + i) * tm // seg_rows, n_seg - 1), 0, 0)
    return pl.pallas_call(
        _router_kernel,
        grid=(m // tm,),
        in_specs=[pl.BlockSpec((tm, d), lambda i: (blk_off + i, 0)),
                  pl.BlockSpec((1, d), lambda i: (0, 0)),
                  pl.BlockSpec((1, 1, d), seg),
                  pl.BlockSpec((1, 1, d), seg),
                  pl.BlockSpec((d, LANES), lambda i: (0, 0)),
                  pl.BlockSpec((d, LANES), lambda i: (0, 0)),
                  pl.BlockSpec((1, LANES), lambda i: (0, 0))],
        out_specs=[pl.BlockSpec((tm, d), lambda i: (i, 0)),
                   pl.BlockSpec((tm, LANES), lambda i: (i, 0)),
                   pl.BlockSpec((tm, LANES), lambda i: (i, 0)),
                   pl.BlockSpec((1, LANES), lambda i: (0, 0))],
        out_shape=[jax.ShapeDtypeStruct((m, d), BF16),
                   jax.ShapeDtypeStruct((m, LANES), jnp.int32),
                   jax.ShapeDtypeStruct((m, LANES), F32),
                   jax.ShapeDtypeStruct((1, LANES), F32)],
        scratch_shapes=[pltpu.VMEM((1, LANES), F32)],
        compiler_params=_cparams(1),
        name="router",
    )(x, g, shift, scale, w_hi, w_lo, b_pad)


def _expert_kernel(blk_e_ref, n_used_ref, x_ref, wgu_ref, bgu_ref, wd_ref, bd_ref, o_ref, wgu_s, wd_s, *, f):
    i = pl.program_id(0)
    used = i < n_used_ref[0]

    @pl.when(used & ((i == 0) | (blk_e_ref[i] != blk_e_ref[jnp.maximum(i - 1, 0)])))
    def _():
        wgu_s[...] = wgu_ref[0].astype(BF16)
        wd_s[...] = wd_ref[0].astype(BF16)

    @pl.when(used)
    def _():
        gu = jnp.dot(x_ref[...], wgu_s[...], preferred_element_type=F32) + bgu_ref[0]
        g = jnp.minimum(gu[:, :f], SWIGLU_LIMIT)
        u = jnp.clip(gu[:, f:], -SWIGLU_LIMIT, SWIGLU_LIMIT)
        act = (u + 1) * (g * jax.nn.sigmoid(SWIGLU_ALPHA * g))
        y = jnp.dot(act.astype(BF16), wd_s[...], preferred_element_type=F32) + bd_ref[0]
        o_ref[...] = y.astype(o_ref.dtype)

    @pl.when(i >= n_used_ref[0])
    def _():
        o_ref[...] = jnp.zeros_like(o_ref)


def expert_mlp(xs, blk_e, n_used, w_gu, b_gu, w_down, b_down, *, layer, bm=MOE_ROWS):
    n_slot, d = xs.shape
    _, n_e, _, f2 = w_gu.shape
    f = f2 // 2
    return pl.pallas_call(
        functools.partial(_expert_kernel, f=f),
        grid_spec=pltpu.PrefetchScalarGridSpec(
            num_scalar_prefetch=2,
            grid=(n_slot // bm,),
            in_specs=[pl.BlockSpec((bm, d), lambda i, be, nu: (i, 0)),
                      pl.BlockSpec((None, 1, d, f2), lambda i, be, nu: (layer, be[i], 0, 0)),
                      pl.BlockSpec((1, 1, f2), lambda i, be, nu: (be[i], 0, 0)),
                      pl.BlockSpec((None, 1, f, d), lambda i, be, nu: (layer, be[i], 0, 0)),
                      pl.BlockSpec((1, 1, d), lambda i, be, nu: (be[i], 0, 0))],
            out_specs=pl.BlockSpec((bm, d), lambda i, be, nu: (i, 0)),
            scratch_shapes=[pltpu.VMEM((d, f2), BF16), pltpu.VMEM((f, d), BF16)]),
        out_shape=jax.ShapeDtypeStruct((n_slot, d), BF16),
        compiler_params=pltpu.CompilerParams(dimension_semantics=("arbitrary",),
                                             vmem_limit_bytes=EXPERT_VMEM_LIMIT),
        name="expert_mlp",
    )(blk_e, n_used, xs, w_gu, b_gu.reshape(n_e, 1, f2), w_down, b_down.reshape(n_e, 1, d))


def _combine_kernel(res_ref, y_ref, w_ref, gate_ref, o_ref):
    w = w_ref[...]
    y = w[:, 0:1] * y_ref[0].astype(F32)
    for k in range(1, TOP_K):
        y = y + w[:, k:k + 1] * y_ref[k].astype(F32)
    o_ref[...] = res_ref[...] + gate_ref[0] * y


def moe_dispatch(x, g, shift, scale, w_router, b_router, *, seg_rows, rows, row_off, bm=MOE_ROWS):
    m = rows
    n_e = w_router.shape[1]
    h, route, gates, counts = router(x, g, shift, scale, w_router, b_router, seg_rows=seg_rows, rows=rows,
                                     row_off=row_off)
    idx = route[:, :TOP_K]
    rank = route[:, TOP_K:2 * TOP_K]
    n_asg = m * TOP_K
    counts = counts[0, :n_e].astype(jnp.int32)
    padded = (counts + bm - 1) // bm * bm
    pad_end = jnp.cumsum(padded)
    pad_start = pad_end - padded
    start_of = jnp.sum(jnp.where(idx[:, :, None] == jnp.arange(n_e, dtype=jnp.int32), pad_start, 0), axis=-1)
    dest = (start_of + rank).reshape(-1)
    n_blk = -(-(n_asg + n_e * (bm - 1)) // bm)
    n_slot = n_blk * bm
    tok = (jnp.arange(n_asg, dtype=jnp.int32) // TOP_K)
    slot_tok = (jnp.arange(n_slot, dtype=jnp.int32) % m).at[dest].set(tok, unique_indices=True)
    blk_start = jnp.arange(n_blk, dtype=jnp.int32) * bm
    blk_e = jnp.minimum(jnp.sum((pad_end[None, :] <= blk_start[:, None]).astype(jnp.int32), axis=1), n_e - 1)
    n_used = (pad_end[-1] // bm).astype(jnp.int32).reshape(1)
    xs = h.at[slot_tok].get(mode="promise_in_bounds")
    dest_km = dest.reshape(m, TOP_K).T.reshape(-1)
    return xs, blk_e, n_used, dest_km, gates


def moe_combine(x, y_slot, dest_km, gates, gate, *, seg_rows, rows, row_off):
    m, d = rows, x.shape[1]
    y_tok = y_slot.at[dest_km].get(mode="promise_in_bounds").reshape(TOP_K, m, d)
    tm = 256
    n_seg = gate.shape[0]
    blk_off = row_off // tm
    seg = lambda i: (jnp.minimum((blk_off + i) * tm // seg_rows, n_seg - 1), 0, 0)
    return pl.pallas_call(
        _combine_kernel,
        grid=(m // tm,),
        in_specs=[pl.BlockSpec((tm, d), lambda i: (blk_off + i, 0)),
                  pl.BlockSpec((TOP_K, tm, d), lambda i: (0, i, 0)),
                  pl.BlockSpec((tm, LANES), lambda i: (i, 0)),
                  pl.BlockSpec((1, 1, d), seg)],
        out_specs=pl.BlockSpec((tm, d), lambda i: (blk_off + i, 0)),
        out_shape=jax.ShapeDtypeStruct(x.shape, F32),
        input_output_aliases={0: 0},
        compiler_params=_cparams(1),
        name="moe_combine",
    )(x, y_tok, gates, gate)


LC_R2 = 128


@functools.lru_cache(maxsize=None)
def _dft_small_mats(n):
    nn = 2 * n
    ang = 2.0 * np.pi * ((np.arange(nn)[:, None] * np.arange(nn)[None, :]) % nn) / nn
    fwd = np.concatenate([np.cos(ang), -np.sin(ang)], 0)
    inv = np.concatenate([np.cos(ang[:n]), -np.sin(ang[:n])], 1) / nn
    return jnp.asarray(fwd, BF16), jnp.asarray(inv, BF16)


def _plain_mm_kernel(a_ref, b_ref, o_ref):
    o_ref[...] = jnp.dot(a_ref[...], b_ref[...].astype(BF16), preferred_element_type=F32)


def plain_mm(a, b, tn=512):
    m, k = a.shape
    n = b.shape[1]
    tn = _col_tile(n, tn)
    return pl.pallas_call(
        _plain_mm_kernel,
        grid=(n // tn,),
        in_specs=[pl.BlockSpec((m, k), lambda j: (0, 0)), pl.BlockSpec((k, tn), lambda j: (0, j))],
        out_specs=pl.BlockSpec((m, tn), lambda j: (0, j)),
        out_shape=jax.ShapeDtypeStruct((m, n), F32),
        compiler_params=_cparams(1),
        name="plain_mm",
    )(a, b)


def _short_conv_kernel(u_ref, gate_ref, h_ref, skip_ref, fwd_ref, inv_ref, o_ref):
    u = u_ref[...]
    x = jnp.dot(fwd_ref[...], u.astype(BF16), preferred_element_type=F32)
    h = h_ref[...]
    nn = x.shape[0] // 2
    xr, xi, hr, hi = x[:nn], x[nn:], h[:nn], h[nn:]
    prod = jnp.concatenate([xr * hr - xi * hi, xr * hi + xi * hr], axis=0).astype(BF16)
    y = jnp.dot(inv_ref[...], prod, preferred_element_type=F32)
    o_ref[...] = (gate_ref[...].astype(F32) * (y + skip_ref[...] * u.astype(F32))).astype(o_ref.dtype)


def short_long_conv(u_arr, u_col, gate_arr, gate_col, spec, skip, fwd, inv, *, batch, seq_len, tl=512):
    d = spec.shape[1]
    tl = _col_tile(d, tl)
    uc, gc = u_col // tl, gate_col // tl
    nn2 = spec.shape[0]
    return pl.pallas_call(
        _short_conv_kernel,
        grid=(batch, d // tl),
        in_specs=[pl.BlockSpec((seq_len, tl), lambda b, j: (b, uc + j)),
                  pl.BlockSpec((seq_len, tl), lambda b, j: (b, gc + j)),
                  pl.BlockSpec((nn2, tl), lambda b, j: (0, j)),
                  pl.BlockSpec((1, tl), lambda b, j: (0, j)),
                  pl.BlockSpec((nn2, seq_len), lambda b, j: (0, 0)),
                  pl.BlockSpec((seq_len, nn2), lambda b, j: (0, 0))],
        out_specs=pl.BlockSpec((seq_len, tl), lambda b, j: (b, j)),
        out_shape=jax.ShapeDtypeStruct((batch * seq_len, d), BF16),
        compiler_params=_cparams(2),
        name="short_long_conv",
    )(u_arr, gate_arr, spec, skip, fwd[:, :seq_len], inv)


def _hyena_taps(n, d, f_w1, f_b1, f_w2, f_b2, f_w3, f_b3, f_freq, f_w4):
    hp = lax.Precision.HIGHEST
    lin = jnp.linspace(0.0, 1.0, n, dtype=F32)
    idx = jnp.arange(n, dtype=F32)
    bands = jnp.linspace(1e-4, HY_BANDS - 1, HY_BANDS, dtype=F32)[None, :]
    deltas = jnp.abs(jnp.linspace(math.log(HY_DECAY_TARGET) / HY_FAST_DECAY,
                                  math.log(HY_DECAY_TARGET) / HY_SLOW_DECAY, d, dtype=F32))

    def branch(t, pos, direction):
        t = t[:, None]
        ang = (2.0 * math.pi / n) * pos[:, None]
        emb = jnp.concatenate([t, jnp.cos(bands * ang), -jnp.sin(bands * ang)], axis=-1)
        a = jnp.sin(f_freq * (jnp.dot(emb, f_w1, precision=hp) + f_b1))
        a = jnp.sin(f_freq * (jnp.dot(a, f_w2, precision=hp) + f_b2))
        a = jnp.sin(f_freq * (jnp.dot(a, f_w3, precision=hp) + f_b3))
        decay = jnp.exp(-t * deltas[None, :])
        w4 = f_w4.reshape(f_w4.shape[0], HY_ORDER, 2, d)[:, :, direction]
        return [jnp.dot(a, w4[:, o], precision=hp) * decay for o in range(HY_ORDER)]

    fwd = branch(lin, idx, 0)
    bwd = branch(lin[:0:-1], idx[:0:-1], 1)
    zero = jnp.zeros((1, d), F32)
    return [jnp.concatenate([fwd[o], zero, bwd[o]], axis=0) for o in range(HY_ORDER)]


HC = 128
HC_PAIR = 2
LC_TL = 32
HY_JN = 2


@functools.lru_cache(maxsize=None)
def _hy_mats(n):
    nn = 2 * n
    r1 = nn // LC_R2
    tau = 2.0 * np.pi
    k1 = np.arange(r1)
    t = LC_R2 * np.arange(r1)[None, :] + np.arange(LC_R2)[:, None]
    ang1 = tau * ((k1[None, :, None] * t[:, None, :]) % nn) / nn
    f1 = np.empty((LC_R2, 2 * r1, r1))
    f1[:, 0::2] = np.cos(ang1)
    f1[:, 1::2] = -np.sin(ang1)
    ang2 = tau * ((np.arange(LC_R2)[:, None] * np.arange(LC_R2)[None, :]) % LC_R2) / LC_R2
    c, s = np.cos(ang2), np.sin(ang2)
    f2 = np.empty((2 * LC_R2, 2 * LC_R2))
    f2[:LC_R2, 0::2], f2[:LC_R2, 1::2] = c, s
    f2[LC_R2:, 0::2], f2[LC_R2:, 1::2] = -s, c
    f3 = np.empty((2 * LC_R2, 2 * LC_R2))
    f3[0::2, :LC_R2], f3[0::2, LC_R2:] = c, -s
    f3[1::2, :LC_R2], f3[1::2, LC_R2:] = s, c
    m = LC_R2 * np.arange(r1 // 2)[None, :] + np.arange(LC_R2)[:, None]
    ang4 = tau * ((m[:, :, None] * k1[None, None, :]) % nn) / nn
    f4 = np.empty((LC_R2, r1 // 2, 2 * r1))
    f4[:, :, 0::2] = np.cos(ang4) / nn
    f4[:, :, 1::2] = -np.sin(ang4) / nn
    return tuple(jnp.asarray(a, BF16) for a in (f1, f2, f3, f4))


def _hy1_kernel(x_ref, f_ref, o_ref, *, tl_n, nh):
    c = pl.program_id(2)
    for tl in range(tl_n):
        xs = jnp.concatenate([x_ref[h, pl.ds(c * tl_n + tl, nh, stride=LC_R2), :] for h in range(HC_PAIR)],
                             axis=1).astype(BF16)
        a = jnp.dot(f_ref[tl], xs, preferred_element_type=F32)
        packed = pltpu.bitcast(a.astype(BF16), jnp.uint32)
        for h in range(HC_PAIR):
            for j in range(packed.shape[0] // 8):
                o_ref[h, j, tl * 8:(tl + 1) * 8, :] = packed[j * 8:(j + 1) * 8, h * HC:(h + 1) * HC]


def hy_stage1(x, f1, *, blk_off, n_blk, batch, rows_hi):
    r2 = f1.shape[1]
    rows = rows_hi * LC_R2
    tl_n = LC_TL
    return pl.pallas_call(
        functools.partial(_hy1_kernel, tl_n=tl_n, nh=rows_hi),
        grid=(n_blk // HC_PAIR, batch, LC_R2 // tl_n),
        in_specs=[pl.BlockSpec((HC_PAIR, rows, HC), lambda p, b, c: (blk_off // HC_PAIR + p, b, 0)),
                  pl.BlockSpec((tl_n, r2, rows_hi), lambda p, b, c: (c, 0, 0))],
        out_specs=pl.BlockSpec((HC_PAIR, None, r2 // 16, tl_n * 8, HC), lambda p, b, c: (p, b, 0, c, 0)),
        out_shape=jax.ShapeDtypeStruct((n_blk, batch, r2 // 16, LC_R2 * 8, HC), jnp.uint32),
        compiler_params=_cparams(3),
        name="hy_stage1",
    )(x, f1[:, :, :rows_hi])


def _hy2_kernel(a_ref, f2_ref, *rest, spectrum_only):
    if spectrum_only:
        (o_ref,) = rest
    else:
        h_ref, f3_ref, o_ref = rest
    for jk in range(8 * a_ref.shape[1]):
        jj, kk = divmod(jk, 8)
        w = jnp.concatenate([a_ref[h, jj, pl.ds(kk, LC_R2, stride=8), :] for h in range(HC_PAIR)], axis=1)
        a_in = pltpu.bitcast(w, BF16)
        x = jnp.dot(f2_ref[...], a_in, preferred_element_type=F32)
        if spectrum_only:
            o_ref[jk] = x.astype(o_ref.dtype)
            continue
        hs = h_ref[jk].astype(F32)
        xr, xi = x[:LC_R2], x[LC_R2:]
        hr, hi = hs[:LC_R2], hs[LC_R2:]
        prod = jnp.concatenate([xr * hr - xi * hi, xr * hi + xi * hr], axis=0).astype(BF16)
        g = jnp.dot(f3_ref[...], prod, preferred_element_type=F32)
        packed = pltpu.bitcast(g.astype(BF16), jnp.uint32)
        for h in range(HC_PAIR):
            o_ref[h, jj, pl.ds(kk, LC_R2, stride=8), :] = packed[:, h * HC:(h + 1) * HC]


def hy_spectrum(a, f2):
    cb, _, nj, rows, _ = a.shape
    jn = HY_JN if nj % HY_JN == 0 else 1
    mat = pl.BlockSpec((2 * LC_R2, 2 * LC_R2), lambda j, p: (0, 0))
    return pl.pallas_call(
        functools.partial(_hy2_kernel, spectrum_only=True),
        grid=(nj // jn, cb // HC_PAIR),
        in_specs=[pl.BlockSpec((HC_PAIR, None, jn, rows, HC), lambda j, p: (p, 0, j, 0, 0)), mat],
        out_specs=pl.BlockSpec((None, 8 * jn, 2 * LC_R2, HC_PAIR * HC), lambda j, p: (p, j, 0, 0)),
        out_shape=jax.ShapeDtypeStruct((cb // HC_PAIR, nj * 8, 2 * LC_R2, HC_PAIR * HC), BF16),
        compiler_params=_cparams(2),
        name="hy_spectrum",
    )(a, f2)


def hy_stage23(a, spec, f2, f3):
    cb, bsz, nj, rows, _ = a.shape
    jn = HY_JN if nj % HY_JN == 0 else 1
    mat = pl.BlockSpec((2 * LC_R2, 2 * LC_R2), lambda j, p, b: (0, 0))
    blk = pl.BlockSpec((HC_PAIR, None, jn, rows, HC), lambda j, p, b: (p, b, j, 0, 0))
    return pl.pallas_call(
        functools.partial(_hy2_kernel, spectrum_only=False),
        grid=(nj // jn, cb // HC_PAIR, bsz),
        in_specs=[blk, mat,
                  pl.BlockSpec((None, 8 * jn, 2 * LC_R2, HC_PAIR * HC), lambda j, p, b: (p, j, 0, 0)), mat],
        out_specs=blk,
        out_shape=jax.ShapeDtypeStruct(a.shape, jnp.uint32),
        compiler_params=_cparams(3),
        name="hy_stage23",
    )(a, f2, spec, f3)


def _hy4_kernel(g_ref, f_ref, o_ref, *, tl_n, nh):
    c = pl.program_id(2)
    for tl in range(tl_n):
        w = jnp.concatenate([g_ref[h, :, tl * 8:(tl + 1) * 8, :].reshape(-1, HC) for h in range(HC_PAIR)], axis=1)
        g_in = pltpu.bitcast(w, BF16)
        y = jnp.dot(f_ref[tl], g_in, preferred_element_type=F32)
        for h in range(HC_PAIR):
            o_ref[h, pl.ds(c * tl_n + tl, nh, stride=LC_R2), :] = y[:, h * HC:(h + 1) * HC]


def hy_stage4(g, f4, *, seq_len):
    cb, bsz, nj, _, _ = g.shape
    nh = seq_len // LC_R2
    tl_n = LC_TL
    return pl.pallas_call(
        functools.partial(_hy4_kernel, tl_n=tl_n, nh=nh),
        grid=(cb // HC_PAIR, bsz, LC_R2 // tl_n),
        in_specs=[pl.BlockSpec((HC_PAIR, None, nj, tl_n * 8, HC), lambda p, b, c: (p, b, 0, c, 0)),
                  pl.BlockSpec((tl_n, nh, 16 * nj), lambda p, b, c: (c, 0, 0))],
        out_specs=pl.BlockSpec((HC_PAIR, seq_len, HC), lambda p, b, c: (p, b, 0)),
        out_shape=jax.ShapeDtypeStruct((cb, bsz * seq_len, HC), F32),
        compiler_params=_cparams(3),
        name="hy_stage4",
    )(g, f4)


def _hy_gate_kernel(y_ref, u_ref, gate_ref, skip_ref, o_ref):
    o_ref[...] = (gate_ref[...] * (y_ref[...] + skip_ref[...] * u_ref[...])).astype(o_ref.dtype)


def hy_gate(y, u_arr, u_off, gate_arr, gate_off, skip, *, natural, tr=4096):
    cb, rows, _ = y.shape
    tr = min(tr, rows)
    if natural:
        out_spec = pl.BlockSpec((tr, HC), lambda i, r: (r, i))
        out_shape = jax.ShapeDtypeStruct((rows, cb * HC), BF16)
    else:
        out_spec = pl.BlockSpec((None, tr, HC), lambda i, r: (i, r, 0))
        out_shape = jax.ShapeDtypeStruct((cb, rows, HC), F32)
    return pl.pallas_call(
        _hy_gate_kernel,
        grid=(cb, rows // tr),
        in_specs=[pl.BlockSpec((None, tr, HC), lambda i, r: (i, r, 0)),
                  pl.BlockSpec((None, tr, HC), lambda i, r: (u_off + i, r, 0)),
                  pl.BlockSpec((None, tr, HC), lambda i, r: (gate_off + i, r, 0)),
                  pl.BlockSpec((None, 1, HC), lambda i, r: (i, 0, 0))],
        out_specs=out_spec,
        out_shape=out_shape,
        compiler_params=_cparams(2),
        name="hy_gate",
    )(y, u_arr, gate_arr, skip.reshape(cb, 1, HC))


def _hy_taps_kernel(a_ref, t_ref, w_ref, dl_ref, o_ref):
    a = a_ref[...]
    a_hi = a.astype(BF16)
    a_lo = (a - a_hi.astype(F32)).astype(BF16)
    w = w_ref[...]
    w_hi = w.astype(BF16)
    w_lo = (w - w_hi.astype(F32)).astype(BF16)
    h = (jnp.dot(a_hi, w_hi, preferred_element_type=F32) + jnp.dot(a_hi, w_lo, preferred_element_type=F32)
         + jnp.dot(a_lo, w_hi, preferred_element_type=F32))
    o_ref[...] = h * jnp.exp(-t_ref[...] * dl_ref[...])


def hy_taps(a_all, t_all, w4, deltas, *, seq_len, order, tr=2048):
    nrow, width = a_all.shape
    d = deltas.shape[0]
    tr = min(tr, seq_len)
    per_dir = seq_len // tr
    return pl.pallas_call(
        _hy_taps_kernel,
        grid=(d // HC, nrow // tr),
        in_specs=[pl.BlockSpec((tr, width), lambda i, r: (r, 0)),
                  pl.BlockSpec((tr, 1), lambda i, r: (r, 0)),
                  pl.BlockSpec((None, None, width, HC), lambda i, r: (order, r // per_dir, 0, i)),
                  pl.BlockSpec((1, HC), lambda i, r: (0, i))],
        out_specs=pl.BlockSpec((None, tr, HC), lambda i, r: (i, r, 0)),
        out_shape=jax.ShapeDtypeStruct((d // HC, nrow, HC), F32),
        compiler_params=_cparams(2),
        name="hy_taps",
    )(a_all, t_all, w4.reshape(width, HY_ORDER, 2, d).transpose(1, 2, 0, 3), deltas.reshape(1, d))


def _hyena_filter_hidden(n, f_w1, f_b1, f_w2, f_b2, f_w3, f_b3, f_freq):
    hp = lax.Precision.HIGHEST
    lin = jnp.linspace(0.0, 1.0, n, dtype=F32)
    idx = jnp.arange(n, dtype=F32)
    bands = jnp.linspace(1e-4, HY_BANDS - 1, HY_BANDS, dtype=F32)[None, :]
    t = jnp.concatenate([lin, jnp.zeros((1,), F32), lin[:0:-1]])[:, None]
    pos = jnp.concatenate([idx, jnp.zeros((1,), F32), idx[:0:-1]])[:, None]
    ang = (2.0 * math.pi / n) * pos
    emb = jnp.concatenate([t, jnp.cos(bands * ang), -jnp.sin(bands * ang)], axis=-1)
    a = jnp.sin(f_freq * (jnp.dot(emb, f_w1, precision=hp) + f_b1))
    a = jnp.sin(f_freq * (jnp.dot(a, f_w2, precision=hp) + f_b2))
    a = jnp.sin(f_freq * (jnp.dot(a, f_w3, precision=hp) + f_b3))
    keep = (jnp.arange(2 * n) != n)[:, None]
    return jnp.where(keep, a, 0.0), t


def _hyena_deltas(d):
    return jnp.abs(jnp.linspace(math.log(HY_DECAY_TARGET) / HY_FAST_DECAY,
                                math.log(HY_DECAY_TARGET) / HY_SLOW_DECAY, d, dtype=F32))


def hyena_long_convs2(zc, a_all, t_all, w4, skip, *, batch, seq_len, d):
    cb = d // HC
    f1, f2, f3, f4 = _hy_mats(seq_len)
    nh = seq_len // LC_R2
    deltas = _hyena_deltas(d)
    u, u_off = zc, 2 * cb
    for o in range(HY_ORDER):
        taps = hy_taps(a_all, t_all, w4, deltas, seq_len=seq_len, order=o)
        spec = hy_spectrum(hy_stage1(taps, f1, blk_off=0, n_blk=cb, batch=1, rows_hi=2 * nh), f2)
        a = hy_stage1(u, f1, blk_off=u_off, n_blk=cb, batch=batch, rows_hi=nh)
        y = hy_stage4(hy_stage23(a, spec, f2, f3), f4, seq_len=seq_len)
        u = hy_gate(y, u, u_off, zc, o * cb, skip[o], natural=(o == HY_ORDER - 1))
        u_off = 0
    return u


def _final_norm_kernel(x_ref, g_ref, o_ref):
    x = x_ref[...]
    o_ref[...] = x * lax.rsqrt(jnp.mean(x * x, axis=-1, keepdims=True) + NORM_EPS) * g_ref[...]


def final_norm(x, g, *, rows, tm=ROW_TILE):
    d = x.shape[1]
    return pl.pallas_call(
        _final_norm_kernel,
        grid=(rows // tm,),
        in_specs=[pl.BlockSpec((tm, d), lambda i: (i, 0)), pl.BlockSpec((1, d), lambda i: (0, 0))],
        out_specs=pl.BlockSpec((tm, d), lambda i: (i, 0)),
        out_shape=jax.ShapeDtypeStruct((rows, d), F32),
        compiler_params=_cparams(1),
        name="final_norm",
    )(x, g)


def _rope_tables(n_lat, seq_len, n_rows):
    t = jnp.arange(n_rows, dtype=jnp.int32)
    row = ((t % seq_len) // GRID_W).astype(F32)
    col = (t % GRID_W).astype(F32)
    lane = jnp.arange(LANES)
    dd = lane % HEAD_DIM
    quarter = HEAD_DIM // 4
    inv_freq = ROPE_BASE ** (-(dd % quarter).astype(F32) / quarter)
    pos = jnp.where(dd[None, :] < HEAD_DIM // 2, row[:, None], col[:, None])
    ang = pos * inv_freq[None, :]
    sign = jnp.where((dd % (HEAD_DIM // 2)) < quarter, -1.0, 1.0).astype(F32)
    lat = (t < n_lat)[:, None]
    cos = jnp.where(lat, jnp.cos(ang), 1.0)
    sin = jnp.where(lat, jnp.sin(ang) * sign[None, :], 0.0)
    return cos, sin


def _wa_head_perm(n_heads):
    order = []
    for p in range(n_heads // (2 * WA_GROUP)):
        for g in range(WA_GROUP):
            order += [2 * WA_GROUP * p + g, 2 * WA_GROUP * p + WA_GROUP + g]
    cols = jnp.asarray(order, jnp.int32)[:, None] * HEAD_DIM + jnp.arange(HEAD_DIM, dtype=jnp.int32)[None, :]
    return cols.reshape(-1)


def kernel(x, c, ctx, c_ctx, w_ada, b_ada, g_mix, g_ffn, hy_w_in, hy_b_in, hy_w_short, hy_b_short, hy_f_w1, hy_f_b1, hy_f_w2, hy_f_b2, hy_f_w3, hy_f_b3, hy_f_freq, hy_f_w4, hy_skip, hy_w_out, hy_b_out, cf_w_pw1, cf_b_pw1, cf_w_dw, cf_b_dw, cf_ln_g, cf_ln_b, cf_w_pw2, cf_b_pw2, wa_w_qkv, wa_w_o, wa_sinks, na_w_qkv, na_w_o, na_rpb, moe_w_router, moe_b_router, moe_w_gu, moe_b_gu, moe_w_down, moe_b_down, g_final):
    bsz, n, d = x.shape
    n_ctx = ctx.shape[1]
    depth = w_ada.shape[0]
    n_lat = bsz * n
    m_all = n_lat + bsz * n_ctx

    cond = jnp.zeros((8, d), F32).at[:bsz].set(jax.nn.silu(c)).at[bsz].set(jax.nn.silu(c_ctx))
    mods = adaln(cond, w_ada, b_ada)
    xu = jnp.concatenate([x.reshape(n_lat, d), ctx.reshape(bsz * n_ctx, d)], axis=0)
    zeros_d = jnp.zeros((1, d), F32)

    for i in range(depth):
        kind, j = i % N_MIXERS, i // N_MIXERS
        last = i == depth - 1
        mod = [mods[i, :bsz + 1, k * d:(k + 1) * d].reshape(bsz + 1, 1, d) for k in range(6)]
        m_out = n_lat if last else m_all
        gm = g_mix[i].reshape(1, d)
        need_ctx_in = (not last) or kind >= 2
        m_in = m_all if need_ctx_in else n_lat
        x_in = xu[:m_in]

        if kind == 0:
            z = nm_matmul(x_in, gm, mod[0], mod[1], hy_w_in[j].astype(BF16), hy_b_in[j].reshape(1, -1),
                          seg_rows=n)
            filt_w = (hy_f_w1[j], hy_f_b1[j], hy_f_w2[j], hy_f_b2[j], hy_f_w3[j], hy_f_b3[j], hy_f_freq[j],
                      hy_f_w4[j])
            w_s, b_s = hy_w_short[j], hy_b_short[j].reshape(1, -1)
            w_o, b_o = hy_w_out[j].astype(BF16), hy_b_out[j].reshape(1, d)
            zc = dwconv(z, w_s, b_s, start=0, total=n_lat, seq_len=n, tc=HC, blocked=True, rows=2048,
                        out_dtype=F32)
            a_all, t_all = _hyena_filter_hidden(n, *filt_w[:-1])
            y = hyena_long_convs2(zc, a_all, t_all, hy_f_w4[j], hy_skip[j], batch=bsz, seq_len=n, d=d)
            xu = mm_res(y, w_o, b_o, xu, mod[2], seg_rows=n, tm=BIG_ROW_TILE)
            if m_in > n_lat:
                zcc = dwconv(z, w_s, b_s, start=n_lat, total=m_in - n_lat, seq_len=n_ctx, tc=512)
                taps_c = _hyena_taps(n_ctx, d, *filt_w)
                fwd, inv = _dft_small_mats(n_ctx)
                yc = short_long_conv(zcc, 2 * d, zcc, 0, plain_mm(fwd, taps_c[0]), hy_skip[j][0].reshape(1, d),
                                     fwd, inv, batch=bsz, seq_len=n_ctx)
                yc = short_long_conv(yc, 0, zcc, d, plain_mm(fwd, taps_c[1]), hy_skip[j][1].reshape(1, d),
                                     fwd, inv, batch=bsz, seq_len=n_ctx)
                if not last:
                    xu = mm_res(yc, w_o, b_o, xu, mod[2], seg_rows=n, row_off=n_lat)
            y = None
        elif kind == 1:
            w1 = cf_w_pw1[j].astype(BF16).reshape(d, 2, d).transpose(1, 0, 2)
            a = nm_matmul(x_in, gm, mod[0], mod[1], w1, cf_b_pw1[j].reshape(2, 1, d), seg_rows=n, mode="glu")
            conv = functools.partial(dwconv, a, cf_w_dw[j], cf_b_dw[j].reshape(1, d), post="ln_silu",
                                     ln=(cf_ln_g[j].reshape(1, d), cf_ln_b[j].reshape(1, d)), rows=64)
            y = conv(start=0, total=n_lat, seq_len=n, same_rows=True)
            if m_in > n_lat:
                y = conv(start=n_lat, total=m_in - n_lat, seq_len=n_ctx, same_rows=True, into=y)
            w_o, b_o = cf_w_pw2[j].astype(BF16), cf_b_pw2[j].reshape(1, d)
        elif kind == 2:
            perm = _wa_head_perm(d // HEAD_DIM)
            scale = HEAD_DIM ** -0.5
            w_qkv = jnp.concatenate([wa_w_qkv[j][:, :d][:, perm] * scale, wa_w_qkv[j][:, d:]], axis=1).astype(BF16)
            n_out = w_qkv.shape[1]
            qkv = nm_matmul(x_in, gm, mod[0], mod[1], w_qkv, jnp.zeros((1, n_out), F32), seg_rows=n,
                            mode="rope", rope=_rope_tables(n_lat, n, m_in), n_rope_cols=d + (n_out - d) // 2)
            y = window_attention(qkv, wa_sinks[j], batch=bsz, seq_len=n, ctx_len=n_ctx, d=d)
            w_o, b_o = wa_w_o[j][perm].astype(BF16), zeros_d
        else:
            scale = HEAD_DIM ** -0.5
            w_qkv = jnp.concatenate([na_w_qkv[j][:, :d] * scale, na_w_qkv[j][:, d:]], axis=1).astype(BF16)
            qkv = nm_matmul(x_in, gm, mod[0], mod[1], w_qkv, jnp.zeros((1, 3 * d), F32), seg_rows=n)
            y = neighbourhood_attention(qkv, na_rpb[j], batch=bsz, seq_len=n, ctx_len=n_ctx, d=d,
                                        with_ctx_out=not last)
            w_o, b_o = na_w_o[j].astype(BF16), zeros_d
        if y is not None:
            xu = mm_res(y, w_o, b_o, xu, mod[2], seg_rows=n, rows=n_lat, tm=BIG_ROW_TILE)
            if m_out > n_lat:
                xu = mm_res(y, w_o, b_o, xu, mod[2], seg_rows=n, rows=m_out - n_lat, row_off=n_lat, a_off=n_lat)

        ranges = ((0, m_out),)
        routed = [moe_dispatch(xu, g_ffn[i].reshape(1, d), mod[3], mod[4], moe_w_router[i], moe_b_router[i],
                               seg_rows=n, rows=cnt, row_off=off) for off, cnt in ranges]
        y_slots = [expert_mlp(xs, blk_e, n_used, moe_w_gu, moe_b_gu[i], moe_w_down, moe_b_down[i], layer=i)
                   for xs, blk_e, n_used, _, _ in routed]
        for (off, cnt), (_, _, _, dest_km, gates), y_slot in zip(ranges, routed, y_slots):
            xu = moe_combine(xu, y_slot, dest_km, gates, mod[5], seg_rows=n, rows=cnt, row_off=off)
    return final_norm(xu, g_final.reshape(1, d), rows=n_lat).reshape(bsz, n, d)
```

```python
import functools
import math

import jax
import jax.numpy as jnp
import numpy as np
from jax import lax
from jax.experimental import pallas as pl
from jax.experimental.pallas import tpu as pltpu

F32 = jnp.float32
BF16 = jnp.bfloat16

GRID_W = 64
N_MIXERS = 4
NORM_EPS = 1e-6
NEG_INF = -1e30
HEAD_DIM = 64
ROPE_BASE = 10000.0

HY_ORDER = 2
HY_BANDS = 16
HY_DECAY_TARGET = 1e-2
HY_FAST_DECAY = 0.3
HY_SLOW_DECAY = 1.5

WA_GROUP = 4
WA_WINDOW = 128
NA_WIN_ROWS = 8
NA_WIN_COLS = 16

TOP_K = 4
SWIGLU_LIMIT = 7.0
SWIGLU_ALPHA = 1.702

LANES = 128
ROW_TILE = 512
BIG_ROW_TILE = 1024
MOE_ROWS = 512
VMEM_LIMIT = 56 * 1024 * 1024
EXPERT_VMEM_LIMIT = 58 * 1024 * 1024


def _cparams(n_axes):
    return pltpu.CompilerParams(dimension_semantics=("arbitrary",) * n_axes,
                                vmem_limit_bytes=VMEM_LIMIT)


def _col_tile(n, pref=1024):
    t = min(pref, n)
    while n % t:
        t //= 2
    return t


def _adaln_kernel(c_ref, w_ref, b_ref, o_ref):
    w = w_ref[0]
    w_hi = w.astype(BF16)
    w_lo = (w - w_hi.astype(F32)).astype(BF16)
    c = c_ref[...]
    c_hi = c.astype(BF16)
    c_lo = (c - c_hi.astype(F32)).astype(BF16)
    acc = jnp.dot(c_hi, w_hi, preferred_element_type=F32)
    acc += jnp.dot(c_hi, w_lo, preferred_element_type=F32)
    acc += jnp.dot(c_lo, w_hi, preferred_element_type=F32)
    o_ref[0] = acc + b_ref[0]


def adaln(cond, w_ada, b_ada):
    depth, d, n6 = w_ada.shape
    tn = _col_tile(n6, 1024)
    return pl.pallas_call(
        _adaln_kernel,
        grid=(depth, n6 // tn),
        in_specs=[pl.BlockSpec((8, d), lambda l, j: (0, 0)),
                  pl.BlockSpec((1, d, tn), lambda l, j: (l, 0, j)),
                  pl.BlockSpec((1, 1, tn), lambda l, j: (l, 0, j))],
        out_specs=pl.BlockSpec((1, 8, tn), lambda l, j: (l, 0, j)),
        out_shape=jax.ShapeDtypeStruct((depth, 8, n6), F32),
        compiler_params=_cparams(2),
        name="adaln",
    )(cond, w_ada, b_ada.reshape(depth, 1, n6))


def _norm_mod(x, g, shift, scale):
    y = x * lax.rsqrt(jnp.mean(x * x, axis=-1, keepdims=True) + NORM_EPS)
    return (y * g) * (1 + scale) + shift


def _rope_tile(x, cos, sin):
    lane = lax.broadcasted_iota(jnp.int32, x.shape, 1)
    nxt = pltpu.roll(x, LANES - 16, axis=1)
    prv = pltpu.roll(x, 16, axis=1)
    partner = jnp.where((lane // 16) % 2 == 0, nxt, prv)
    return x * cos + partner * sin


def _nm_mm_kernel(x_ref, g_ref, sh_ref, sc_ref, w_ref, b_ref, *rest, mode, n_rope, n_alias):
    if n_alias:
        rest = rest[:-3] + rest[-2:]
    if mode == "rope":
        cos_ref, sin_ref, o_ref, h_ref = rest
    else:
        o_ref, h_ref = rest
    j = pl.program_id(1)

    @pl.when(j == 0)
    def _():
        h_ref[...] = _norm_mod(x_ref[...], g_ref[...], sh_ref[0], sc_ref[0]).astype(BF16)

    h = h_ref[...]
    if mode == "glu":
        a = jnp.dot(h, w_ref[0], preferred_element_type=F32) + b_ref[0]
        gate = jnp.dot(h, w_ref[1], preferred_element_type=F32) + b_ref[1]
        o_ref[...] = (a * jax.nn.sigmoid(gate)).astype(o_ref.dtype)
        return
    acc = jnp.dot(h, w_ref[...], preferred_element_type=F32) + b_ref[...]
    if mode == "rope":
        @pl.when(j < n_rope)
        def _():
            cos = cos_ref[...]
            sin = sin_ref[...]
            for g in range(acc.shape[1] // LANES):
                sl = slice(g * LANES, (g + 1) * LANES)
                o_ref[:, sl] = _rope_tile(acc[:, sl], cos, sin).astype(o_ref.dtype)

        @pl.when(j >= n_rope)
        def _():
            o_ref[...] = acc.astype(o_ref.dtype)
    else:
        o_ref[...] = acc.astype(o_ref.dtype)


def nm_matmul(x, g, shift, scale, w, b, *, seg_rows, mode="plain", rope=None, n_rope_cols=0,
              tm=ROW_TILE, tn=None):
    m, d = x.shape
    n_seg = shift.shape[0]
    n = w.shape[-1]
    tn = tn or (_col_tile(math.gcd(n, n_rope_cols), 512) if mode == "rope" else _col_tile(n, 1024))
    big = BIG_ROW_TILE if seg_rows % BIG_ROW_TILE == 0 else tm
    lat = min(m, (m // seg_rows) * seg_rows) if m >= seg_rows else 0
    ranges = [(0, lat, big)] if lat else []
    if m > lat:
        ranges.append((lat, m - lat, tm))
    out = jnp.zeros((m, n), BF16) if len(ranges) > 1 else None
    for row_off, rows, rt in ranges:
        blk_off = row_off // rt
        seg = lambda i, j, rt=rt, blk_off=blk_off: (jnp.minimum((blk_off + i) * rt // seg_rows, n_seg - 1), 0, 0)
        row_blk = lambda i, j, blk_off=blk_off: (blk_off + i, 0)
        in_specs = [pl.BlockSpec((rt, d), row_blk),
                    pl.BlockSpec((1, d), lambda i, j: (0, 0)),
                    pl.BlockSpec((1, 1, d), seg),
                    pl.BlockSpec((1, 1, d), seg)]
        if mode == "glu":
            in_specs += [pl.BlockSpec((2, d, tn), lambda i, j: (0, 0, j)),
                         pl.BlockSpec((2, 1, tn), lambda i, j: (0, 0, j))]
        else:
            in_specs += [pl.BlockSpec((d, tn), lambda i, j: (0, j)),
                         pl.BlockSpec((1, tn), lambda i, j: (0, j))]
        args = [x, g, shift, scale, w, b]
        if mode == "rope":
            in_specs += [pl.BlockSpec((rt, LANES), row_blk)] * 2
            args += list(rope)
        aliases = {}
        if out is not None:
            in_specs.append(pl.BlockSpec(memory_space=pl.ANY))
            args.append(out)
            aliases = {len(args) - 1: 0}
        out = pl.pallas_call(
            functools.partial(_nm_mm_kernel, mode=mode, n_rope=n_rope_cols // tn, n_alias=len(aliases)),
            grid=(rows // rt, n // tn),
            in_specs=in_specs,
            out_specs=pl.BlockSpec((rt, tn), lambda i, j, blk_off=blk_off: (blk_off + i, j)),
            out_shape=jax.ShapeDtypeStruct((m, n), BF16),
            input_output_aliases=aliases,
            scratch_shapes=[pltpu.VMEM((rt, d), BF16)],
            compiler_params=_cparams(2),
            name="nm_matmul_" + mode,
        )(*args)
    return out


def _mm_res_kernel(a_ref, w_ref, b_ref, res_ref, gate_ref, o_ref):
    if len(a_ref.shape) == 3:
        kc = a_ref.shape[2]
        acc = b_ref[...]
        for cb in range(a_ref.shape[0]):
            acc = acc + jnp.dot(a_ref[cb], w_ref[cb * kc:(cb + 1) * kc, :], preferred_element_type=F32)
    else:
        acc = jnp.dot(a_ref[...], w_ref[...], preferred_element_type=F32) + b_ref[...]
    o_ref[...] = res_ref[...] + gate_ref[0] * acc


def mm_res(a, w, b, res, gate, *, seg_rows, row_off=0, rows=None, a_off=0, tm=ROW_TILE, tn=None):
    a_blk = a_off // tm
    if a.ndim == 3:
        m = rows or a.shape[1]
        k = a.shape[0] * a.shape[2]
        a_spec = pl.BlockSpec((a.shape[0], tm, a.shape[2]), lambda i, j: (0, a_blk + i, 0))
    else:
        m, k = rows or a.shape[0], a.shape[1]
        a_spec = pl.BlockSpec((tm, k), lambda i, j: (a_blk + i, 0))
    n = w.shape[1]
    n_seg = gate.shape[0]
    tn = tn or _col_tile(n, 1024)
    blk_off = row_off // tm
    seg = lambda i, j: (jnp.minimum((blk_off + i) * tm // seg_rows, n_seg - 1), 0, j)
    return pl.pallas_call(
        _mm_res_kernel,
        grid=(m // tm, n // tn),
        in_specs=[a_spec,
                  pl.BlockSpec((k, tn), lambda i, j: (0, j)),
                  pl.BlockSpec((1, tn), lambda i, j: (0, j)),
                  pl.BlockSpec((tm, tn), lambda i, j: (blk_off + i, j)),
                  pl.BlockSpec((1, 1, tn), seg)],
        out_specs=pl.BlockSpec((tm, tn), lambda i, j: (blk_off + i, j)),
        out_shape=jax.ShapeDtypeStruct(res.shape, F32),
        input_output_aliases={3: 0},
        compiler_params=_cparams(2),
        name="mm_res",
    )(a, w, b, res, gate)


CONV_HALO = 16


def _dwconv_kernel(prev_ref, x_ref, next_ref, w_ref, b_ref, *rest, taps, blocks_per_seq, post, sub, n_alias):
    if n_alias:
        n_tail = 3 if post == "ln_silu" else 2
        rest = rest[:-n_tail - 1] + rest[-n_tail:]
    if post == "ln_silu":
        g_ref, beta_ref, o_ref, win_ref, conv_ref = rest
    else:
        o_ref, win_ref = rest
    i = pl.program_id(0)
    pos = i % blocks_per_seq
    rows, c = x_ref.shape
    half = taps // 2
    zero = jnp.zeros((CONV_HALO, c), F32)
    win_ref[CONV_HALO:CONV_HALO + rows, :] = x_ref[...].astype(F32)

    @pl.when(pos == 0)
    def _():
        win_ref[0:CONV_HALO, :] = zero

    @pl.when(pos > 0)
    def _():
        win_ref[0:CONV_HALO, :] = prev_ref[...].astype(F32)

    @pl.when(pos == blocks_per_seq - 1)
    def _():
        win_ref[CONV_HALO + rows:, :] = zero

    @pl.when(pos < blocks_per_seq - 1)
    def _():
        win_ref[CONV_HALO + rows:, :] = next_ref[...].astype(F32)

    bias = b_ref[...]
    if post == "ln_silu":
        ext = rows + sub
        groups = {}
        for t in range(taps):
            off = CONV_HALO - half + t
            groups.setdefault(off % sub, []).append((t, off // sub))
        lw = min(2 * LANES, c)
        for lc in range(0, c, lw):
            lanes = slice(lc, lc + lw)
            y = jnp.broadcast_to(bias[:, lanes], (rows, lw))
            for s, members in sorted(groups.items()):
                acc = None
                for t, a in members:
                    w_rows = jnp.concatenate([w_ref[t, :, lanes]] * (ext // sub), axis=0)
                    term = w_rows * win_ref[sub * a:sub * a + ext, lanes]
                    acc = term if acc is None else acc + term
                y = y + acc[s:s + rows]
            conv_ref[:, lanes] = y
        for s in range(rows // sub):
            acc = conv_ref[s * sub:(s + 1) * sub, :]
            mu = jnp.mean(acc, axis=-1, keepdims=True)
            xc = acc - mu
            var = jnp.mean(xc * xc, axis=-1, keepdims=True)
            y = xc * lax.rsqrt(var + NORM_EPS) * g_ref[...] + beta_ref[...]
            o_ref[s * sub:(s + 1) * sub, :] = (y * jax.nn.sigmoid(y)).astype(o_ref.dtype)
        return
    for s in range(rows // sub):
        base = CONV_HALO + s * sub - half
        acc = jnp.broadcast_to(bias, (sub, c))
        for t in range(taps):
            acc = acc + w_ref[t] * win_ref[base + t:base + t + sub, :]
        o_ref[s * sub:(s + 1) * sub, :] = acc.astype(o_ref.dtype)


def dwconv(x, w, b, *, start, total, seq_len, post=None, ln=None, rows=256, tc=None, out_dtype=None,
           blocked=False, same_rows=False, into=None):
    m, c = x.shape
    taps = w.shape[0]
    tc = tc or c
    out_dtype = out_dtype or BF16
    sub = 8
    w = jnp.broadcast_to(w[:, None, :], (taps, sub, c))
    r = min(rows, seq_len)
    bps = seq_len // r
    hb = r // CONV_HALO
    off = start // r
    offh = start // CONV_HALO
    nh = m // CONV_HALO
    in_specs = [pl.BlockSpec((CONV_HALO, tc), lambda i, j: (jnp.maximum(offh + i * hb - 1, 0), j)),
                pl.BlockSpec((r, tc), lambda i, j: (off + i, j)),
                pl.BlockSpec((CONV_HALO, tc), lambda i, j: (jnp.minimum(offh + (i + 1) * hb, nh - 1), j)),
                pl.BlockSpec((taps, sub, tc), lambda i, j: (0, 0, j)),
                pl.BlockSpec((1, tc), lambda i, j: (0, j))]
    args = [x, x, x, w, b]
    if post == "ln_silu":
        in_specs += [pl.BlockSpec((1, tc), lambda i, j: (0, j))] * 2
        args += list(ln)
    aliases = {}
    if blocked:
        out_spec = pl.BlockSpec((None, r, tc), lambda i, j: (j, i, 0))
        out_shape = jax.ShapeDtypeStruct((c // tc, total, tc), out_dtype)
    elif same_rows:
        out_spec = pl.BlockSpec((r, tc), lambda i, j: (off + i, j))
        out_shape = jax.ShapeDtypeStruct((m, c), out_dtype)
        if into is not None:
            in_specs.append(pl.BlockSpec(memory_space=pl.ANY))
            args.append(into)
            aliases = {len(args) - 1: 0}
    else:
        out_spec = pl.BlockSpec((r, tc), lambda i, j: (i, j))
        out_shape = jax.ShapeDtypeStruct((total, c), out_dtype)
    return pl.pallas_call(
        functools.partial(_dwconv_kernel, taps=taps, blocks_per_seq=bps, post=post, sub=sub,
                          n_alias=len(aliases)),
        grid=(total // r, c // tc),
        in_specs=in_specs,
        out_specs=out_spec,
        out_shape=out_shape,
        input_output_aliases=aliases,
        scratch_shapes=[pltpu.VMEM((r + 2 * CONV_HALO, tc), F32)]
        + ([pltpu.VMEM((r, tc), F32)] if post == "ln_silu" else []),
        compiler_params=_cparams(2),
        name="dwconv%d" % taps,
    )(*args)


def _masked_halves(q, lane_lo):
    zero = jnp.zeros_like(q)
    return jnp.where(lane_lo, q, zero), jnp.where(lane_lo, zero, q)


def _softmax_pv(s_parts, v_parts, sink):
    m = s_parts[0].max(axis=-1, keepdims=True)
    for s in s_parts[1:]:
        m = jnp.maximum(m, s.max(axis=-1, keepdims=True))
    if sink is not None:
        m = jnp.maximum(m, sink)
    denom = jnp.exp(sink - m) if sink is not None else 0.0
    o = None
    for s, v in zip(s_parts, v_parts):
        p = jnp.exp(s - m)
        denom = denom + p.sum(axis=-1, keepdims=True)
        pv = jnp.dot(p.astype(BF16), v, preferred_element_type=F32)
        o = pv if o is None else o + pv
    return o / denom


def _nt_dot(a, b):
    return lax.dot_general(a, b, (((1,), (1,)), ((), ())), preferred_element_type=F32)


def _wattn_kernel(sink_ref, q_ref, *rest, local, seq_len, blk):
    if local:
        n_kb = (len(rest) - 4) // 2
        k_refs, v_refs = rest[:n_kb], rest[n_kb:2 * n_kb]
    kc_ref, vc_ref, _, o_ref = rest[-4:]
    p = pl.program_id(2)
    i = pl.program_id(1)
    lane_lo = lax.broadcasted_iota(jnp.int32, (blk, LANES), 1) < HEAD_DIM
    kc = kc_ref[...]
    vc = vc_ref[...]
    if local:
        kl = jnp.concatenate([r[...] for r in k_refs], axis=0)
        vl = jnp.concatenate([r[...] for r in v_refs], axis=0)
        n_keys = n_kb * WA_WINDOW
        qpos = i * blk + lax.broadcasted_iota(jnp.int32, (blk, n_keys), 0)
        kpos = i * blk - WA_WINDOW + lax.broadcasted_iota(jnp.int32, (blk, n_keys), 1)
        valid = (jnp.abs(kpos - qpos) <= WA_WINDOW) & (kpos >= 0) & (kpos < seq_len)
        valid = jnp.concatenate([valid] * WA_GROUP, axis=0)
    qa, qb = [], []
    for g in range(WA_GROUP):
        a, b = _masked_halves(q_ref[:, g * LANES:(g + 1) * LANES], lane_lo)
        qa.append(a)
        qb.append(b)
    outs = []
    for half, qs in enumerate((qa, qb)):
        qs = jnp.concatenate(qs, axis=0)
        sink = jnp.concatenate(
            [jnp.full((blk, 1), sink_ref[8 * p + 4 * half + g], F32) for g in range(WA_GROUP)], axis=0)
        s_parts, v_parts = [], []
        if local:
            s_parts.append(jnp.where(valid, _nt_dot(qs, kl), NEG_INF))
            v_parts.append(vl)
        s_parts.append(_nt_dot(qs, kc))
        v_parts.append(vc)
        outs.append(_softmax_pv(s_parts, v_parts, sink))
    for g in range(WA_GROUP):
        rows = slice(g * blk, (g + 1) * blk)
        o_ref[:, g * LANES:(g + 1) * LANES] = jnp.where(lane_lo, outs[0][rows], outs[1][rows]).astype(o_ref.dtype)


def window_attention(qkv, sinks, *, batch, seq_len, ctx_len, d, blk=512):
    n_pairs = d // (2 * WA_GROUP * HEAD_DIM)
    kcol = d // LANES
    vcol = kcol + n_pairs
    nblk = seq_len // blk
    kb_per_q = blk // WA_WINDOW
    n_kblk = seq_len // WA_WINDOW
    shifts = range(-1, kb_per_q + 1)
    cb0 = batch * seq_len // ctx_len
    qw = WA_GROUP * LANES
    kern = functools.partial(_wattn_kernel, seq_len=seq_len)
    smem = pl.BlockSpec(memory_space=pltpu.SMEM)

    def kspec(col0, shift):
        return pl.BlockSpec((WA_WINDOW, LANES),
                            lambda b, i, p: (b * n_kblk + jnp.clip(i * kb_per_q + shift, 0, n_kblk - 1), col0 + p))

    ctx_k = pl.BlockSpec((ctx_len, LANES), lambda b, i, p: (cb0 + b, kcol + p))
    ctx_v = pl.BlockSpec((ctx_len, LANES), lambda b, i, p: (cb0 + b, vcol + p))
    lat = pl.pallas_call(
        functools.partial(kern, local=True, blk=blk),
        grid=(batch, nblk, n_pairs),
        in_specs=[smem, pl.BlockSpec((blk, qw), lambda b, i, p: (b * nblk + i, p))]
        + [kspec(kcol, s) for s in shifts] + [kspec(vcol, s) for s in shifts] + [ctx_k, ctx_v]
        + [pl.BlockSpec(memory_space=pl.ANY)],
        out_specs=pl.BlockSpec((blk, qw), lambda b, i, p: (b * nblk + i, p)),
        out_shape=jax.ShapeDtypeStruct((qkv.shape[0], d), BF16),
        input_output_aliases={2 * len(shifts) + 4: 0},
        compiler_params=_cparams(3),
        name="window_attn",
    )(sinks, qkv, *([qkv] * (2 * len(shifts) + 2)), jnp.zeros((qkv.shape[0], d), BF16))
    return pl.pallas_call(
        functools.partial(kern, local=False, blk=ctx_len),
        grid=(batch, 1, n_pairs),
        in_specs=[smem, pl.BlockSpec((ctx_len, qw), lambda b, i, p: (cb0 + b, p)), ctx_k, ctx_v,
                  pl.BlockSpec(memory_space=pl.ANY)],
        out_specs=pl.BlockSpec((ctx_len, qw), lambda b, i, p: (cb0 + b, p)),
        out_shape=jax.ShapeDtypeStruct((qkv.shape[0], d), BF16),
        input_output_aliases={4: 0},
        compiler_params=_cparams(3),
        name="ctx_attn",
    )(sinks, qkv, qkv, qkv, lat)


def _nattn_kernel(q_ref, k0, k1, k2, v0, v1, v2, kc_ref, vc_ref, bias_ref, o_ref, ks_ref, vs_ref,
                  *, grid_rows, rows_per_blk):
    j = pl.program_id(2)
    blk = k0.shape[0]
    for t, (kr, vr) in enumerate(((k0, v0), (k1, v1), (k2, v2))):
        ks_ref[t * blk:(t + 1) * blk, :] = kr[...]
        vs_ref[t * blk:(t + 1) * blk, :] = vr[...]
    kc = kc_ref[...]
    vc = vc_ref[...]
    lane_lo = lax.broadcasted_iota(jnp.int32, (GRID_W, LANES), 1) < HEAD_DIM
    strip = NA_WIN_ROWS * GRID_W
    for r in range(rows_per_blk):
        row = j * rows_per_blk + r
        r0 = jnp.clip(row - NA_WIN_ROWS // 2, 0, grid_rows - NA_WIN_ROWS)
        start = pl.multiple_of((r0 - (j - 1) * rows_per_blk) * GRID_W, GRID_W)
        cls = row - r0
        qa, qb = _masked_halves(q_ref[r * GRID_W:(r + 1) * GRID_W, :], lane_lo)
        qs = jnp.concatenate([qa, qb], axis=0)
        kn = ks_ref[pl.ds(start, strip), :]
        vn = vs_ref[pl.ds(start, strip), :]
        s_nb = _nt_dot(qs, kn) + bias_ref[cls]
        s_cx = _nt_dot(qs, kc)
        o = _softmax_pv([s_nb, s_cx], [vn, vc], None)
        o_ref[r * GRID_W:(r + 1) * GRID_W, :] = jnp.where(lane_lo, o[:GRID_W], o[GRID_W:]).astype(o_ref.dtype)


def _na_bias_table(rpb):
    h = rpb.shape[0]
    n_dcol = 2 * NA_WIN_COLS - 1
    cols = jnp.arange(GRID_W)
    col_start = jnp.clip(cols - NA_WIN_COLS // 2, 0, GRID_W - NA_WIN_COLS)
    inwin = (cols[None, :] >= col_start[:, None]) & (cols[None, :] < col_start[:, None] + NA_WIN_COLS)
    dcol = cols[None, :] - cols[:, None] + NA_WIN_COLS - 1
    pick = (dcol[None] == jnp.arange(n_dcol)[:, None, None]).astype(F32).reshape(n_dcol, -1)
    spread = jnp.dot(rpb.astype(F32).reshape(-1, n_dcol), pick, precision=lax.Precision.HIGHEST)
    spread = spread.reshape(h, 2 * NA_WIN_ROWS - 1, GRID_W, GRID_W)
    spread = jnp.where(inwin[None, None], spread, NEG_INF)
    tab = jnp.stack([spread[:, NA_WIN_ROWS - 1 - cls:2 * NA_WIN_ROWS - 1 - cls]
                     for cls in range(NA_WIN_ROWS)], axis=1)
    tab = tab.transpose(0, 1, 3, 2, 4).reshape(h, NA_WIN_ROWS, GRID_W, NA_WIN_ROWS * GRID_W)
    tab = tab.reshape(h // 2, 2, NA_WIN_ROWS, GRID_W, NA_WIN_ROWS * GRID_W).transpose(0, 2, 1, 3, 4)
    return tab.reshape(h // 2, NA_WIN_ROWS, 2 * GRID_W, NA_WIN_ROWS * GRID_W)


def neighbourhood_attention(qkv, rpb, *, batch, seq_len, ctx_len, d, with_ctx_out):
    n_pairs = d // LANES
    rows_per_blk = min(4 * NA_WIN_ROWS, seq_len // GRID_W)
    blk = rows_per_blk * GRID_W
    grid_rows = seq_len // GRID_W
    nblk = seq_len // blk
    cb0 = batch * seq_len // ctx_len
    bias = _na_bias_table(rpb)

    def kspec(col0, shift):
        return pl.BlockSpec((blk, LANES),
                            lambda b, p, j: (b * nblk + jnp.clip(j + shift, 0, nblk - 1), col0 + p))

    ctx_k = pl.BlockSpec((ctx_len, LANES), lambda b, p, j: (cb0 + b, n_pairs + p))
    ctx_v = pl.BlockSpec((ctx_len, LANES), lambda b, p, j: (cb0 + b, 2 * n_pairs + p))
    lat = pl.pallas_call(
        functools.partial(_nattn_kernel, grid_rows=grid_rows, rows_per_blk=rows_per_blk),
        grid=(batch, n_pairs, nblk),
        in_specs=[pl.BlockSpec((blk, LANES), lambda b, p, j: (b * nblk + j, p))]
        + [kspec(n_pairs, s) for s in (-1, 0, 1)] + [kspec(2 * n_pairs, s) for s in (-1, 0, 1)]
        + [ctx_k, ctx_v,
           pl.BlockSpec((None, NA_WIN_ROWS, 2 * GRID_W, NA_WIN_ROWS * GRID_W), lambda b, p, j: (p, 0, 0, 0))],
        out_specs=pl.BlockSpec((blk, LANES), lambda b, p, j: (b * nblk + j, p)),
        out_shape=jax.ShapeDtypeStruct((batch * seq_len, d), BF16),
        scratch_shapes=[pltpu.VMEM((3 * blk, LANES), BF16), pltpu.VMEM((3 * blk, LANES), BF16)],
        compiler_params=_cparams(3),
        name="neighbourhood_attn",
    )(qkv, *([qkv] * 8), bias)
    if not with_ctx_out:
        return lat
    ctx = pl.pallas_call(
        _cattn_kernel,
        grid=(batch, n_pairs),
        in_specs=[pl.BlockSpec((ctx_len, LANES), lambda b, p: (cb0 + b, p)),
                  pl.BlockSpec((ctx_len, LANES), lambda b, p: (cb0 + b, n_pairs + p)),
                  pl.BlockSpec((ctx_len, LANES), lambda b, p: (cb0 + b, 2 * n_pairs + p))],
        out_specs=pl.BlockSpec((ctx_len, LANES), lambda b, p: (b, p)),
        out_shape=jax.ShapeDtypeStruct((batch * ctx_len, d), BF16),
        compiler_params=_cparams(2),
        name="ctx_mha",
    )(qkv, qkv, qkv)
    return jnp.concatenate([lat, ctx], axis=0)


def _cattn_kernel(q_ref, k_ref, v_ref, o_ref):
    rows = q_ref.shape[0]
    lane_lo = lax.broadcasted_iota(jnp.int32, (rows, LANES), 1) < HEAD_DIM
    qa, qb = _masked_halves(q_ref[...], lane_lo)
    qs = jnp.concatenate([qa, qb], axis=0)
    o = _softmax_pv([_nt_dot(qs, k_ref[...])], [v_ref[...]], None)
    o_ref[...] = jnp.where(lane_lo, o[:rows], o[rows:]).astype(o_ref.dtype)


def _router_kernel(x_ref, g_ref, sh_ref, sc_ref, wh_ref, wl_ref, b_ref, h_ref, idx_ref, gate_ref, cnt_ref,
                   run_ref):
    @pl.when(pl.program_id(0) == 0)
    def _():
        run_ref[...] = jnp.zeros_like(run_ref)

    h = _norm_mod(x_ref[...], g_ref[...], sh_ref[0], sc_ref[0])
    h_hi = h.astype(BF16)
    h_ref[...] = h_hi
    h_lo = (h - h_hi.astype(F32)).astype(BF16)
    logits = (jnp.dot(h_hi, wh_ref[...], preferred_element_type=F32)
              + jnp.dot(h_hi, wl_ref[...], preferred_element_type=F32)
              + jnp.dot(h_lo, wh_ref[...], preferred_element_type=F32)) + b_ref[...]
    lane = lax.broadcasted_iota(jnp.int32, logits.shape, 1)
    tm = logits.shape[0]
    idx_out = jnp.zeros(logits.shape, jnp.int32)
    val_out = jnp.zeros(logits.shape, F32)
    tri = jnp.where(lax.broadcasted_iota(jnp.int32, (tm, tm), 0) > lax.broadcasted_iota(jnp.int32, (tm, tm), 1),
                    1.0, 0.0).astype(BF16)
    run = run_ref[...]
    top = None
    denom = 0.0
    for k in range(TOP_K):
        m = logits.max(axis=-1, keepdims=True)
        sel = jnp.min(jnp.where(logits == m, lane, LANES), axis=-1, keepdims=True)
        if top is None:
            top = m
        e = jnp.exp(m - top)
        denom = denom + e
        onehot = lane == sel
        before = jnp.dot(tri, jnp.where(onehot, 1.0, 0.0).astype(BF16), preferred_element_type=F32)
        rank = jnp.sum(jnp.where(onehot, before + run, 0.0), axis=-1, keepdims=True).astype(jnp.int32)
        run = run + jnp.sum(jnp.where(onehot, 1.0, 0.0), axis=0, keepdims=True)
        idx_out = jnp.where(lane == k, sel, idx_out)
        idx_out = jnp.where(lane == TOP_K + k, rank, idx_out)
        val_out = jnp.where(lane == k, e, val_out)
        logits = jnp.where(onehot, -jnp.inf, logits)
    run_ref[...] = run
    cnt_ref[...] = run
    idx_ref[...] = idx_out
    gate_ref[...] = val_out / denom


def router(x, g, shift, scale, w_router, b_router, *, seg_rows, rows, row_off=0, tm=ROW_TILE):
    m, d = rows, x.shape[1]
    n_seg = shift.shape[0]
    n_e = w_router.shape[1]
    w_pad = jnp.zeros((d, LANES), F32).at[:, :n_e].set(w_router)
    w_hi = w_pad.astype(BF16)
    w_lo = (w_pad - w_hi.astype(F32)).astype(BF16)
    b_pad = jnp.full((1, LANES), -jnp.inf, F32).at[0, :n_e].set(b_router)
    blk_off = row_off // tm
    seg = lambda i: (jnp.minimum((blk_off + i) * tm // seg_rows, n_seg - 1), 0, 0)
    return pl.pallas_call(
        _router_kernel,
        grid=(m // tm,),
        in_specs=[pl.BlockSpec((tm, d), lambda i: (blk_off + i, 0)),
                  pl.BlockSpec((1, d), lambda i: (0, 0)),
                  pl.BlockSpec((1, 1, d), seg),
                  pl.BlockSpec((1, 1, d), seg),
                  pl.BlockSpec((d, LANES), lambda i: (0, 0)),
                  pl.BlockSpec((d, LANES), lambda i: (0, 0)),
                  pl.BlockSpec((1, LANES), lambda i: (0, 0))],
        out_specs=[pl.BlockSpec((tm, d), lambda i: (i, 0)),
                   pl.BlockSpec((tm, LANES), lambda i: (i, 0)),
                   pl.BlockSpec((tm, LANES), lambda i: (i, 0)),
                   pl.BlockSpec((1, LANES), lambda i: (0, 0))],
        out_shape=[jax.ShapeDtypeStruct((m, d), BF16),
                   jax.ShapeDtypeStruct((m, LANES), jnp.int32),
                   jax.ShapeDtypeStruct((m, LANES), F32),
                   jax.ShapeDtypeStruct((1, LANES), F32)],
        scratch_shapes=[pltpu.VMEM((1, LANES), F32)],
        compiler_params=_cparams(1),
        name="router",
    )(x, g, shift, scale, w_hi, w_lo, b_pad)


def _expert_kernel(blk_e_ref, n_used_ref, x_ref, wgu_ref, bgu_ref, wd_ref, bd_ref, o_ref, wgu_s, wd_s, *, f):
    i = pl.program_id(0)
    used = i < n_used_ref[0]

    @pl.when(used & ((i == 0) | (blk_e_ref[i] != blk_e_ref[jnp.maximum(i - 1, 0)])))
    def _():
        wgu_s[...] = wgu_ref[0].astype(BF16)
        wd_s[...] = wd_ref[0].astype(BF16)

    @pl.when(used)
    def _():
        gu = jnp.dot(x_ref[...], wgu_s[...], preferred_element_type=F32) + bgu_ref[0]
        g = jnp.minimum(gu[:, :f], SWIGLU_LIMIT)
        u = jnp.clip(gu[:, f:], -SWIGLU_LIMIT, SWIGLU_LIMIT)
        act = (u + 1) * (g * jax.nn.sigmoid(SWIGLU_ALPHA * g))
        y = jnp.dot(act.astype(BF16), wd_s[...], preferred_element_type=F32) + bd_ref[0]
        o_ref[...] = y.astype(o_ref.dtype)

    @pl.when(i >= n_used_ref[0])
    def _():
        o_ref[...] = jnp.zeros_like(o_ref)


def expert_mlp(xs, blk_e, n_used, w_gu, b_gu, w_down, b_down, *, layer, bm=MOE_ROWS):
    n_slot, d = xs.shape
    _, n_e, _, f2 = w_gu.shape
    f = f2 // 2
    return pl.pallas_call(
        functools.partial(_expert_kernel, f=f),
        grid_spec=pltpu.PrefetchScalarGridSpec(
            num_scalar_prefetch=2,
            grid=(n_slot // bm,),
            in_specs=[pl.BlockSpec((bm, d), lambda i, be, nu: (i, 0)),
                      pl.BlockSpec((None, 1, d, f2), lambda i, be, nu: (layer, be[i], 0, 0)),
                      pl.BlockSpec((1, 1, f2), lambda i, be, nu: (be[i], 0, 0)),
                      pl.BlockSpec((None, 1, f, d), lambda i, be, nu: (layer, be[i], 0, 0)),
                      pl.BlockSpec((1, 1, d), lambda i, be, nu: (be[i], 0, 0))],
            out_specs=pl.BlockSpec((bm, d), lambda i, be, nu: (i, 0)),
            scratch_shapes=[pltpu.VMEM((d, f2), BF16), pltpu.VMEM((f, d), BF16)]),
        out_shape=jax.ShapeDtypeStruct((n_slot, d), BF16),
        compiler_params=pltpu.CompilerParams(dimension_semantics=("arbitrary",),
                                             vmem_limit_bytes=EXPERT_VMEM_LIMIT),
        name="expert_mlp",
    )(blk_e, n_used, xs, w_gu, b_gu.reshape(n_e, 1, f2), w_down, b_down.reshape(n_e, 1, d))


def _combine_kernel(res_ref, y_ref, w_ref, gate_ref, o_ref):
    w = w_ref[...]
    y = w[:, 0:1] * y_ref[0].astype(F32)
    for k in range(1, TOP_K):
        y = y + w[:, k:k + 1] * y_ref[k].astype(F32)
    o_ref[...] = res_ref[...] + gate_ref[0] * y


def moe_dispatch(x, g, shift, scale, w_router, b_router, *, seg_rows, rows, row_off, bm=MOE_ROWS):
    m = rows
    n_e = w_router.shape[1]
    h, route, gates, counts = router(x, g, shift, scale, w_router, b_router, seg_rows=seg_rows, rows=rows,
                                     row_off=row_off)
    idx = route[:, :TOP_K]
    rank = route[:, TOP_K:2 * TOP_K]
    n_asg = m * TOP_K
    counts = counts[0, :n_e].astype(jnp.int32)
    padded = (counts + bm - 1) // bm * bm
    pad_end = jnp.cumsum(padded)
    pad_start = pad_end - padded
    start_of = jnp.sum(jnp.where(idx[:, :, None] == jnp.arange(n_e, dtype=jnp.int32), pad_start, 0), axis=-1)
    dest = (start_of + rank).reshape(-1)
    n_blk = -(-(n_asg + n_e * (bm - 1)) // bm)
    n_slot = n_blk * bm
    tok = (jnp.arange(n_asg, dtype=jnp.int32) // TOP_K)
    slot_tok = (jnp.arange(n_slot, dtype=jnp.int32) % m).at[dest].set(tok, unique_indices=True)
    blk_start = jnp.arange(n_blk, dtype=jnp.int32) * bm
    blk_e = jnp.minimum(jnp.sum((pad_end[None, :] <= blk_start[:, None]).astype(jnp.int32), axis=1), n_e - 1)
    n_used = (pad_end[-1] // bm).astype(jnp.int32).reshape(1)
    xs = h.at[slot_tok].get(mode="promise_in_bounds")
    dest_km = dest.reshape(m, TOP_K).T.reshape(-1)
    return xs, blk_e, n_used, dest_km, gates


def moe_combine(x, y_slot, dest_km, gates, gate, *, seg_rows, rows, row_off):
    m, d = rows, x.shape[1]
    y_tok = y_slot.at[dest_km].get(mode="promise_in_bounds").reshape(TOP_K, m, d)
    tm = 256
    n_seg = gate.shape[0]
    blk_off = row_off // tm
    seg = lambda i: (jnp.minimum((blk_off + i) * tm // seg_rows, n_seg - 1), 0, 0)
    return pl.pallas_call(
        _combine_kernel,
        grid=(m // tm,),
        in_specs=[pl.BlockSpec((tm, d), lambda i: (blk_off + i, 0)),
                  pl.BlockSpec((TOP_K, tm, d), lambda i: (0, i, 0)),
                  pl.BlockSpec((tm, LANES), lambda i: (i, 0)),
                  pl.BlockSpec((1, 1, d), seg)],
        out_specs=pl.BlockSpec((tm, d), lambda i: (blk_off + i, 0)),
        out_shape=jax.ShapeDtypeStruct(x.shape, F32),
        input_output_aliases={0: 0},
        compiler_params=_cparams(1),
        name="moe_combine",
    )(x, y_tok, gates, gate)


LC_R2 = 128


@functools.lru_cache(maxsize=None)
def _dft_small_mats(n):
    nn = 2 * n
    ang = 2.0 * np.pi * ((np.arange(nn)[:, None] * np.arange(nn)[None, :]) % nn) / nn
    fwd = np.concatenate([np.cos(ang), -np.sin(ang)], 0)
    inv = np.concatenate([np.cos(ang[:n]), -np.sin(ang[:n])], 1) / nn
    return jnp.asarray(fwd, BF16), jnp.asarray(inv, BF16)


def _plain_mm_kernel(a_ref, b_ref, o_ref):
    o_ref[...] = jnp.dot(a_ref[...], b_ref[...].astype(BF16), preferred_element_type=F32)


def plain_mm(a, b, tn=512):
    m, k = a.shape
    n = b.shape[1]
    tn = _col_tile(n, tn)
    return pl.pallas_call(
        _plain_mm_kernel,
        grid=(n // tn,),
        in_specs=[pl.BlockSpec((m, k), lambda j: (0, 0)), pl.BlockSpec((k, tn), lambda j: (0, j))],
        out_specs=pl.BlockSpec((m, tn), lambda j: (0, j)),
        out_shape=jax.ShapeDtypeStruct((m, n), F32),
        compiler_params=_cparams(1),
        name="plain_mm",
    )(a, b)


def _short_conv_kernel(u_ref, gate_ref, h_ref, skip_ref, fwd_ref, inv_ref, o_ref):
    u = u_ref[...]
    x = jnp.dot(fwd_ref[...], u.astype(BF16), preferred_element_type=F32)
    h = h_ref[...]
    nn = x.shape[0] // 2
    xr, xi, hr, hi = x[:nn], x[nn:], h[:nn], h[nn:]
    prod = jnp.concatenate([xr * hr - xi * hi, xr * hi + xi * hr], axis=0).astype(BF16)
    y = jnp.dot(inv_ref[...], prod, preferred_element_type=F32)
    o_ref[...] = (gate_ref[...].astype(F32) * (y + skip_ref[...] * u.astype(F32))).astype(o_ref.dtype)


def short_long_conv(u_arr, u_col, gate_arr, gate_col, spec, skip, fwd, inv, *, batch, seq_len, tl=512):
    d = spec.shape[1]
    tl = _col_tile(d, tl)
    uc, gc = u_col // tl, gate_col // tl
    nn2 = spec.shape[0]
    return pl.pallas_call(
        _short_conv_kernel,
        grid=(batch, d // tl),
        in_specs=[pl.BlockSpec((seq_len, tl), lambda b, j: (b, uc + j)),
                  pl.BlockSpec((seq_len, tl), lambda b, j: (b, gc + j)),
                  pl.BlockSpec((nn2, tl), lambda b, j: (0, j)),
                  pl.BlockSpec((1, tl), lambda b, j: (0, j)),
                  pl.BlockSpec((nn2, seq_len), lambda b, j: (0, 0)),
                  pl.BlockSpec((seq_len, nn2), lambda b, j: (0, 0))],
        out_specs=pl.BlockSpec((seq_len, tl), lambda b, j: (b, j)),
        out_shape=jax.ShapeDtypeStruct((batch * seq_len, d), BF16),
        compiler_params=_cparams(2),
        name="short_long_conv",
    )(u_arr, gate_arr, spec, skip, fwd[:, :seq_len], inv)


def _hyena_taps(n, d, f_w1, f_b1, f_w2, f_b2, f_w3, f_b3, f_freq, f_w4):
    hp = lax.Precision.HIGHEST
    lin = jnp.linspace(0.0, 1.0, n, dtype=F32)
    idx = jnp.arange(n, dtype=F32)
    bands = jnp.linspace(1e-4, HY_BANDS - 1, HY_BANDS, dtype=F32)[None, :]
    deltas = jnp.abs(jnp.linspace(math.log(HY_DECAY_TARGET) / HY_FAST_DECAY,
                                  math.log(HY_DECAY_TARGET) / HY_SLOW_DECAY, d, dtype=F32))

    def branch(t, pos, direction):
        t = t[:, None]
        ang = (2.0 * math.pi / n) * pos[:, None]
        emb = jnp.concatenate([t, jnp.cos(bands * ang), -jnp.sin(bands * ang)], axis=-1)
        a = jnp.sin(f_freq * (jnp.dot(emb, f_w1, precision=hp) + f_b1))
        a = jnp.sin(f_freq * (jnp.dot(a, f_w2, precision=hp) + f_b2))
        a = jnp.sin(f_freq * (jnp.dot(a, f_w3, precision=hp) + f_b3))
        decay = jnp.exp(-t * deltas[None, :])
        w4 = f_w4.reshape(f_w4.shape[0], HY_ORDER, 2, d)[:, :, direction]
        return [jnp.dot(a, w4[:, o], precision=hp) * decay for o in range(HY_ORDER)]

    fwd = branch(lin, idx, 0)
    bwd = branch(lin[:0:-1], idx[:0:-1], 1)
    zero = jnp.zeros((1, d), F32)
    return [jnp.concatenate([fwd[o], zero, bwd[o]], axis=0) for o in range(HY_ORDER)]


HC = 128
HC_PAIR = 2
LC_TL = 32
HY_JN = 2


@functools.lru_cache(maxsize=None)
def _hy_mats(n):
    nn = 2 * n
    r1 = nn // LC_R2
    tau = 2.0 * np.pi
    k1 = np.arange(r1)
    t = LC_R2 * np.arange(r1)[None, :] + np.arange(LC_R2)[:, None]
    ang1 = tau * ((k1[None, :, None] * t[:, None, :]) % nn) / nn
    f1 = np.empty((LC_R2, 2 * r1, r1))
    f1[:, 0::2] = np.cos(ang1)
    f1[:, 1::2] = -np.sin(ang1)
    ang2 = tau * ((np.arange(LC_R2)[:, None] * np.arange(LC_R2)[None, :]) % LC_R2) / LC_R2
    c, s = np.cos(ang2), np.sin(ang2)
    f2 = np.empty((2 * LC_R2, 2 * LC_R2))
    f2[:LC_R2, 0::2], f2[:LC_R2, 1::2] = c, s
    f2[LC_R2:, 0::2], f2[LC_R2:, 1::2] = -s, c
    f3 = np.empty((2 * LC_R2, 2 * LC_R2))
    f3[0::2, :LC_R2], f3[0::2, LC_R2:] = c, -s
    f3[1::2, :LC_R2], f3[1::2, LC_R2:] = s, c
    m = LC_R2 * np.arange(r1 // 2)[None, :] + np.arange(LC_R2)[:, None]
    ang4 = tau * ((m[:, :, None] * k1[None, None, :]) % nn) / nn
    f4 = np.empty((LC_R2, r1 // 2, 2 * r1))
    f4[:, :, 0::2] = np.cos(ang4) / nn
    f4[:, :, 1::2] = -np.sin(ang4) / nn
    return tuple(jnp.asarray(a, BF16) for a in (f1, f2, f3, f4))


def _hy1_kernel(x_ref, f_ref, o_ref, *, tl_n, nh):
    c = pl.program_id(2)
    for tl in range(tl_n):
        xs = jnp.concatenate([x_ref[h, pl.ds(c * tl_n + tl, nh, stride=LC_R2), :] for h in range(HC_PAIR)],
                             axis=1).astype(BF16)
        a = jnp.dot(f_ref[tl], xs, preferred_element_type=F32)
        packed = pltpu.bitcast(a.astype(BF16), jnp.uint32)
        for h in range(HC_PAIR):
            for j in range(packed.shape[0] // 8):
                o_ref[h, j, tl * 8:(tl + 1) * 8, :] = packed[j * 8:(j + 1) * 8, h * HC:(h + 1) * HC]


def hy_stage1(x, f1, *, blk_off, n_blk, batch, rows_hi):
    r2 = f1.shape[1]
    rows = rows_hi * LC_R2
    tl_n = LC_TL
    return pl.pallas_call(
        functools.partial(_hy1_kernel, tl_n=tl_n, nh=rows_hi),
        grid=(n_blk // HC_PAIR, batch, LC_R2 // tl_n),
        in_specs=[pl.BlockSpec((HC_PAIR, rows, HC), lambda p, b, c: (blk_off // HC_PAIR + p, b, 0)),
                  pl.BlockSpec((tl_n, r2, rows_hi), lambda p, b, c: (c, 0, 0))],
        out_specs=pl.BlockSpec((HC_PAIR, None, r2 // 16, tl_n * 8, HC), lambda p, b, c: (p, b, 0, c, 0)),
        out_shape=jax.ShapeDtypeStruct((n_blk, batch, r2 // 16, LC_R2 * 8, HC), jnp.uint32),
        compiler_params=_cparams(3),
        name="hy_stage1",
    )(x, f1[:, :, :rows_hi])


def _hy2_kernel(a_ref, f2_ref, *rest, spectrum_only):
    if spectrum_only:
        (o_ref,) = rest
    else:
        h_ref, f3_ref, o_ref = rest
    for jk in range(8 * a_ref.shape[1]):
        jj, kk = divmod(jk, 8)
        w = jnp.concatenate([a_ref[h, jj, pl.ds(kk, LC_R2, stride=8), :] for h in range(HC_PAIR)], axis=1)
        a_in = pltpu.bitcast(w, BF16)
        x = jnp.dot(f2_ref[...], a_in, preferred_element_type=F32)
        if spectrum_only:
            o_ref[jk] = x.astype(o_ref.dtype)
            continue
        hs = h_ref[jk].astype(F32)
        xr, xi = x[:LC_R2], x[LC_R2:]
        hr, hi = hs[:LC_R2], hs[LC_R2:]
        prod = jnp.concatenate([xr * hr - xi * hi, xr * hi + xi * hr], axis=0).astype(BF16)
        g = jnp.dot(f3_ref[...], prod, preferred_element_type=F32)
        packed = pltpu.bitcast(g.astype(BF16), jnp.uint32)
        for h in range(HC_PAIR):
            o_ref[h, jj, pl.ds(kk, LC_R2, stride=8), :] = packed[:, h * HC:(h + 1) * HC]


def hy_spectrum(a, f2):
    cb, _, nj, rows, _ = a.shape
    jn = HY_JN if nj % HY_JN == 0 else 1
    mat = pl.BlockSpec((2 * LC_R2, 2 * LC_R2), lambda j, p: (0, 0))
    return pl.pallas_call(
        functools.partial(_hy2_kernel, spectrum_only=True),
        grid=(nj // jn, cb // HC_PAIR),
        in_specs=[pl.BlockSpec((HC_PAIR, None, jn, rows, HC), lambda j, p: (p, 0, j, 0, 0)), mat],
        out_specs=pl.BlockSpec((None, 8 * jn, 2 * LC_R2, HC_PAIR * HC), lambda j, p: (p, j, 0, 0)),
        out_shape=jax.ShapeDtypeStruct((cb // HC_PAIR, nj * 8, 2 * LC_R2, HC_PAIR * HC), BF16),
        compiler_params=_cparams(2),
        name="hy_spectrum",
    )(a, f2)


def hy_stage23(a, spec, f2, f3):
    cb, bsz, nj, rows, _ = a.shape
    jn = HY_JN if nj % HY_JN == 0 else 1
    mat = pl.BlockSpec((2 * LC_R2, 2 * LC_R2), lambda j, p, b: (0, 0))
    blk = pl.BlockSpec((HC_PAIR, None, jn, rows, HC), lambda j, p, b: (p, b, j, 0, 0))
    return pl.pallas_call(
        functools.partial(_hy2_kernel, spectrum_only=False),
        grid=(nj // jn, cb // HC_PAIR, bsz),
        in_specs=[blk, mat,
                  pl.BlockSpec((None, 8 * jn, 2 * LC_R2, HC_PAIR * HC), lambda j, p, b: (p, j, 0, 0)), mat],
        out_specs=blk,
        out_shape=jax.ShapeDtypeStruct(a.shape, jnp.uint32),
        compiler_params=_cparams(3),
        name="hy_stage23",
    )(a, f2, spec, f3)


def _hy4_kernel(g_ref, f_ref, o_ref, *, tl_n, nh):
    c = pl.program_id(2)
    for tl in range(tl_n):
        w = jnp.concatenate([g_ref[h, :, tl * 8:(tl + 1) * 8, :].reshape(-1, HC) for h in range(HC_PAIR)], axis=1)
        g_in = pltpu.bitcast(w, BF16)
        y = jnp.dot(f_ref[tl], g_in, preferred_element_type=F32)
        for h in range(HC_PAIR):
            o_ref[h, pl.ds(c * tl_n + tl, nh, stride=LC_R2), :] = y[:, h * HC:(h + 1) * HC]


def hy_stage4(g, f4, *, seq_len):
    cb, bsz, nj, _, _ = g.shape
    nh = seq_len // LC_R2
    tl_n = LC_TL
    return pl.pallas_call(
        functools.partial(_hy4_kernel, tl_n=tl_n, nh=nh),
        grid=(cb // HC_PAIR, bsz, LC_R2 // tl_n),
        in_specs=[pl.BlockSpec((HC_PAIR, None, nj, tl_n * 8, HC), lambda p, b, c: (p, b, 0, c, 0)),
                  pl.BlockSpec((tl_n, nh, 16 * nj), lambda p, b, c: (c, 0, 0))],
        out_specs=pl.BlockSpec((HC_PAIR, seq_len, HC), lambda p, b, c: (p, b, 0)),
        out_shape=jax.ShapeDtypeStruct((cb, bsz * seq_len, HC), F32),
        compiler_params=_cparams(3),
        name="hy_stage4",
    )(g, f4)


def _hy_gate_kernel(y_ref, u_ref, gate_ref, skip_ref, o_ref):
    o_ref[...] = (gate_ref[...] * (y_ref[...] + skip_ref[...] * u_ref[...])).astype(o_ref.dtype)


def hy_gate(y, u_arr, u_off, gate_arr, gate_off, skip, *, natural, tr=4096):
    cb, rows, _ = y.shape
    tr = min(tr, rows)
    if natural:
        out_spec = pl.BlockSpec((tr, HC), lambda i, r: (r, i))
        out_shape = jax.ShapeDtypeStruct((rows, cb * HC), BF16)
    else:
        out_spec = pl.BlockSpec((None, tr, HC), lambda i, r: (i, r, 0))
        out_shape = jax.ShapeDtypeStruct((cb, rows, HC), F32)
    return pl.pallas_call(
        _hy_gate_kernel,
        grid=(cb, rows // tr),
        in_specs=[pl.BlockSpec((None, tr, HC), lambda i, r: (i, r, 0)),
                  pl.BlockSpec((None, tr, HC), lambda i, r: (u_off + i, r, 0)),
                  pl.BlockSpec((None, tr, HC), lambda i, r: (gate_off + i, r, 0)),
                  pl.BlockSpec((None, 1, HC), lambda i, r: (i, 0, 0))],
        out_specs=out_spec,
        out_shape=out_shape,
        compiler_params=_cparams(2),
        name="hy_gate",
    )(y, u_arr, gate_arr, skip.reshape(cb, 1, HC))


def _hy_taps_kernel(a_ref, t_ref, w_ref, dl_ref, o_ref):
    a = a_ref[...]
    a_hi = a.astype(BF16)
    a_lo = (a - a_hi.astype(F32)).astype(BF16)
    w = w_ref[...]
    w_hi = w.astype(BF16)
    w_lo = (w - w_hi.astype(F32)).astype(BF16)
    h = (jnp.dot(a_hi, w_hi, preferred_element_type=F32) + jnp.dot(a_hi, w_lo, preferred_element_type=F32)
         + jnp.dot(a_lo, w_hi, preferred_element_type=F32))
    o_ref[...] = h * jnp.exp(-t_ref[...] * dl_ref[...])


def hy_taps(a_all, t_all, w4, deltas, *, seq_len, order, tr=2048):
    nrow, width = a_all.shape
    d = deltas.shape[0]
    tr = min(tr, seq_len)
    per_dir = seq_len // tr
    return pl.pallas_call(
        _hy_taps_kernel,
        grid=(d // HC, nrow // tr),
        in_specs=[pl.BlockSpec((tr, width), lambda i, r: (r, 0)),
                  pl.BlockSpec((tr, 1), lambda i, r: (r, 0)),
                  pl.BlockSpec((None, None, width, HC), lambda i, r: (order, r // per_dir, 0, i)),
                  pl.BlockSpec((1, HC), lambda i, r: (0, i))],
        out_specs=pl.BlockSpec((None, tr, HC), lambda i, r: (i, r, 0)),
        out_shape=jax.ShapeDtypeStruct((d // HC, nrow, HC), F32),
        compiler_params=_cparams(2),
        name="hy_taps",
    )(a_all, t_all, w4.reshape(width, HY_ORDER, 2, d).transpose(1, 2, 0, 3), deltas.reshape(1, d))


def _hyena_filter_hidden(n, f_w1, f_b1, f_w2, f_b2, f_w3, f_b3, f_freq):
    hp = lax.Precision.HIGHEST
    lin = jnp.linspace(0.0, 1.0, n, dtype=F32)
    idx = jnp.arange(n, dtype=F32)
    bands = jnp.linspace(1e-4, HY_BANDS - 1, HY_BANDS, dtype=F32)[None, :]
    t = jnp.concatenate([lin, jnp.zeros((1,), F32), lin[:0:-1]])[:, None]
    pos = jnp.concatenate([idx, jnp.zeros((1,), F32), idx[:0:-1]])[:, None]
    ang = (2.0 * math.pi / n) * pos
    emb = jnp.concatenate([t, jnp.cos(bands * ang), -jnp.sin(bands * ang)], axis=-1)
    a = jnp.sin(f_freq * (jnp.dot(emb, f_w1, precision=hp) + f_b1))
    a = jnp.sin(f_freq * (jnp.dot(a, f_w2, precision=hp) + f_b2))
    a = jnp.sin(f_freq * (jnp.dot(a, f_w3, precision=hp) + f_b3))
    keep = (jnp.arange(2 * n) != n)[:, None]
    return jnp.where(keep, a, 0.0), t


def _hyena_deltas(d):
    return jnp.abs(jnp.linspace(math.log(HY_DECAY_TARGET) / HY_FAST_DECAY,
                                math.log(HY_DECAY_TARGET) / HY_SLOW_DECAY, d, dtype=F32))


def hyena_long_convs2(zc, a_all, t_all, w4, skip, *, batch, seq_len, d):
    cb = d // HC
    f1, f2, f3, f4 = _hy_mats(seq_len)
    nh = seq_len // LC_R2
    deltas = _hyena_deltas(d)
    u, u_off = zc, 2 * cb
    for o in range(HY_ORDER):
        taps = hy_taps(a_all, t_all, w4, deltas, seq_len=seq_len, order=o)
        spec = hy_spectrum(hy_stage1(taps, f1, blk_off=0, n_blk=cb, batch=1, rows_hi=2 * nh), f2)
        a = hy_stage1(u, f1, blk_off=u_off, n_blk=cb, batch=batch, rows_hi=nh)
        y = hy_stage4(hy_stage23(a, spec, f2, f3), f4, seq_len=seq_len)
        u = hy_gate(y, u, u_off, zc, o * cb, skip[o], natural=(o == HY_ORDER - 1))
        u_off = 0
    return u


def _final_norm_kernel(x_ref, g_ref, o_ref):
    x = x_ref[...]
    o_ref[...] = x * lax.rsqrt(jnp.mean(x * x, axis=-1, keepdims=True) + NORM_EPS) * g_ref[...]


def final_norm(x, g, *, rows, tm=ROW_TILE):
    d = x.shape[1]
    return pl.pallas_call(
        _final_norm_kernel,
        grid=(rows // tm,),
        in_specs=[pl.BlockSpec((tm, d), lambda i: (i, 0)), pl.BlockSpec((1, d), lambda i: (0, 0))],
        out_specs=pl.BlockSpec((tm, d), lambda i: (i, 0)),
        out_shape=jax.ShapeDtypeStruct((rows, d), F32),
        compiler_params=_cparams(1),
        name="final_norm",
    )(x, g)


def _rope_tables(n_lat, seq_len, n_rows):
    t = jnp.arange(n_rows, dtype=jnp.int32)
    row = ((t % seq_len) // GRID_W).astype(F32)
    col = (t % GRID_W).astype(F32)
    lane = jnp.arange(LANES)
    dd = lane % HEAD_DIM
    quarter = HEAD_DIM // 4
    inv_freq = ROPE_BASE ** (-(dd % quarter).astype(F32) / quarter)
    pos = jnp.where(dd[None, :] < HEAD_DIM // 2, row[:, None], col[:, None])
    ang = pos * inv_freq[None, :]
    sign = jnp.where((dd % (HEAD_DIM // 2)) < quarter, -1.0, 1.0).astype(F32)
    lat = (t < n_lat)[:, None]
    cos = jnp.where(lat, jnp.cos(ang), 1.0)
    sin = jnp.where(lat, jnp.sin(ang) * sign[None, :], 0.0)
    return cos, sin


def _wa_head_perm(n_heads):
    order = []
    for p in range(n_heads // (2 * WA_GROUP)):
        for g in range(WA_GROUP):
            order += [2 * WA_GROUP * p + g, 2 * WA_GROUP * p + WA_GROUP + g]
    cols = jnp.asarray(order, jnp.int32)[:, None] * HEAD_DIM + jnp.arange(HEAD_DIM, dtype=jnp.int32)[None, :]
    return cols.reshape(-1)


def kernel(x, c, ctx, c_ctx, w_ada, b_ada, g_mix, g_ffn, hy_w_in, hy_b_in, hy_w_short, hy_b_short, hy_f_w1, hy_f_b1, hy_f_w2, hy_f_b2, hy_f_w3, hy_f_b3, hy_f_freq, hy_f_w4, hy_skip, hy_w_out, hy_b_out, cf_w_pw1, cf_b_pw1, cf_w_dw, cf_b_dw, cf_ln_g, cf_ln_b, cf_w_pw2, cf_b_pw2, wa_w_qkv, wa_w_o, wa_sinks, na_w_qkv, na_w_o, na_rpb, moe_w_router, moe_b_router, moe_w_gu, moe_b_gu, moe_w_down, moe_b_down, g_final):
    bsz, n, d = x.shape
    n_ctx = ctx.shape[1]
    depth = w_ada.shape[0]
    n_lat = bsz * n
    m_all = n_lat + bsz * n_ctx

    cond = jnp.zeros((8, d), F32).at[:bsz].set(jax.nn.silu(c)).at[bsz].set(jax.nn.silu(c_ctx))
    mods = adaln(cond, w_ada, b_ada)
    xu = jnp.concatenate([x.reshape(n_lat, d), ctx.reshape(bsz * n_ctx, d)], axis=0)
    zeros_d = jnp.zeros((1, d), F32)

    for i in range(depth):
        kind, j = i % N_MIXERS, i // N_MIXERS
        last = i == depth - 1
        mod = [mods[i, :bsz + 1, k * d:(k + 1) * d].reshape(bsz + 1, 1, d) for k in range(6)]
        m_out = n_lat if last else m_all
        gm = g_mix[i].reshape(1, d)
        need_ctx_in = (not last) or kind >= 2
        m_in = m_all if need_ctx_in else n_lat
        x_in = xu[:m_in]

        if kind == 0:
            z = nm_matmul(x_in, gm, mod[0], mod[1], hy_w_in[j].astype(BF16), hy_b_in[j].reshape(1, -1),
                          seg_rows=n)
            filt_w = (hy_f_w1[j], hy_f_b1[j], hy_f_w2[j], hy_f_b2[j], hy_f_w3[j], hy_f_b3[j], hy_f_freq[j],
                      hy_f_w4[j])
            w_s, b_s = hy_w_short[j], hy_b_short[j].reshape(1, -1)
            w_o, b_o = hy_w_out[j].astype(BF16), hy_b_out[j].reshape(1, d)
            zc = dwconv(z, w_s, b_s, start=0, total=n_lat, seq_len=n, tc=HC, blocked=True, rows=2048,
                        out_dtype=F32)
            a_all, t_all = _hyena_filter_hidden(n, *filt_w[:-1])
            y = hyena_long_convs2(zc, a_all, t_all, hy_f_w4[j], hy_skip[j], batch=bsz, seq_len=n, d=d)
            xu = mm_res(y, w_o, b_o, xu, mod[2], seg_rows=n, tm=BIG_ROW_TILE)
            if m_in > n_lat:
                zcc = dwconv(z, w_s, b_s, start=n_lat, total=m_in - n_lat, seq_len=n_ctx, tc=512)
                taps_c = _hyena_taps(n_ctx, d, *filt_w)
                fwd, inv = _dft_small_mats(n_ctx)
                yc = short_long_conv(zcc, 2 * d, zcc, 0, plain_mm(fwd, taps_c[0]), hy_skip[j][0].reshape(1, d),
                                     fwd, inv, batch=bsz, seq_len=n_ctx)
                yc = short_long_conv(yc, 0, zcc, d, plain_mm(fwd, taps_c[1]), hy_skip[j][1].reshape(1, d),
                                     fwd, inv, batch=bsz, seq_len=n_ctx)
                if not last:
                    xu = mm_res(yc, w_o, b_o, xu, mod[2], seg_rows=n, row_off=n_lat)
            y = None
        elif kind == 1:
            w1 = cf_w_pw1[j].astype(BF16).reshape(d, 2, d).transpose(1, 0, 2)
            a = nm_matmul(x_in, gm, mod[0], mod[1], w1, cf_b_pw1[j].reshape(2, 1, d), seg_rows=n, mode="glu")
            conv = functools.partial(dwconv, a, cf_w_dw[j], cf_b_dw[j].reshape(1, d), post="ln_silu",
                                     ln=(cf_ln_g[j].reshape(1, d), cf_ln_b[j].reshape(1, d)), rows=64)
            y = conv(start=0, total=n_lat, seq_len=n, same_rows=True,
                     into=jnp.zeros((m_in, d), BF16) if m_in > n_lat else None)
            if m_in > n_lat:
                y = conv(start=n_lat, total=m_in - n_lat, seq_len=n_ctx, same_rows=True, into=y)
            w_o, b_o = cf_w_pw2[j].astype(BF16), cf_b_pw2[j].reshape(1, d)
        elif kind == 2:
            perm = _wa_head_perm(d // HEAD_DIM)
            scale = HEAD_DIM ** -0.5
            w_qkv = jnp.concatenate([wa_w_qkv[j][:, :d][:, perm] * scale, wa_w_qkv[j][:, d:]], axis=1).astype(BF16)
            n_out = w_qkv.shape[1]
            qkv = nm_matmul(x_in, gm, mod[0], mod[1], w_qkv, jnp.zeros((1, n_out), F32), seg_rows=n,
                            mode="rope", rope=_rope_tables(n_lat, n, m_in), n_rope_cols=d + (n_out - d) // 2)
            y = window_attention(qkv, wa_sinks[j], batch=bsz, seq_len=n, ctx_len=n_ctx, d=d)
            w_o, b_o = wa_w_o[j][perm].astype(BF16), zeros_d
        else:
            scale = HEAD_DIM ** -0.5
            w_qkv = jnp.concatenate([na_w_qkv[j][:, :d] * scale, na_w_qkv[j][:, d:]], axis=1).astype(BF16)
            qkv = nm_matmul(x_in, gm, mod[0], mod[1], w_qkv, jnp.zeros((1, 3 * d), F32), seg_rows=n)
            y = neighbourhood_attention(qkv, na_rpb[j], batch=bsz, seq_len=n, ctx_len=n_ctx, d=d,
                                        with_ctx_out=not last)
            w_o, b_o = na_w_o[j].astype(BF16), zeros_d
        if y is not None:
            xu = mm_res(y, w_o, b_o, xu, mod[2], seg_rows=n, rows=n_lat, tm=BIG_ROW_TILE)
            if m_out > n_lat:
                xu = mm_res(y, w_o, b_o, xu, mod[2], seg_rows=n, rows=m_out - n_lat, row_off=n_lat, a_off=n_lat)

        ranges = ((0, m_out),)
        routed = [moe_dispatch(xu, g_ffn[i].reshape(1, d), mod[3], mod[4], moe_w_router[i], moe_b_router[i],
                               seg_rows=n, rows=cnt, row_off=off) for off, cnt in ranges]
        y_slots = [expert_mlp(xs, blk_e, n_used, moe_w_gu, moe_b_gu[i], moe_w_down, moe_b_down[i], layer=i)
                   for xs, blk_e, n_used, _, _ in routed]
        for (off, cnt), (_, _, _, dest_km, gates), y_slot in zip(ranges, routed, y_slots):
            xu = moe_combine(xu, y_slot, dest_km, gates, mod[5], seg_rows=n, rows=cnt, row_off=off)
    return final_norm(xu, g_final.reshape(1, d), rows=n_lat).reshape(bsz, n, d)
```
